```python
import math
import jax
import jax.numpy as jnp
from jax import lax
import numpy as np


D_MODEL = 1024
BATCH = 16
SEQ = 4096
DEPTH = 2

MEM_LEN = 256
RMS_EPS = 1e-6
ROPE_THETA = 10000.0
NEG_INF = -1e30
FORCE_BONUS = 1e3

NSA_HEADS = 8
NSA_KV_GROUPS = 2
NSA_HPG = NSA_HEADS // NSA_KV_GROUPS
NSA_HD = 64
CMP_LEN = 32
CMP_STRIDE = 16
CMP_HID = 2 * NSA_HD
SLC_BLK = 64
SLC_TOPK = 16
WINDOW = 512
NSA_QBLK = 32

GDN_HEADS = 8
GDN_HD = 64
GDN_CONV = 4
GDN_CHUNK = 64

SC_WIDTH = 3

XA_HEADS = 4
XA_HD = 128

D_FF = 4 * D_MODEL

NSA_WIDTH = NSA_HEADS * NSA_HD
NSA_KV_WIDTH = NSA_KV_GROUPS * NSA_HD
GDN_WIDTH = GDN_HEADS * GDN_HD
MIX_WIDTH = NSA_WIDTH + GDN_WIDTH
XA_WIDTH = XA_HEADS * XA_HD
IN_SIZES = (NSA_WIDTH, NSA_KV_WIDTH, NSA_KV_WIDTH, NSA_KV_WIDTH, NSA_KV_WIDTH, NSA_KV_WIDTH, NSA_KV_WIDTH, 3 * NSA_HEADS, GDN_WIDTH, GDN_WIDTH, GDN_WIDTH, GDN_HEADS, GDN_HEADS, GDN_WIDTH)
IN_COLS = sum(IN_SIZES)
N_HYB = (DEPTH + 1) // 2
N_SC = DEPTH // 2

kernel_name = 'hybrid_nsa_gdn_shortconv_decoder'


def rms_norm(x, w):
    xf = x.astype(jnp.float32)
    y = xf * lax.rsqrt(jnp.mean(xf * xf, axis=-1, keepdims=True) + RMS_EPS)
    return (y * w.astype(jnp.float32)).astype(x.dtype)


def l2_norm(x):
    xf = x.astype(jnp.float32)
    return xf * lax.rsqrt(jnp.sum(xf * xf, axis=-1, keepdims=True) + 1e-6)


def rope(x, positions):
    d = x.shape[-1]
    inv_freq = ROPE_THETA ** (-jnp.arange(0, d, 2, dtype=jnp.float32) / d)
    ang = positions.astype(jnp.float32)[..., None] * inv_freq
    cos = jnp.cos(ang)[:, :, None, :]
    sin = jnp.sin(ang)[:, :, None, :]
    xf = x.astype(jnp.float32)
    x1, x2 = xf[..., : d // 2], xf[..., d // 2:]
    return jnp.concatenate([x1 * cos - x2 * sin, x2 * cos + x1 * sin], axis=-1).astype(x.dtype)


def causal_dwconv(x, w):
    k_len, ch = w.shape
    return lax.conv_general_dilated(x, w[:, None, :].astype(x.dtype), window_strides=(1,), padding=[(k_len - 1, 0)], dimension_numbers=('NWC', 'WIO', 'NWC'), feature_group_count=ch)


def compress_blocks(t, pos_emb, w1, w2):
    b_, s_, g_, d = t.shape
    n_cmp = (s_ - CMP_LEN) // CMP_STRIDE + 1
    idx = np.arange(n_cmp)[:, None] * CMP_STRIDE + np.arange(CMP_LEN)[None, :]
    blocks = t[:, idx] + pos_emb[None, None, :, None, :]
    flat = blocks.transpose(0, 3, 1, 2, 4).reshape(b_, g_, n_cmp, CMP_LEN * d)
    return jax.nn.silu(flat @ w1) @ w2


def cmp_to_slc_matrix(n_cmp, n_slc):
    cs = np.arange(n_cmp)[:, None] * CMP_STRIDE
    js = np.arange(n_slc)[None, :] * SLC_BLK
    ov = np.clip(np.minimum(cs + CMP_LEN, js + SLC_BLK) - np.maximum(cs, js), 0, None)
    return jnp.asarray(ov / CMP_LEN, dtype=jnp.float32)


def nsa_attention(q, k_cmp, v_cmp, k_slc, v_slc, k_win, v_win, gates, ck_pos, ck_w1, ck_w2, cv_pos, cv_w1, cv_w2):
    b_, s_ = q.shape[:2]
    g_, hpg, d = NSA_KV_GROUPS, NSA_HPG, NSA_HD
    scale = d ** -0.5
    kc = compress_blocks(k_cmp, ck_pos, ck_w1, ck_w2)
    vc = compress_blocks(v_cmp, cv_pos, cv_w1, cv_w2)
    n_cmp = kc.shape[2]
    cmp_end = jnp.arange(n_cmp) * CMP_STRIDE + CMP_LEN - 1
    n_slc = s_ // SLC_BLK
    n_sel = min(SLC_TOPK, n_slc)
    overlap = cmp_to_slc_matrix(n_cmp, n_slc)
    ks_blk = k_slc.reshape(b_, n_slc, SLC_BLK, g_, d).transpose(0, 3, 1, 2, 4)
    vs_blk = v_slc.reshape(b_, n_slc, SLC_BLK, g_, d).transpose(0, 3, 1, 2, 4)
    kw = jnp.pad(k_win, ((0, 0), (WINDOW, 0), (0, 0), (0, 0))).transpose(0, 2, 1, 3)
    vw = jnp.pad(v_win, ((0, 0), (WINDOW, 0), (0, 0), (0, 0))).transpose(0, 2, 1, 3)
    n_qb = s_ // NSA_QBLK
    qb = q.reshape(b_, n_qb, NSA_QBLK, g_, hpg, d).transpose(1, 0, 3, 4, 2, 5)
    gb = gates.reshape(b_, n_qb, NSA_QBLK, g_, hpg, 3).transpose(1, 0, 3, 4, 2, 5)
    b_ix = jnp.arange(b_)[:, None, None, None]
    g_ix = jnp.arange(g_)[None, :, None, None]
    slc_j = jnp.arange(n_slc)

    def query_block(args):
        qc, gc, c = args
        t = c * NSA_QBLK + jnp.arange(NSA_QBLK)
        s = jnp.einsum('bghqd,bgnd->bghqn', qc, kc).astype(jnp.float32) * scale
        valid = cmp_end[None, :] <= t[:, None]
        p_cmp = jax.nn.softmax(jnp.where(valid, s, NEG_INF), axis=-1) * valid
        o_cmp = jnp.einsum('bghqn,bgnd->bghqd', p_cmp.astype(vc.dtype), vc)
        imp = jnp.einsum('bghqn,nj->bgqj', p_cmp, overlap)
        cur = t // SLC_BLK
        forced = (slc_j[None, :] == 0) | (slc_j[None, :] == cur[:, None]) | (slc_j[None, :] == cur[:, None] - 1)
        causal_blk = slc_j[None, :] * SLC_BLK <= t[:, None]
        imp = jnp.where(causal_blk, jnp.where(forced, imp + FORCE_BONUS, imp), -1.0)
        _, sel = lax.top_k(imp, n_sel)
        kg = ks_blk[b_ix, g_ix, sel]
        vg = vs_blk[b_ix, g_ix, sel]
        tok = sel[..., None] * SLC_BLK + jnp.arange(SLC_BLK)
        tok_ok = tok <= t[None, None, :, None, None]
        s = jnp.einsum('bghqd,bgqnld->bghqnl', qc, kg).astype(jnp.float32) * scale
        s = jnp.where(tok_ok[:, :, None], s, NEG_INF).reshape(b_, g_, hpg, NSA_QBLK, n_sel * SLC_BLK)
        p = jax.nn.softmax(s, axis=-1).reshape(b_, g_, hpg, NSA_QBLK, n_sel, SLC_BLK)
        o_slc = jnp.einsum('bghqnl,bgqnld->bghqd', p.astype(vg.dtype), vg)
        kwc = lax.dynamic_slice_in_dim(kw, c * NSA_QBLK, NSA_QBLK + WINDOW, axis=2)
        vwc = lax.dynamic_slice_in_dim(vw, c * NSA_QBLK, NSA_QBLK + WINDOW, axis=2)
        kpos = c * NSA_QBLK - WINDOW + jnp.arange(NSA_QBLK + WINDOW)
        win_ok = (kpos[None, :] <= t[:, None]) & (kpos[None, :] > t[:, None] - WINDOW) & (kpos[None, :] >= 0)
        s = jnp.einsum('bghqd,bgkd->bghqk', qc, kwc).astype(jnp.float32) * scale
        p = jax.nn.softmax(jnp.where(win_ok, s, NEG_INF), axis=-1)
        o_win = jnp.einsum('bghqk,bgkd->bghqd', p.astype(vwc.dtype), vwc)
        return gc[..., 0:1] * o_cmp + gc[..., 1:2] * o_slc + gc[..., 2:3] * o_win

    o = lax.map(query_block, (qb, gb, jnp.arange(n_qb)))
    return o.transpose(1, 0, 4, 2, 3, 5).reshape(b_, s_, NSA_HEADS * d)


def chunk_gated_delta_rule(q, k, v, g, beta):
    b_, s_, h_, dk = q.shape
    dv = v.shape[-1]
    c_len = GDN_CHUNK
    n_ch = s_ // c_len

    def to_chunks(t):
        t = t.reshape(b_, n_ch, c_len, h_, *t.shape[3:])
        return jnp.moveaxis(t, (1, 3), (0, 2))

    qc, kc, vc, bc = to_chunks(q), to_chunks(k), to_chunks(v), to_chunks(beta)
    gcs = jnp.cumsum(to_chunks(g), axis=-1)
    incl = jnp.tril(jnp.ones((c_len, c_len), dtype=bool))
    strict = jnp.tril(jnp.ones((c_len, c_len), dtype=bool), -1)
    decay = jnp.exp(jnp.where(incl, gcs[..., :, None] - gcs[..., None, :], NEG_INF))
    kb = kc * bc[..., None]
    a_mat = jnp.where(strict, jnp.einsum('nbhcd,nbhed->nbhce', kb, kc) * decay, 0.0)
    eye = jnp.eye(c_len, dtype=jnp.float32)
    t_inv = lax.linalg.triangular_solve(eye + a_mat, jnp.broadcast_to(eye, a_mat.shape), left_side=True, lower=True, unit_diagonal=True)
    u = t_inv @ (vc * bc[..., None])
    w = t_inv @ (kb * jnp.exp(gcs)[..., None])
    qk = jnp.einsum('nbhcd,nbhed->nbhce', qc, kc) * decay

    def step(state, xs):
        q_i, k_i, u_i, w_i, qk_i, g_i = xs
        v_new = u_i - w_i @ state
        o_i = (q_i * jnp.exp(g_i)[..., None]) @ state + qk_i @ v_new
        g_last = g_i[..., -1:]
        state = state * jnp.exp(g_last)[..., None] + jnp.einsum('bhcd,bhce->bhde', k_i * jnp.exp(g_last - g_i)[..., None], v_new)
        return state, o_i

    state0 = jnp.zeros((b_, h_, dk, dv), jnp.float32)
    _, o = lax.scan(step, state0, (qc, kc, u, w, qk, gcs))
    return jnp.moveaxis(o, (0, 2), (1, 3)).reshape(b_, s_, h_, dv)


def gated_deltanet(q, k, v, a, b, z, conv_w, a_log, dt_bias, norm_w):
    b_, s_ = q.shape[:2]
    h_, d = GDN_HEADS, GDN_HD
    qkv = jax.nn.silu(causal_dwconv(jnp.concatenate([q, k, v], axis=-1), conv_w))
    q, k, v = jnp.split(qkv, 3, axis=-1)
    q = l2_norm(q.reshape(b_, s_, h_, d)) * (d ** -0.5)
    k = l2_norm(k.reshape(b_, s_, h_, d))
    v = v.reshape(b_, s_, h_, d).astype(jnp.float32)
    beta = jax.nn.sigmoid(b.astype(jnp.float32))
    g = -jnp.exp(a_log.astype(jnp.float32)) * jax.nn.softplus(a.astype(jnp.float32) + dt_bias.astype(jnp.float32))
    o = chunk_gated_delta_rule(q, k, v, g, beta)
    o = rms_norm(o, norm_w) * jax.nn.silu(z.reshape(b_, s_, h_, d).astype(jnp.float32))
    return o.reshape(b_, s_, h_ * d).astype(z.dtype)


def hybrid_mixer(h, positions, w_in, ck_pos, ck_w1, ck_w2, cv_pos, cv_w1, cv_w2, gdn_conv, gdn_a_log, gdn_dt_bias, gdn_norm, w_out):
    b_, s_, _ = h.shape
    splits = [int(p) for p in np.cumsum(IN_SIZES)[:-1]]
    (nq, kcmp, vcmp, kslc, vslc, kwin, vwin, ngate, gq, gk, gv, ga, gb, gz) = jnp.split(h @ w_in, splits, axis=-1)

    def heads(t, n):
        return t.reshape(b_, s_, n, -1)

    g_ = NSA_KV_GROUPS
    o_nsa = nsa_attention(rope(heads(nq, NSA_HEADS), positions), heads(kcmp, g_), heads(vcmp, g_), rope(heads(kslc, g_), positions), heads(vslc, g_), rope(heads(kwin, g_), positions), heads(vwin, g_), jax.nn.sigmoid(ngate).reshape(b_, s_, NSA_HEADS, 3), ck_pos, ck_w1, ck_w2, cv_pos, cv_w1, cv_w2)
    o_gdn = gated_deltanet(gq, gk, gv, ga, gb, gz, gdn_conv, gdn_a_log, gdn_dt_bias, gdn_norm)
    return jnp.concatenate([o_nsa, o_gdn], axis=-1) @ w_out


def short_conv_mixer(h, w_in, conv_w, w_out):
    b_gate, c_gate, u = jnp.split(h @ w_in, 3, axis=-1)
    return (b_gate * causal_dwconv(c_gate * u, conv_w)) @ w_out


def cross_attention(h, mem_n, wq, wkv, wo):
    b_, s_, _ = h.shape
    q = (h @ wq).reshape(b_, s_, XA_HEADS, XA_HD)
    k, v = jnp.split(mem_n @ wkv, 2, axis=-1)
    k = k.reshape(b_, -1, XA_HEADS, XA_HD)
    v = v.reshape(b_, -1, XA_HEADS, XA_HD)
    s = jnp.einsum('bshd,bmhd->bhsm', q, k).astype(jnp.float32) * (XA_HD ** -0.5)
    p = jax.nn.softmax(s, axis=-1).astype(v.dtype)
    o = jnp.einsum('bhsm,bmhd->bshd', p, v).reshape(b_, s_, XA_WIDTH)
    return o @ wo


def squared_relu_mlp(h, w1, w2):
    return jnp.square(jax.nn.relu(h @ w1)) @ w2


def setup_inputs(seed: int = 0) -> dict:
    key = jax.random.key(seed)
    ks = list(jax.random.split(key, 32))
    f32 = jnp.float32

    def dense(shape, fan_in, gain=1.0):
        return jax.random.normal(ks.pop(), shape, f32) * (gain * fan_in ** -0.5)

    def norm_gain(shape):
        return 1.0 + 0.05 * jax.random.normal(ks.pop(), shape, f32)

    x = jax.random.normal(ks.pop(), (BATCH, SEQ, D_MODEL), f32)
    mem = jax.random.normal(ks.pop(), (BATCH, MEM_LEN, D_MODEL), f32)
    offs = jax.random.randint(ks.pop(), (BATCH, 1), 0, 1024, dtype=jnp.int32)
    positions = offs + jnp.arange(SEQ, dtype=jnp.int32)[None, :]
    dt = jnp.exp(jax.random.uniform(ks.pop(), (N_HYB, GDN_HEADS), f32, math.log(1e-3), math.log(1e-1)))
    return {
        'x': x,
        'mem': mem,
        'positions': positions,
        'norm_mix': norm_gain((DEPTH, D_MODEL)),
        'norm_xattn': norm_gain((DEPTH, D_MODEL)),
        'norm_mlp': norm_gain((DEPTH, D_MODEL)),
        'hyb_w_in': dense((N_HYB, D_MODEL, IN_COLS), D_MODEL),
        'hyb_cmp_k_pos': 0.02 * jax.random.normal(ks.pop(), (N_HYB, CMP_LEN, NSA_HD), f32),
        'hyb_cmp_k_w1': dense((N_HYB, CMP_LEN * NSA_HD, CMP_HID), CMP_LEN * NSA_HD),
        'hyb_cmp_k_w2': dense((N_HYB, CMP_HID, NSA_HD), CMP_HID),
        'hyb_cmp_v_pos': 0.02 * jax.random.normal(ks.pop(), (N_HYB, CMP_LEN, NSA_HD), f32),
        'hyb_cmp_v_w1': dense((N_HYB, CMP_LEN * NSA_HD, CMP_HID), CMP_LEN * NSA_HD),
        'hyb_cmp_v_w2': dense((N_HYB, CMP_HID, NSA_HD), CMP_HID),
        'hyb_gdn_conv': dense((N_HYB, GDN_CONV, 3 * GDN_WIDTH), GDN_CONV),
        'hyb_gdn_a_log': jnp.log(jax.random.uniform(ks.pop(), (N_HYB, GDN_HEADS), f32, 1.0, 16.0)),
        'hyb_gdn_dt_bias': dt + jnp.log(-jnp.expm1(-dt)),
        'hyb_gdn_norm': norm_gain((N_HYB, GDN_HD)),
        'hyb_w_out': dense((N_HYB, MIX_WIDTH, D_MODEL), MIX_WIDTH, 0.5),
        'sc_w_in': dense((N_SC, D_MODEL, 3 * D_MODEL), D_MODEL),
        'sc_conv': dense((N_SC, SC_WIDTH, D_MODEL), SC_WIDTH),
        'sc_w_out': dense((N_SC, D_MODEL, D_MODEL), D_MODEL, 0.5),
        'mem_norm': norm_gain((D_MODEL,)),
        'xa_wq': dense((DEPTH, D_MODEL, XA_WIDTH), D_MODEL),
        'xa_wkv': dense((DEPTH, D_MODEL, 2 * XA_WIDTH), D_MODEL),
        'xa_wo': dense((DEPTH, XA_WIDTH, D_MODEL), XA_WIDTH, 0.5),
        'mlp_w1': dense((DEPTH, D_MODEL, D_FF), D_MODEL),
        'mlp_w2': dense((DEPTH, D_FF, D_MODEL), D_FF, 0.5),
        'final_norm': norm_gain((D_MODEL,)),
    }


def reference(x, mem, positions, norm_mix, norm_xattn, norm_mlp, hyb_w_in, hyb_cmp_k_pos, hyb_cmp_k_w1, hyb_cmp_k_w2, hyb_cmp_v_pos, hyb_cmp_v_w1, hyb_cmp_v_w2, hyb_gdn_conv, hyb_gdn_a_log, hyb_gdn_dt_bias, hyb_gdn_norm, hyb_w_out, sc_w_in, sc_conv, sc_w_out, mem_norm, xa_wq, xa_wkv, xa_wo, mlp_w1, mlp_w2, final_norm):
    mem_n = rms_norm(mem, mem_norm)
    for layer in range(DEPTH):
        j = layer // 2
        hn = rms_norm(x, norm_mix[layer])
        if layer % 2 == 0:
            mix = hybrid_mixer(hn, positions, hyb_w_in[j], hyb_cmp_k_pos[j], hyb_cmp_k_w1[j], hyb_cmp_k_w2[j], hyb_cmp_v_pos[j], hyb_cmp_v_w1[j], hyb_cmp_v_w2[j], hyb_gdn_conv[j], hyb_gdn_a_log[j], hyb_gdn_dt_bias[j], hyb_gdn_norm[j], hyb_w_out[j])
        else:
            mix = short_conv_mixer(hn, sc_w_in[j], sc_conv[j], sc_w_out[j])
        x = x + mix
        x = x + cross_attention(rms_norm(x, norm_xattn[layer]), mem_n, xa_wq[layer], xa_wkv[layer], xa_wo[layer])
        x = x + squared_relu_mlp(rms_norm(x, norm_mlp[layer]), mlp_w1[layer], mlp_w2[layer])
    return rms_norm(x, final_norm)
```

```python
import functools

import numpy as np
import jax
import jax.numpy as jnp
from jax import lax
from jax.experimental import pallas as pl
from jax.experimental.pallas import tpu as pltpu

F32 = jnp.float32
BF16 = jnp.bfloat16
HI = lax.Precision.HIGHEST

D_MODEL = 1024
MEM_LEN = 256
RMS_EPS = 1e-6
ROPE_THETA = 10000.0
NEG_INF = -1e30
FORCE_BONUS = 1e3

NSA_HEADS = 8
NSA_KV_GROUPS = 2
NSA_HPG = NSA_HEADS // NSA_KV_GROUPS
HD = 64
CMP_LEN = 32
CMP_STRIDE = 16
CMP_HID = 2 * HD
SLC_BLK = 64
SLC_TOPK = 16
WINDOW = 512

GDN_HEADS = 8
GDN_CONV = 4
GDN_CHUNK = 64
SC_WIDTH = 3
XA_HEADS = 4
XA_HD = 128
D_FF = 4 * D_MODEL

NSA_WIDTH = NSA_HEADS * HD
NSA_KV_WIDTH = NSA_KV_GROUPS * HD
GDN_WIDTH = GDN_HEADS * HD
XA_WIDTH = XA_HEADS * XA_HD
IN_SIZES = (NSA_WIDTH,) + (NSA_KV_WIDTH,) * 6 + (3 * NSA_HEADS,) + (GDN_WIDTH,) * 3 + (GDN_HEADS, GDN_HEADS, GDN_WIDTH)

V7X_VMEM_BYTES = 64 * 1024 * 1024
VMEM_LIMIT = V7X_VMEM_BYTES * 3 // 4
LANES = 128

C_Q, C_KR, C_VT, C_CKV, C_GQKV, C_GZ, C_SM, NC0 = 0, 512, 768, 1024, 1280, 2816, 3328, 3456
SM_GATE, SM_A, SM_B = 0, 24, 32

NT_DIMS = (((1,), (1,)), ((), ()))
TN_DIMS = (((0,), (0,)), ((), ()))


def _cparams(sem):
    return pltpu.CompilerParams(dimension_semantics=sem, vmem_limit_bytes=VMEM_LIMIT)


def _rms(x, w):
    return x * lax.rsqrt(jnp.mean(x * x, axis=-1, keepdims=True) + RMS_EPS) * w


def _softplus(x):
    return jnp.maximum(x, 0.0) + jnp.log(1.0 + jnp.exp(-jnp.abs(x)))


def _memkv_kernel(m_ref, nw_ref, w_ref, o_ref):
    hn = _rms(m_ref[...], nw_ref[...]).astype(BF16)
    o_ref[...] = jnp.dot(hn, w_ref[...], preferred_element_type=F32).astype(BF16)


def _memkv(mem2d, mem_norm, wkv_bf):
    rows = mem2d.shape[0]
    depth = wkv_bf.shape[0]
    tm = 512
    return pl.pallas_call(
        _memkv_kernel,
        grid=(depth, rows // tm),
        in_specs=[
            pl.BlockSpec((tm, D_MODEL), lambda l, i: (i, 0)),
            pl.BlockSpec((1, D_MODEL), lambda l, i: (0, 0)),
            pl.BlockSpec((None, D_MODEL, 2 * XA_WIDTH), lambda l, i: (l, 0, 0)),
        ],
        out_specs=pl.BlockSpec((None, tm, 2 * XA_WIDTH), lambda l, i: (l, i, 0)),
        out_shape=jax.ShapeDtypeStruct((depth, rows, 2 * XA_WIDTH), BF16),
        compiler_params=_cparams(("arbitrary", "arbitrary")),
        name="memkv",
    )(mem2d, mem_norm.reshape(1, D_MODEL), wkv_bf)


def _proj0_kernel(x_ref, nw_ref, w_ref, pos_ref, invf_ref,
                  qT_ref, kr_ref, vT_ref, ckv_ref, gqkv_ref, gz_ref, sm_ref, gT_ref):
    tm = x_ref.shape[0]
    hn = _rms(x_ref[...], nw_ref[...]).astype(BF16)
    y = jnp.dot(hn, w_ref[...], preferred_element_type=F32)

    ang = invf_ref[...] * pos_ref[...]
    c = jnp.cos(ang)
    s = jnp.sin(ang)
    cos_n = jnp.concatenate([c, c, c, c], axis=0).T
    sin_n = jnp.concatenate([-s, s, -s, s], axis=0).T
    lane = lax.broadcasted_iota(jnp.int32, (tm, LANES), 1)
    first_half = (lane % HD) < (HD // 2)

    def rope(xc):
        rot = jnp.where(first_half, pltpu.roll(xc, LANES - HD // 2, 1), pltpu.roll(xc, HD // 2, 1))
        return xc * cos_n + rot * sin_n

    roped = [rope(y[:, LANES * i:LANES * (i + 1)]) for i in range(6)]
    q = jnp.concatenate(roped[:4], axis=1) * (HD ** -0.5)
    qT_ref[...] = q.T.astype(BF16)
    kr_ref[...] = jnp.concatenate(roped[4:6], axis=1).astype(BF16)
    vT_ref[...] = y[:, C_VT:C_CKV].T.astype(BF16)
    for i in range(4):
        ckv_ref[i] = y[:, C_CKV + HD * i:C_CKV + HD * (i + 1)].astype(BF16)
    gqkv_ref[...] = y[:, C_GQKV:C_GZ]
    gz_ref[...] = y[:, C_GZ:C_SM]
    sm = y[:, C_SM:NC0]
    sm_ref[...] = sm
    gT_ref[...] = jax.nn.sigmoid(sm).T[:3 * NSA_HEADS, :]


def _prep_w_in(w_in):
    offs = np.cumsum((0,) + IN_SIZES)
    (nq, kcmp, vcmp, kslc, vslc, kwin, vwin, ngate, gq, gk, gv, ga, gb, gz) = [
        w_in[:, offs[i]:offs[i + 1]] for i in range(len(IN_SIZES))]
    pad = jnp.zeros((D_MODEL, LANES - 3 * NSA_HEADS - 2 * GDN_HEADS), w_in.dtype)
    small = jnp.concatenate([ngate, ga, gb, pad], axis=1)
    return jnp.concatenate([nq, kslc, kwin, vslc, vwin, kcmp, vcmp, gq, gk, gv, gz, small], axis=1).astype(BF16)


def _proj0(x2d, norm_w, w_bf, pos3, invf, B, S, tm):
    rows = B * S
    nt = S // tm
    row_spec = lambda n: pl.BlockSpec((tm, n), lambda i: (i, 0))
    t_spec = lambda n: pl.BlockSpec((None, n, tm), lambda i: (i // nt, 0, i % nt))
    return pl.pallas_call(
        _proj0_kernel,
        grid=(rows // tm,),
        in_specs=[
            row_spec(D_MODEL),
            pl.BlockSpec((1, D_MODEL), lambda i: (0, 0)),
            pl.BlockSpec((D_MODEL, NC0), lambda i: (0, 0)),
            pl.BlockSpec((None, 1, tm), lambda i: (i, 0, 0)),
            pl.BlockSpec((HD // 2, 1), lambda i: (0, 0)),
        ],
        out_specs=[
            t_spec(NSA_WIDTH),
            row_spec(2 * NSA_KV_WIDTH),
            t_spec(2 * NSA_KV_WIDTH),
            pl.BlockSpec((4, tm, HD), lambda i: (0, i, 0)),
            row_spec(3 * GDN_WIDTH),
            row_spec(GDN_WIDTH),
            row_spec(LANES),
            t_spec(3 * NSA_HEADS),
        ],
        out_shape=[
            jax.ShapeDtypeStruct((B, NSA_WIDTH, S), BF16),
            jax.ShapeDtypeStruct((rows, 2 * NSA_KV_WIDTH), BF16),
            jax.ShapeDtypeStruct((B, 2 * NSA_KV_WIDTH, S), BF16),
            jax.ShapeDtypeStruct((4, rows, HD), BF16),
            jax.ShapeDtypeStruct((rows, 3 * GDN_WIDTH), F32),
            jax.ShapeDtypeStruct((rows, GDN_WIDTH), F32),
            jax.ShapeDtypeStruct((rows, LANES), F32),
            jax.ShapeDtypeStruct((B, 3 * NSA_HEADS, S), F32),
        ],
        compiler_params=_cparams(("arbitrary",)),
        name="proj0",
    )(x2d, norm_w.reshape(1, D_MODEL), w_bf, pos3, invf)


def _compress_kernel(t_ref, w1_ref, pos_ref, w2_ref, o_ref, oT_ref):
    nseg = t_ref.shape[0]
    w1 = w1_ref[...]
    ab = jnp.dot(t_ref[...], w1, preferred_element_type=F32)
    pc = jnp.dot(pos_ref[...].astype(BF16), w1, preferred_element_type=F32)
    bias = pc[0:1, :CMP_HID] + pc[1:2, CMP_HID:]
    hid = ab[:, :CMP_HID] + pltpu.roll(ab[:, CMP_HID:], nseg - 1, 0) + bias
    act = hid * jax.nn.sigmoid(hid)
    out = jnp.dot(act.astype(BF16), w2_ref[...], preferred_element_type=F32)
    o_ref[...] = out
    oT_ref[...] = out.T[:HD, :]


def _compress(ckv, w1s, pos8, w2s, B, S):
    nseg = S // CMP_STRIDE
    t = ckv.reshape(4, B, nseg, CMP_STRIDE * HD)
    return pl.pallas_call(
        _compress_kernel,
        grid=(4, B),
        in_specs=[
            pl.BlockSpec((None, None, nseg, CMP_STRIDE * HD), lambda k, b: (k, b, 0, 0)),
            pl.BlockSpec((None, CMP_STRIDE * HD, 2 * CMP_HID), lambda k, b: (k // 2, 0, 0)),
            pl.BlockSpec((None, 8, CMP_STRIDE * HD), lambda k, b: (k // 2, 0, 0)),
            pl.BlockSpec((None, CMP_HID, LANES), lambda k, b: (k // 2, 0, 0)),
        ],
        out_specs=[
            pl.BlockSpec((None, None, nseg, LANES), lambda k, b: (k, b, 0, 0)),
            pl.BlockSpec((None, None, HD, nseg), lambda k, b: (k, b, 0, 0)),
        ],
        out_shape=[
            jax.ShapeDtypeStruct((4, B, nseg, LANES), F32),
            jax.ShapeDtypeStruct((4, B, HD, nseg), F32),
        ],
        compiler_params=_cparams(("arbitrary", "arbitrary")),
        name="nsa_compress",
    )(t, w1s, pos8, w2s)


def _prep_compress_weights(k_pos, k_w1, k_w2, v_pos, v_w1, v_w2):
    half = CMP_STRIDE * HD

    def w1cat(w1):
        return jnp.concatenate([w1[:half], w1[half:]], axis=1)

    def pos8(p):
        flat = p.reshape(2, half)
        return jnp.concatenate([flat, jnp.zeros((6, half), p.dtype)], axis=0)

    def w2pad(w2):
        return jnp.concatenate([w2, jnp.zeros((CMP_HID, LANES - HD), w2.dtype)], axis=1)

    w1s = jnp.stack([w1cat(k_w1), w1cat(v_w1)]).astype(BF16)
    p8 = jnp.stack([pos8(k_pos), pos8(v_pos)])
    w2s = jnp.stack([w2pad(k_w2), w2pad(v_w2)]).astype(BF16)
    return w1s, p8, w2s


def _nsa_kernel(qT_ref, kr_ref, vT_ref, kc_ref, vcT_ref, gT_ref, ovT_ref, o_ref,
                imp_scr, sel_scr, *, S, TQ):
    CH = 128
    n_cmp = S // CMP_STRIDE
    n_slc = S // SLC_BLK
    n_sel = min(SLC_TOPK, n_slc)
    qi = pl.program_id(1)
    q0 = qi * TQ
    t1 = q0 + lax.broadcasted_iota(jnp.int32, (1, TQ), 1)
    t4 = jnp.concatenate([t1] * NSA_HPG, axis=1)
    n_io = lax.broadcasted_iota(jnp.int32, (n_cmp, 1), 0)
    j_io = lax.broadcasted_iota(jnp.int32, (n_slc, 1), 0)
    k_io = lax.broadcasted_iota(jnp.int32, (CH, 1), 0)

    for g in range(NSA_KV_GROUPS):
        q64 = jnp.concatenate(
            [qT_ref[NSA_HPG * HD * g + HD * h:NSA_HPG * HD * g + HD * (h + 1), :] for h in range(NSA_HPG)], axis=1)
        zq = jnp.zeros_like(q64)
        qs = jnp.concatenate([q64, zq] if g == 0 else [zq, q64], axis=0)

        kc = kc_ref[g, :, 0:HD].astype(BF16)
        s = jnp.dot(kc, q64, preferred_element_type=F32)
        valid = (n_io * CMP_STRIDE + (CMP_LEN - 1)) <= t4
        s = jnp.where(valid, s, NEG_INF)
        m = jnp.max(s, axis=0, keepdims=True)
        e = jnp.where(valid, jnp.exp(s - m), 0.0)
        l = jnp.sum(e, axis=0, keepdims=True)
        p = e / jnp.where(l > 0.0, l, 1.0)
        o_cmp = jnp.dot(vcT_ref[g].astype(BF16), p.astype(BF16), preferred_element_type=F32)
        psum = p[:, 0:TQ]
        for h in range(1, NSA_HPG):
            psum = psum + p[:, h * TQ:(h + 1) * TQ]
        imp = jnp.dot(ovT_ref[...], psum, precision=HI, preferred_element_type=F32)
        cur = t1 // SLC_BLK
        forced = (j_io == 0) | (j_io == cur) | (j_io == cur - 1)
        causal = j_io <= cur
        val = jnp.where(causal, jnp.where(forced, imp + FORCE_BONUS, imp), -1.0)
        imp_scr[...] = val
        cnt = jnp.zeros((n_slc, TQ), F32)
        for i in range(n_slc):
            vi = imp_scr[i:i + 1, :]
            tie = (j_io > i).astype(F32)
            cnt = cnt + jnp.where(vi > val, 1.0, 0.0) + jnp.where(vi == val, tie, 0.0)
        sel = jnp.where(cnt < float(n_sel), 1.0, 0.0)
        for j in range(n_slc):
            sel_scr[j] = jnp.broadcast_to(sel[j:j + 1, :], (8, TQ))

        def flash(kcol, vrow, lo, hi, mask_fn):
            def body(jc, carry):
                m_i, l_i, acc = carry
                r0 = pl.multiple_of(jc * CH, CH)
                k = kr_ref[pl.ds(r0, CH), kcol:kcol + LANES]
                sc = jnp.dot(k, qs, preferred_element_type=F32)
                okf = mask_fn(jc)
                ok4 = jnp.concatenate([okf] * NSA_HPG, axis=1)
                sc = jnp.where(ok4 > 0.5, sc, NEG_INF)
                m_new = jnp.maximum(m_i, jnp.max(sc, axis=0, keepdims=True))
                alpha = jnp.exp(m_i - m_new)
                pr = jnp.exp(sc - m_new)
                l_new = l_i * alpha + jnp.sum(pr, axis=0, keepdims=True)
                vt = vT_ref[vrow + HD * g:vrow + HD * (g + 1), pl.ds(r0, CH)]
                acc = acc * alpha + jnp.dot(vt, pr.astype(BF16), preferred_element_type=F32)
                return m_new, l_new, acc

            init = (jnp.full((1, NSA_HPG * TQ), NEG_INF, F32), jnp.zeros((1, NSA_HPG * TQ), F32),
                    jnp.zeros((HD, NSA_HPG * TQ), F32))
            _, l_f, acc_f = lax.fori_loop(lo, hi, body, init)
            return acc_f / l_f

        def slc_mask(jc):
            kpos = jc * CH + k_io
            top = jnp.concatenate([sel_scr[2 * jc]] * (SLC_BLK // 8), axis=0)
            bot = jnp.concatenate([sel_scr[2 * jc + 1]] * (SLC_BLK // 8), axis=0)
            selm = jnp.concatenate([top, bot], axis=0)
            return jnp.where(kpos <= t1, selm, 0.0)

        def win_mask(jc):
            kpos = jc * CH + k_io
            return jnp.where((kpos <= t1) & (kpos > t1 - WINDOW), 1.0, 0.0)

        n_chunks = (q0 + TQ) // CH
        o_slc = flash(0, 0, 0, n_chunks, slc_mask)
        o_win = flash(LANES, NSA_KV_WIDTH, jnp.maximum(n_chunks - 1 - WINDOW // CH, 0), n_chunks, win_mask)

        rows = []
        for h in range(NSA_HPG):
            hh = NSA_HPG * g + h
            sl = slice(h * TQ, (h + 1) * TQ)
            rows.append(gT_ref[3 * hh:3 * hh + 1, :] * o_cmp[:, sl]
                        + gT_ref[3 * hh + 1:3 * hh + 2, :] * o_slc[:, sl]
                        + gT_ref[3 * hh + 2:3 * hh + 3, :] * o_win[:, sl])
        og = jnp.concatenate(rows, axis=0)
        o_ref[:, NSA_HPG * HD * g:NSA_HPG * HD * (g + 1)] = og.T


def _overlap_T(S):
    n_cmp = S // CMP_STRIDE
    n_slc = S // SLC_BLK
    cs = np.arange(n_cmp)[:, None] * CMP_STRIDE
    js = np.arange(n_slc)[None, :] * SLC_BLK
    ov = np.clip(np.minimum(cs + CMP_LEN, js + SLC_BLK) - np.maximum(cs, js), 0, None) / CMP_LEN
    ov[n_cmp - 1] = 0.0
    return jnp.asarray(ov.T, dtype=F32)


def _nsa(qT, kr, vT, kc, vcT, gT, B, S, TQ=128):
    n_cmp = S // CMP_STRIDE
    n_slc = S // SLC_BLK
    nq = S // TQ
    return pl.pallas_call(
        functools.partial(_nsa_kernel, S=S, TQ=TQ),
        grid=(B, nq),
        in_specs=[
            pl.BlockSpec((None, NSA_WIDTH, TQ), lambda b, i: (b, 0, i)),
            pl.BlockSpec((None, S, 2 * NSA_KV_WIDTH), lambda b, i: (b, 0, 0)),
            pl.BlockSpec((None, 2 * NSA_KV_WIDTH, S), lambda b, i: (b, 0, 0)),
            pl.BlockSpec((2, None, n_cmp, LANES), lambda b, i: (0, b, 0, 0)),
            pl.BlockSpec((2, None, HD, n_cmp), lambda b, i: (1, b, 0, 0)),
            pl.BlockSpec((None, 3 * NSA_HEADS, TQ), lambda b, i: (b, 0, i)),
            pl.BlockSpec((n_slc, n_cmp), lambda b, i: (0, 0)),
        ],
        out_specs=pl.BlockSpec((TQ, NSA_WIDTH), lambda b, i: (b * nq + i, 0)),
        out_shape=jax.ShapeDtypeStruct((B * S, NSA_WIDTH), F32),
        scratch_shapes=[
            pltpu.VMEM((n_slc, TQ), F32),
            pltpu.VMEM((n_slc, 8, TQ), F32),
        ],
        compiler_params=_cparams(("arbitrary", "arbitrary")),
        name="nsa_attention",
    )(qT, kr.reshape(B, S, 2 * NSA_KV_WIDTH), vT, kc, vcT, gT, _overlap_T(S))


def _gdn_kernel(x_ref, xp_ref, sm_ref, z_ref, cw_ref, alog_ref, dt_ref, nw_ref,
                ebd_ref, eg_ref, eb_ref, lblk_ref, o_ref,
                xe_scr, q_scr, k_scr, v_scr, b_scr, g_scr, st_scr, *, tg):
    C = GDN_CHUNK
    i = pl.program_id(1)

    @pl.when(i == 0)
    def _():
        st_scr[...] = jnp.zeros_like(st_scr)

    xe_scr[0:8, :] = jnp.where(i == 0, 0.0, xp_ref[...])
    xe_scr[8:8 + tg, :] = x_ref[...]
    y = cw_ref[0:1, :] * xe_scr[8 - 3:8 - 3 + tg, :]
    for j in range(1, GDN_CONV):
        y = y + cw_ref[j:j + 1, :] * xe_scr[8 - 3 + j:8 - 3 + j + tg, :]
    qkv = y * jax.nn.sigmoid(y)
    q = qkv[:, 0:GDN_WIDTH]
    k = qkv[:, GDN_WIDTH:2 * GDN_WIDTH]
    ebd = ebd_ref[...]
    qss = jnp.dot((q * q).astype(BF16), ebd, preferred_element_type=F32)
    kss = jnp.dot((k * k).astype(BF16), ebd, preferred_element_type=F32)
    q_scr[...] = q * lax.rsqrt(qss + 1e-6) * (HD ** -0.5)
    k_scr[...] = k * lax.rsqrt(kss + 1e-6)
    v_scr[...] = qkv[:, 2 * GDN_WIDTH:]

    sm = sm_ref[...]
    g128 = -jnp.exp(alog_ref[...]) * _softplus(sm + dt_ref[...])
    gcs128 = jnp.dot(lblk_ref[...], g128, precision=HI, preferred_element_type=F32)
    g_scr[...] = jnp.dot(gcs128, eg_ref[...], precision=HI, preferred_element_type=F32)
    b_scr[...] = jnp.dot(jax.nn.sigmoid(sm), eb_ref[...], precision=HI, preferred_element_type=F32)

    ii = lax.broadcasted_iota(jnp.int32, (C, C), 0)
    jj = lax.broadcasted_iota(jnp.int32, (C, C), 1)
    eye = jnp.where(ii == jj, 1.0, 0.0)
    nw = nw_ref[...]

    def dot(a, b):
        return jnp.dot(a, b, precision=HI, preferred_element_type=F32)

    def dot_nt(a, b):
        return lax.dot_general(a, b, NT_DIMS, precision=HI, preferred_element_type=F32)

    def chunk(c, carry):
        r0 = pl.multiple_of(c * C, C)
        outs = []
        for pr in range(GDN_HEADS // 2):
            cols = slice(LANES * pr, LANES * (pr + 1))
            qp = q_scr[pl.ds(r0, C), cols]
            kp = k_scr[pl.ds(r0, C), cols]
            vp = v_scr[pl.ds(r0, C), cols]
            bp = b_scr[pl.ds(r0, C), cols]
            gp = g_scr[pl.ds(r0, C), cols]
            zp = z_ref[pl.ds(r0, C), cols]
            for hh in range(2):
                h = 2 * pr + hh
                sl = slice(HD * hh, HD * (hh + 1))
                qn, kn, vv, be, gc, zz = qp[:, sl], kp[:, sl], vp[:, sl], bp[:, sl], gp[:, sl], zp[:, sl]
                kb = kn * be
                vb = vv * be
                eg = jnp.exp(gc)
                glast = gc[C - 1:C, :]
                kdec = kn * jnp.exp(glast - gc)
                gct = dot_nt(eye, gc)
                dec = jnp.exp(jnp.where(ii >= jj, gc - gct, NEG_INF))
                a = jnp.where(ii > jj, dot_nt(kb, kn) * dec, 0.0)
                qk = dot_nt(qn, kn) * dec
                x = eye - a
                pw = dot(a, a)
                for _ in range(4):
                    x = x + dot(x, pw)
                    pw = dot(pw, pw)
                x = x + dot(x, pw)
                u = dot(x, vb)
                w = dot(x, kb * eg)
                st = st_scr[h]
                v_new = u - dot(w, st)
                o = dot(qn * eg, st) + dot(qk, v_new)
                st_scr[h] = st * jnp.exp(glast) + lax.dot_general(
                    kdec, v_new, TN_DIMS, precision=HI, preferred_element_type=F32)
                on = o * lax.rsqrt(jnp.mean(o * o, axis=-1, keepdims=True) + RMS_EPS) * nw
                outs.append(on * (zz * jax.nn.sigmoid(zz)))
        o_ref[pl.ds(r0, C), :] = jnp.concatenate(outs, axis=1)
        return carry

    lax.fori_loop(0, tg // C, chunk, 0)


def _gdn(gqkv, sm, gz, conv_w, a_log, dt_bias, norm_w, B, S, tg=256):
    rows = B * S
    nt = S // tg
    head_of = np.arange(GDN_WIDTH) // HD
    ebd = jnp.asarray(head_of[:, None] == head_of[None, :], dtype=BF16)
    eg = np.zeros((LANES, GDN_WIDTH), np.float32)
    eb = np.zeros((LANES, GDN_WIDTH), np.float32)
    eg[SM_A + head_of, np.arange(GDN_WIDTH)] = 1.0
    eb[SM_B + head_of, np.arange(GDN_WIDTH)] = 1.0
    r = np.arange(tg)
    lblk = ((r[:, None] // GDN_CHUNK == r[None, :] // GDN_CHUNK) & (r[None, :] <= r[:, None])).astype(np.float32)
    alog128 = jnp.zeros((1, LANES), F32).at[0, SM_A:SM_A + GDN_HEADS].set(a_log.astype(F32))
    dt128 = jnp.zeros((1, LANES), F32).at[0, SM_A:SM_A + GDN_HEADS].set(dt_bias.astype(F32))
    full = lambda shape: pl.BlockSpec(shape, lambda b, i: (0,) * len(shape))
    return pl.pallas_call(
        functools.partial(_gdn_kernel, tg=tg),
        grid=(B, nt),
        in_specs=[
            pl.BlockSpec((tg, 3 * GDN_WIDTH), lambda b, i: (b * nt + i, 0)),
            pl.BlockSpec((8, 3 * GDN_WIDTH), lambda b, i: (jnp.maximum((b * nt + i) * (tg // 8) - 1, 0), 0)),
            pl.BlockSpec((tg, LANES), lambda b, i: (b * nt + i, 0)),
            pl.BlockSpec((tg, GDN_WIDTH), lambda b, i: (b * nt + i, 0)),
            full((GDN_CONV, 3 * GDN_WIDTH)),
            full((1, LANES)),
            full((1, LANES)),
            full((1, HD)),
            full((GDN_WIDTH, GDN_WIDTH)),
            full((LANES, GDN_WIDTH)),
            full((LANES, GDN_WIDTH)),
            full((tg, tg)),
        ],
        out_specs=pl.BlockSpec((tg, GDN_WIDTH), lambda b, i: (b * nt + i, 0)),
        out_shape=jax.ShapeDtypeStruct((rows, GDN_WIDTH), F32),
        scratch_shapes=[
            pltpu.VMEM((tg + 8, 3 * GDN_WIDTH), F32),
            pltpu.VMEM((tg, GDN_WIDTH), F32),
            pltpu.VMEM((tg, GDN_WIDTH), F32),
            pltpu.VMEM((tg, GDN_WIDTH), F32),
            pltpu.VMEM((tg, GDN_WIDTH), F32),
            pltpu.VMEM((tg, GDN_WIDTH), F32),
            pltpu.VMEM((GDN_HEADS, HD, HD), F32),
        ],
        compiler_params=_cparams(("arbitrary", "arbitrary")),
        name="gdn",
    )(gqkv, gqkv, sm, gz, conv_w, alog128, dt128, norm_w.reshape(1, HD),
      ebd, jnp.asarray(eg), jnp.asarray(eb), jnp.asarray(lblk))


def _outproj_kernel(x_ref, a_ref, b_ref, wa_ref, wb_ref, o_ref):
    acc = jnp.dot(a_ref[...].astype(BF16), wa_ref[...], preferred_element_type=F32)
    acc = acc + jnp.dot(b_ref[...].astype(BF16), wb_ref[...], preferred_element_type=F32)
    o_ref[...] = x_ref[...] + acc


def _outproj(x2d, o_nsa, o_gdn, w_out_bf, tm=512):
    rows = x2d.shape[0]
    return pl.pallas_call(
        _outproj_kernel,
        grid=(rows // tm,),
        in_specs=[
            pl.BlockSpec((tm, D_MODEL), lambda i: (i, 0)),
            pl.BlockSpec((tm, NSA_WIDTH), lambda i: (i, 0)),
            pl.BlockSpec((tm, GDN_WIDTH), lambda i: (i, 0)),
            pl.BlockSpec((NSA_WIDTH, D_MODEL), lambda i: (0, 0)),
            pl.BlockSpec((GDN_WIDTH, D_MODEL), lambda i: (1, 0)),
        ],
        out_specs=pl.BlockSpec((tm, D_MODEL), lambda i: (i, 0)),
        out_shape=jax.ShapeDtypeStruct((rows, D_MODEL), F32),
        compiler_params=_cparams(("arbitrary",)),
        name="mix_outproj",
    )(x2d, o_nsa, o_gdn, w_out_bf, w_out_bf)


def _xattn_kernel(x_ref, nw_ref, wq_ref, kv_ref, wo_ref, o_ref):
    x = x_ref[...]
    hn = _rms(x, nw_ref[...]).astype(BF16)
    q = jnp.dot(hn, wq_ref[...], preferred_element_type=F32) * (XA_HD ** -0.5)
    outs = []
    for h in range(XA_HEADS):
        qh = q[:, XA_HD * h:XA_HD * (h + 1)].astype(BF16)
        kh = kv_ref[:, XA_HD * h:XA_HD * (h + 1)]
        vh = kv_ref[:, XA_WIDTH + XA_HD * h:XA_WIDTH + XA_HD * (h + 1)]
        s = lax.dot_general(qh, kh, NT_DIMS, preferred_element_type=F32)
        e = jnp.exp(s - jnp.max(s, axis=-1, keepdims=True))
        p = e / jnp.sum(e, axis=-1, keepdims=True)
        outs.append(jnp.dot(p.astype(BF16), vh, preferred_element_type=F32))
    o = jnp.concatenate(outs, axis=1).astype(BF16)
    o_ref[...] = x + jnp.dot(o, wo_ref[...], preferred_element_type=F32)


def _xattn(x2d, norm_w, wq_bf, memkv_l, wo_bf, S, tm=512):
    rows = x2d.shape[0]
    per_b = S // tm
    mem_len = memkv_l.shape[0] // (rows // S)
    return pl.pallas_call(
        _xattn_kernel,
        grid=(rows // tm,),
        in_specs=[
            pl.BlockSpec((tm, D_MODEL), lambda i: (i, 0)),
            pl.BlockSpec((1, D_MODEL), lambda i: (0, 0)),
            pl.BlockSpec((D_MODEL, XA_WIDTH), lambda i: (0, 0)),
            pl.BlockSpec((mem_len, 2 * XA_WIDTH), lambda i: (i // per_b, 0)),
            pl.BlockSpec((XA_WIDTH, D_MODEL), lambda i: (0, 0)),
        ],
        out_specs=pl.BlockSpec((tm, D_MODEL), lambda i: (i, 0)),
        out_shape=jax.ShapeDtypeStruct((rows, D_MODEL), F32),
        compiler_params=_cparams(("arbitrary",)),
        name="cross_attention",
    )(x2d, norm_w.reshape(1, D_MODEL), wq_bf, memkv_l, wo_bf)


def _mlp_kernel(x_ref, nw_ref, w1_ref, w2_ref, fw_ref, o_ref, hn_scr, acc_scr, *, final_norm):
    j = pl.program_id(1)

    @pl.when(j == 0)
    def _():
        x = x_ref[...]
        hn_scr[...] = _rms(x, nw_ref[...]).astype(BF16)
        acc_scr[...] = x

    h = jnp.dot(hn_scr[...], w1_ref[...], preferred_element_type=F32)
    h = jnp.square(jnp.maximum(h, 0.0)).astype(BF16)
    acc_scr[...] += jnp.dot(h, w2_ref[...], preferred_element_type=F32)

    @pl.when(j == pl.num_programs(1) - 1)
    def _():
        y = acc_scr[...]
        if final_norm:
            y = _rms(y, fw_ref[...])
        o_ref[...] = y


def _mlp(x2d, norm_w, w1_bf, w2_bf, final_w, final_norm, tm=1024, tf=1024):
    rows = x2d.shape[0]
    tm = min(tm, rows)
    return pl.pallas_call(
        functools.partial(_mlp_kernel, final_norm=final_norm),
        grid=(rows // tm, D_FF // tf),
        in_specs=[
            pl.BlockSpec((tm, D_MODEL), lambda i, j: (i, 0)),
            pl.BlockSpec((1, D_MODEL), lambda i, j: (0, 0)),
            pl.BlockSpec((D_MODEL, tf), lambda i, j: (0, j)),
            pl.BlockSpec((tf, D_MODEL), lambda i, j: (j, 0)),
            pl.BlockSpec((1, D_MODEL), lambda i, j: (0, 0)),
        ],
        out_specs=pl.BlockSpec((tm, D_MODEL), lambda i, j: (i, 0)),
        out_shape=jax.ShapeDtypeStruct((rows, D_MODEL), F32),
        scratch_shapes=[pltpu.VMEM((tm, D_MODEL), BF16), pltpu.VMEM((tm, D_MODEL), F32)],
        compiler_params=_cparams(("arbitrary", "arbitrary")),
        name="mlp",
    )(x2d, norm_w.reshape(1, D_MODEL), w1_bf, w2_bf, final_w.reshape(1, D_MODEL))


def _sc_kernel(x_ref, xp_ref, nw_ref, win_ref, cw_ref, wout_ref, o_ref, cu_scr, *, tm, per_b):
    i = pl.program_id(0)
    x = x_ref[...]
    nw = nw_ref[...]
    y = jnp.dot(_rms(x, nw).astype(BF16), win_ref[...], preferred_element_type=F32)
    yp = jnp.dot(_rms(xp_ref[...], nw).astype(BF16), win_ref[:, D_MODEL:], preferred_element_type=F32)
    cu_prev = yp[:, :D_MODEL] * yp[:, D_MODEL:]
    cu_scr[0:8, :] = jnp.where(i % per_b == 0, 0.0, cu_prev)
    cu_scr[8:8 + tm, :] = y[:, D_MODEL:2 * D_MODEL] * y[:, 2 * D_MODEL:]
    conv = cw_ref[0:1, :] * cu_scr[8 - 2:8 - 2 + tm, :]
    for j in range(1, SC_WIDTH):
        conv = conv + cw_ref[j:j + 1, :] * cu_scr[8 - 2 + j:8 - 2 + j + tm, :]
    gated = (y[:, :D_MODEL] * conv).astype(BF16)
    o_ref[...] = x + jnp.dot(gated, wout_ref[...], preferred_element_type=F32)


def _sc_mixer(x2d, norm_w, win_bf, conv_w, wout_bf, S, tm=512):
    rows = x2d.shape[0]
    per_b = S // tm
    return pl.pallas_call(
        functools.partial(_sc_kernel, tm=tm, per_b=per_b),
        grid=(rows // tm,),
        in_specs=[
            pl.BlockSpec((tm, D_MODEL), lambda i: (i, 0)),
            pl.BlockSpec((8, D_MODEL), lambda i: (jnp.maximum(i * (tm // 8) - 1, 0), 0)),
            pl.BlockSpec((1, D_MODEL), lambda i: (0, 0)),
            pl.BlockSpec((D_MODEL, 3 * D_MODEL), lambda i: (0, 0)),
            pl.BlockSpec((SC_WIDTH, D_MODEL), lambda i: (0, 0)),
            pl.BlockSpec((D_MODEL, D_MODEL), lambda i: (0, 0)),
        ],
        out_specs=pl.BlockSpec((tm, D_MODEL), lambda i: (i, 0)),
        out_shape=jax.ShapeDtypeStruct((rows, D_MODEL), F32),
        scratch_shapes=[pltpu.VMEM((tm + 8, D_MODEL), F32)],
        compiler_params=_cparams(("arbitrary",)),
        name="short_conv_mixer",
    )(x2d, x2d, norm_w.reshape(1, D_MODEL), win_bf, conv_w, wout_bf)


def _hybrid_mixer(x2d, positions, norm_w, w_in, ck_pos, ck_w1, ck_w2, cv_pos, cv_w1, cv_w2,
                  gdn_conv, gdn_a_log, gdn_dt_bias, gdn_norm, w_out, B, S):
    tm = 256
    invf = (ROPE_THETA ** (-jnp.arange(0, HD, 2, dtype=F32) / HD)).reshape(HD // 2, 1)
    pos3 = positions.astype(F32).reshape(B * S // tm, 1, tm)
    qT, kr, vT, ckv, gqkv, gz, sm, gT = _proj0(x2d, norm_w, _prep_w_in(w_in), pos3, invf, B, S, tm)
    w1s, p8, w2s = _prep_compress_weights(ck_pos, ck_w1, ck_w2, cv_pos, cv_w1, cv_w2)
    kc, vcT = _compress(ckv, w1s, p8, w2s, B, S)
    o_nsa = _nsa(qT, kr, vT, kc, vcT, gT, B, S)
    o_gdn = _gdn(gqkv, sm, gz, gdn_conv, gdn_a_log, gdn_dt_bias, gdn_norm, B, S)
    return _outproj(x2d, o_nsa, o_gdn, w_out.astype(BF16))


def kernel(x, mem, positions, norm_mix, norm_xattn, norm_mlp, hyb_w_in, hyb_cmp_k_pos, hyb_cmp_k_w1, hyb_cmp_k_w2, hyb_cmp_v_pos, hyb_cmp_v_w1, hyb_cmp_v_w2, hyb_gdn_conv, hyb_gdn_a_log, hyb_gdn_dt_bias, hyb_gdn_norm, hyb_w_out, sc_w_in, sc_conv, sc_w_out, mem_norm, xa_wq, xa_wkv, xa_wo, mlp_w1, mlp_w2, final_norm):
    B, S, _ = x.shape
    depth = norm_mix.shape[0]
    x2d = x.reshape(B * S, D_MODEL)
    memkv = _memkv(mem.reshape(-1, D_MODEL), mem_norm, xa_wkv.astype(BF16))
    for layer in range(depth):
        j = layer // 2
        if layer % 2 == 0:
            x2d = _hybrid_mixer(x2d, positions, norm_mix[layer], hyb_w_in[j], hyb_cmp_k_pos[j], hyb_cmp_k_w1[j],
                                hyb_cmp_k_w2[j], hyb_cmp_v_pos[j], hyb_cmp_v_w1[j], hyb_cmp_v_w2[j],
                                hyb_gdn_conv[j], hyb_gdn_a_log[j], hyb_gdn_dt_bias[j], hyb_gdn_norm[j],
                                hyb_w_out[j], B, S)
        else:
            x2d = _sc_mixer(x2d, norm_mix[layer], sc_w_in[j].astype(BF16), sc_conv[j], sc_w_out[j].astype(BF16), S)
        x2d = _xattn(x2d, norm_xattn[layer], xa_wq[layer].astype(BF16), memkv[layer], xa_wo[layer].astype(BF16), S)
        x2d = _mlp(x2d, norm_mlp[layer], mlp_w1[layer].astype(BF16), mlp_w2[layer].astype(BF16),
                   final_norm, layer == depth - 1)
    return x2d.reshape(B, S, D_MODEL)
```

```python
import functools

import numpy as np
import jax
import jax.numpy as jnp
from jax import lax
from jax.experimental import pallas as pl
from jax.experimental.pallas import tpu as pltpu

F32 = jnp.float32
BF16 = jnp.bfloat16
HI = lax.Precision.HIGHEST

D_MODEL = 1024
MEM_LEN = 256
RMS_EPS = 1e-6
ROPE_THETA = 10000.0
NEG_INF = -1e30
FORCE_BONUS = 1e3
LOG2E = 1.4426950408889634

NSA_HEADS = 8
NSA_KV_GROUPS = 2
NSA_HPG = NSA_HEADS // NSA_KV_GROUPS
HD = 64
CMP_LEN = 32
CMP_STRIDE = 16
CMP_HID = 2 * HD
SLC_BLK = 64
SLC_TOPK = 16
WINDOW = 512

GDN_HEADS = 8
GDN_CONV = 4
GDN_CHUNK = 64
SC_WIDTH = 3
XA_HEADS = 4
XA_HD = 128
D_FF = 4 * D_MODEL

NSA_WIDTH = NSA_HEADS * HD
NSA_KV_WIDTH = NSA_KV_GROUPS * HD
GDN_WIDTH = GDN_HEADS * HD
XA_WIDTH = XA_HEADS * XA_HD
IN_SIZES = (NSA_WIDTH,) + (NSA_KV_WIDTH,) * 6 + (3 * NSA_HEADS,) + (GDN_WIDTH,) * 3 + (GDN_HEADS, GDN_HEADS, GDN_WIDTH)

V7X_VMEM_BYTES = 64 * 1024 * 1024
VMEM_LIMIT = V7X_VMEM_BYTES * 3 // 4
LANES = 128

C_Q, C_KR, C_VT, C_CKV, C_GQKV, C_GZ, C_SM, NC0 = 0, 512, 768, 1024, 1280, 2816, 3328, 3456
SM_GATE, SM_A, SM_B = 0, 24, 32

NT_DIMS = (((1,), (1,)), ((), ()))
TN_DIMS = (((0,), (0,)), ((), ()))


def _cparams(sem):
    return pltpu.CompilerParams(dimension_semantics=sem, vmem_limit_bytes=VMEM_LIMIT)


def _rms(x, w):
    return x * lax.rsqrt(jnp.mean(x * x, axis=-1, keepdims=True) + RMS_EPS) * w


def _softplus(x):
    return jnp.maximum(x, 0.0) + jnp.log(1.0 + jnp.exp(-jnp.abs(x)))


def _memkv_kernel(m_ref, nw_ref, w_ref, o_ref):
    hn = _rms(m_ref[...], nw_ref[...]).astype(BF16)
    o_ref[...] = jnp.dot(hn, w_ref[...], preferred_element_type=F32).astype(BF16)


def _memkv(mem2d, mem_norm, wkv_bf):
    rows = mem2d.shape[0]
    depth = wkv_bf.shape[0]
    tm = 512
    return pl.pallas_call(
        _memkv_kernel,
        grid=(depth, rows // tm),
        in_specs=[
            pl.BlockSpec((tm, D_MODEL), lambda l, i: (i, 0)),
            pl.BlockSpec((1, D_MODEL), lambda l, i: (0, 0)),
            pl.BlockSpec((None, D_MODEL, 2 * XA_WIDTH), lambda l, i: (l, 0, 0)),
        ],
        out_specs=pl.BlockSpec((None, tm, 2 * XA_WIDTH), lambda l, i: (l, i, 0)),
        out_shape=jax.ShapeDtypeStruct((depth, rows, 2 * XA_WIDTH), BF16),
        compiler_params=_cparams(("arbitrary", "arbitrary")),
        name="memkv",
    )(mem2d, mem_norm.reshape(1, D_MODEL), wkv_bf)


def _proj0_kernel(x_ref, nw_ref, w_ref, pos_ref, invf_ref,
                  qT_ref, kr_ref, vT_ref, ckv_ref, gqkv_ref, gz_ref, sm_ref, gT_ref, aT_ref):
    tm = x_ref.shape[0]
    hn = _rms(x_ref[...], nw_ref[...]).astype(BF16)
    y = jnp.dot(hn, w_ref[...], preferred_element_type=F32)

    ang = invf_ref[...] * pos_ref[...]
    c = jnp.cos(ang)
    s = jnp.sin(ang)
    cos_n = jnp.concatenate([c, c, c, c], axis=0).T
    sin_n = jnp.concatenate([-s, s, -s, s], axis=0).T
    lane = lax.broadcasted_iota(jnp.int32, (tm, LANES), 1)
    first_half = (lane % HD) < (HD // 2)

    def rope(xc):
        rot = jnp.where(first_half, pltpu.roll(xc, LANES - HD // 2, 1), pltpu.roll(xc, HD // 2, 1))
        return xc * cos_n + rot * sin_n

    roped = [rope(y[:, LANES * i:LANES * (i + 1)]) for i in range(6)]
    q = jnp.concatenate(roped[:4], axis=1) * (LOG2E * HD ** -0.5)
    qT_ref[...] = q.T.astype(BF16)
    kr_ref[...] = jnp.concatenate(roped[4:6], axis=1).astype(BF16)
    vT_ref[...] = y[:, C_VT:C_CKV].T.astype(BF16)
    for i in range(4):
        ckv_ref[i] = y[:, C_CKV + HD * i:C_CKV + HD * (i + 1)].astype(BF16)
    gqkv_ref[...] = y[:, C_GQKV:C_GZ]
    gz_ref[...] = y[:, C_GZ:C_SM]
    sm = y[:, C_SM:NC0]
    sm_ref[...] = sm
    gT_ref[...] = jax.nn.sigmoid(sm).T[:3 * NSA_HEADS, :]
    aT_ref[...] = sm.T[SM_A:SM_A + GDN_HEADS, :]


def _prep_w_in(w_in):
    offs = np.cumsum((0,) + IN_SIZES)
    (nq, kcmp, vcmp, kslc, vslc, kwin, vwin, ngate, gq, gk, gv, ga, gb, gz) = [
        w_in[:, offs[i]:offs[i + 1]] for i in range(len(IN_SIZES))]
    pad = jnp.zeros((D_MODEL, LANES - 3 * NSA_HEADS - 2 * GDN_HEADS), w_in.dtype)
    small = jnp.concatenate([ngate, ga, gb, pad], axis=1)
    return jnp.concatenate([nq, kslc, kwin, vslc, vwin, kcmp, vcmp, gq, gk, gv, gz, small], axis=1).astype(BF16)


def _proj0(x2d, norm_w, w_bf, pos3, invf, B, S, tm):
    rows = B * S
    nt = S // tm
    row_spec = lambda n: pl.BlockSpec((tm, n), lambda i: (i, 0))
    t_spec = lambda n: pl.BlockSpec((None, n, tm), lambda i: (i // nt, 0, i % nt))
    return pl.pallas_call(
        _proj0_kernel,
        grid=(rows // tm,),
        in_specs=[
            row_spec(D_MODEL),
            pl.BlockSpec((1, D_MODEL), lambda i: (0, 0)),
            pl.BlockSpec((D_MODEL, NC0), lambda i: (0, 0)),
            pl.BlockSpec((None, 1, tm), lambda i: (i, 0, 0)),
            pl.BlockSpec((HD // 2, 1), lambda i: (0, 0)),
        ],
        out_specs=[
            t_spec(NSA_WIDTH),
            row_spec(2 * NSA_KV_WIDTH),
            t_spec(2 * NSA_KV_WIDTH),
            pl.BlockSpec((4, tm, HD), lambda i: (0, i, 0)),
            row_spec(3 * GDN_WIDTH),
            row_spec(GDN_WIDTH),
            row_spec(LANES),
            t_spec(3 * NSA_HEADS),
            t_spec(GDN_HEADS),
        ],
        out_shape=[
            jax.ShapeDtypeStruct((B, NSA_WIDTH, S), BF16),
            jax.ShapeDtypeStruct((rows, 2 * NSA_KV_WIDTH), BF16),
            jax.ShapeDtypeStruct((B, 2 * NSA_KV_WIDTH, S), BF16),
            jax.ShapeDtypeStruct((4, rows, HD), BF16),
            jax.ShapeDtypeStruct((rows, 3 * GDN_WIDTH), F32),
            jax.ShapeDtypeStruct((rows, GDN_WIDTH), F32),
            jax.ShapeDtypeStruct((rows, LANES), F32),
            jax.ShapeDtypeStruct((B, 3 * NSA_HEADS, S), F32),
            jax.ShapeDtypeStruct((B, GDN_HEADS, S), F32),
        ],
        compiler_params=_cparams(("arbitrary",)),
        name="proj0",
    )(x2d, norm_w.reshape(1, D_MODEL), w_bf, pos3, invf)


def _compress_kernel(t_ref, w1_ref, pos_ref, w2_ref, o_ref, oT_ref):
    nseg = t_ref.shape[0]
    w1 = w1_ref[...]
    ab = jnp.dot(t_ref[...], w1, preferred_element_type=F32)
    pc = jnp.dot(pos_ref[...].astype(BF16), w1, preferred_element_type=F32)
    bias = pc[0:1, :CMP_HID] + pc[1:2, CMP_HID:]
    hid = ab[:, :CMP_HID] + pltpu.roll(ab[:, CMP_HID:], nseg - 1, 0) + bias
    act = hid * jax.nn.sigmoid(hid)
    out = jnp.dot(act.astype(BF16), w2_ref[...], preferred_element_type=F32)
    o_ref[...] = out
    oT_ref[...] = out.T[:HD, :]


def _compress(ckv, w1s, pos8, w2s, B, S):
    nseg = S // CMP_STRIDE
    t = ckv.reshape(4, B, nseg, CMP_STRIDE * HD)
    return pl.pallas_call(
        _compress_kernel,
        grid=(4, B),
        in_specs=[
            pl.BlockSpec((None, None, nseg, CMP_STRIDE * HD), lambda k, b: (k, b, 0, 0)),
            pl.BlockSpec((None, CMP_STRIDE * HD, 2 * CMP_HID), lambda k, b: (k // 2, 0, 0)),
            pl.BlockSpec((None, 8, CMP_STRIDE * HD), lambda k, b: (k // 2, 0, 0)),
            pl.BlockSpec((None, CMP_HID, LANES), lambda k, b: (k // 2, 0, 0)),
        ],
        out_specs=[
            pl.BlockSpec((None, None, nseg, LANES), lambda k, b: (k, b, 0, 0)),
            pl.BlockSpec((None, None, HD, nseg), lambda k, b: (k, b, 0, 0)),
        ],
        out_shape=[
            jax.ShapeDtypeStruct((4, B, nseg, LANES), F32),
            jax.ShapeDtypeStruct((4, B, HD, nseg), F32),
        ],
        compiler_params=_cparams(("arbitrary", "arbitrary")),
        name="nsa_compress",
    )(t, w1s, pos8, w2s)


def _prep_compress_weights(k_pos, k_w1, k_w2, v_pos, v_w1, v_w2):
    half = CMP_STRIDE * HD

    def w1cat(w1):
        return jnp.concatenate([w1[:half], w1[half:]], axis=1)

    def pos8(p):
        flat = p.reshape(2, half)
        return jnp.concatenate([flat, jnp.zeros((6, half), p.dtype)], axis=0)

    def w2pad(w2):
        return jnp.concatenate([w2, jnp.zeros((CMP_HID, LANES - HD), w2.dtype)], axis=1)

    w1s = jnp.stack([w1cat(k_w1), w1cat(v_w1)]).astype(BF16)
    p8 = jnp.stack([pos8(k_pos), pos8(v_pos)])
    w2s = jnp.stack([w2pad(k_w2), w2pad(v_w2)]).astype(BF16)
    return w1s, p8, w2s


def _nsa_kernel(qT_ref, kr_ref, vT_ref, kc_ref, vcT_ref, gT_ref, ovT_ref, o_ref,
                imp_scr, sel_scr, s_scr, p_scr, b_scr, *, S, TQ):
    CH = TQ
    groups = range(NSA_KV_GROUPS)
    n_cmp = S // CMP_STRIDE
    n_slc = S // SLC_BLK
    n_sel = min(SLC_TOPK, n_slc)
    qi = pl.program_id(1)
    q0 = qi * TQ
    t1 = q0 + lax.broadcasted_iota(jnp.int32, (1, TQ), 1)
    t4 = jnp.concatenate([t1] * NSA_HPG, axis=1)
    n_io = lax.broadcasted_iota(jnp.int32, (n_cmp, 1), 0)
    j_io = lax.broadcasted_iota(jnp.int32, (n_slc, 1), 0)
    k_io = lax.broadcasted_iota(jnp.int32, (CH, 1), 0)
    n_causal = (q0 + TQ) // SLC_BLK

    qs, o_cmp = [], []
    for g in groups:
        q64 = jnp.concatenate(
            [qT_ref[NSA_HPG * HD * g + HD * h:NSA_HPG * HD * g + HD * (h + 1), :] for h in range(NSA_HPG)], axis=1)
        zq = jnp.zeros_like(q64)
        qs.append(jnp.concatenate([q64, zq] if g == 0 else [zq, q64], axis=0))

        kc = kc_ref[g, :, 0:HD].astype(BF16)
        s = jnp.dot(kc, q64, preferred_element_type=F32)
        valid = (n_io * CMP_STRIDE + (CMP_LEN - 1)) <= t4
        s = jnp.where(valid, s, NEG_INF)
        m = jnp.max(s, axis=0, keepdims=True)
        e = jnp.where(valid, jnp.exp2(s - m), 0.0)
        l = jnp.sum(e, axis=0, keepdims=True)
        p = e / jnp.where(l > 0.0, l, 1.0)
        o_cmp.append(jnp.dot(vcT_ref[g].astype(BF16), p.astype(BF16), preferred_element_type=F32))
        psum = p[:, 0:TQ]
        for h in range(1, NSA_HPG):
            psum = psum + p[:, h * TQ:(h + 1) * TQ]
        imp = jnp.dot(ovT_ref[...], psum, precision=HI, preferred_element_type=F32)
        cur = t1 // SLC_BLK
        forced = (j_io == 0) | (j_io == cur) | (j_io == cur - 1)
        causal = j_io <= cur
        val = jnp.where(causal, jnp.where(forced, imp + FORCE_BONUS, imp), -1.0)
        imp_scr[...] = val

        def rank_body(i, cnt, val=val):
            vi = imp_scr[pl.ds(i, 1), :]
            tie = jnp.where(j_io > i, 1.0, 0.0)
            return cnt + jnp.where(vi > val, 1.0, 0.0) + jnp.where(vi == val, tie, 0.0)

        cnt = lax.fori_loop(0, jnp.where(n_causal > n_sel, n_causal, 0), rank_body, jnp.zeros((n_slc, TQ), F32))
        selb = jnp.where(cnt < float(n_sel), 0.0, NEG_INF)
        for j in range(n_slc):
            sel_scr[g, j] = jnp.broadcast_to(selb[j:j + 1, :], (8, TQ))

    n_lanes = NSA_HPG * TQ
    ones_rows = jnp.ones((16, CH), BF16)

    def scores(jc, kcol):
        r0 = pl.multiple_of(jc * CH, CH)
        k = kr_ref[pl.ds(r0, CH), kcol:kcol + LANES]
        return [jnp.dot(k, qs[g], preferred_element_type=F32) for g in groups]

    def values(jc, vrow):
        r0 = pl.multiple_of(jc * CH, CH)
        return [jnp.concatenate([vT_ref[vrow + HD * g:vrow + HD * (g + 1), pl.ds(r0, CH)], ones_rows], axis=0)
                for g in groups]

    def probs(sc, bias, m_i):
        if bias is not None:
            sc = sc + jnp.concatenate([bias] * NSA_HPG, axis=1)
        m_new = jnp.maximum(m_i, jnp.max(sc, axis=0, keepdims=True))
        return m_new, jnp.exp2(m_i - m_new), jnp.exp2(sc - m_new).astype(BF16)

    def pv(vt, pr):
        return jnp.dot(vt, pr, preferred_element_type=F32)

    def step(jc, carry, kcol, vrow, biases):
        scs = scores(jc, kcol)
        vts = values(jc, vrow)
        parts = [probs(scs[g], None if biases is None else biases[g], carry[g][0]) for g in groups]
        return tuple((parts[g][0], carry[g][1] * parts[g][1] + pv(vts[g], parts[g][2])) for g in groups)

    def sel_bias(jc):
        nb = CH // SLC_BLK
        return [jnp.concatenate([sel_scr[g, nb * jc + b] for b in range(nb) for _ in range(SLC_BLK // 8)], axis=0)
                for g in groups]

    def m_init():
        return jnp.full((1, n_lanes), NEG_INF, F32)

    def acc_init():
        return jnp.zeros((HD + 16, n_lanes), F32)

    u_io = lax.broadcasted_iota(jnp.int32, (1, TQ), 1)
    causal_bias = jnp.where(k_io <= u_io, 0.0, NEG_INF)
    window_bias = jnp.where(k_io > u_io, 0.0, NEG_INF)

    def slc_half(jc, slot, ms):
        nxt = scores(jc + 1, 0)
        v_prev = values(jnp.maximum(jc - 1, 0), 0)
        pvs = [pv(v_prev[g], p_scr[g]) for g in groups]
        for g in groups:
            s_scr[1 - slot, g] = nxt[g]
        bias = sel_bias(jc)
        out = []
        for g in groups:
            m_new, alpha, pr = probs(s_scr[slot, g], bias[g], ms[g])
            b_scr[g] = (b_scr[g] + pvs[g]) * alpha
            p_scr[g] = pr
            out.append(m_new)
        return tuple(out)

    first = scores(0, 0)
    for g in groups:
        s_scr[0, g] = first[g]
        p_scr[g] = jnp.zeros((CH, n_lanes), BF16)
        b_scr[g] = acc_init()
    ms = lax.fori_loop(0, qi // 2, lambda i, m: slc_half(2 * i + 1, 1, slc_half(2 * i, 0, m)),
                       tuple(m_init() for _ in groups))

    def slc_odd(_, m):
        m = slc_half(qi - 1, 0, m)
        s_scr[0] = s_scr[1]
        return m

    ms = lax.fori_loop(0, qi % 2, slc_odd, ms)

    n_back = WINDOW // CH
    wcar = lax.fori_loop(0, jnp.where(qi >= n_back, 1, 0),
                         lambda _, c: step(qi - n_back, c, LANES, NSA_KV_WIDTH, [window_bias] * len(groups)),
                         tuple((m_init(), acc_init()) for _ in groups))
    wcar = lax.fori_loop(jnp.maximum(qi - n_back + 1, 0), qi,
                         lambda jc, c: step(jc, c, LANES, NSA_KV_WIDTH, None), wcar)

    v_prev = values(jnp.maximum(qi - 1, 0), 0)
    pvs = [pv(v_prev[g], p_scr[g]) for g in groups]
    s_win = scores(qi, LANES)
    bias_d = sel_bias(qi)
    slc_parts = [probs(s_scr[0, g], bias_d[g] + causal_bias, ms[g]) for g in groups]
    win_parts = [probs(s_win[g], causal_bias, wcar[g][0]) for g in groups]
    v_slc = values(qi, 0)
    v_win = values(qi, NSA_KV_WIDTH)
    o_slc, o_win = [], []
    for g in groups:
        acc = (b_scr[g] + pvs[g]) * slc_parts[g][1] + pv(v_slc[g], slc_parts[g][2])
        o_slc.append(acc[:HD] / acc[HD:HD + 1])
        acc = wcar[g][1] * win_parts[g][1] + pv(v_win[g], win_parts[g][2])
        o_win.append(acc[:HD] / acc[HD:HD + 1])

    for g in groups:
        rows = []
        for h in range(NSA_HPG):
            hh = NSA_HPG * g + h
            sl = slice(h * TQ, (h + 1) * TQ)
            rows.append(gT_ref[3 * hh:3 * hh + 1, :] * o_cmp[g][:, sl]
                        + gT_ref[3 * hh + 1:3 * hh + 2, :] * o_slc[g][:, sl]
                        + gT_ref[3 * hh + 2:3 * hh + 3, :] * o_win[g][:, sl])
        og = jnp.concatenate(rows, axis=0)
        o_ref[:, NSA_HPG * HD * g:NSA_HPG * HD * (g + 1)] = og.T


def _overlap_T(S):
    n_cmp = S // CMP_STRIDE
    n_slc = S // SLC_BLK
    cs = np.arange(n_cmp)[:, None] * CMP_STRIDE
    js = np.arange(n_slc)[None, :] * SLC_BLK
    ov = np.clip(np.minimum(cs + CMP_LEN, js + SLC_BLK) - np.maximum(cs, js), 0, None) / CMP_LEN
    ov[n_cmp - 1] = 0.0
    return jnp.asarray(ov.T, dtype=F32)


def _nsa(qT, kr, vT, kc, vcT, gT, B, S, TQ=256):
    n_cmp = S // CMP_STRIDE
    n_slc = S // SLC_BLK
    nq = S // TQ
    return pl.pallas_call(
        functools.partial(_nsa_kernel, S=S, TQ=TQ),
        grid=(B, nq),
        in_specs=[
            pl.BlockSpec((None, NSA_WIDTH, TQ), lambda b, i: (b, 0, i)),
            pl.BlockSpec((None, S, 2 * NSA_KV_WIDTH), lambda b, i: (b, 0, 0)),
            pl.BlockSpec((None, 2 * NSA_KV_WIDTH, S), lambda b, i: (b, 0, 0)),
            pl.BlockSpec((2, None, n_cmp, LANES), lambda b, i: (0, b, 0, 0)),
            pl.BlockSpec((2, None, HD, n_cmp), lambda b, i: (1, b, 0, 0)),
            pl.BlockSpec((None, 3 * NSA_HEADS, TQ), lambda b, i: (b, 0, i)),
            pl.BlockSpec((n_slc, n_cmp), lambda b, i: (0, 0)),
        ],
        out_specs=pl.BlockSpec((TQ, NSA_WIDTH), lambda b, i: (b * nq + i, 0)),
        out_shape=jax.ShapeDtypeStruct((B * S, NSA_WIDTH), F32),
        scratch_shapes=[
            pltpu.VMEM((n_slc, TQ), F32),
            pltpu.VMEM((NSA_KV_GROUPS, n_slc, 8, TQ), F32),
            pltpu.VMEM((2, NSA_KV_GROUPS, TQ, NSA_HPG * TQ), F32),
            pltpu.VMEM((NSA_KV_GROUPS, TQ, NSA_HPG * TQ), BF16),
            pltpu.VMEM((NSA_KV_GROUPS, HD + 16, NSA_HPG * TQ), F32),
        ],
        compiler_params=_cparams(("arbitrary", "arbitrary")),
        name="nsa_attention",
    )(qT, kr.reshape(B, S, 2 * NSA_KV_WIDTH), vT, kc, vcT, gT, _overlap_T(S))


def _gdn_kernel(x_ref, xp_ref, sm_ref, aT_ref, z_ref, cw_ref, alog_ref, dt_ref, alogc_ref, dtc_ref, nw_ref,
                ebd_ref, eg_ref, eb_ref, lblk_ref, lT_ref, o_ref,
                xe_scr, st_scr, *, tg):
    C = GDN_CHUNK
    n_ch = tg // C
    n_pr = GDN_HEADS // 2
    i = pl.program_id(1)

    @pl.when(i == 0)
    def _():
        st_scr[...] = jnp.zeros_like(st_scr)

    xe_scr[0:8, :] = jnp.where(i == 0, 0.0, xp_ref[...])
    xe_scr[8:8 + tg, :] = x_ref[...]
    y = cw_ref[0:1, :] * xe_scr[8 - 3:8 - 3 + tg, :]
    for j in range(1, GDN_CONV):
        y = y + cw_ref[j:j + 1, :] * xe_scr[8 - 3 + j:8 - 3 + j + tg, :]
    qkv = y * jax.nn.sigmoid(y)
    q = qkv[:, 0:GDN_WIDTH]
    k = qkv[:, GDN_WIDTH:2 * GDN_WIDTH]
    ebd = ebd_ref[...]
    qss = jnp.dot((q * q).astype(BF16), ebd, preferred_element_type=F32)
    kss = jnp.dot((k * k).astype(BF16), ebd, preferred_element_type=F32)
    q_all = q * lax.rsqrt(qss + 1e-6) * (HD ** -0.5)
    k_all = k * lax.rsqrt(kss + 1e-6)
    v_all = qkv[:, 2 * GDN_WIDTH:]

    sm = sm_ref[...]
    g128 = -jnp.exp(alog_ref[...]) * _softplus(sm + dt_ref[...])
    gcs128 = jnp.dot(lblk_ref[...], g128, precision=HI, preferred_element_type=F32)
    g_all = jnp.dot(gcs128, eg_ref[...], precision=HI, preferred_element_type=F32)
    b_all = jnp.dot(jax.nn.sigmoid(sm).astype(BF16), eb_ref[...], preferred_element_type=F32)
    gT = -jnp.exp(alogc_ref[...]) * _softplus(aT_ref[...] + dtc_ref[...])
    gts = [jnp.dot(gT, lT_ref[c], precision=HI, preferred_element_type=F32) for c in range(n_ch)]

    ii = lax.broadcasted_iota(jnp.int32, (C, LANES), 0)
    lane = lax.broadcasted_iota(jnp.int32, (C, LANES), 1)
    jj = lane % HD
    eye2 = jnp.where(ii == jj, 1.0, 0.0)
    lo_half = lane < HD
    lane2 = lax.broadcasted_iota(jnp.int32, (LANES, LANES), 1)
    row2 = lax.broadcasted_iota(jnp.int32, (LANES, LANES), 0)
    same_head = (lane2 < HD) == (row2 < HD)

    def bd(x):
        xb = x.astype(BF16)
        zero = jnp.zeros_like(xb)
        return jnp.concatenate([jnp.where(lo_half, xb, zero), jnp.where(lo_half, zero, xb)], axis=0)

    def mm(a, b):
        return jnp.dot(a.astype(BF16), b, preferred_element_type=F32)

    chains = [(c, pr) for c in range(n_ch) for pr in range(n_pr)]

    def blk(arr, c, pr):
        return arr[C * c:C * (c + 1), LANES * pr:LANES * (pr + 1)]

    qg, kdec, dec, aq, vb_bd, kbg_bd, eglast = {}, {}, {}, {}, {}, {}, {}
    for ch in chains:
        c, pr = ch
        qn, k_, gc, be = blk(q_all, c, pr), blk(k_all, c, pr), blk(g_all, c, pr), blk(b_all, c, pr)
        kb = k_ * be
        eg = jnp.exp(gc)
        glast = gc[C - 1:C, :]
        qg[ch] = qn * eg
        kdec[ch] = k_ * jnp.exp(glast - gc)
        eglast[ch] = jnp.exp(glast)
        vb_bd[ch] = bd(blk(v_all, c, pr) * be)
        kbg_bd[ch] = bd(kb * eg)
        gt = gts[c]
        gct = jnp.concatenate([jnp.broadcast_to(gt[2 * pr:2 * pr + 1, :], (C, HD)),
                               jnp.broadcast_to(gt[2 * pr + 1:2 * pr + 2, :], (C, HD))], axis=1)
        dec[ch] = jnp.exp(jnp.where(ii >= jj, gc - gct, NEG_INF))
        aq[ch] = lax.dot_general(jnp.concatenate([kb, qn], axis=0).astype(BF16), bd(k_), NT_DIMS,
                                 preferred_element_type=F32)
    a = {ch: jnp.where(ii > jj, aq[ch][:C] * dec[ch], 0.0) for ch in chains}
    qk = {ch: aq[ch][C:] * dec[ch] for ch in chains}
    x = {ch: eye2 - a[ch] for ch in chains}
    pw = {ch: mm(a[ch], bd(a[ch])) for ch in chains}
    for _ in range(4):
        pbd = {ch: bd(pw[ch]) for ch in chains}
        x = {ch: x[ch] + mm(x[ch], pbd[ch]) for ch in chains}
        pw = {ch: mm(pw[ch], pbd[ch]) for ch in chains}
    x = {ch: x[ch] + mm(x[ch], bd(pw[ch])) for ch in chains}
    uw = {ch: mm(x[ch], jnp.concatenate([vb_bd[ch], kbg_bd[ch]], axis=1)) for ch in chains}

    st = [st_scr[:, LANES * pr:LANES * (pr + 1)] for pr in range(n_pr)]
    o_rows = []
    for c in range(n_ch):
        ws = [mm(jnp.concatenate([uw[c, pr][:, LANES:], qg[c, pr]], axis=0), st[pr].astype(BF16))
              for pr in range(n_pr)]
        v_new = [uw[c, pr][:, :LANES] - ws[pr][:C] for pr in range(n_pr)]
        o_rows.append(jnp.concatenate([ws[pr][C:] + mm(qk[c, pr], bd(v_new[pr])) for pr in range(n_pr)], axis=1))
        upd = [lax.dot_general(kdec[c, pr].astype(BF16), v_new[pr].astype(BF16), TN_DIMS,
                               preferred_element_type=F32) for pr in range(n_pr)]
        st = [st[pr] * eglast[c, pr] + jnp.where(same_head, upd[pr], 0.0) for pr in range(n_pr)]
    st_scr[...] = jnp.concatenate(st, axis=1)

    o = jnp.concatenate(o_rows, axis=0)
    oms = jnp.dot((o * o).astype(BF16), ebd, preferred_element_type=F32) * (1.0 / HD)
    z = z_ref[...]
    o_ref[...] = o * lax.rsqrt(oms + RMS_EPS) * nw_ref[...] * (z * jax.nn.sigmoid(z))


def _gdn(gqkv, sm, aT, gz, conv_w, a_log, dt_bias, norm_w, B, S, tg=256):
    rows = B * S
    nt = S // tg
    n_ch = tg // GDN_CHUNK
    head_of = np.arange(GDN_WIDTH) // HD
    ebd = jnp.asarray(head_of[:, None] == head_of[None, :], dtype=BF16)
    eg = np.zeros((LANES, GDN_WIDTH), np.float32)
    eb = np.zeros((LANES, GDN_WIDTH), np.float32)
    eg[SM_A + head_of, np.arange(GDN_WIDTH)] = 1.0
    eb[SM_B + head_of, np.arange(GDN_WIDTH)] = 1.0
    r = np.arange(tg)
    lblk = ((r[:, None] // GDN_CHUNK == r[None, :] // GDN_CHUNK) & (r[None, :] <= r[:, None])).astype(np.float32)
    lT = np.stack([lblk.T[:, GDN_CHUNK * c:GDN_CHUNK * (c + 1)] for c in range(n_ch)])
    a_log = a_log.astype(F32)
    dt_bias = dt_bias.astype(F32)
    alog128 = jnp.zeros((1, LANES), F32).at[0, SM_A:SM_A + GDN_HEADS].set(a_log)
    dt128 = jnp.zeros((1, LANES), F32).at[0, SM_A:SM_A + GDN_HEADS].set(dt_bias)
    nw512 = jnp.tile(norm_w.astype(F32), GDN_HEADS).reshape(1, GDN_WIDTH)
    full = lambda shape: pl.BlockSpec(shape, lambda b, i: (0,) * len(shape))
    return pl.pallas_call(
        functools.partial(_gdn_kernel, tg=tg),
        grid=(B, nt),
        in_specs=[
            pl.BlockSpec((tg, 3 * GDN_WIDTH), lambda b, i: (b * nt + i, 0)),
            pl.BlockSpec((8, 3 * GDN_WIDTH), lambda b, i: (jnp.maximum((b * nt + i) * (tg // 8) - 1, 0), 0)),
            pl.BlockSpec((tg, LANES), lambda b, i: (b * nt + i, 0)),
            pl.BlockSpec((None, GDN_HEADS, tg), lambda b, i: (b, 0, i)),
            pl.BlockSpec((tg, GDN_WIDTH), lambda b, i: (b * nt + i, 0)),
            full((GDN_CONV, 3 * GDN_WIDTH)),
            full((1, LANES)),
            full((1, LANES)),
            full((GDN_HEADS, 1)),
            full((GDN_HEADS, 1)),
            full((1, GDN_WIDTH)),
            full((GDN_WIDTH, GDN_WIDTH)),
            full((LANES, GDN_WIDTH)),
            full((LANES, GDN_WIDTH)),
            full((tg, tg)),
            full((n_ch, tg, GDN_CHUNK)),
        ],
        out_specs=pl.BlockSpec((tg, GDN_WIDTH), lambda b, i: (b * nt + i, 0)),
        out_shape=jax.ShapeDtypeStruct((rows, GDN_WIDTH), F32),
        scratch_shapes=[
            pltpu.VMEM((tg + 8, 3 * GDN_WIDTH), F32),
            pltpu.VMEM((LANES, GDN_WIDTH), F32),
        ],
        compiler_params=_cparams(("arbitrary", "arbitrary")),
        name="gdn",
    )(gqkv, gqkv, sm, aT, gz, conv_w, alog128, dt128, a_log.reshape(GDN_HEADS, 1), dt_bias.reshape(GDN_HEADS, 1),
      nw512, ebd, jnp.asarray(eg), jnp.asarray(eb, dtype=BF16), jnp.asarray(lblk), jnp.asarray(lT))


def _mix_xattn_kernel(x_ref, a_ref, b_ref, wa_ref, wb_ref, nw_ref, wq_ref, kv_ref, wo_ref, o_ref):
    mix = jnp.dot(a_ref[...].astype(BF16), wa_ref[...], preferred_element_type=F32)
    mix = mix + jnp.dot(b_ref[...].astype(BF16), wb_ref[...], preferred_element_type=F32)
    _xattn_rows(x_ref[...] + mix, nw_ref, wq_ref, kv_ref, wo_ref, o_ref)


def _xattn_kernel(x_ref, nw_ref, wq_ref, kv_ref, wo_ref, o_ref):
    _xattn_rows(x_ref[...], nw_ref, wq_ref, kv_ref, wo_ref, o_ref)


def _xattn_rows(x, nw_ref, wq_ref, kv_ref, wo_ref, o_ref):
    hn = _rms(x, nw_ref[...]).astype(BF16)
    q = jnp.dot(hn, wq_ref[...], preferred_element_type=F32) * (XA_HD ** -0.5)
    outs = []
    for h in range(XA_HEADS):
        qh = q[:, XA_HD * h:XA_HD * (h + 1)].astype(BF16)
        kh = kv_ref[:, XA_HD * h:XA_HD * (h + 1)]
        vh = kv_ref[:, XA_WIDTH + XA_HD * h:XA_WIDTH + XA_HD * (h + 1)]
        s = lax.dot_general(qh, kh, NT_DIMS, preferred_element_type=F32)
        e = jnp.exp(s - jnp.max(s, axis=-1, keepdims=True))
        p = e / jnp.sum(e, axis=-1, keepdims=True)
        outs.append(jnp.dot(p.astype(BF16), vh, preferred_element_type=F32))
    o = jnp.concatenate(outs, axis=1).astype(BF16)
    o_ref[...] = x + jnp.dot(o, wo_ref[...], preferred_element_type=F32)


def _xattn(x2d, norm_w, wq_bf, memkv_l, wo_bf, S, tm=512):
    rows = x2d.shape[0]
    per_b = S // tm
    mem_len = memkv_l.shape[0] // (rows // S)
    return pl.pallas_call(
        _xattn_kernel,
        grid=(rows // tm,),
        in_specs=[
            pl.BlockSpec((tm, D_MODEL), lambda i: (i, 0)),
            pl.BlockSpec((1, D_MODEL), lambda i: (0, 0)),
            pl.BlockSpec((D_MODEL, XA_WIDTH), lambda i: (0, 0)),
            pl.BlockSpec((mem_len, 2 * XA_WIDTH), lambda i: (i // per_b, 0)),
            pl.BlockSpec((XA_WIDTH, D_MODEL), lambda i: (0, 0)),
        ],
        out_specs=pl.BlockSpec((tm, D_MODEL), lambda i: (i, 0)),
        out_shape=jax.ShapeDtypeStruct((rows, D_MODEL), F32),
        compiler_params=_cparams(("arbitrary",)),
        name="cross_attention",
    )(x2d, norm_w.reshape(1, D_MODEL), wq_bf, memkv_l, wo_bf)


def _mix_xattn(x2d, o_nsa, o_gdn, w_out_bf, norm_w, wq_bf, memkv_l, wo_bf, S, tm=512):
    rows = x2d.shape[0]
    per_b = S // tm
    mem_len = memkv_l.shape[0] // (rows // S)
    return pl.pallas_call(
        _mix_xattn_kernel,
        grid=(rows // tm,),
        in_specs=[
            pl.BlockSpec((tm, D_MODEL), lambda i: (i, 0)),
            pl.BlockSpec((tm, NSA_WIDTH), lambda i: (i, 0)),
            pl.BlockSpec((tm, GDN_WIDTH), lambda i: (i, 0)),
            pl.BlockSpec((NSA_WIDTH, D_MODEL), lambda i: (0, 0)),
            pl.BlockSpec((GDN_WIDTH, D_MODEL), lambda i: (1, 0)),
            pl.BlockSpec((1, D_MODEL), lambda i: (0, 0)),
            pl.BlockSpec((D_MODEL, XA_WIDTH), lambda i: (0, 0)),
            pl.BlockSpec((mem_len, 2 * XA_WIDTH), lambda i: (i // per_b, 0)),
            pl.BlockSpec((XA_WIDTH, D_MODEL), lambda i: (0, 0)),
        ],
        out_specs=pl.BlockSpec((tm, D_MODEL), lambda i: (i, 0)),
        out_shape=jax.ShapeDtypeStruct((rows, D_MODEL), F32),
        compiler_params=_cparams(("arbitrary",)),
        name="mix_cross_attention",
    )(x2d, o_nsa, o_gdn, w_out_bf, w_out_bf, norm_w.reshape(1, D_MODEL), wq_bf, memkv_l, wo_bf)


def _mlp_kernel(x_ref, nw_ref, w1_ref, w2_ref, fw_ref, o_ref, hn_scr, acc_scr, *, final_norm):
    j = pl.program_id(1)

    @pl.when(j == 0)
    def _():
        x = x_ref[...]
        hn_scr[...] = _rms(x, nw_ref[...]).astype(BF16)
        acc_scr[...] = x

    h = jnp.dot(hn_scr[...], w1_ref[...], preferred_element_type=F32)
    h = jnp.square(jnp.maximum(h, 0.0)).astype(BF16)
    acc_scr[...] += jnp.dot(h, w2_ref[...], preferred_element_type=F32)

    @pl.when(j == pl.num_programs(1) - 1)
    def _():
        y = acc_scr[...]
        if final_norm:
            y = _rms(y, fw_ref[...])
        o_ref[...] = y


def _mlp(x2d, norm_w, w1_bf, w2_bf, final_w, final_norm, tm=1024, tf=1024):
    rows = x2d.shape[0]
    tm = min(tm, rows)
    return pl.pallas_call(
        functools.partial(_mlp_kernel, final_norm=final_norm),
        grid=(rows // tm, D_FF // tf),
        in_specs=[
            pl.BlockSpec((tm, D_MODEL), lambda i, j: (i, 0)),
            pl.BlockSpec((1, D_MODEL), lambda i, j: (0, 0)),
            pl.BlockSpec((D_MODEL, tf), lambda i, j: (0, j)),
            pl.BlockSpec((tf, D_MODEL), lambda i, j: (j, 0)),
            pl.BlockSpec((1, D_MODEL), lambda i, j: (0, 0)),
        ],
        out_specs=pl.BlockSpec((tm, D_MODEL), lambda i, j: (i, 0)),
        out_shape=jax.ShapeDtypeStruct((rows, D_MODEL), F32),
        scratch_shapes=[pltpu.VMEM((tm, D_MODEL), BF16), pltpu.VMEM((tm, D_MODEL), F32)],
        compiler_params=_cparams(("arbitrary", "arbitrary")),
        name="mlp",
    )(x2d, norm_w.reshape(1, D_MODEL), w1_bf, w2_bf, final_w.reshape(1, D_MODEL))


def _sc_kernel(x_ref, xp_ref, nw_ref, win_ref, cw_ref, wout_ref, o_ref, cu_scr, *, tm, per_b):
    i = pl.program_id(0)
    x = x_ref[...]
    nw = nw_ref[...]
    y = jnp.dot(_rms(x, nw).astype(BF16), win_ref[...], preferred_element_type=F32)
    yp = jnp.dot(_rms(xp_ref[...], nw).astype(BF16), win_ref[:, D_MODEL:], preferred_element_type=F32)
    cu_prev = yp[:, :D_MODEL] * yp[:, D_MODEL:]
    cu_scr[0:8, :] = jnp.where(i % per_b == 0, 0.0, cu_prev)
    cu_scr[8:8 + tm, :] = y[:, D_MODEL:2 * D_MODEL] * y[:, 2 * D_MODEL:]
    conv = cw_ref[0:1, :] * cu_scr[8 - 2:8 - 2 + tm, :]
    for j in range(1, SC_WIDTH):
        conv = conv + cw_ref[j:j + 1, :] * cu_scr[8 - 2 + j:8 - 2 + j + tm, :]
    gated = (y[:, :D_MODEL] * conv).astype(BF16)
    o_ref[...] = x + jnp.dot(gated, wout_ref[...], preferred_element_type=F32)


def _sc_mixer(x2d, norm_w, win_bf, conv_w, wout_bf, S, tm=512):
    rows = x2d.shape[0]
    per_b = S // tm
    return pl.pallas_call(
        functools.partial(_sc_kernel, tm=tm, per_b=per_b),
        grid=(rows // tm,),
        in_specs=[
            pl.BlockSpec((tm, D_MODEL), lambda i: (i, 0)),
            pl.BlockSpec((8, D_MODEL), lambda i: (jnp.maximum(i * (tm // 8) - 1, 0), 0)),
            pl.BlockSpec((1, D_MODEL), lambda i: (0, 0)),
            pl.BlockSpec((D_MODEL, 3 * D_MODEL), lambda i: (0, 0)),
            pl.BlockSpec((SC_WIDTH, D_MODEL), lambda i: (0, 0)),
            pl.BlockSpec((D_MODEL, D_MODEL), lambda i: (0, 0)),
        ],
        out_specs=pl.BlockSpec((tm, D_MODEL), lambda i: (i, 0)),
        out_shape=jax.ShapeDtypeStruct((rows, D_MODEL), F32),
        scratch_shapes=[pltpu.VMEM((tm + 8, D_MODEL), F32)],
        compiler_params=_cparams(("arbitrary",)),
        name="short_conv_mixer",
    )(x2d, x2d, norm_w.reshape(1, D_MODEL), win_bf, conv_w, wout_bf)


def _hybrid_mixer(x2d, positions, norm_w, w_in, ck_pos, ck_w1, ck_w2, cv_pos, cv_w1, cv_w2,
                  gdn_conv, gdn_a_log, gdn_dt_bias, gdn_norm, B, S):
    tm = 256
    invf = (ROPE_THETA ** (-jnp.arange(0, HD, 2, dtype=F32) / HD)).reshape(HD // 2, 1)
    pos3 = positions.astype(F32).reshape(B * S // tm, 1, tm)
    qT, kr, vT, ckv, gqkv, gz, sm, gT, aT = _proj0(x2d, norm_w, _prep_w_in(w_in), pos3, invf, B, S, tm)
    w1s, p8, w2s = _prep_compress_weights(ck_pos, ck_w1, ck_w2, cv_pos, cv_w1, cv_w2)
    kc, vcT = _compress(ckv, w1s, p8, w2s, B, S)
    o_nsa = _nsa(qT, kr, vT, kc, vcT, gT, B, S)
    o_gdn = _gdn(gqkv, sm, aT, gz, gdn_conv, gdn_a_log, gdn_dt_bias, gdn_norm, B, S)
    return o_nsa, o_gdn


def kernel(x, mem, positions, norm_mix, norm_xattn, norm_mlp, hyb_w_in, hyb_cmp_k_pos, hyb_cmp_k_w1, hyb_cmp_k_w2, hyb_cmp_v_pos, hyb_cmp_v_w1, hyb_cmp_v_w2, hyb_gdn_conv, hyb_gdn_a_log, hyb_gdn_dt_bias, hyb_gdn_norm, hyb_w_out, sc_w_in, sc_conv, sc_w_out, mem_norm, xa_wq, xa_wkv, xa_wo, mlp_w1, mlp_w2, final_norm):
    B, S, _ = x.shape
    depth = norm_mix.shape[0]
    x2d = x.reshape(B * S, D_MODEL)
    memkv = _memkv(mem.reshape(-1, D_MODEL), mem_norm, xa_wkv.astype(BF16))
    for layer in range(depth):
        j = layer // 2
        xa = (norm_xattn[layer], xa_wq[layer].astype(BF16), memkv[layer], xa_wo[layer].astype(BF16), S)
        if layer % 2 == 0:
            o_nsa, o_gdn = _hybrid_mixer(x2d, positions, norm_mix[layer], hyb_w_in[j], hyb_cmp_k_pos[j],
                                         hyb_cmp_k_w1[j], hyb_cmp_k_w2[j], hyb_cmp_v_pos[j], hyb_cmp_v_w1[j],
                                         hyb_cmp_v_w2[j], hyb_gdn_conv[j], hyb_gdn_a_log[j], hyb_gdn_dt_bias[j],
                                         hyb_gdn_norm[j], B, S)
            x2d = _mix_xattn(x2d, o_nsa, o_gdn, hyb_w_out[j].astype(BF16), *xa)
        else:
            x2d = _sc_mixer(x2d, norm_mix[layer], sc_w_in[j].astype(BF16), sc_conv[j], sc_w_out[j].astype(BF16), S)
            x2d = _xattn(x2d, *xa)
        x2d = _mlp(x2d, norm_mlp[layer], mlp_w1[layer].astype(BF16), mlp_w2[layer].astype(BF16),
                   final_norm, layer == depth - 1)
    return x2d.reshape(B, S, D_MODEL)
```

```python
import functools

import numpy as np
import jax
import jax.numpy as jnp
from jax import lax
from jax.experimental import pallas as pl
from jax.experimental.pallas import tpu as pltpu

F32 = jnp.float32
BF16 = jnp.bfloat16
HI = lax.Precision.HIGHEST

D_MODEL = 1024
MEM_LEN = 256
RMS_EPS = 1e-6
ROPE_THETA = 10000.0
NEG_INF = -1e30
FORCE_BONUS = 1e3
LOG2E = 1.4426950408889634

NSA_HEADS = 8
NSA_KV_GROUPS = 2
NSA_HPG = NSA_HEADS // NSA_KV_GROUPS
HD = 64
CMP_LEN = 32
CMP_STRIDE = 16
CMP_HID = 2 * HD
SLC_BLK = 64
SLC_TOPK = 16
WINDOW = 512

GDN_HEADS = 8
GDN_CONV = 4
GDN_CHUNK = 64
SC_WIDTH = 3
XA_HEADS = 4
XA_HD = 128
D_FF = 4 * D_MODEL

NSA_WIDTH = NSA_HEADS * HD
NSA_KV_WIDTH = NSA_KV_GROUPS * HD
GDN_WIDTH = GDN_HEADS * HD
XA_WIDTH = XA_HEADS * XA_HD
IN_SIZES = (NSA_WIDTH,) + (NSA_KV_WIDTH,) * 6 + (3 * NSA_HEADS,) + (GDN_WIDTH,) * 3 + (GDN_HEADS, GDN_HEADS, GDN_WIDTH)

V7X_VMEM_BYTES = 64 * 1024 * 1024
VMEM_LIMIT = V7X_VMEM_BYTES * 3 // 4
LANES = 128

C_Q, C_KR, C_VT, C_CKV, C_GQKV, C_GZ, C_SM, NC0 = 0, 512, 768, 1024, 1280, 2816, 3328, 3456
SM_GATE, SM_A, SM_B = 0, 24, 32

NT_DIMS = (((1,), (1,)), ((), ()))
TN_DIMS = (((0,), (0,)), ((), ()))


def _cparams(sem):
    return pltpu.CompilerParams(dimension_semantics=sem, vmem_limit_bytes=VMEM_LIMIT)


def _rms(x, w):
    return x * lax.rsqrt(jnp.mean(x * x, axis=-1, keepdims=True) + RMS_EPS) * w


def _softplus(x):
    return jnp.maximum(x, 0.0) + jnp.log(1.0 + jnp.exp(-jnp.abs(x)))


def _memkv_kernel(m_ref, nw_ref, w_ref, o_ref):
    hn = _rms(m_ref[...], nw_ref[...]).astype(BF16)
    o_ref[...] = jnp.dot(hn, w_ref[...], preferred_element_type=F32).astype(BF16)


def _memkv(mem2d, mem_norm, wkv_bf):
    rows = mem2d.shape[0]
    depth = wkv_bf.shape[0]
    tm = 512
    return pl.pallas_call(
        _memkv_kernel,
        grid=(depth, rows // tm),
        in_specs=[
            pl.BlockSpec((tm, D_MODEL), lambda l, i: (i, 0)),
            pl.BlockSpec((1, D_MODEL), lambda l, i: (0, 0)),
            pl.BlockSpec((None, D_MODEL, 2 * XA_WIDTH), lambda l, i: (l, 0, 0)),
        ],
        out_specs=pl.BlockSpec((None, tm, 2 * XA_WIDTH), lambda l, i: (l, i, 0)),
        out_shape=jax.ShapeDtypeStruct((depth, rows, 2 * XA_WIDTH), BF16),
        compiler_params=_cparams(("arbitrary", "arbitrary")),
        name="memkv",
    )(mem2d, mem_norm.reshape(1, D_MODEL), wkv_bf)


def _proj0_kernel(x_ref, nw_ref, w_ref, pos_ref, invf_ref,
                  qT_ref, kr_ref, vT_ref, ckv_ref, gqkv_ref, gz_ref, sm_ref, gT_ref, aT_ref):
    tm = x_ref.shape[0]
    hn = _rms(x_ref[...], nw_ref[...]).astype(BF16)
    y = jnp.dot(hn, w_ref[...], preferred_element_type=F32)

    ang = invf_ref[...] * pos_ref[...]
    c = jnp.cos(ang)
    s = jnp.sin(ang)
    cos_n = jnp.concatenate([c, c, c, c], axis=0).T
    sin_n = jnp.concatenate([-s, s, -s, s], axis=0).T
    lane = lax.broadcasted_iota(jnp.int32, (tm, LANES), 1)
    first_half = (lane % HD) < (HD // 2)

    def rope(xc):
        rot = jnp.where(first_half, pltpu.roll(xc, LANES - HD // 2, 1), pltpu.roll(xc, HD // 2, 1))
        return xc * cos_n + rot * sin_n

    roped = [rope(y[:, LANES * i:LANES * (i + 1)]) for i in range(6)]
    q = jnp.concatenate(roped[:4], axis=1) * (LOG2E * HD ** -0.5)
    qT_ref[...] = q.T.astype(BF16)
    kr_ref[...] = jnp.concatenate(roped[4:6], axis=1).astype(BF16)
    vT_ref[...] = y[:, C_VT:C_CKV].T.astype(BF16)
    for i in range(4):
        ckv_ref[i] = y[:, C_CKV + HD * i:C_CKV + HD * (i + 1)].astype(BF16)
    gqkv_ref[...] = y[:, C_GQKV:C_GZ]
    gz_ref[...] = y[:, C_GZ:C_SM]
    sm = y[:, C_SM:NC0]
    sm_ref[...] = sm
    gT_ref[...] = jax.nn.sigmoid(sm).T[:3 * NSA_HEADS, :]
    aT_ref[...] = sm.T[SM_A:SM_A + GDN_HEADS, :]


def _prep_w_in(w_in):
    offs = np.cumsum((0,) + IN_SIZES)
    (nq, kcmp, vcmp, kslc, vslc, kwin, vwin, ngate, gq, gk, gv, ga, gb, gz) = [
        w_in[:, offs[i]:offs[i + 1]] for i in range(len(IN_SIZES))]
    pad = jnp.zeros((D_MODEL, LANES - 3 * NSA_HEADS - 2 * GDN_HEADS), w_in.dtype)
    small = jnp.concatenate([ngate, ga, gb, pad], axis=1)
    return jnp.concatenate([nq, kslc, kwin, vslc, vwin, kcmp, vcmp, gq, gk, gv, gz, small], axis=1).astype(BF16)


def _proj0(x2d, norm_w, w_bf, pos3, invf, B, S, tm):
    rows = B * S
    nt = S // tm
    row_spec = lambda n: pl.BlockSpec((tm, n), lambda i: (i, 0))
    t_spec = lambda n: pl.BlockSpec((None, n, tm), lambda i: (i // nt, 0, i % nt))
    return pl.pallas_call(
        _proj0_kernel,
        grid=(rows // tm,),
        in_specs=[
            row_spec(D_MODEL),
            pl.BlockSpec((1, D_MODEL), lambda i: (0, 0)),
            pl.BlockSpec((D_MODEL, NC0), lambda i: (0, 0)),
            pl.BlockSpec((None, 1, tm), lambda i: (i, 0, 0)),
            pl.BlockSpec((HD // 2, 1), lambda i: (0, 0)),
        ],
        out_specs=[
            t_spec(NSA_WIDTH),
            row_spec(2 * NSA_KV_WIDTH),
            t_spec(2 * NSA_KV_WIDTH),
            pl.BlockSpec((4, tm, HD), lambda i: (0, i, 0)),
            row_spec(3 * GDN_WIDTH),
            row_spec(GDN_WIDTH),
            row_spec(LANES),
            t_spec(3 * NSA_HEADS),
            t_spec(GDN_HEADS),
        ],
        out_shape=[
            jax.ShapeDtypeStruct((B, NSA_WIDTH, S), BF16),
            jax.ShapeDtypeStruct((rows, 2 * NSA_KV_WIDTH), BF16),
            jax.ShapeDtypeStruct((B, 2 * NSA_KV_WIDTH, S), BF16),
            jax.ShapeDtypeStruct((4, rows, HD), BF16),
            jax.ShapeDtypeStruct((rows, 3 * GDN_WIDTH), F32),
            jax.ShapeDtypeStruct((rows, GDN_WIDTH), F32),
            jax.ShapeDtypeStruct((rows, LANES), F32),
            jax.ShapeDtypeStruct((B, 3 * NSA_HEADS, S), F32),
            jax.ShapeDtypeStruct((B, GDN_HEADS, S), F32),
        ],
        compiler_params=_cparams(("arbitrary",)),
        name="proj0",
    )(x2d, norm_w.reshape(1, D_MODEL), w_bf, pos3, invf)


def _compress_kernel(t_ref, w1_ref, pos_ref, w2_ref, o_ref, oT_ref):
    nseg = t_ref.shape[0]
    w1 = w1_ref[...]
    ab = jnp.dot(t_ref[...], w1, preferred_element_type=F32)
    pc = jnp.dot(pos_ref[...].astype(BF16), w1, preferred_element_type=F32)
    bias = pc[0:1, :CMP_HID] + pc[1:2, CMP_HID:]
    hid = ab[:, :CMP_HID] + pltpu.roll(ab[:, CMP_HID:], nseg - 1, 0) + bias
    act = hid * jax.nn.sigmoid(hid)
    out = jnp.dot(act.astype(BF16), w2_ref[...], preferred_element_type=F32)
    o_ref[...] = out
    oT_ref[...] = out.T[:HD, :]


def _compress(ckv, w1s, pos8, w2s, B, S):
    nseg = S // CMP_STRIDE
    t = ckv.reshape(4, B, nseg, CMP_STRIDE * HD)
    return pl.pallas_call(
        _compress_kernel,
        grid=(4, B),
        in_specs=[
            pl.BlockSpec((None, None, nseg, CMP_STRIDE * HD), lambda k, b: (k, b, 0, 0)),
            pl.BlockSpec((None, CMP_STRIDE * HD, 2 * CMP_HID), lambda k, b: (k // 2, 0, 0)),
            pl.BlockSpec((None, 8, CMP_STRIDE * HD), lambda k, b: (k // 2, 0, 0)),
            pl.BlockSpec((None, CMP_HID, LANES), lambda k, b: (k // 2, 0, 0)),
        ],
        out_specs=[
            pl.BlockSpec((None, None, nseg, LANES), lambda k, b: (k, b, 0, 0)),
            pl.BlockSpec((None, None, HD, nseg), lambda k, b: (k, b, 0, 0)),
        ],
        out_shape=[
            jax.ShapeDtypeStruct((4, B, nseg, LANES), F32),
            jax.ShapeDtypeStruct((4, B, HD, nseg), F32),
        ],
        compiler_params=_cparams(("arbitrary", "arbitrary")),
        name="nsa_compress",
    )(t, w1s, pos8, w2s)


def _prep_compress_weights(k_pos, k_w1, k_w2, v_pos, v_w1, v_w2):
    half = CMP_STRIDE * HD

    def w1cat(w1):
        return jnp.concatenate([w1[:half], w1[half:]], axis=1)

    def pos8(p):
        flat = p.reshape(2, half)
        return jnp.concatenate([flat, jnp.zeros((6, half), p.dtype)], axis=0)

    def w2pad(w2):
        return jnp.concatenate([w2, jnp.zeros((CMP_HID, LANES - HD), w2.dtype)], axis=1)

    w1s = jnp.stack([w1cat(k_w1), w1cat(v_w1)]).astype(BF16)
    p8 = jnp.stack([pos8(k_pos), pos8(v_pos)])
    w2s = jnp.stack([w2pad(k_w2), w2pad(v_w2)]).astype(BF16)
    return w1s, p8, w2s


def _nsa_kernel(qT_ref, kr_ref, vT_ref, kc_ref, vcT_ref, gT_ref, ovT_ref, o_ref,
                imp_scr, sel_scr, s_scr, p_scr, b_scr, *, S, TQ):
    CH = TQ
    groups = range(NSA_KV_GROUPS)
    n_cmp = S // CMP_STRIDE
    n_slc = S // SLC_BLK
    n_sel = min(SLC_TOPK, n_slc)
    qi = pl.program_id(1)
    q0 = qi * TQ
    t1 = q0 + lax.broadcasted_iota(jnp.int32, (1, TQ), 1)
    t4 = jnp.concatenate([t1] * NSA_HPG, axis=1)
    n_io = lax.broadcasted_iota(jnp.int32, (n_cmp, 1), 0)
    j_io = lax.broadcasted_iota(jnp.int32, (n_slc, 1), 0)
    k_io = lax.broadcasted_iota(jnp.int32, (CH, 1), 0)
    n_causal = (q0 + TQ) // SLC_BLK

    qs, q64s = [], []
    for g in groups:
        q64 = jnp.concatenate(
            [qT_ref[NSA_HPG * HD * g + HD * h:NSA_HPG * HD * g + HD * (h + 1), :] for h in range(NSA_HPG)], axis=1)
        zq = jnp.zeros_like(q64)
        qs.append(jnp.concatenate([q64, zq] if g == 0 else [zq, q64], axis=0))
        q64s.append(q64)

    s_cmp = [jnp.dot(kc_ref[g, :, 0:HD].astype(BF16), q64s[g], preferred_element_type=F32)
             for g in groups]
    valid = (n_io * CMP_STRIDE + (CMP_LEN - 1)) <= t4
    any_valid = t4 >= CMP_LEN - 1
    p_cmp, psum = [], []
    for g in groups:
        s = jnp.where(valid, s_cmp[g], NEG_INF)
        m = jnp.max(s, axis=0, keepdims=True)
        e = jnp.exp2(s - m)
        l = jnp.sum(e, axis=0, keepdims=True)
        p = e * jnp.where(any_valid, 1.0 / l, 0.0)
        p_cmp.append(p.astype(BF16))
        ph = p[:, 0:TQ]
        for h in range(1, NSA_HPG):
            ph = ph + p[:, h * TQ:(h + 1) * TQ]
        psum.append(ph)
    o_cmp = [jnp.dot(vcT_ref[g].astype(BF16), p_cmp[g], preferred_element_type=F32) for g in groups]
    imps = [jnp.dot(ovT_ref[...], psum[g], precision=HI, preferred_element_type=F32) for g in groups]
    cur = t1 // SLC_BLK
    forced = (j_io == 0) | (j_io == cur) | (j_io == cur - 1)
    causal = j_io <= cur
    vals = [jnp.where(causal, jnp.where(forced, imps[g] + FORCE_BONUS, imps[g]), -1.0) for g in groups]
    for g in groups:
        imp_scr[g] = vals[g]

    def rank_body(i, cnts):
        tie = jnp.where(j_io > i, 1.0, 0.0)
        out = []
        for g in groups:
            vi = imp_scr[g, pl.ds(i, 1), :]
            out.append(cnts[g] + jnp.where(vi > vals[g], 1.0, 0.0) + jnp.where(vi == vals[g], tie, 0.0))
        return tuple(out)

    cnts = lax.fori_loop(0, jnp.where(n_causal > n_sel, n_causal, 0), rank_body,
                         tuple(jnp.zeros((n_slc, TQ), F32) for _ in groups))
    for g in groups:
        selb = jnp.where(cnts[g] < float(n_sel), 0.0, NEG_INF)
        for j in range(n_slc):
            sel_scr[g, j] = jnp.broadcast_to(selb[j:j + 1, :], (8, TQ))

    n_lanes = NSA_HPG * TQ
    ones_rows = jnp.ones((16, CH), BF16)

    def scores(jc, kcol):
        r0 = pl.multiple_of(jc * CH, CH)
        k = kr_ref[pl.ds(r0, CH), kcol:kcol + LANES]
        return [jnp.dot(k, qs[g], preferred_element_type=F32) for g in groups]

    def values(jc, vrow):
        r0 = pl.multiple_of(jc * CH, CH)
        return [jnp.concatenate([vT_ref[vrow + HD * g:vrow + HD * (g + 1), pl.ds(r0, CH)], ones_rows], axis=0)
                for g in groups]

    def probs(sc, bias, m_i):
        if bias is not None:
            sc = sc + jnp.concatenate([bias] * NSA_HPG, axis=1)
        m_new = jnp.maximum(m_i, jnp.max(sc, axis=0, keepdims=True))
        return m_new, jnp.exp2(m_i - m_new), jnp.exp2(sc - m_new).astype(BF16)

    def pv(vt, pr):
        return jnp.dot(vt, pr, preferred_element_type=F32)

    def step(jc, carry, kcol, vrow, biases):
        scs = scores(jc, kcol)
        vts = values(jc, vrow)
        parts = [probs(scs[g], None if biases is None else biases[g], carry[g][0]) for g in groups]
        return tuple((parts[g][0], carry[g][1] * parts[g][1] + pv(vts[g], parts[g][2])) for g in groups)

    def sel_bias(jc):
        nb = CH // SLC_BLK
        return [jnp.concatenate([sel_scr[g, nb * jc + b] for b in range(nb) for _ in range(SLC_BLK // 8)], axis=0)
                for g in groups]

    def m_init():
        return jnp.full((1, n_lanes), NEG_INF, F32)

    def acc_init():
        return jnp.zeros((HD + 16, n_lanes), F32)

    u_io = lax.broadcasted_iota(jnp.int32, (1, TQ), 1)
    causal_bias = jnp.where(k_io <= u_io, 0.0, NEG_INF)
    window_bias = jnp.where(k_io > u_io, 0.0, NEG_INF)

    def slc_half(jc, slot, ms):
        nxt = scores(jc + 1, 0)
        v_prev = values(jnp.maximum(jc - 1, 0), 0)
        pvs = [pv(v_prev[g], p_scr[g]) for g in groups]
        for g in groups:
            s_scr[1 - slot, g] = nxt[g]
        bias = sel_bias(jc)
        out = []
        for g in groups:
            m_new, alpha, pr = probs(s_scr[slot, g], bias[g], ms[g])
            b_scr[g] = (b_scr[g] + pvs[g]) * alpha
            p_scr[g] = pr
            out.append(m_new)
        return tuple(out)

    first = scores(0, 0)
    for g in groups:
        s_scr[0, g] = first[g]
        p_scr[g] = jnp.zeros((CH, n_lanes), BF16)
        b_scr[g] = acc_init()
    ms = lax.fori_loop(0, qi // 2, lambda i, m: slc_half(2 * i + 1, 1, slc_half(2 * i, 0, m)),
                       tuple(m_init() for _ in groups))

    def slc_odd(_, m):
        m = slc_half(qi - 1, 0, m)
        s_scr[0] = s_scr[1]
        return m

    ms = lax.fori_loop(0, qi % 2, slc_odd, ms)

    n_back = WINDOW // CH
    wcar = lax.fori_loop(0, jnp.where(qi >= n_back, 1, 0),
                         lambda _, c: step(qi - n_back, c, LANES, NSA_KV_WIDTH, [window_bias] * len(groups)),
                         tuple((m_init(), acc_init()) for _ in groups))
    wcar = lax.fori_loop(jnp.maximum(qi - n_back + 1, 0), qi,
                         lambda jc, c: step(jc, c, LANES, NSA_KV_WIDTH, None), wcar)

    v_prev = values(jnp.maximum(qi - 1, 0), 0)
    pvs = [pv(v_prev[g], p_scr[g]) for g in groups]
    s_win = scores(qi, LANES)
    bias_d = sel_bias(qi)
    slc_parts = [probs(s_scr[0, g], bias_d[g] + causal_bias, ms[g]) for g in groups]
    win_parts = [probs(s_win[g], causal_bias, wcar[g][0]) for g in groups]
    v_slc = values(qi, 0)
    v_win = values(qi, NSA_KV_WIDTH)
    o_slc, o_win = [], []
    for g in groups:
        acc = (b_scr[g] + pvs[g]) * slc_parts[g][1] + pv(v_slc[g], slc_parts[g][2])
        o_slc.append(acc[:HD] / acc[HD:HD + 1])
        acc = wcar[g][1] * win_parts[g][1] + pv(v_win[g], win_parts[g][2])
        o_win.append(acc[:HD] / acc[HD:HD + 1])

    for g in groups:
        rows = []
        for h in range(NSA_HPG):
            hh = NSA_HPG * g + h
            sl = slice(h * TQ, (h + 1) * TQ)
            rows.append(gT_ref[3 * hh:3 * hh + 1, :] * o_cmp[g][:, sl]
                        + gT_ref[3 * hh + 1:3 * hh + 2, :] * o_slc[g][:, sl]
                        + gT_ref[3 * hh + 2:3 * hh + 3, :] * o_win[g][:, sl])
        og = jnp.concatenate(rows, axis=0)
        o_ref[:, NSA_HPG * HD * g:NSA_HPG * HD * (g + 1)] = og.T


def _overlap_T(S):
    n_cmp = S // CMP_STRIDE
    n_slc = S // SLC_BLK
    cs = np.arange(n_cmp)[:, None] * CMP_STRIDE
    js = np.arange(n_slc)[None, :] * SLC_BLK
    ov = np.clip(np.minimum(cs + CMP_LEN, js + SLC_BLK) - np.maximum(cs, js), 0, None) / CMP_LEN
    ov[n_cmp - 1] = 0.0
    return jnp.asarray(ov.T, dtype=F32)


def _nsa(qT, kr, vT, kc, vcT, gT, B, S, TQ=256):
    n_cmp = S // CMP_STRIDE
    n_slc = S // SLC_BLK
    nq = S // TQ
    return pl.pallas_call(
        functools.partial(_nsa_kernel, S=S, TQ=TQ),
        grid=(B, nq),
        in_specs=[
            pl.BlockSpec((None, NSA_WIDTH, TQ), lambda b, i: (b, 0, i)),
            pl.BlockSpec((None, S, 2 * NSA_KV_WIDTH), lambda b, i: (b, 0, 0)),
            pl.BlockSpec((None, 2 * NSA_KV_WIDTH, S), lambda b, i: (b, 0, 0)),
            pl.BlockSpec((2, None, n_cmp, LANES), lambda b, i: (0, b, 0, 0)),
            pl.BlockSpec((2, None, HD, n_cmp), lambda b, i: (1, b, 0, 0)),
            pl.BlockSpec((None, 3 * NSA_HEADS, TQ), lambda b, i: (b, 0, i)),
            pl.BlockSpec((n_slc, n_cmp), lambda b, i: (0, 0)),
        ],
        out_specs=pl.BlockSpec((TQ, NSA_WIDTH), lambda b, i: (b * nq + i, 0)),
        out_shape=jax.ShapeDtypeStruct((B * S, NSA_WIDTH), F32),
        scratch_shapes=[
            pltpu.VMEM((NSA_KV_GROUPS, n_slc, TQ), F32),
            pltpu.VMEM((NSA_KV_GROUPS, n_slc, 8, TQ), F32),
            pltpu.VMEM((2, NSA_KV_GROUPS, TQ, NSA_HPG * TQ), F32),
            pltpu.VMEM((NSA_KV_GROUPS, TQ, NSA_HPG * TQ), BF16),
            pltpu.VMEM((NSA_KV_GROUPS, HD + 16, NSA_HPG * TQ), F32),
        ],
        compiler_params=_cparams(("arbitrary", "arbitrary")),
        name="nsa_attention",
    )(qT, kr.reshape(B, S, 2 * NSA_KV_WIDTH), vT, kc, vcT, gT, _overlap_T(S))


def _gdn_kernel(x_ref, xp_ref, sm_ref, aT_ref, z_ref, cw_ref, alog_ref, dt_ref, alogc_ref, dtc_ref, nw_ref,
                ebd_ref, eg_ref, eb_ref, lblk_ref, lT_ref, o_ref,
                xe_scr, st_scr, *, tg):
    C = GDN_CHUNK
    n_ch = tg // C
    n_pr = GDN_HEADS // 2
    i = pl.program_id(1)

    @pl.when(i == 0)
    def _():
        st_scr[...] = jnp.zeros_like(st_scr)

    sm = sm_ref[...]
    g128 = -jnp.exp(alog_ref[...]) * _softplus(sm + dt_ref[...])
    gcs128 = jnp.dot(lblk_ref[...], g128, precision=HI, preferred_element_type=F32)
    g_all = jnp.dot(gcs128, eg_ref[...], precision=HI, preferred_element_type=F32)
    b_all = jnp.dot(jax.nn.sigmoid(sm).astype(BF16), eb_ref[...], preferred_element_type=F32)
    gT = -jnp.exp(alogc_ref[...]) * _softplus(aT_ref[...] + dtc_ref[...])
    gts = [jnp.dot(gT, lT_ref[c], precision=HI, preferred_element_type=F32) for c in range(n_ch)]

    xe_scr[0:8, :] = jnp.where(i == 0, 0.0, xp_ref[...])
    xe_scr[8:8 + tg, :] = x_ref[...]
    ebd = ebd_ref[...]
    n_split = 2
    rh = tg // n_split
    q_parts, k_parts, v_parts = [], [], []
    for hf in range(n_split):
        base = 8 - (GDN_CONV - 1) + rh * hf
        y = cw_ref[0:1, :] * xe_scr[base:base + rh, :]
        for j in range(1, GDN_CONV):
            y = y + cw_ref[j:j + 1, :] * xe_scr[base + j:base + j + rh, :]
        qkv = y * jax.nn.sigmoid(y)
        q = qkv[:, 0:GDN_WIDTH]
        k = qkv[:, GDN_WIDTH:2 * GDN_WIDTH]
        qss = jnp.dot((q * q).astype(BF16), ebd, preferred_element_type=F32)
        kss = jnp.dot((k * k).astype(BF16), ebd, preferred_element_type=F32)
        q_parts.append(q * lax.rsqrt(qss + 1e-6) * (HD ** -0.5))
        k_parts.append(k * lax.rsqrt(kss + 1e-6))
        v_parts.append(qkv[:, 2 * GDN_WIDTH:])
    q_all = jnp.concatenate(q_parts, axis=0)
    k_all = jnp.concatenate(k_parts, axis=0)
    v_all = jnp.concatenate(v_parts, axis=0)

    ii = lax.broadcasted_iota(jnp.int32, (C, LANES), 0)
    lane = lax.broadcasted_iota(jnp.int32, (C, LANES), 1)
    jj = lane % HD
    eye2 = jnp.where(ii == jj, 1.0, 0.0)
    lo_half = lane < HD
    lane2 = lax.broadcasted_iota(jnp.int32, (LANES, LANES), 1)
    row2 = lax.broadcasted_iota(jnp.int32, (LANES, LANES), 0)
    same_head = (lane2 < HD) == (row2 < HD)

    def bd(x):
        xb = x.astype(BF16)
        zero = jnp.zeros_like(xb)
        return jnp.concatenate([jnp.where(lo_half, xb, zero), jnp.where(lo_half, zero, xb)], axis=0)

    def mm(a, b):
        return jnp.dot(a.astype(BF16), b, preferred_element_type=F32)

    chains = [(c, pr) for c in range(n_ch) for pr in range(n_pr)]

    def blk(arr, c, pr):
        return arr[C * c:C * (c + 1), LANES * pr:LANES * (pr + 1)]

    qg, kdec, dec, aq, vb_bd, kbg_bd, eglast = {}, {}, {}, {}, {}, {}, {}
    for ch in chains:
        c, pr = ch
        qn, k_, gc, be = blk(q_all, c, pr), blk(k_all, c, pr), blk(g_all, c, pr), blk(b_all, c, pr)
        kb = k_ * be
        eg = jnp.exp(gc)
        glast = gc[C - 1:C, :]
        qg[ch] = qn * eg
        kdec[ch] = k_ * jnp.exp(glast - gc)
        eglast[ch] = jnp.exp(glast)
        vb_bd[ch] = bd(blk(v_all, c, pr) * be)
        kbg_bd[ch] = bd(kb * eg)
        gt = gts[c]
        gct = jnp.concatenate([jnp.broadcast_to(gt[2 * pr:2 * pr + 1, :], (C, HD)),
                               jnp.broadcast_to(gt[2 * pr + 1:2 * pr + 2, :], (C, HD))], axis=1)
        dec[ch] = jnp.exp(jnp.where(ii >= jj, gc - gct, NEG_INF))
        aq[ch] = lax.dot_general(jnp.concatenate([kb, qn], axis=0).astype(BF16), bd(k_), NT_DIMS,
                                 preferred_element_type=F32)
    a = {ch: jnp.where(ii > jj, aq[ch][:C] * dec[ch], 0.0) for ch in chains}
    qk = {ch: aq[ch][C:] * dec[ch] for ch in chains}
    x = {ch: eye2 - a[ch] for ch in chains}
    pw = {ch: mm(a[ch], bd(a[ch])) for ch in chains}
    for _ in range(4):
        pbd = {ch: bd(pw[ch]) for ch in chains}
        x = {ch: x[ch] + mm(x[ch], pbd[ch]) for ch in chains}
        pw = {ch: mm(pw[ch], pbd[ch]) for ch in chains}
    x = {ch: x[ch] + mm(x[ch], bd(pw[ch])) for ch in chains}
    uw = {ch: mm(x[ch], jnp.concatenate([vb_bd[ch], kbg_bd[ch]], axis=1)) for ch in chains}

    st = [st_scr[:, LANES * pr:LANES * (pr + 1)] for pr in range(n_pr)]
    o_rows = []
    for c in range(n_ch):
        ws = [mm(jnp.concatenate([uw[c, pr][:, LANES:], qg[c, pr]], axis=0), st[pr].astype(BF16))
              for pr in range(n_pr)]
        v_new = [uw[c, pr][:, :LANES] - ws[pr][:C] for pr in range(n_pr)]
        o_rows.append(jnp.concatenate([ws[pr][C:] + mm(qk[c, pr], bd(v_new[pr])) for pr in range(n_pr)], axis=1))
        upd = [lax.dot_general(kdec[c, pr].astype(BF16), v_new[pr].astype(BF16), TN_DIMS,
                               preferred_element_type=F32) for pr in range(n_pr)]
        st = [st[pr] * eglast[c, pr] + jnp.where(same_head, upd[pr], 0.0) for pr in range(n_pr)]
    st_scr[...] = jnp.concatenate(st, axis=1)

    o = jnp.concatenate(o_rows, axis=0)
    oms = jnp.dot((o * o).astype(BF16), ebd, preferred_element_type=F32) * (1.0 / HD)
    z = z_ref[...]
    o_ref[...] = o * lax.rsqrt(oms + RMS_EPS) * nw_ref[...] * (z * jax.nn.sigmoid(z))


def _gdn(gqkv, sm, aT, gz, conv_w, a_log, dt_bias, norm_w, B, S, tg=256):
    rows = B * S
    nt = S // tg
    n_ch = tg // GDN_CHUNK
    head_of = np.arange(GDN_WIDTH) // HD
    ebd = jnp.asarray(head_of[:, None] == head_of[None, :], dtype=BF16)
    eg = np.zeros((LANES, GDN_WIDTH), np.float32)
    eb = np.zeros((LANES, GDN_WIDTH), np.float32)
    eg[SM_A + head_of, np.arange(GDN_WIDTH)] = 1.0
    eb[SM_B + head_of, np.arange(GDN_WIDTH)] = 1.0
    r = np.arange(tg)
    lblk = ((r[:, None] // GDN_CHUNK == r[None, :] // GDN_CHUNK) & (r[None, :] <= r[:, None])).astype(np.float32)
    lT = np.stack([lblk.T[:, GDN_CHUNK * c:GDN_CHUNK * (c + 1)] for c in range(n_ch)])
    a_log = a_log.astype(F32)
    dt_bias = dt_bias.astype(F32)
    alog128 = jnp.zeros((1, LANES), F32).at[0, SM_A:SM_A + GDN_HEADS].set(a_log)
    dt128 = jnp.zeros((1, LANES), F32).at[0, SM_A:SM_A + GDN_HEADS].set(dt_bias)
    nw512 = jnp.tile(norm_w.astype(F32), GDN_HEADS).reshape(1, GDN_WIDTH)
    full = lambda shape: pl.BlockSpec(shape, lambda b, i: (0,) * len(shape))
    return pl.pallas_call(
        functools.partial(_gdn_kernel, tg=tg),
        grid=(B, nt),
        in_specs=[
            pl.BlockSpec((tg, 3 * GDN_WIDTH), lambda b, i: (b * nt + i, 0)),
            pl.BlockSpec((8, 3 * GDN_WIDTH), lambda b, i: (jnp.maximum((b * nt + i) * (tg // 8) - 1, 0), 0)),
            pl.BlockSpec((tg, LANES), lambda b, i: (b * nt + i, 0)),
            pl.BlockSpec((None, GDN_HEADS, tg), lambda b, i: (b, 0, i)),
            pl.BlockSpec((tg, GDN_WIDTH), lambda b, i: (b * nt + i, 0)),
            full((GDN_CONV, 3 * GDN_WIDTH)),
            full((1, LANES)),
            full((1, LANES)),
            full((GDN_HEADS, 1)),
            full((GDN_HEADS, 1)),
            full((1, GDN_WIDTH)),
            full((GDN_WIDTH, GDN_WIDTH)),
            full((LANES, GDN_WIDTH)),
            full((LANES, GDN_WIDTH)),
            full((tg, tg)),
            full((n_ch, tg, GDN_CHUNK)),
        ],
        out_specs=pl.BlockSpec((tg, GDN_WIDTH), lambda b, i: (b * nt + i, 0)),
        out_shape=jax.ShapeDtypeStruct((rows, GDN_WIDTH), F32),
        scratch_shapes=[
            pltpu.VMEM((tg + 8, 3 * GDN_WIDTH), F32),
            pltpu.VMEM((LANES, GDN_WIDTH), F32),
        ],
        compiler_params=_cparams(("arbitrary", "arbitrary")),
        name="gdn",
    )(gqkv, gqkv, sm, aT, gz, conv_w, alog128, dt128, a_log.reshape(GDN_HEADS, 1), dt_bias.reshape(GDN_HEADS, 1),
      nw512, ebd, jnp.asarray(eg), jnp.asarray(eb, dtype=BF16), jnp.asarray(lblk), jnp.asarray(lT))


def _mix_xattn_kernel(x_ref, a_ref, b_ref, wa_ref, wb_ref, nw_ref, wq_ref, kv_ref, wo_ref, o_ref):
    mix = jnp.dot(a_ref[...].astype(BF16), wa_ref[...], preferred_element_type=F32)
    mix = mix + jnp.dot(b_ref[...].astype(BF16), wb_ref[...], preferred_element_type=F32)
    _xattn_rows(x_ref[...] + mix, nw_ref, wq_ref, kv_ref, wo_ref, o_ref)


def _xattn_kernel(x_ref, nw_ref, wq_ref, kv_ref, wo_ref, o_ref):
    _xattn_rows(x_ref[...], nw_ref, wq_ref, kv_ref, wo_ref, o_ref)


def _xattn_rows(x, nw_ref, wq_ref, kv_ref, wo_ref, o_ref):
    hn = _rms(x, nw_ref[...]).astype(BF16)
    q = jnp.dot(hn, wq_ref[...], preferred_element_type=F32) * (XA_HD ** -0.5)
    heads = range(XA_HEADS)
    ss = [lax.dot_general(q[:, XA_HD * h:XA_HD * (h + 1)].astype(BF16), kv_ref[:, XA_HD * h:XA_HD * (h + 1)],
                          NT_DIMS, preferred_element_type=F32) for h in heads]
    ps = []
    for h in heads:
        e = jnp.exp(ss[h] - jnp.max(ss[h], axis=-1, keepdims=True))
        ps.append((e / jnp.sum(e, axis=-1, keepdims=True)).astype(BF16))
    outs = [jnp.dot(ps[h], kv_ref[:, XA_WIDTH + XA_HD * h:XA_WIDTH + XA_HD * (h + 1)],
                    preferred_element_type=F32) for h in heads]
    o = jnp.concatenate(outs, axis=1).astype(BF16)
    o_ref[...] = x + jnp.dot(o, wo_ref[...], preferred_element_type=F32)


def _xattn(x2d, norm_w, wq_bf, memkv_l, wo_bf, S, tm=512):
    rows = x2d.shape[0]
    per_b = S // tm
    mem_len = memkv_l.shape[0] // (rows // S)
    return pl.pallas_call(
        _xattn_kernel,
        grid=(rows // tm,),
        in_specs=[
            pl.BlockSpec((tm, D_MODEL), lambda i: (i, 0)),
            pl.BlockSpec((1, D_MODEL), lambda i: (0, 0)),
            pl.BlockSpec((D_MODEL, XA_WIDTH), lambda i: (0, 0)),
            pl.BlockSpec((mem_len, 2 * XA_WIDTH), lambda i: (i // per_b, 0)),
            pl.BlockSpec((XA_WIDTH, D_MODEL), lambda i: (0, 0)),
        ],
        out_specs=pl.BlockSpec((tm, D_MODEL), lambda i: (i, 0)),
        out_shape=jax.ShapeDtypeStruct((rows, D_MODEL), F32),
        compiler_params=_cparams(("arbitrary",)),
        name="cross_attention",
    )(x2d, norm_w.reshape(1, D_MODEL), wq_bf, memkv_l, wo_bf)


def _mix_xattn(x2d, o_nsa, o_gdn, w_out_bf, norm_w, wq_bf, memkv_l, wo_bf, S, tm=512):
    rows = x2d.shape[0]
    per_b = S // tm
    mem_len = memkv_l.shape[0] // (rows // S)
    return pl.pallas_call(
        _mix_xattn_kernel,
        grid=(rows // tm,),
        in_specs=[
            pl.BlockSpec((tm, D_MODEL), lambda i: (i, 0)),
            pl.BlockSpec((tm, NSA_WIDTH), lambda i: (i, 0)),
            pl.BlockSpec((tm, GDN_WIDTH), lambda i: (i, 0)),
            pl.BlockSpec((NSA_WIDTH, D_MODEL), lambda i: (0, 0)),
            pl.BlockSpec((GDN_WIDTH, D_MODEL), lambda i: (1, 0)),
            pl.BlockSpec((1, D_MODEL), lambda i: (0, 0)),
            pl.BlockSpec((D_MODEL, XA_WIDTH), lambda i: (0, 0)),
            pl.BlockSpec((mem_len, 2 * XA_WIDTH), lambda i: (i // per_b, 0)),
            pl.BlockSpec((XA_WIDTH, D_MODEL), lambda i: (0, 0)),
        ],
        out_specs=pl.BlockSpec((tm, D_MODEL), lambda i: (i, 0)),
        out_shape=jax.ShapeDtypeStruct((rows, D_MODEL), F32),
        compiler_params=_cparams(("arbitrary",)),
        name="mix_cross_attention",
    )(x2d, o_nsa, o_gdn, w_out_bf, w_out_bf, norm_w.reshape(1, D_MODEL), wq_bf, memkv_l, wo_bf)


def _mlp_kernel(x_ref, nw_ref, w1_ref, w2_ref, fw_ref, o_ref, hn_scr, acc_scr, *, final_norm):
    j = pl.program_id(1)

    @pl.when(j == 0)
    def _():
        x = x_ref[...]
        hn_scr[...] = _rms(x, nw_ref[...]).astype(BF16)
        acc_scr[...] = x

    h = jnp.dot(hn_scr[...], w1_ref[...], preferred_element_type=F32)
    h = jnp.square(jnp.maximum(h, 0.0)).astype(BF16)
    acc_scr[...] += jnp.dot(h, w2_ref[...], preferred_element_type=F32)

    @pl.when(j == pl.num_programs(1) - 1)
    def _():
        y = acc_scr[...]
        if final_norm:
            y = _rms(y, fw_ref[...])
        o_ref[...] = y


def _mlp(x2d, norm_w, w1_bf, w2_bf, final_w, final_norm, tm=1024, tf=1024):
    rows = x2d.shape[0]
    tm = min(tm, rows)
    return pl.pallas_call(
        functools.partial(_mlp_kernel, final_norm=final_norm),
        grid=(rows // tm, D_FF // tf),
        in_specs=[
            pl.BlockSpec((tm, D_MODEL), lambda i, j: (i, 0)),
            pl.BlockSpec((1, D_MODEL), lambda i, j: (0, 0)),
            pl.BlockSpec((D_MODEL, tf), lambda i, j: (0, j)),
            pl.BlockSpec((tf, D_MODEL), lambda i, j: (j, 0)),
            pl.BlockSpec((1, D_MODEL), lambda i, j: (0, 0)),
        ],
        out_specs=pl.BlockSpec((tm, D_MODEL), lambda i, j: (i, 0)),
        out_shape=jax.ShapeDtypeStruct((rows, D_MODEL), F32),
        scratch_shapes=[pltpu.VMEM((tm, D_MODEL), BF16), pltpu.VMEM((tm, D_MODEL), F32)],
        compiler_params=_cparams(("arbitrary", "arbitrary")),
        name="mlp",
    )(x2d, norm_w.reshape(1, D_MODEL), w1_bf, w2_bf, final_w.reshape(1, D_MODEL))


def _sc_kernel(x_ref, xp_ref, nw_ref, win_ref, cw_ref, wout_ref, o_ref, cu_scr, *, tm, per_b):
    i = pl.program_id(0)
    x = x_ref[...]
    nw = nw_ref[...]
    y = jnp.dot(_rms(x, nw).astype(BF16), win_ref[...], preferred_element_type=F32)
    yp = jnp.dot(_rms(xp_ref[...], nw).astype(BF16), win_ref[:, D_MODEL:], preferred_element_type=F32)
    cu_prev = yp[:, :D_MODEL] * yp[:, D_MODEL:]
    cu_scr[0:8, :] = jnp.where(i % per_b == 0, 0.0, cu_prev)
    cu_scr[8:8 + tm, :] = y[:, D_MODEL:2 * D_MODEL] * y[:, 2 * D_MODEL:]
    conv = cw_ref[0:1, :] * cu_scr[8 - 2:8 - 2 + tm, :]
    for j in range(1, SC_WIDTH):
        conv = conv + cw_ref[j:j + 1, :] * cu_scr[8 - 2 + j:8 - 2 + j + tm, :]
    gated = (y[:, :D_MODEL] * conv).astype(BF16)
    o_ref[...] = x + jnp.dot(gated, wout_ref[...], preferred_element_type=F32)


def _sc_mixer(x2d, norm_w, win_bf, conv_w, wout_bf, S, tm=512):
    rows = x2d.shape[0]
    per_b = S // tm
    return pl.pallas_call(
        functools.partial(_sc_kernel, tm=tm, per_b=per_b),
        grid=(rows // tm,),
        in_specs=[
            pl.BlockSpec((tm, D_MODEL), lambda i: (i, 0)),
            pl.BlockSpec((8, D_MODEL), lambda i: (jnp.maximum(i * (tm // 8) - 1, 0), 0)),
            pl.BlockSpec((1, D_MODEL), lambda i: (0, 0)),
            pl.BlockSpec((D_MODEL, 3 * D_MODEL), lambda i: (0, 0)),
            pl.BlockSpec((SC_WIDTH, D_MODEL), lambda i: (0, 0)),
            pl.BlockSpec((D_MODEL, D_MODEL), lambda i: (0, 0)),
        ],
        out_specs=pl.BlockSpec((tm, D_MODEL), lambda i: (i, 0)),
        out_shape=jax.ShapeDtypeStruct((rows, D_MODEL), F32),
        scratch_shapes=[pltpu.VMEM((tm + 8, D_MODEL), F32)],
        compiler_params=_cparams(("arbitrary",)),
        name="short_conv_mixer",
    )(x2d, x2d, norm_w.reshape(1, D_MODEL), win_bf, conv_w, wout_bf)


def _hybrid_mixer(x2d, positions, norm_w, w_in, ck_pos, ck_w1, ck_w2, cv_pos, cv_w1, cv_w2,
                  gdn_conv, gdn_a_log, gdn_dt_bias, gdn_norm, B, S):
    tm = 512
    invf = (ROPE_THETA ** (-jnp.arange(0, HD, 2, dtype=F32) / HD)).reshape(HD // 2, 1)
    pos3 = positions.astype(F32).reshape(B * S // tm, 1, tm)
    qT, kr, vT, ckv, gqkv, gz, sm, gT, aT = _proj0(x2d, norm_w, _prep_w_in(w_in), pos3, invf, B, S, tm)
    w1s, p8, w2s = _prep_compress_weights(ck_pos, ck_w1, ck_w2, cv_pos, cv_w1, cv_w2)
    kc, vcT = _compress(ckv, w1s, p8, w2s, B, S)
    o_nsa = _nsa(qT, kr, vT, kc, vcT, gT, B, S)
    o_gdn = _gdn(gqkv, sm, aT, gz, gdn_conv, gdn_a_log, gdn_dt_bias, gdn_norm, B, S)
    return o_nsa, o_gdn


def kernel(x, mem, positions, norm_mix, norm_xattn, norm_mlp, hyb_w_in, hyb_cmp_k_pos, hyb_cmp_k_w1, hyb_cmp_k_w2, hyb_cmp_v_pos, hyb_cmp_v_w1, hyb_cmp_v_w2, hyb_gdn_conv, hyb_gdn_a_log, hyb_gdn_dt_bias, hyb_gdn_norm, hyb_w_out, sc_w_in, sc_conv, sc_w_out, mem_norm, xa_wq, xa_wkv, xa_wo, mlp_w1, mlp_w2, final_norm):
    B, S, _ = x.shape
    depth = norm_mix.shape[0]
    x2d = x.reshape(B * S, D_MODEL)
    memkv = _memkv(mem.reshape(-1, D_MODEL), mem_norm, xa_wkv.astype(BF16))
    for layer in range(depth):
        j = layer // 2
        xa = (norm_xattn[layer], xa_wq[layer].astype(BF16), memkv[layer], xa_wo[layer].astype(BF16), S)
        if layer % 2 == 0:
            o_nsa, o_gdn = _hybrid_mixer(x2d, positions, norm_mix[layer], hyb_w_in[j], hyb_cmp_k_pos[j],
                                         hyb_cmp_k_w1[j], hyb_cmp_k_w2[j], hyb_cmp_v_pos[j], hyb_cmp_v_w1[j],
                                         hyb_cmp_v_w2[j], hyb_gdn_conv[j], hyb_gdn_a_log[j], hyb_gdn_dt_bias[j],
                                         hyb_gdn_norm[j], B, S)
            x2d = _mix_xattn(x2d, o_nsa, o_gdn, hyb_w_out[j].astype(BF16), *xa)
        else:
            x2d = _sc_mixer(x2d, norm_mix[layer], sc_w_in[j].astype(BF16), sc_conv[j], sc_w_out[j].astype(BF16), S)
            x2d = _xattn(x2d, *xa)
        x2d = _mlp(x2d, norm_mlp[layer], mlp_w1[layer].astype(BF16), mlp_w2[layer].astype(BF16),
                   final_norm, layer == depth - 1)
    return x2d.reshape(B, S, D_MODEL)
```

```python
import functools

import numpy as np
import jax
import jax.numpy as jnp
from jax import lax
from jax.experimental import pallas as pl
from jax.experimental.pallas import tpu as pltpu

F32 = jnp.float32
BF16 = jnp.bfloat16
HI = lax.Precision.HIGHEST

D_MODEL = 1024
MEM_LEN = 256
RMS_EPS = 1e-6
ROPE_THETA = 10000.0
NEG_INF = -1e30
FORCE_BONUS = 1e3
LOG2E = 1.4426950408889634

NSA_HEADS = 8
NSA_KV_GROUPS = 2
NSA_HPG = NSA_HEADS // NSA_KV_GROUPS
HD = 64
CMP_LEN = 32
CMP_STRIDE = 16
CMP_HID = 2 * HD
SLC_BLK = 64
SLC_TOPK = 16
WINDOW = 512

GDN_HEADS = 8
GDN_CONV = 4
GDN_CHUNK = 64
SC_WIDTH = 3
XA_HEADS = 4
XA_HD = 128
D_FF = 4 * D_MODEL

NSA_WIDTH = NSA_HEADS * HD
NSA_KV_WIDTH = NSA_KV_GROUPS * HD
GDN_WIDTH = GDN_HEADS * HD
XA_WIDTH = XA_HEADS * XA_HD
IN_SIZES = (NSA_WIDTH,) + (NSA_KV_WIDTH,) * 6 + (3 * NSA_HEADS,) + (GDN_WIDTH,) * 3 + (GDN_HEADS, GDN_HEADS, GDN_WIDTH)

V7X_VMEM_BYTES = 64 * 1024 * 1024
VMEM_LIMIT = V7X_VMEM_BYTES * 3 // 4
LANES = 128

C_Q, C_KR, C_VT, C_CKV, C_GQKV, C_GZ, C_SM, NC0 = 0, 512, 768, 1024, 1280, 2816, 3328, 3456
SM_GATE, SM_A, SM_B = 0, 24, 32
NSA_CHUNK = 256
KR_WIDTH = 3 * LANES
KR_WIN = 2 * LANES

NT_DIMS = (((1,), (1,)), ((), ()))
TN_DIMS = (((0,), (0,)), ((), ()))


def _cparams(sem):
    return pltpu.CompilerParams(dimension_semantics=sem, vmem_limit_bytes=VMEM_LIMIT)


def _rms(x, w):
    return x * lax.rsqrt(jnp.mean(x * x, axis=-1, keepdims=True) + RMS_EPS) * w


def _softplus(x):
    return jnp.maximum(x, 0.0) + jnp.log(1.0 + jnp.exp(-jnp.abs(x)))


def _memkv_kernel(m_ref, nw_ref, w_ref, o_ref):
    hn = _rms(m_ref[...], nw_ref[...]).astype(BF16)
    o_ref[...] = jnp.dot(hn, w_ref[...], preferred_element_type=F32).astype(BF16)


def _memkv(mem2d, mem_norm, wkv_bf):
    rows = mem2d.shape[0]
    depth = wkv_bf.shape[0]
    tm = 512
    return pl.pallas_call(
        _memkv_kernel,
        grid=(depth, rows // tm),
        in_specs=[
            pl.BlockSpec((tm, D_MODEL), lambda l, i: (i, 0)),
            pl.BlockSpec((1, D_MODEL), lambda l, i: (0, 0)),
            pl.BlockSpec((None, D_MODEL, 2 * XA_WIDTH), lambda l, i: (l, 0, 0)),
        ],
        out_specs=pl.BlockSpec((None, tm, 2 * XA_WIDTH), lambda l, i: (l, i, 0)),
        out_shape=jax.ShapeDtypeStruct((depth, rows, 2 * XA_WIDTH), BF16),
        compiler_params=_cparams(("arbitrary", "arbitrary")),
        name="memkv",
    )(mem2d, mem_norm.reshape(1, D_MODEL), wkv_bf)


def _proj0_kernel(x_ref, nw_ref, w_ref, pos_ref, invf_ref,
                  qT_ref, kr_ref, vT_ref, ckv_ref, gqkv_ref, gz_ref, sm_ref, gT_ref, aT_ref):
    tm = x_ref.shape[0]
    hn = _rms(x_ref[...], nw_ref[...]).astype(BF16)
    y = jnp.dot(hn, w_ref[...], preferred_element_type=F32)

    ang = invf_ref[...] * pos_ref[...]
    c = jnp.cos(ang)
    s = jnp.sin(ang)
    cos_n = jnp.concatenate([c, c, c, c], axis=0).T
    sin_n = jnp.concatenate([-s, s, -s, s], axis=0).T
    lane = lax.broadcasted_iota(jnp.int32, (tm, LANES), 1)
    first_half = (lane % HD) < (HD // 2)

    def rope(xc):
        rot = jnp.where(first_half, pltpu.roll(xc, LANES - HD // 2, 1), pltpu.roll(xc, HD // 2, 1))
        return xc * cos_n + rot * sin_n

    roped = [rope(y[:, LANES * i:LANES * (i + 1)]) for i in range(6)]
    q = jnp.concatenate(roped[:4], axis=1) * (LOG2E * HD ** -0.5)
    qT_ref[...] = q.T.astype(BF16)
    row = lax.broadcasted_iota(jnp.int32, (tm, LANES), 0)
    blk_hot = jnp.where(lane - HD == (row // SLC_BLK) % (NSA_CHUNK // SLC_BLK), 1.0, 0.0)
    k_slc = roped[4]
    lo = lane < HD
    kr_ref[...] = jnp.concatenate([jnp.where(lo, k_slc, blk_hot),
                                   jnp.where(lo, pltpu.roll(k_slc, HD, 1), blk_hot),
                                   roped[5]], axis=1).astype(BF16)
    vT_ref[...] = y[:, C_VT:C_CKV].T.astype(BF16)
    for i in range(4):
        ckv_ref[i] = y[:, C_CKV + HD * i:C_CKV + HD * (i + 1)].astype(BF16)
    gqkv_ref[...] = y[:, C_GQKV:C_GZ]
    gz_ref[...] = y[:, C_GZ:C_SM]
    sm = y[:, C_SM:NC0]
    sm_ref[...] = sm
    gT_ref[...] = jax.nn.sigmoid(sm).T[:3 * NSA_HEADS, :]
    aT_ref[...] = sm.T[SM_A:SM_A + GDN_HEADS, :]


def _prep_w_in(w_in):
    offs = np.cumsum((0,) + IN_SIZES)
    (nq, kcmp, vcmp, kslc, vslc, kwin, vwin, ngate, gq, gk, gv, ga, gb, gz) = [
        w_in[:, offs[i]:offs[i + 1]] for i in range(len(IN_SIZES))]
    pad = jnp.zeros((D_MODEL, LANES - 3 * NSA_HEADS - 2 * GDN_HEADS), w_in.dtype)
    small = jnp.concatenate([ngate, ga, gb, pad], axis=1)
    return jnp.concatenate([nq, kslc, kwin, vslc, vwin, kcmp, vcmp, gq, gk, gv, gz, small], axis=1).astype(BF16)


def _proj0(x2d, norm_w, w_bf, pos3, invf, B, S, tm):
    rows = B * S
    nt = S // tm
    row_spec = lambda n: pl.BlockSpec((tm, n), lambda i: (i, 0))
    t_spec = lambda n: pl.BlockSpec((None, n, tm), lambda i: (i // nt, 0, i % nt))
    return pl.pallas_call(
        _proj0_kernel,
        grid=(rows // tm,),
        in_specs=[
            row_spec(D_MODEL),
            pl.BlockSpec((1, D_MODEL), lambda i: (0, 0)),
            pl.BlockSpec((D_MODEL, NC0), lambda i: (0, 0)),
            pl.BlockSpec((None, 1, tm), lambda i: (i, 0, 0)),
            pl.BlockSpec((HD // 2, 1), lambda i: (0, 0)),
        ],
        out_specs=[
            t_spec(NSA_WIDTH),
            row_spec(KR_WIDTH),
            t_spec(2 * NSA_KV_WIDTH),
            pl.BlockSpec((4, tm, HD), lambda i: (0, i, 0)),
            row_spec(3 * GDN_WIDTH),
            row_spec(GDN_WIDTH),
            row_spec(LANES),
            t_spec(3 * NSA_HEADS),
            t_spec(GDN_HEADS),
        ],
        out_shape=[
            jax.ShapeDtypeStruct((B, NSA_WIDTH, S), BF16),
            jax.ShapeDtypeStruct((rows, KR_WIDTH), BF16),
            jax.ShapeDtypeStruct((B, 2 * NSA_KV_WIDTH, S), BF16),
            jax.ShapeDtypeStruct((4, rows, HD), BF16),
            jax.ShapeDtypeStruct((rows, 3 * GDN_WIDTH), F32),
            jax.ShapeDtypeStruct((rows, GDN_WIDTH), F32),
            jax.ShapeDtypeStruct((rows, LANES), F32),
            jax.ShapeDtypeStruct((B, 3 * NSA_HEADS, S), F32),
            jax.ShapeDtypeStruct((B, GDN_HEADS, S), F32),
        ],
        compiler_params=_cparams(("arbitrary",)),
        name="proj0",
    )(x2d, norm_w.reshape(1, D_MODEL), w_bf, pos3, invf)


def _compress_kernel(t_ref, w1_ref, pos_ref, w2_ref, o_ref, oT_ref):
    nseg = t_ref.shape[0]
    w1 = w1_ref[...]
    ab = jnp.dot(t_ref[...], w1, preferred_element_type=F32)
    pc = jnp.dot(pos_ref[...].astype(BF16), w1, preferred_element_type=F32)
    bias = pc[0:1, :CMP_HID] + pc[1:2, CMP_HID:]
    hid = ab[:, :CMP_HID] + pltpu.roll(ab[:, CMP_HID:], nseg - 1, 0) + bias
    act = hid * jax.nn.sigmoid(hid)
    out = jnp.dot(act.astype(BF16), w2_ref[...], preferred_element_type=F32)
    o_ref[...] = out
    oT_ref[...] = out.T[:HD, :]


def _compress(ckv, w1s, pos8, w2s, B, S):
    nseg = S // CMP_STRIDE
    t = ckv.reshape(4, B, nseg, CMP_STRIDE * HD)
    return pl.pallas_call(
        _compress_kernel,
        grid=(4, B),
        in_specs=[
            pl.BlockSpec((None, None, nseg, CMP_STRIDE * HD), lambda k, b: (k, b, 0, 0)),
            pl.BlockSpec((None, CMP_STRIDE * HD, 2 * CMP_HID), lambda k, b: (k // 2, 0, 0)),
            pl.BlockSpec((None, 8, CMP_STRIDE * HD), lambda k, b: (k // 2, 0, 0)),
            pl.BlockSpec((None, CMP_HID, LANES), lambda k, b: (k // 2, 0, 0)),
        ],
        out_specs=[
            pl.BlockSpec((None, None, nseg, LANES), lambda k, b: (k, b, 0, 0)),
            pl.BlockSpec((None, None, HD, nseg), lambda k, b: (k, b, 0, 0)),
        ],
        out_shape=[
            jax.ShapeDtypeStruct((4, B, nseg, LANES), F32),
            jax.ShapeDtypeStruct((4, B, HD, nseg), F32),
        ],
        compiler_params=_cparams(("arbitrary", "arbitrary")),
        name="nsa_compress",
    )(t, w1s, pos8, w2s)


def _prep_compress_weights(k_pos, k_w1, k_w2, v_pos, v_w1, v_w2):
    half = CMP_STRIDE * HD

    def w1cat(w1):
        return jnp.concatenate([w1[:half], w1[half:]], axis=1)

    def pos8(p):
        flat = p.reshape(2, half)
        return jnp.concatenate([flat, jnp.zeros((6, half), p.dtype)], axis=0)

    def w2pad(w2):
        return jnp.concatenate([w2, jnp.zeros((CMP_HID, LANES - HD), w2.dtype)], axis=1)

    w1s = jnp.stack([w1cat(k_w1), w1cat(v_w1)]).astype(BF16)
    p8 = jnp.stack([pos8(k_pos), pos8(v_pos)])
    w2s = jnp.stack([w2pad(k_w2), w2pad(v_w2)]).astype(BF16)
    return w1s, p8, w2s


def _nsa_kernel(qT_ref, kr_ref, vT_ref, kc_ref, vcT_ref, gT_ref, ovT_ref, o_ref,
                imp_scr, sel_scr, s_scr, p_scr, b_scr, *, S, TQ):
    CH = TQ
    groups = range(NSA_KV_GROUPS)
    n_cmp = S // CMP_STRIDE
    n_slc = S // SLC_BLK
    n_sel = min(SLC_TOPK, n_slc)
    qi = pl.program_id(1)
    q0 = qi * TQ
    t1 = q0 + lax.broadcasted_iota(jnp.int32, (1, TQ), 1)
    t4 = jnp.concatenate([t1] * NSA_HPG, axis=1)
    n_io = lax.broadcasted_iota(jnp.int32, (n_cmp, 1), 0)
    j_io = lax.broadcasted_iota(jnp.int32, (n_slc, 1), 0)
    k_io = lax.broadcasted_iota(jnp.int32, (CH, 1), 0)
    n_causal = (q0 + TQ) // SLC_BLK

    qs, q64s = [], []
    for g in groups:
        q64 = jnp.concatenate(
            [qT_ref[NSA_HPG * HD * g + HD * h:NSA_HPG * HD * g + HD * (h + 1), :] for h in range(NSA_HPG)], axis=1)
        zq = jnp.zeros_like(q64)
        qs.append(jnp.concatenate([q64, zq] if g == 0 else [zq, q64], axis=0))
        q64s.append(q64)

    s_cmp = [jnp.dot(kc_ref[g, :, 0:HD].astype(BF16), q64s[g], preferred_element_type=F32)
             for g in groups]
    valid = (n_io * CMP_STRIDE + (CMP_LEN - 1)) <= t4
    any_valid = t4 >= CMP_LEN - 1
    p_cmp, psum = [], []
    for g in groups:
        s = jnp.where(valid, s_cmp[g], NEG_INF)
        m = jnp.max(s, axis=0, keepdims=True)
        e = jnp.exp2(s - m)
        l = jnp.sum(e, axis=0, keepdims=True)
        p = e * jnp.where(any_valid, 1.0 / l, 0.0)
        p_cmp.append(p.astype(BF16))
        ph = p[:, 0:TQ]
        for h in range(1, NSA_HPG):
            ph = ph + p[:, h * TQ:(h + 1) * TQ]
        psum.append(ph)
    o_cmp = [jnp.dot(vcT_ref[g].astype(BF16), p_cmp[g], preferred_element_type=F32) for g in groups]
    imps = [jnp.dot(ovT_ref[...], psum[g], precision=HI, preferred_element_type=F32) for g in groups]
    cur = t1 // SLC_BLK
    forced = (j_io == 0) | (j_io == cur) | (j_io == cur - 1)
    causal = j_io <= cur
    vals = [jnp.where(causal, jnp.where(forced, imps[g] + FORCE_BONUS, imps[g]), -1.0) for g in groups]
    for g in groups:
        imp_scr[g] = vals[g]

    def rank_body(i, cnts):
        tie = jnp.where(j_io > i, 1.0, 0.0)
        out = []
        for g in groups:
            vi = imp_scr[g, pl.ds(i, 1), :]
            out.append(cnts[g] + jnp.where(vi > vals[g], 1.0, 0.0) + jnp.where(vi == vals[g], tie, 0.0))
        return tuple(out)

    cnts = lax.fori_loop(0, jnp.where(n_causal > n_sel, n_causal, 0), rank_body,
                         tuple(jnp.zeros((n_slc, TQ), F32) for _ in groups))
    n_lanes = NSA_HPG * TQ
    nb = CH // SLC_BLK
    for g in groups:
        selb = jnp.where(cnts[g] < float(n_sel), 0.0, NEG_INF)
        for jc in range(S // CH):
            rows = jnp.concatenate([selb[nb * jc:nb * (jc + 1), :], jnp.zeros((16 - nb, TQ), F32)], axis=0)
            sel_scr[g, jc] = jnp.concatenate([rows] * NSA_HPG, axis=1).astype(BF16)

    ones_rows = jnp.ones((16, CH), BF16)
    zero_rows = jnp.zeros((HD - 16, n_lanes), BF16)

    def win_scores(jc):
        r0 = pl.multiple_of(jc * CH, CH)
        k = kr_ref[pl.ds(r0, CH), KR_WIN:KR_WIN + LANES]
        return [jnp.dot(k, qs[g], preferred_element_type=F32) for g in groups]

    def slc_scores(jc):
        r0 = pl.multiple_of(jc * CH, CH)
        return [jnp.dot(kr_ref[pl.ds(r0, CH), LANES * g:LANES * (g + 1)],
                        jnp.concatenate([q64s[g], sel_scr[g, jc], zero_rows], axis=0),
                        preferred_element_type=F32) for g in groups]

    def values(jc, vrow):
        r0 = pl.multiple_of(jc * CH, CH)
        return [jnp.concatenate([vT_ref[vrow + HD * g:vrow + HD * (g + 1), pl.ds(r0, CH)], ones_rows], axis=0)
                for g in groups]

    def probs(sc, bias, m_i):
        if bias is not None:
            sc = sc + jnp.concatenate([bias] * NSA_HPG, axis=1)
        m_new = jnp.maximum(m_i, jnp.max(sc, axis=0, keepdims=True))
        return m_new, jnp.exp2(m_i - m_new), jnp.exp2(sc - m_new).astype(BF16)

    def pv(vt, pr):
        return jnp.dot(vt, pr, preferred_element_type=F32)

    def win_step(jc, carry, bias):
        scs = win_scores(jc)
        vts = values(jc, NSA_KV_WIDTH)
        parts = [probs(scs[g], bias, carry[g][0]) for g in groups]
        return tuple((parts[g][0], carry[g][1] * parts[g][1] + pv(vts[g], parts[g][2])) for g in groups)

    def m_init():
        return jnp.full((1, n_lanes), NEG_INF, F32)

    def acc_init():
        return jnp.zeros((HD + 16, n_lanes), F32)

    u_io = lax.broadcasted_iota(jnp.int32, (1, TQ), 1)
    causal_bias = jnp.where(k_io <= u_io, 0.0, NEG_INF)
    window_bias = jnp.where(k_io > u_io, 0.0, NEG_INF)

    def slc_half(jc, slot, ms):
        nxt = slc_scores(jc + 1)
        v_prev = values(jnp.maximum(jc - 1, 0), 0)
        pvs = [pv(v_prev[g], p_scr[g]) for g in groups]
        for g in groups:
            s_scr[1 - slot, g] = nxt[g]
        out = []
        for g in groups:
            m_new, alpha, pr = probs(s_scr[slot, g], None, ms[g])
            b_scr[g] = (b_scr[g] + pvs[g]) * alpha
            p_scr[g] = pr
            out.append(m_new)
        return tuple(out)

    first = slc_scores(0)
    for g in groups:
        s_scr[0, g] = first[g]
        p_scr[g] = jnp.zeros((CH, n_lanes), BF16)
        b_scr[g] = acc_init()
    ms = lax.fori_loop(0, qi // 2, lambda i, m: slc_half(2 * i + 1, 1, slc_half(2 * i, 0, m)),
                       tuple(m_init() for _ in groups))

    def slc_odd(_, m):
        m = slc_half(qi - 1, 0, m)
        s_scr[0] = s_scr[1]
        return m

    ms = lax.fori_loop(0, qi % 2, slc_odd, ms)

    n_back = WINDOW // CH
    wcar = lax.fori_loop(0, jnp.where(qi >= n_back, 1, 0), lambda _, c: win_step(qi - n_back, c, window_bias),
                         tuple((m_init(), acc_init()) for _ in groups))
    wcar = lax.fori_loop(jnp.maximum(qi - n_back + 1, 0), qi, lambda jc, c: win_step(jc, c, None), wcar)

    v_prev = values(jnp.maximum(qi - 1, 0), 0)
    pvs = [pv(v_prev[g], p_scr[g]) for g in groups]
    s_win = win_scores(qi)
    slc_parts = [probs(s_scr[0, g], causal_bias, ms[g]) for g in groups]
    win_parts = [probs(s_win[g], causal_bias, wcar[g][0]) for g in groups]
    v_slc = values(qi, 0)
    v_win = values(qi, NSA_KV_WIDTH)
    o_slc, o_win = [], []
    for g in groups:
        acc = (b_scr[g] + pvs[g]) * slc_parts[g][1] + pv(v_slc[g], slc_parts[g][2])
        o_slc.append(acc[:HD] / acc[HD:HD + 1])
        acc = wcar[g][1] * win_parts[g][1] + pv(v_win[g], win_parts[g][2])
        o_win.append(acc[:HD] / acc[HD:HD + 1])

    for g in groups:
        rows = []
        for h in range(NSA_HPG):
            hh = NSA_HPG * g + h
            sl = slice(h * TQ, (h + 1) * TQ)
            rows.append(gT_ref[3 * hh:3 * hh + 1, :] * o_cmp[g][:, sl]
                        + gT_ref[3 * hh + 1:3 * hh + 2, :] * o_slc[g][:, sl]
                        + gT_ref[3 * hh + 2:3 * hh + 3, :] * o_win[g][:, sl])
        og = jnp.concatenate(rows, axis=0)
        o_ref[:, NSA_HPG * HD * g:NSA_HPG * HD * (g + 1)] = og.T


def _overlap_T(S):
    n_cmp = S // CMP_STRIDE
    n_slc = S // SLC_BLK
    cs = np.arange(n_cmp)[:, None] * CMP_STRIDE
    js = np.arange(n_slc)[None, :] * SLC_BLK
    ov = np.clip(np.minimum(cs + CMP_LEN, js + SLC_BLK) - np.maximum(cs, js), 0, None) / CMP_LEN
    ov[n_cmp - 1] = 0.0
    return jnp.asarray(ov.T, dtype=F32)


def _nsa(qT, kr, vT, kc, vcT, gT, B, S, TQ=NSA_CHUNK):
    n_cmp = S // CMP_STRIDE
    n_slc = S // SLC_BLK
    nq = S // TQ
    return pl.pallas_call(
        functools.partial(_nsa_kernel, S=S, TQ=TQ),
        grid=(B, nq),
        in_specs=[
            pl.BlockSpec((None, NSA_WIDTH, TQ), lambda b, i: (b, 0, i)),
            pl.BlockSpec((None, S, KR_WIDTH), lambda b, i: (b, 0, 0)),
            pl.BlockSpec((None, 2 * NSA_KV_WIDTH, S), lambda b, i: (b, 0, 0)),
            pl.BlockSpec((2, None, n_cmp, LANES), lambda b, i: (0, b, 0, 0)),
            pl.BlockSpec((2, None, HD, n_cmp), lambda b, i: (1, b, 0, 0)),
            pl.BlockSpec((None, 3 * NSA_HEADS, TQ), lambda b, i: (b, 0, i)),
            pl.BlockSpec((n_slc, n_cmp), lambda b, i: (0, 0)),
        ],
        out_specs=pl.BlockSpec((TQ, NSA_WIDTH), lambda b, i: (b * nq + i, 0)),
        out_shape=jax.ShapeDtypeStruct((B * S, NSA_WIDTH), F32),
        scratch_shapes=[
            pltpu.VMEM((NSA_KV_GROUPS, n_slc, TQ), F32),
            pltpu.VMEM((NSA_KV_GROUPS, S // TQ, 16, NSA_HPG * TQ), BF16),
            pltpu.VMEM((2, NSA_KV_GROUPS, TQ, NSA_HPG * TQ), F32),
            pltpu.VMEM((NSA_KV_GROUPS, TQ, NSA_HPG * TQ), BF16),
            pltpu.VMEM((NSA_KV_GROUPS, HD + 16, NSA_HPG * TQ), F32),
        ],
        compiler_params=_cparams(("arbitrary", "arbitrary")),
        name="nsa_attention",
    )(qT, kr.reshape(B, S, KR_WIDTH), vT, kc, vcT, gT, _overlap_T(S))


def _gdn_kernel(x_ref, xp_ref, sm_ref, aT_ref, z_ref, cw_ref, alog_ref, dt_ref, alogc_ref, dtc_ref, nw_ref,
                ebd_ref, eg_ref, eb_ref, lblk_ref, lT_ref, o_ref,
                xe_scr, st_scr, *, tg):
    C = GDN_CHUNK
    n_ch = tg // C
    n_pr = GDN_HEADS // 2
    i = pl.program_id(1)

    @pl.when(i == 0)
    def _():
        st_scr[...] = jnp.zeros_like(st_scr)

    sm = sm_ref[...]
    g128 = -jnp.exp(alog_ref[...]) * _softplus(sm + dt_ref[...])
    gcs128 = jnp.dot(lblk_ref[...], g128, precision=HI, preferred_element_type=F32)
    g_all = jnp.dot(gcs128, eg_ref[...], precision=HI, preferred_element_type=F32)
    b_all = jnp.dot(jax.nn.sigmoid(sm).astype(BF16), eb_ref[...], preferred_element_type=F32)
    gT = -jnp.exp(alogc_ref[...]) * _softplus(aT_ref[...] + dtc_ref[...])
    gts = [jnp.dot(gT, lT_ref[c], precision=HI, preferred_element_type=F32) for c in range(n_ch)]

    xe_scr[0:8, :] = jnp.where(i == 0, 0.0, xp_ref[...])
    xe_scr[8:8 + tg, :] = x_ref[...]
    ebd = ebd_ref[...]
    n_split = 2
    rh = tg // n_split
    q_parts, k_parts, v_parts = [], [], []
    for hf in range(n_split):
        base = 8 - (GDN_CONV - 1) + rh * hf
        y = cw_ref[0:1, :] * xe_scr[base:base + rh, :]
        for j in range(1, GDN_CONV):
            y = y + cw_ref[j:j + 1, :] * xe_scr[base + j:base + j + rh, :]
        qkv = y * jax.nn.sigmoid(y)
        q = qkv[:, 0:GDN_WIDTH]
        k = qkv[:, GDN_WIDTH:2 * GDN_WIDTH]
        qss = jnp.dot((q * q).astype(BF16), ebd, preferred_element_type=F32)
        kss = jnp.dot((k * k).astype(BF16), ebd, preferred_element_type=F32)
        q_parts.append(q * lax.rsqrt(qss + 1e-6) * (HD ** -0.5))
        k_parts.append(k * lax.rsqrt(kss + 1e-6))
        v_parts.append(qkv[:, 2 * GDN_WIDTH:])
    q_all = jnp.concatenate(q_parts, axis=0)
    k_all = jnp.concatenate(k_parts, axis=0)
    v_all = jnp.concatenate(v_parts, axis=0)

    ii = lax.broadcasted_iota(jnp.int32, (C, LANES), 0)
    lane = lax.broadcasted_iota(jnp.int32, (C, LANES), 1)
    jj = lane % HD
    eye2 = jnp.where(ii == jj, 1.0, 0.0)
    lo_half = lane < HD
    lane2 = lax.broadcasted_iota(jnp.int32, (LANES, LANES), 1)
    row2 = lax.broadcasted_iota(jnp.int32, (LANES, LANES), 0)
    same_head = (lane2 < HD) == (row2 < HD)

    def bd(x):
        xb = x.astype(BF16)
        zero = jnp.zeros_like(xb)
        return jnp.concatenate([jnp.where(lo_half, xb, zero), jnp.where(lo_half, zero, xb)], axis=0)

    def mm(a, b):
        return jnp.dot(a.astype(BF16), b, preferred_element_type=F32)

    chains = [(c, pr) for c in range(n_ch) for pr in range(n_pr)]

    def blk(arr, c, pr):
        return arr[C * c:C * (c + 1), LANES * pr:LANES * (pr + 1)]

    qg, kdec, dec, aq, vb_bd, kbg_bd, eglast = {}, {}, {}, {}, {}, {}, {}
    for ch in chains:
        c, pr = ch
        qn, k_, gc, be = blk(q_all, c, pr), blk(k_all, c, pr), blk(g_all, c, pr), blk(b_all, c, pr)
        kb = k_ * be
        eg = jnp.exp(gc)
        glast = gc[C - 1:C, :]
        qg[ch] = qn * eg
        kdec[ch] = k_ * jnp.exp(glast - gc)
        eglast[ch] = jnp.exp(glast)
        vb_bd[ch] = bd(blk(v_all, c, pr) * be)
        kbg_bd[ch] = bd(kb * eg)
        gt = gts[c]
        gct = jnp.concatenate([jnp.broadcast_to(gt[2 * pr:2 * pr + 1, :], (C, HD)),
                               jnp.broadcast_to(gt[2 * pr + 1:2 * pr + 2, :], (C, HD))], axis=1)
        dec[ch] = jnp.exp(jnp.where(ii >= jj, gc - gct, NEG_INF))
        aq[ch] = lax.dot_general(jnp.concatenate([kb, qn], axis=0).astype(BF16), bd(k_), NT_DIMS,
                                 preferred_element_type=F32)
    a = {ch: jnp.where(ii > jj, aq[ch][:C] * dec[ch], 0.0) for ch in chains}
    qk = {ch: aq[ch][C:] * dec[ch] for ch in chains}
    x = {ch: eye2 - a[ch] for ch in chains}
    pw = {ch: mm(a[ch], bd(a[ch])) for ch in chains}
    for _ in range(4):
        pbd = {ch: bd(pw[ch]) for ch in chains}
        x = {ch: x[ch] + mm(x[ch], pbd[ch]) for ch in chains}
        pw = {ch: mm(pw[ch], pbd[ch]) for ch in chains}
    x = {ch: x[ch] + mm(x[ch], bd(pw[ch])) for ch in chains}
    uw = {ch: mm(x[ch], jnp.concatenate([vb_bd[ch], kbg_bd[ch]], axis=1)) for ch in chains}

    st = [st_scr[:, LANES * pr:LANES * (pr + 1)] for pr in range(n_pr)]
    o_rows = []
    for c in range(n_ch):
        ws = [mm(jnp.concatenate([uw[c, pr][:, LANES:], qg[c, pr]], axis=0), st[pr].astype(BF16))
              for pr in range(n_pr)]
        v_new = [uw[c, pr][:, :LANES] - ws[pr][:C] for pr in range(n_pr)]
        o_rows.append(jnp.concatenate([ws[pr][C:] + mm(qk[c, pr], bd(v_new[pr])) for pr in range(n_pr)], axis=1))
        upd = [lax.dot_general(kdec[c, pr].astype(BF16), v_new[pr].astype(BF16), TN_DIMS,
                               preferred_element_type=F32) for pr in range(n_pr)]
        st = [st[pr] * eglast[c, pr] + jnp.where(same_head, upd[pr], 0.0) for pr in range(n_pr)]
    st_scr[...] = jnp.concatenate(st, axis=1)

    o = jnp.concatenate(o_rows, axis=0)
    oms = jnp.dot((o * o).astype(BF16), ebd, preferred_element_type=F32) * (1.0 / HD)
    z = z_ref[...]
    o_ref[...] = o * lax.rsqrt(oms + RMS_EPS) * nw_ref[...] * (z * jax.nn.sigmoid(z))


def _gdn(gqkv, sm, aT, gz, conv_w, a_log, dt_bias, norm_w, B, S, tg=256):
    rows = B * S
    nt = S // tg
    n_ch = tg // GDN_CHUNK
    head_of = np.arange(GDN_WIDTH) // HD
    ebd = jnp.asarray(head_of[:, None] == head_of[None, :], dtype=BF16)
    eg = np.zeros((LANES, GDN_WIDTH), np.float32)
    eb = np.zeros((LANES, GDN_WIDTH), np.float32)
    eg[SM_A + head_of, np.arange(GDN_WIDTH)] = 1.0
    eb[SM_B + head_of, np.arange(GDN_WIDTH)] = 1.0
    r = np.arange(tg)
    lblk = ((r[:, None] // GDN_CHUNK == r[None, :] // GDN_CHUNK) & (r[None, :] <= r[:, None])).astype(np.float32)
    lT = np.stack([lblk.T[:, GDN_CHUNK * c:GDN_CHUNK * (c + 1)] for c in range(n_ch)])
    a_log = a_log.astype(F32)
    dt_bias = dt_bias.astype(F32)
    alog128 = jnp.zeros((1, LANES), F32).at[0, SM_A:SM_A + GDN_HEADS].set(a_log)
    dt128 = jnp.zeros((1, LANES), F32).at[0, SM_A:SM_A + GDN_HEADS].set(dt_bias)
    nw512 = jnp.tile(norm_w.astype(F32), GDN_HEADS).reshape(1, GDN_WIDTH)
    full = lambda shape: pl.BlockSpec(shape, lambda b, i: (0,) * len(shape))
    return pl.pallas_call(
        functools.partial(_gdn_kernel, tg=tg),
        grid=(B, nt),
        in_specs=[
            pl.BlockSpec((tg, 3 * GDN_WIDTH), lambda b, i: (b * nt + i, 0)),
            pl.BlockSpec((8, 3 * GDN_WIDTH), lambda b, i: (jnp.maximum((b * nt + i) * (tg // 8) - 1, 0), 0)),
            pl.BlockSpec((tg, LANES), lambda b, i: (b * nt + i, 0)),
            pl.BlockSpec((None, GDN_HEADS, tg), lambda b, i: (b, 0, i)),
            pl.BlockSpec((tg, GDN_WIDTH), lambda b, i: (b * nt + i, 0)),
            full((GDN_CONV, 3 * GDN_WIDTH)),
            full((1, LANES)),
            full((1, LANES)),
            full((GDN_HEADS, 1)),
            full((GDN_HEADS, 1)),
            full((1, GDN_WIDTH)),
            full((GDN_WIDTH, GDN_WIDTH)),
            full((LANES, GDN_WIDTH)),
            full((LANES, GDN_WIDTH)),
            full((tg, tg)),
            full((n_ch, tg, GDN_CHUNK)),
        ],
        out_specs=pl.BlockSpec((tg, GDN_WIDTH), lambda b, i: (b * nt + i, 0)),
        out_shape=jax.ShapeDtypeStruct((rows, GDN_WIDTH), F32),
        scratch_shapes=[
            pltpu.VMEM((tg + 8, 3 * GDN_WIDTH), F32),
            pltpu.VMEM((LANES, GDN_WIDTH), F32),
        ],
        compiler_params=_cparams(("arbitrary", "arbitrary")),
        name="gdn",
    )(gqkv, gqkv, sm, aT, gz, conv_w, alog128, dt128, a_log.reshape(GDN_HEADS, 1), dt_bias.reshape(GDN_HEADS, 1),
      nw512, ebd, jnp.asarray(eg), jnp.asarray(eb, dtype=BF16), jnp.asarray(lblk), jnp.asarray(lT))


def _mix_xattn_kernel(x_ref, a_ref, b_ref, wa_ref, wb_ref, nw_ref, wq_ref, kv_ref, wo_ref, o_ref):
    mix = jnp.dot(a_ref[...].astype(BF16), wa_ref[...], preferred_element_type=F32)
    mix = mix + jnp.dot(b_ref[...].astype(BF16), wb_ref[...], preferred_element_type=F32)
    _xattn_rows(x_ref[...] + mix, nw_ref, wq_ref, kv_ref, wo_ref, o_ref)


def _xattn_kernel(x_ref, nw_ref, wq_ref, kv_ref, wo_ref, o_ref):
    _xattn_rows(x_ref[...], nw_ref, wq_ref, kv_ref, wo_ref, o_ref)


def _xattn_rows(x, nw_ref, wq_ref, kv_ref, wo_ref, o_ref):
    hn = _rms(x, nw_ref[...]).astype(BF16)
    q = jnp.dot(hn, wq_ref[...], preferred_element_type=F32) * (XA_HD ** -0.5)
    heads = range(XA_HEADS)
    ss = [lax.dot_general(q[:, XA_HD * h:XA_HD * (h + 1)].astype(BF16), kv_ref[:, XA_HD * h:XA_HD * (h + 1)],
                          NT_DIMS, preferred_element_type=F32) for h in heads]
    ps = []
    for h in heads:
        e = jnp.exp(ss[h] - jnp.max(ss[h], axis=-1, keepdims=True))
        ps.append((e / jnp.sum(e, axis=-1, keepdims=True)).astype(BF16))
    outs = [jnp.dot(ps[h], kv_ref[:, XA_WIDTH + XA_HD * h:XA_WIDTH + XA_HD * (h + 1)],
                    preferred_element_type=F32) for h in heads]
    o = jnp.concatenate(outs, axis=1).astype(BF16)
    o_ref[...] = x + jnp.dot(o, wo_ref[...], preferred_element_type=F32)


def _xattn(x2d, norm_w, wq_bf, memkv_l, wo_bf, S, tm=512):
    rows = x2d.shape[0]
    per_b = S // tm
    mem_len = memkv_l.shape[0] // (rows // S)
    return pl.pallas_call(
        _xattn_kernel,
        grid=(rows // tm,),
        in_specs=[
            pl.BlockSpec((tm, D_MODEL), lambda i: (i, 0)),
            pl.BlockSpec((1, D_MODEL), lambda i: (0, 0)),
            pl.BlockSpec((D_MODEL, XA_WIDTH), lambda i: (0, 0)),
            pl.BlockSpec((mem_len, 2 * XA_WIDTH), lambda i: (i // per_b, 0)),
            pl.BlockSpec((XA_WIDTH, D_MODEL), lambda i: (0, 0)),
        ],
        out_specs=pl.BlockSpec((tm, D_MODEL), lambda i: (i, 0)),
        out_shape=jax.ShapeDtypeStruct((rows, D_MODEL), F32),
        compiler_params=_cparams(("arbitrary",)),
        name="cross_attention",
    )(x2d, norm_w.reshape(1, D_MODEL), wq_bf, memkv_l, wo_bf)


def _mix_xattn(x2d, o_nsa, o_gdn, w_out_bf, norm_w, wq_bf, memkv_l, wo_bf, S, tm=512):
    rows = x2d.shape[0]
    per_b = S // tm
    mem_len = memkv_l.shape[0] // (rows // S)
    return pl.pallas_call(
        _mix_xattn_kernel,
        grid=(rows // tm,),
        in_specs=[
            pl.BlockSpec((tm, D_MODEL), lambda i: (i, 0)),
            pl.BlockSpec((tm, NSA_WIDTH), lambda i: (i, 0)),
            pl.BlockSpec((tm, GDN_WIDTH), lambda i: (i, 0)),
            pl.BlockSpec((NSA_WIDTH, D_MODEL), lambda i: (0, 0)),
            pl.BlockSpec((GDN_WIDTH, D_MODEL), lambda i: (1, 0)),
            pl.BlockSpec((1, D_MODEL), lambda i: (0, 0)),
            pl.BlockSpec((D_MODEL, XA_WIDTH), lambda i: (0, 0)),
            pl.BlockSpec((mem_len, 2 * XA_WIDTH), lambda i: (i // per_b, 0)),
            pl.BlockSpec((XA_WIDTH, D_MODEL), lambda i: (0, 0)),
        ],
        out_specs=pl.BlockSpec((tm, D_MODEL), lambda i: (i, 0)),
        out_shape=jax.ShapeDtypeStruct((rows, D_MODEL), F32),
        compiler_params=_cparams(("arbitrary",)),
        name="mix_cross_attention",
    )(x2d, o_nsa, o_gdn, w_out_bf, w_out_bf, norm_w.reshape(1, D_MODEL), wq_bf, memkv_l, wo_bf)


def _mlp_kernel(x_ref, nw_ref, w1_ref, w2_ref, fw_ref, o_ref, hn_scr, acc_scr, *, final_norm):
    j = pl.program_id(1)

    @pl.when(j == 0)
    def _():
        x = x_ref[...]
        hn_scr[...] = _rms(x, nw_ref[...]).astype(BF16)
        acc_scr[...] = x

    h = jnp.dot(hn_scr[...], w1_ref[...], preferred_element_type=F32)
    h = jnp.square(jnp.maximum(h, 0.0)).astype(BF16)
    acc_scr[...] += jnp.dot(h, w2_ref[...], preferred_element_type=F32)

    @pl.when(j == pl.num_programs(1) - 1)
    def _():
        y = acc_scr[...]
        if final_norm:
            y = _rms(y, fw_ref[...])
        o_ref[...] = y


def _mlp(x2d, norm_w, w1_bf, w2_bf, final_w, final_norm, tm=1024, tf=1024):
    rows = x2d.shape[0]
    tm = min(tm, rows)
    return pl.pallas_call(
        functools.partial(_mlp_kernel, final_norm=final_norm),
        grid=(rows // tm, D_FF // tf),
        in_specs=[
            pl.BlockSpec((tm, D_MODEL), lambda i, j: (i, 0)),
            pl.BlockSpec((1, D_MODEL), lambda i, j: (0, 0)),
            pl.BlockSpec((D_MODEL, tf), lambda i, j: (0, j)),
            pl.BlockSpec((tf, D_MODEL), lambda i, j: (j, 0)),
            pl.BlockSpec((1, D_MODEL), lambda i, j: (0, 0)),
        ],
        out_specs=pl.BlockSpec((tm, D_MODEL), lambda i, j: (i, 0)),
        out_shape=jax.ShapeDtypeStruct((rows, D_MODEL), F32),
        scratch_shapes=[pltpu.VMEM((tm, D_MODEL), BF16), pltpu.VMEM((tm, D_MODEL), F32)],
        compiler_params=_cparams(("arbitrary", "arbitrary")),
        name="mlp",
    )(x2d, norm_w.reshape(1, D_MODEL), w1_bf, w2_bf, final_w.reshape(1, D_MODEL))


def _sc_kernel(x_ref, xp_ref, nw_ref, win_ref, cw_ref, wout_ref, o_ref, cu_scr, *, tm, per_b):
    i = pl.program_id(0)
    x = x_ref[...]
    nw = nw_ref[...]
    y = jnp.dot(_rms(x, nw).astype(BF16), win_ref[...], preferred_element_type=F32)
    yp = jnp.dot(_rms(xp_ref[...], nw).astype(BF16), win_ref[:, D_MODEL:], preferred_element_type=F32)
    cu_prev = yp[:, :D_MODEL] * yp[:, D_MODEL:]
    cu_scr[0:8, :] = jnp.where(i % per_b == 0, 0.0, cu_prev)
    cu_scr[8:8 + tm, :] = y[:, D_MODEL:2 * D_MODEL] * y[:, 2 * D_MODEL:]
    conv = cw_ref[0:1, :] * cu_scr[8 - 2:8 - 2 + tm, :]
    for j in range(1, SC_WIDTH):
        conv = conv + cw_ref[j:j + 1, :] * cu_scr[8 - 2 + j:8 - 2 + j + tm, :]
    gated = (y[:, :D_MODEL] * conv).astype(BF16)
    o_ref[...] = x + jnp.dot(gated, wout_ref[...], preferred_element_type=F32)


def _sc_mixer(x2d, norm_w, win_bf, conv_w, wout_bf, S, tm=512):
    rows = x2d.shape[0]
    per_b = S // tm
    return pl.pallas_call(
        functools.partial(_sc_kernel, tm=tm, per_b=per_b),
        grid=(rows // tm,),
        in_specs=[
            pl.BlockSpec((tm, D_MODEL), lambda i: (i, 0)),
            pl.BlockSpec((8, D_MODEL), lambda i: (jnp.maximum(i * (tm // 8) - 1, 0), 0)),
            pl.BlockSpec((1, D_MODEL), lambda i: (0, 0)),
            pl.BlockSpec((D_MODEL, 3 * D_MODEL), lambda i: (0, 0)),
            pl.BlockSpec((SC_WIDTH, D_MODEL), lambda i: (0, 0)),
            pl.BlockSpec((D_MODEL, D_MODEL), lambda i: (0, 0)),
        ],
        out_specs=pl.BlockSpec((tm, D_MODEL), lambda i: (i, 0)),
        out_shape=jax.ShapeDtypeStruct((rows, D_MODEL), F32),
        scratch_shapes=[pltpu.VMEM((tm + 8, D_MODEL), F32)],
        compiler_params=_cparams(("arbitrary",)),
        name="short_conv_mixer",
    )(x2d, x2d, norm_w.reshape(1, D_MODEL), win_bf, conv_w, wout_bf)


def _hybrid_mixer(x2d, positions, norm_w, w_in, ck_pos, ck_w1, ck_w2, cv_pos, cv_w1, cv_w2,
                  gdn_conv, gdn_a_log, gdn_dt_bias, gdn_norm, B, S):
    tm = 512
    invf = (ROPE_THETA ** (-jnp.arange(0, HD, 2, dtype=F32) / HD)).reshape(HD // 2, 1)
    pos3 = positions.astype(F32).reshape(B * S // tm, 1, tm)
    qT, kr, vT, ckv, gqkv, gz, sm, gT, aT = _proj0(x2d, norm_w, _prep_w_in(w_in), pos3, invf, B, S, tm)
    w1s, p8, w2s = _prep_compress_weights(ck_pos, ck_w1, ck_w2, cv_pos, cv_w1, cv_w2)
    kc, vcT = _compress(ckv, w1s, p8, w2s, B, S)
    o_nsa = _nsa(qT, kr, vT, kc, vcT, gT, B, S)
    o_gdn = _gdn(gqkv, sm, aT, gz, gdn_conv, gdn_a_log, gdn_dt_bias, gdn_norm, B, S)
    return o_nsa, o_gdn


def kernel(x, mem, positions, norm_mix, norm_xattn, norm_mlp, hyb_w_in, hyb_cmp_k_pos, hyb_cmp_k_w1, hyb_cmp_k_w2, hyb_cmp_v_pos, hyb_cmp_v_w1, hyb_cmp_v_w2, hyb_gdn_conv, hyb_gdn_a_log, hyb_gdn_dt_bias, hyb_gdn_norm, hyb_w_out, sc_w_in, sc_conv, sc_w_out, mem_norm, xa_wq, xa_wkv, xa_wo, mlp_w1, mlp_w2, final_norm):
    B, S, _ = x.shape
    depth = norm_mix.shape[0]
    x2d = x.reshape(B * S, D_MODEL)
    memkv = _memkv(mem.reshape(-1, D_MODEL), mem_norm, xa_wkv.astype(BF16))
    for layer in range(depth):
        j = layer // 2
        xa = (norm_xattn[layer], xa_wq[layer].astype(BF16), memkv[layer], xa_wo[layer].astype(BF16), S)
        if layer % 2 == 0:
            o_nsa, o_gdn = _hybrid_mixer(x2d, positions, norm_mix[layer], hyb_w_in[j], hyb_cmp_k_pos[j],
                                         hyb_cmp_k_w1[j], hyb_cmp_k_w2[j], hyb_cmp_v_pos[j], hyb_cmp_v_w1[j],
                                         hyb_cmp_v_w2[j], hyb_gdn_conv[j], hyb_gdn_a_log[j], hyb_gdn_dt_bias[j],
                                         hyb_gdn_norm[j], B, S)
            x2d = _mix_xattn(x2d, o_nsa, o_gdn, hyb_w_out[j].astype(BF16), *xa)
        else:
            x2d = _sc_mixer(x2d, norm_mix[layer], sc_w_in[j].astype(BF16), sc_conv[j], sc_w_out[j].astype(BF16), S)
            x2d = _xattn(x2d, *xa)
        x2d = _mlp(x2d, norm_mlp[layer], mlp_w1[layer].astype(BF16), mlp_w2[layer].astype(BF16),
                   final_norm, layer == depth - 1)
    return x2d.reshape(B, S, D_MODEL)
```

```python
import functools

import numpy as np
import jax
import jax.numpy as jnp
from jax import lax
from jax.experimental import pallas as pl
from jax.experimental.pallas import tpu as pltpu

F32 = jnp.float32
BF16 = jnp.bfloat16
HI = lax.Precision.HIGHEST

D_MODEL = 1024
MEM_LEN = 256
RMS_EPS = 1e-6
ROPE_THETA = 10000.0
NEG_INF = -1e30
FORCE_BONUS = 1e3
LOG2E = 1.4426950408889634

NSA_HEADS = 8
NSA_KV_GROUPS = 2
NSA_HPG = NSA_HEADS // NSA_KV_GROUPS
HD = 64
CMP_LEN = 32
CMP_STRIDE = 16
CMP_HID = 2 * HD
SLC_BLK = 64
SLC_TOPK = 16
WINDOW = 512

GDN_HEADS = 8
GDN_CONV = 4
GDN_CHUNK = 64
SC_WIDTH = 3
XA_HEADS = 4
XA_HD = 128
D_FF = 4 * D_MODEL

NSA_WIDTH = NSA_HEADS * HD
NSA_KV_WIDTH = NSA_KV_GROUPS * HD
GDN_WIDTH = GDN_HEADS * HD
XA_WIDTH = XA_HEADS * XA_HD
IN_SIZES = (NSA_WIDTH,) + (NSA_KV_WIDTH,) * 6 + (3 * NSA_HEADS,) + (GDN_WIDTH,) * 3 + (GDN_HEADS, GDN_HEADS, GDN_WIDTH)

V7X_VMEM_BYTES = 64 * 1024 * 1024
VMEM_LIMIT = V7X_VMEM_BYTES * 3 // 4
LANES = 128

C_Q, C_KR, C_VT, C_CKV, C_GQKV, C_GZ, C_SM, NC0 = 0, 512, 768, 1024, 1280, 2816, 3328, 3456
SM_GATE, SM_A, SM_B = 0, 24, 32
NSA_CHUNK = 256
KR_WIDTH = 3 * LANES
KR_WIN = 2 * LANES

NT_DIMS = (((1,), (1,)), ((), ()))
TN_DIMS = (((0,), (0,)), ((), ()))


def _cparams(sem):
    return pltpu.CompilerParams(dimension_semantics=sem, vmem_limit_bytes=VMEM_LIMIT)


def _rms(x, w):
    return x * lax.rsqrt(jnp.mean(x * x, axis=-1, keepdims=True) + RMS_EPS) * w


def _softplus(x):
    return jnp.maximum(x, 0.0) + jnp.log(1.0 + jnp.exp(-jnp.abs(x)))


def _memkv_kernel(m_ref, nw_ref, w_ref, o_ref):
    hn = _rms(m_ref[...], nw_ref[...]).astype(BF16)
    o_ref[...] = jnp.dot(hn, w_ref[...], preferred_element_type=F32).astype(BF16)


def _memkv(mem2d, mem_norm, wkv_bf):
    rows = mem2d.shape[0]
    depth = wkv_bf.shape[0]
    tm = 512
    return pl.pallas_call(
        _memkv_kernel,
        grid=(depth, rows // tm),
        in_specs=[
            pl.BlockSpec((tm, D_MODEL), lambda l, i: (i, 0)),
            pl.BlockSpec((1, D_MODEL), lambda l, i: (0, 0)),
            pl.BlockSpec((None, D_MODEL, 2 * XA_WIDTH), lambda l, i: (l, 0, 0)),
        ],
        out_specs=pl.BlockSpec((None, tm, 2 * XA_WIDTH), lambda l, i: (l, i, 0)),
        out_shape=jax.ShapeDtypeStruct((depth, rows, 2 * XA_WIDTH), BF16),
        compiler_params=_cparams(("arbitrary", "arbitrary")),
        name="memkv",
    )(mem2d, mem_norm.reshape(1, D_MODEL), wkv_bf)


def _proj0_kernel(x_ref, nw_ref, w_ref, pos_ref, invf_ref,
                  qT_ref, kr_ref, vT_ref, ckv_ref, gqkv_ref, gz_ref, sm_ref, gT_ref, aT_ref):
    tm = x_ref.shape[0]
    hn = _rms(x_ref[...], nw_ref[...]).astype(BF16)
    y = jnp.dot(hn, w_ref[...], preferred_element_type=F32)

    ang = invf_ref[...] * pos_ref[...]
    c = jnp.cos(ang)
    s = jnp.sin(ang)
    cos_n = jnp.concatenate([c, c, c, c], axis=0).T
    sin_n = jnp.concatenate([-s, s, -s, s], axis=0).T
    lane = lax.broadcasted_iota(jnp.int32, (tm, LANES), 1)
    first_half = (lane % HD) < (HD // 2)

    def rope(xc):
        rot = jnp.where(first_half, pltpu.roll(xc, LANES - HD // 2, 1), pltpu.roll(xc, HD // 2, 1))
        return xc * cos_n + rot * sin_n

    roped = [rope(y[:, LANES * i:LANES * (i + 1)]) for i in range(6)]
    q = jnp.concatenate(roped[:4], axis=1) * (LOG2E * HD ** -0.5)
    qT_ref[...] = q.T.astype(BF16)
    row = lax.broadcasted_iota(jnp.int32, (tm, LANES), 0)
    blk_hot = jnp.where(lane - HD == (row // SLC_BLK) % (NSA_CHUNK // SLC_BLK), 1.0, 0.0)
    k_slc = roped[4]
    lo = lane < HD
    kr_ref[...] = jnp.concatenate([jnp.where(lo, k_slc, blk_hot),
                                   jnp.where(lo, pltpu.roll(k_slc, HD, 1), blk_hot),
                                   roped[5]], axis=1).astype(BF16)
    vT_ref[...] = y[:, C_VT:C_CKV].T.astype(BF16)
    for i in range(4):
        ckv_ref[i] = y[:, C_CKV + HD * i:C_CKV + HD * (i + 1)].astype(BF16)
    gqkv_ref[...] = y[:, C_GQKV:C_GZ]
    gz_ref[...] = y[:, C_GZ:C_SM]
    sm = y[:, C_SM:NC0]
    sm_ref[...] = sm
    gT_ref[...] = jax.nn.sigmoid(sm).T[:3 * NSA_HEADS, :]
    aT_ref[...] = sm.T[SM_A:SM_A + GDN_HEADS, :]


def _prep_w_in(w_in):
    offs = np.cumsum((0,) + IN_SIZES)
    (nq, kcmp, vcmp, kslc, vslc, kwin, vwin, ngate, gq, gk, gv, ga, gb, gz) = [
        w_in[:, offs[i]:offs[i + 1]] for i in range(len(IN_SIZES))]
    pad = jnp.zeros((D_MODEL, LANES - 3 * NSA_HEADS - 2 * GDN_HEADS), w_in.dtype)
    small = jnp.concatenate([ngate, ga, gb, pad], axis=1)
    return jnp.concatenate([nq, kslc, kwin, vslc, vwin, kcmp, vcmp, gq, gk, gv, gz, small], axis=1).astype(BF16)


def _proj0(x2d, norm_w, w_bf, pos3, invf, B, S, tm):
    rows = B * S
    nt = S // tm
    row_spec = lambda n: pl.BlockSpec((tm, n), lambda i: (i, 0))
    t_spec = lambda n: pl.BlockSpec((None, n, tm), lambda i: (i // nt, 0, i % nt))
    return pl.pallas_call(
        _proj0_kernel,
        grid=(rows // tm,),
        in_specs=[
            row_spec(D_MODEL),
            pl.BlockSpec((1, D_MODEL), lambda i: (0, 0)),
            pl.BlockSpec((D_MODEL, NC0), lambda i: (0, 0)),
            pl.BlockSpec((None, 1, tm), lambda i: (i, 0, 0)),
            pl.BlockSpec((HD // 2, 1), lambda i: (0, 0)),
        ],
        out_specs=[
            t_spec(NSA_WIDTH),
            row_spec(KR_WIDTH),
            t_spec(2 * NSA_KV_WIDTH),
            pl.BlockSpec((4, tm, HD), lambda i: (0, i, 0)),
            row_spec(3 * GDN_WIDTH),
            row_spec(GDN_WIDTH),
            row_spec(LANES),
            t_spec(3 * NSA_HEADS),
            t_spec(GDN_HEADS),
        ],
        out_shape=[
            jax.ShapeDtypeStruct((B, NSA_WIDTH, S), BF16),
            jax.ShapeDtypeStruct((rows, KR_WIDTH), BF16),
            jax.ShapeDtypeStruct((B, 2 * NSA_KV_WIDTH, S), BF16),
            jax.ShapeDtypeStruct((4, rows, HD), BF16),
            jax.ShapeDtypeStruct((rows, 3 * GDN_WIDTH), F32),
            jax.ShapeDtypeStruct((rows, GDN_WIDTH), F32),
            jax.ShapeDtypeStruct((rows, LANES), F32),
            jax.ShapeDtypeStruct((B, 3 * NSA_HEADS, S), F32),
            jax.ShapeDtypeStruct((B, GDN_HEADS, S), F32),
        ],
        compiler_params=_cparams(("arbitrary",)),
        name="proj0",
    )(x2d, norm_w.reshape(1, D_MODEL), w_bf, pos3, invf)


def _compress_kernel(t_ref, w1_ref, pos_ref, w2_ref, o_ref, oT_ref):
    nseg = t_ref.shape[0]
    w1 = w1_ref[...]
    ab = jnp.dot(t_ref[...], w1, preferred_element_type=F32)
    pc = jnp.dot(pos_ref[...].astype(BF16), w1, preferred_element_type=F32)
    bias = pc[0:1, :CMP_HID] + pc[1:2, CMP_HID:]
    hid = ab[:, :CMP_HID] + pltpu.roll(ab[:, CMP_HID:], nseg - 1, 0) + bias
    act = hid * jax.nn.sigmoid(hid)
    out = jnp.dot(act.astype(BF16), w2_ref[...], preferred_element_type=F32)
    o_ref[...] = out
    oT_ref[...] = out.T[:HD, :]


def _compress(ckv, w1s, pos8, w2s, B, S):
    nseg = S // CMP_STRIDE
    t = ckv.reshape(4, B, nseg, CMP_STRIDE * HD)
    return pl.pallas_call(
        _compress_kernel,
        grid=(4, B),
        in_specs=[
            pl.BlockSpec((None, None, nseg, CMP_STRIDE * HD), lambda k, b: (k, b, 0, 0)),
            pl.BlockSpec((None, CMP_STRIDE * HD, 2 * CMP_HID), lambda k, b: (k // 2, 0, 0)),
            pl.BlockSpec((None, 8, CMP_STRIDE * HD), lambda k, b: (k // 2, 0, 0)),
            pl.BlockSpec((None, CMP_HID, LANES), lambda k, b: (k // 2, 0, 0)),
        ],
        out_specs=[
            pl.BlockSpec((None, None, nseg, LANES), lambda k, b: (k, b, 0, 0)),
            pl.BlockSpec((None, None, HD, nseg), lambda k, b: (k, b, 0, 0)),
        ],
        out_shape=[
            jax.ShapeDtypeStruct((4, B, nseg, LANES), F32),
            jax.ShapeDtypeStruct((4, B, HD, nseg), F32),
        ],
        compiler_params=_cparams(("arbitrary", "arbitrary")),
        name="nsa_compress",
    )(t, w1s, pos8, w2s)


def _prep_compress_weights(k_pos, k_w1, k_w2, v_pos, v_w1, v_w2):
    half = CMP_STRIDE * HD

    def w1cat(w1):
        return jnp.concatenate([w1[:half], w1[half:]], axis=1)

    def pos8(p):
        flat = p.reshape(2, half)
        return jnp.concatenate([flat, jnp.zeros((6, half), p.dtype)], axis=0)

    def w2pad(w2):
        return jnp.concatenate([w2, jnp.zeros((CMP_HID, LANES - HD), w2.dtype)], axis=1)

    w1s = jnp.stack([w1cat(k_w1), w1cat(v_w1)]).astype(BF16)
    p8 = jnp.stack([pos8(k_pos), pos8(v_pos)])
    w2s = jnp.stack([w2pad(k_w2), w2pad(v_w2)]).astype(BF16)
    return w1s, p8, w2s


def _nsa_kernel(qT_ref, kr_ref, vT_ref, kc_ref, vcT_ref, gT_ref, ovT_ref, o_ref,
                imp_scr, sel_scr, s_scr, p_scr, b_scr, *, S, TQ):
    CH = TQ
    groups = range(NSA_KV_GROUPS)
    n_cmp = S // CMP_STRIDE
    n_slc = S // SLC_BLK
    n_sel = min(SLC_TOPK, n_slc)
    qi = pl.program_id(1)
    q0 = qi * TQ
    t1 = q0 + lax.broadcasted_iota(jnp.int32, (1, TQ), 1)
    t4 = jnp.concatenate([t1] * NSA_HPG, axis=1)
    n_io = lax.broadcasted_iota(jnp.int32, (n_cmp, 1), 0)
    j_io = lax.broadcasted_iota(jnp.int32, (n_slc, 1), 0)
    k_io = lax.broadcasted_iota(jnp.int32, (CH, 1), 0)
    n_causal = (q0 + TQ) // SLC_BLK

    qs, q64s = [], []
    for g in groups:
        q64 = jnp.concatenate(
            [qT_ref[NSA_HPG * HD * g + HD * h:NSA_HPG * HD * g + HD * (h + 1), :] for h in range(NSA_HPG)], axis=1)
        zq = jnp.zeros_like(q64)
        qs.append(jnp.concatenate([q64, zq] if g == 0 else [zq, q64], axis=0))
        q64s.append(q64)

    s_cmp = [jnp.dot(kc_ref[g, :, 0:HD].astype(BF16), q64s[g], preferred_element_type=F32)
             for g in groups]
    valid = (n_io * CMP_STRIDE + (CMP_LEN - 1)) <= t4
    any_valid = t4 >= CMP_LEN - 1
    p_cmp, psum = [], []
    for g in groups:
        s = jnp.where(valid, s_cmp[g], NEG_INF)
        m = jnp.max(s, axis=0, keepdims=True)
        e = jnp.exp2(s - m)
        l = jnp.sum(e, axis=0, keepdims=True)
        p = e * jnp.where(any_valid, 1.0 / l, 0.0)
        p_cmp.append(p.astype(BF16))
        ph = p[:, 0:TQ]
        for h in range(1, NSA_HPG):
            ph = ph + p[:, h * TQ:(h + 1) * TQ]
        psum.append(ph)
    o_cmp = [jnp.dot(vcT_ref[g].astype(BF16), p_cmp[g], preferred_element_type=F32) for g in groups]
    imps = [jnp.dot(ovT_ref[...], psum[g], precision=HI, preferred_element_type=F32) for g in groups]
    cur = t1 // SLC_BLK
    forced = (j_io == 0) | (j_io == cur) | (j_io == cur - 1)
    causal = j_io <= cur
    vals = [jnp.where(causal, jnp.where(forced, imps[g] + FORCE_BONUS, imps[g]), -1.0) for g in groups]
    for g in groups:
        imp_scr[g] = vals[g]

    def rank_body(i2, cnts):
        out = list(cnts)
        for i in (2 * i2, 2 * i2 + 1):
            tie = jnp.where(j_io > i, 1.0, 0.0)
            for g in groups:
                vi = imp_scr[g, pl.ds(i, 1), :]
                out[g] = out[g] + jnp.where(vi > vals[g], 1.0, 0.0) + jnp.where(vi == vals[g], tie, 0.0)
        return tuple(out)

    cnts = lax.fori_loop(0, jnp.where(n_causal > n_sel, n_causal // 2, 0), rank_body,
                         tuple(jnp.zeros((n_slc, TQ), F32) for _ in groups))
    n_lanes = NSA_HPG * TQ
    nb = CH // SLC_BLK
    for g in groups:
        selb = jnp.where(cnts[g] < float(n_sel), 0.0, NEG_INF)
        for jc in range(S // CH):
            rows = jnp.concatenate([selb[nb * jc:nb * (jc + 1), :], jnp.zeros((16 - nb, TQ), F32)], axis=0)
            sel_scr[g, jc] = jnp.concatenate([rows] * NSA_HPG, axis=1).astype(BF16)

    ones_rows = jnp.ones((16, CH), BF16)
    zero_rows = jnp.zeros((HD - 16, n_lanes), BF16)

    def slc_scores(jc):
        r0 = pl.multiple_of(jc * CH, CH)
        return [jnp.dot(kr_ref[pl.ds(r0, CH), LANES * g:LANES * (g + 1)],
                        jnp.concatenate([q64s[g], sel_scr[g, jc], zero_rows], axis=0),
                        preferred_element_type=F32) for g in groups]

    def values(jc, vrow):
        r0 = pl.multiple_of(jc * CH, CH)
        return [jnp.concatenate([vT_ref[vrow + HD * g:vrow + HD * (g + 1), pl.ds(r0, CH)], ones_rows], axis=0)
                for g in groups]

    def probs(sc, bias, m_i):
        if bias is not None:
            sc = sc + jnp.concatenate([bias] * NSA_HPG, axis=1)
        m_new = jnp.maximum(m_i, jnp.max(sc, axis=0, keepdims=True))
        return m_new, jnp.exp2(m_i - m_new), jnp.exp2(sc - m_new).astype(BF16)

    def pv(vt, pr):
        return jnp.dot(vt, pr, preferred_element_type=F32)

    def m_init():
        return jnp.full((1, n_lanes), NEG_INF, F32)

    def acc_init():
        return jnp.zeros((HD + 16, n_lanes), F32)

    u_io = lax.broadcasted_iota(jnp.int32, (1, TQ), 1)
    causal_bias = jnp.where(k_io <= u_io, 0.0, NEG_INF)
    window_bias = jnp.where(k_io > u_io, 0.0, NEG_INF)

    def slc_half(jc, slot, ms):
        nxt = slc_scores(jc + 1)
        v_prev = values(jnp.maximum(jc - 1, 0), 0)
        pvs = [pv(v_prev[g], p_scr[g]) for g in groups]
        for g in groups:
            s_scr[1 - slot, g] = nxt[g]
        out = []
        for g in groups:
            m_new, alpha, pr = probs(s_scr[slot, g], None, ms[g])
            b_scr[g] = (b_scr[g] + pvs[g]) * alpha
            p_scr[g] = pr
            out.append(m_new)
        return tuple(out)

    first = slc_scores(0)
    for g in groups:
        s_scr[0, g] = first[g]
        p_scr[g] = jnp.zeros((CH, n_lanes), BF16)
        b_scr[g] = acc_init()
    ms = lax.fori_loop(0, qi // 2, lambda i, m: slc_half(2 * i + 1, 1, slc_half(2 * i, 0, m)),
                       tuple(m_init() for _ in groups))

    def slc_odd(_, m):
        m = slc_half(qi - 1, 0, m)
        s_scr[0] = s_scr[1]
        return m

    ms = lax.fori_loop(0, qi % 2, slc_odd, ms)

    n_back = WINDOW // CH
    n_win = n_back + 1
    w0 = jnp.maximum(qi - n_back, 0)
    rw = pl.multiple_of(w0 * CH, CH)
    v_prev = values(jnp.maximum(qi - 1, 0), 0)
    pvs = [pv(v_prev[g], p_scr[g]) for g in groups]
    k_win = kr_ref[pl.ds(rw, n_win * CH), KR_WIN:KR_WIN + LANES]
    s_win = [jnp.dot(k_win, qs[g], preferred_element_type=F32) for g in groups]
    hidden = jnp.full((CH, TQ), NEG_INF, F32)
    slab_bias = []
    for c in range(n_win):
        d = w0 + c - qi
        slab_bias.append(jnp.where(d == 0, causal_bias,
                                   jnp.where(d > 0, hidden, jnp.where(d == -n_back, window_bias, 0.0))))
    win_bias = jnp.concatenate(slab_bias, axis=0)
    slc_parts = [probs(s_scr[0, g], causal_bias, ms[g]) for g in groups]
    win_parts = [probs(s_win[g], win_bias, m_init()) for g in groups]
    v_slc = values(qi, 0)
    v_win = [jnp.concatenate([vT_ref[NSA_KV_WIDTH + HD * g:NSA_KV_WIDTH + HD * (g + 1), pl.ds(rw, n_win * CH)],
                              jnp.ones((16, n_win * CH), BF16)], axis=0) for g in groups]
    o_slc, o_win = [], []
    for g in groups:
        acc = (b_scr[g] + pvs[g]) * slc_parts[g][1] + pv(v_slc[g], slc_parts[g][2])
        o_slc.append(acc[:HD] / acc[HD:HD + 1])
        acc = pv(v_win[g], win_parts[g][2])
        o_win.append(acc[:HD] / acc[HD:HD + 1])

    for g in groups:
        rows = []
        for h in range(NSA_HPG):
            hh = NSA_HPG * g + h
            sl = slice(h * TQ, (h + 1) * TQ)
            rows.append(gT_ref[3 * hh:3 * hh + 1, :] * o_cmp[g][:, sl]
                        + gT_ref[3 * hh + 1:3 * hh + 2, :] * o_slc[g][:, sl]
                        + gT_ref[3 * hh + 2:3 * hh + 3, :] * o_win[g][:, sl])
        og = jnp.concatenate(rows, axis=0)
        o_ref[:, NSA_HPG * HD * g:NSA_HPG * HD * (g + 1)] = og.T


def _overlap_T(S):
    n_cmp = S // CMP_STRIDE
    n_slc = S // SLC_BLK
    cs = np.arange(n_cmp)[:, None] * CMP_STRIDE
    js = np.arange(n_slc)[None, :] * SLC_BLK
    ov = np.clip(np.minimum(cs + CMP_LEN, js + SLC_BLK) - np.maximum(cs, js), 0, None) / CMP_LEN
    ov[n_cmp - 1] = 0.0
    return jnp.asarray(ov.T, dtype=F32)


def _nsa(qT, kr, vT, kc, vcT, gT, B, S, TQ=NSA_CHUNK):
    n_cmp = S // CMP_STRIDE
    n_slc = S // SLC_BLK
    nq = S // TQ
    return pl.pallas_call(
        functools.partial(_nsa_kernel, S=S, TQ=TQ),
        grid=(B, nq),
        in_specs=[
            pl.BlockSpec((None, NSA_WIDTH, TQ), lambda b, i: (b, 0, i)),
            pl.BlockSpec((None, S, KR_WIDTH), lambda b, i: (b, 0, 0)),
            pl.BlockSpec((None, 2 * NSA_KV_WIDTH, S), lambda b, i: (b, 0, 0)),
            pl.BlockSpec((2, None, n_cmp, LANES), lambda b, i: (0, b, 0, 0)),
            pl.BlockSpec((2, None, HD, n_cmp), lambda b, i: (1, b, 0, 0)),
            pl.BlockSpec((None, 3 * NSA_HEADS, TQ), lambda b, i: (b, 0, i)),
            pl.BlockSpec((n_slc, n_cmp), lambda b, i: (0, 0)),
        ],
        out_specs=pl.BlockSpec((TQ, NSA_WIDTH), lambda b, i: (b * nq + i, 0)),
        out_shape=jax.ShapeDtypeStruct((B * S, NSA_WIDTH), F32),
        scratch_shapes=[
            pltpu.VMEM((NSA_KV_GROUPS, n_slc, TQ), F32),
            pltpu.VMEM((NSA_KV_GROUPS, S // TQ, 16, NSA_HPG * TQ), BF16),
            pltpu.VMEM((2, NSA_KV_GROUPS, TQ, NSA_HPG * TQ), F32),
            pltpu.VMEM((NSA_KV_GROUPS, TQ, NSA_HPG * TQ), BF16),
            pltpu.VMEM((NSA_KV_GROUPS, HD + 16, NSA_HPG * TQ), F32),
        ],
        compiler_params=_cparams(("arbitrary", "arbitrary")),
        name="nsa_attention",
    )(qT, kr.reshape(B, S, KR_WIDTH), vT, kc, vcT, gT, _overlap_T(S))


def _gdn_kernel(x_ref, xp_ref, sm_ref, aT_ref, z_ref, cw_ref, alog_ref, dt_ref, alogc_ref, dtc_ref, nw_ref,
                ebd_ref, eg_ref, eb_ref, lblk_ref, lT_ref, o_ref,
                xe_scr, st_scr, *, tg):
    C = GDN_CHUNK
    n_ch = tg // C
    n_pr = GDN_HEADS // 2
    i = pl.program_id(1)

    @pl.when(i == 0)
    def _():
        st_scr[...] = jnp.zeros_like(st_scr)

    sm = sm_ref[...]
    g128 = -jnp.exp(alog_ref[...]) * _softplus(sm + dt_ref[...])
    gcs128 = jnp.dot(lblk_ref[...], g128, precision=HI, preferred_element_type=F32)
    g_all = jnp.dot(gcs128, eg_ref[...], precision=HI, preferred_element_type=F32)
    b_all = jnp.dot(jax.nn.sigmoid(sm).astype(BF16), eb_ref[...], preferred_element_type=F32)
    gT = -jnp.exp(alogc_ref[...]) * _softplus(aT_ref[...] + dtc_ref[...])
    gts = [jnp.dot(gT, lT_ref[c], precision=HI, preferred_element_type=F32) for c in range(n_ch)]

    xe_scr[0:8, :] = jnp.where(i == 0, 0.0, xp_ref[...])
    xe_scr[8:8 + tg, :] = x_ref[...]
    ebd = ebd_ref[...]
    n_split = 2
    rh = tg // n_split
    q_parts, k_parts, v_parts = [], [], []
    for hf in range(n_split):
        base = 8 - (GDN_CONV - 1) + rh * hf
        y = cw_ref[0:1, :] * xe_scr[base:base + rh, :]
        for j in range(1, GDN_CONV):
            y = y + cw_ref[j:j + 1, :] * xe_scr[base + j:base + j + rh, :]
        qkv = y * jax.nn.sigmoid(y)
        q = qkv[:, 0:GDN_WIDTH]
        k = qkv[:, GDN_WIDTH:2 * GDN_WIDTH]
        qss = jnp.dot((q * q).astype(BF16), ebd, preferred_element_type=F32)
        kss = jnp.dot((k * k).astype(BF16), ebd, preferred_element_type=F32)
        q_parts.append(q * lax.rsqrt(qss + 1e-6) * (HD ** -0.5))
        k_parts.append(k * lax.rsqrt(kss + 1e-6))
        v_parts.append(qkv[:, 2 * GDN_WIDTH:])
    q_all = jnp.concatenate(q_parts, axis=0)
    k_all = jnp.concatenate(k_parts, axis=0)
    v_all = jnp.concatenate(v_parts, axis=0)

    ii = lax.broadcasted_iota(jnp.int32, (C, LANES), 0)
    lane = lax.broadcasted_iota(jnp.int32, (C, LANES), 1)
    jj = lane % HD
    eye2 = jnp.where(ii == jj, 1.0, 0.0)
    lo_half = lane < HD
    lane2 = lax.broadcasted_iota(jnp.int32, (LANES, LANES), 1)
    row2 = lax.broadcasted_iota(jnp.int32, (LANES, LANES), 0)
    same_head = (lane2 < HD) == (row2 < HD)

    def bd(x):
        xb = x.astype(BF16)
        zero = jnp.zeros_like(xb)
        return jnp.concatenate([jnp.where(lo_half, xb, zero), jnp.where(lo_half, zero, xb)], axis=0)

    def mm(a, b):
        return jnp.dot(a.astype(BF16), b, preferred_element_type=F32)

    chains = [(c, pr) for c in range(n_ch) for pr in range(n_pr)]

    def blk(arr, c, pr):
        return arr[C * c:C * (c + 1), LANES * pr:LANES * (pr + 1)]

    qg, kdec, dec, aq, vb_bd, kbg_bd, eglast = {}, {}, {}, {}, {}, {}, {}
    for ch in chains:
        c, pr = ch
        qn, k_, gc, be = blk(q_all, c, pr), blk(k_all, c, pr), blk(g_all, c, pr), blk(b_all, c, pr)
        kb = k_ * be
        eg = jnp.exp(gc)
        glast = gc[C - 1:C, :]
        qg[ch] = qn * eg
        kdec[ch] = k_ * jnp.exp(glast - gc)
        eglast[ch] = jnp.exp(glast)
        vb_bd[ch] = bd(blk(v_all, c, pr) * be)
        kbg_bd[ch] = bd(kb * eg)
        gt = gts[c]
        gct = jnp.concatenate([jnp.broadcast_to(gt[2 * pr:2 * pr + 1, :], (C, HD)),
                               jnp.broadcast_to(gt[2 * pr + 1:2 * pr + 2, :], (C, HD))], axis=1)
        dec[ch] = jnp.exp(jnp.where(ii >= jj, gc - gct, NEG_INF))
        aq[ch] = lax.dot_general(jnp.concatenate([kb, qn], axis=0).astype(BF16), bd(k_), NT_DIMS,
                                 preferred_element_type=F32)
    a = {ch: jnp.where(ii > jj, aq[ch][:C] * dec[ch], 0.0) for ch in chains}
    qk = {ch: aq[ch][C:] * dec[ch] for ch in chains}
    x = {ch: eye2 - a[ch] for ch in chains}
    pw = {ch: mm(a[ch], bd(a[ch])) for ch in chains}
    for _ in range(4):
        pbd = {ch: bd(pw[ch]) for ch in chains}
        x = {ch: x[ch] + mm(x[ch], pbd[ch]) for ch in chains}
        pw = {ch: mm(pw[ch], pbd[ch]) for ch in chains}
    x = {ch: x[ch] + mm(x[ch], bd(pw[ch])) for ch in chains}
    uw = {ch: mm(x[ch], jnp.concatenate([vb_bd[ch], kbg_bd[ch]], axis=1)) for ch in chains}

    st = [st_scr[:, LANES * pr:LANES * (pr + 1)] for pr in range(n_pr)]
    o_rows = []
    for c in range(n_ch):
        ws = [mm(jnp.concatenate([uw[c, pr][:, LANES:], qg[c, pr]], axis=0), st[pr].astype(BF16))
              for pr in range(n_pr)]
        v_new = [uw[c, pr][:, :LANES] - ws[pr][:C] for pr in range(n_pr)]
        o_rows.append(jnp.concatenate([ws[pr][C:] + mm(qk[c, pr], bd(v_new[pr])) for pr in range(n_pr)], axis=1))
        upd = [lax.dot_general(kdec[c, pr].astype(BF16), v_new[pr].astype(BF16), TN_DIMS,
                               preferred_element_type=F32) for pr in range(n_pr)]
        st = [st[pr] * eglast[c, pr] + jnp.where(same_head, upd[pr], 0.0) for pr in range(n_pr)]
    st_scr[...] = jnp.concatenate(st, axis=1)

    o = jnp.concatenate(o_rows, axis=0)
    oms = jnp.dot((o * o).astype(BF16), ebd, preferred_element_type=F32) * (1.0 / HD)
    z = z_ref[...]
    o_ref[...] = o * lax.rsqrt(oms + RMS_EPS) * nw_ref[...] * (z * jax.nn.sigmoid(z))


def _gdn(gqkv, sm, aT, gz, conv_w, a_log, dt_bias, norm_w, B, S, tg=256):
    rows = B * S
    nt = S // tg
    n_ch = tg // GDN_CHUNK
    head_of = np.arange(GDN_WIDTH) // HD
    ebd = jnp.asarray(head_of[:, None] == head_of[None, :], dtype=BF16)
    eg = np.zeros((LANES, GDN_WIDTH), np.float32)
    eb = np.zeros((LANES, GDN_WIDTH), np.float32)
    eg[SM_A + head_of, np.arange(GDN_WIDTH)] = 1.0
    eb[SM_B + head_of, np.arange(GDN_WIDTH)] = 1.0
    r = np.arange(tg)
    lblk = ((r[:, None] // GDN_CHUNK == r[None, :] // GDN_CHUNK) & (r[None, :] <= r[:, None])).astype(np.float32)
    lT = np.stack([lblk.T[:, GDN_CHUNK * c:GDN_CHUNK * (c + 1)] for c in range(n_ch)])
    a_log = a_log.astype(F32)
    dt_bias = dt_bias.astype(F32)
    alog128 = jnp.zeros((1, LANES), F32).at[0, SM_A:SM_A + GDN_HEADS].set(a_log)
    dt128 = jnp.zeros((1, LANES), F32).at[0, SM_A:SM_A + GDN_HEADS].set(dt_bias)
    nw512 = jnp.tile(norm_w.astype(F32), GDN_HEADS).reshape(1, GDN_WIDTH)
    full = lambda shape: pl.BlockSpec(shape, lambda b, i: (0,) * len(shape))
    return pl.pallas_call(
        functools.partial(_gdn_kernel, tg=tg),
        grid=(B, nt),
        in_specs=[
            pl.BlockSpec((tg, 3 * GDN_WIDTH), lambda b, i: (b * nt + i, 0)),
            pl.BlockSpec((8, 3 * GDN_WIDTH), lambda b, i: (jnp.maximum((b * nt + i) * (tg // 8) - 1, 0), 0)),
            pl.BlockSpec((tg, LANES), lambda b, i: (b * nt + i, 0)),
            pl.BlockSpec((None, GDN_HEADS, tg), lambda b, i: (b, 0, i)),
            pl.BlockSpec((tg, GDN_WIDTH), lambda b, i: (b * nt + i, 0)),
            full((GDN_CONV, 3 * GDN_WIDTH)),
            full((1, LANES)),
            full((1, LANES)),
            full((GDN_HEADS, 1)),
            full((GDN_HEADS, 1)),
            full((1, GDN_WIDTH)),
            full((GDN_WIDTH, GDN_WIDTH)),
            full((LANES, GDN_WIDTH)),
            full((LANES, GDN_WIDTH)),
            full((tg, tg)),
            full((n_ch, tg, GDN_CHUNK)),
        ],
        out_specs=pl.BlockSpec((tg, GDN_WIDTH), lambda b, i: (b * nt + i, 0)),
        out_shape=jax.ShapeDtypeStruct((rows, GDN_WIDTH), F32),
        scratch_shapes=[
            pltpu.VMEM((tg + 8, 3 * GDN_WIDTH), F32),
            pltpu.VMEM((LANES, GDN_WIDTH), F32),
        ],
        compiler_params=_cparams(("arbitrary", "arbitrary")),
        name="gdn",
    )(gqkv, gqkv, sm, aT, gz, conv_w, alog128, dt128, a_log.reshape(GDN_HEADS, 1), dt_bias.reshape(GDN_HEADS, 1),
      nw512, ebd, jnp.asarray(eg), jnp.asarray(eb, dtype=BF16), jnp.asarray(lblk), jnp.asarray(lT))


def _mix_xattn_kernel(x_ref, a_ref, b_ref, wa_ref, wb_ref, nw_ref, wq_ref, kv_ref, wo_ref, o_ref):
    mix = jnp.dot(a_ref[...].astype(BF16), wa_ref[...], preferred_element_type=F32)
    mix = mix + jnp.dot(b_ref[...].astype(BF16), wb_ref[...], preferred_element_type=F32)
    _xattn_rows(x_ref[...] + mix, nw_ref, wq_ref, kv_ref, wo_ref, o_ref)


def _xattn_kernel(x_ref, nw_ref, wq_ref, kv_ref, wo_ref, o_ref):
    _xattn_rows(x_ref[...], nw_ref, wq_ref, kv_ref, wo_ref, o_ref)


def _xattn_rows(x, nw_ref, wq_ref, kv_ref, wo_ref, o_ref):
    hn = _rms(x, nw_ref[...]).astype(BF16)
    q = jnp.dot(hn, wq_ref[...], preferred_element_type=F32) * (XA_HD ** -0.5)
    heads = range(XA_HEADS)
    ss = [lax.dot_general(q[:, XA_HD * h:XA_HD * (h + 1)].astype(BF16), kv_ref[:, XA_HD * h:XA_HD * (h + 1)],
                          NT_DIMS, preferred_element_type=F32) for h in heads]
    ps = []
    for h in heads:
        e = jnp.exp(ss[h] - jnp.max(ss[h], axis=-1, keepdims=True))
        ps.append((e / jnp.sum(e, axis=-1, keepdims=True)).astype(BF16))
    outs = [jnp.dot(ps[h], kv_ref[:, XA_WIDTH + XA_HD * h:XA_WIDTH + XA_HD * (h + 1)],
                    preferred_element_type=F32) for h in heads]
    o = jnp.concatenate(outs, axis=1).astype(BF16)
    o_ref[...] = x + jnp.dot(o, wo_ref[...], preferred_element_type=F32)


def _xattn(x2d, norm_w, wq_bf, memkv_l, wo_bf, S, tm=512):
    rows = x2d.shape[0]
    per_b = S // tm
    mem_len = memkv_l.shape[0] // (rows // S)
    return pl.pallas_call(
        _xattn_kernel,
        grid=(rows // tm,),
        in_specs=[
            pl.BlockSpec((tm, D_MODEL), lambda i: (i, 0)),
            pl.BlockSpec((1, D_MODEL), lambda i: (0, 0)),
            pl.BlockSpec((D_MODEL, XA_WIDTH), lambda i: (0, 0)),
            pl.BlockSpec((mem_len, 2 * XA_WIDTH), lambda i: (i // per_b, 0)),
            pl.BlockSpec((XA_WIDTH, D_MODEL), lambda i: (0, 0)),
        ],
        out_specs=pl.BlockSpec((tm, D_MODEL), lambda i: (i, 0)),
        out_shape=jax.ShapeDtypeStruct((rows, D_MODEL), F32),
        compiler_params=_cparams(("arbitrary",)),
        name="cross_attention",
    )(x2d, norm_w.reshape(1, D_MODEL), wq_bf, memkv_l, wo_bf)


def _mix_xattn(x2d, o_nsa, o_gdn, w_out_bf, norm_w, wq_bf, memkv_l, wo_bf, S, tm=512):
    rows = x2d.shape[0]
    per_b = S // tm
    mem_len = memkv_l.shape[0] // (rows // S)
    return pl.pallas_call(
        _mix_xattn_kernel,
        grid=(rows // tm,),
        in_specs=[
            pl.BlockSpec((tm, D_MODEL), lambda i: (i, 0)),
            pl.BlockSpec((tm, NSA_WIDTH), lambda i: (i, 0)),
            pl.BlockSpec((tm, GDN_WIDTH), lambda i: (i, 0)),
            pl.BlockSpec((NSA_WIDTH, D_MODEL), lambda i: (0, 0)),
            pl.BlockSpec((GDN_WIDTH, D_MODEL), lambda i: (1, 0)),
            pl.BlockSpec((1, D_MODEL), lambda i: (0, 0)),
            pl.BlockSpec((D_MODEL, XA_WIDTH), lambda i: (0, 0)),
            pl.BlockSpec((mem_len, 2 * XA_WIDTH), lambda i: (i // per_b, 0)),
            pl.BlockSpec((XA_WIDTH, D_MODEL), lambda i: (0, 0)),
        ],
        out_specs=pl.BlockSpec((tm, D_MODEL), lambda i: (i, 0)),
        out_shape=jax.ShapeDtypeStruct((rows, D_MODEL), F32),
        compiler_params=_cparams(("arbitrary",)),
        name="mix_cross_attention",
    )(x2d, o_nsa, o_gdn, w_out_bf, w_out_bf, norm_w.reshape(1, D_MODEL), wq_bf, memkv_l, wo_bf)


def _mlp_kernel(x_ref, nw_ref, w1_ref, w2_ref, fw_ref, o_ref, hn_scr, acc_scr, *, final_norm):
    j = pl.program_id(1)

    @pl.when(j == 0)
    def _():
        x = x_ref[...]
        hn_scr[...] = _rms(x, nw_ref[...]).astype(BF16)
        acc_scr[...] = x

    h = jnp.dot(hn_scr[...], w1_ref[...], preferred_element_type=F32)
    h = jnp.square(jnp.maximum(h, 0.0)).astype(BF16)
    acc_scr[...] += jnp.dot(h, w2_ref[...], preferred_element_type=F32)

    @pl.when(j == pl.num_programs(1) - 1)
    def _():
        y = acc_scr[...]
        if final_norm:
            y = _rms(y, fw_ref[...])
        o_ref[...] = y


def _mlp(x2d, norm_w, w1_bf, w2_bf, final_w, final_norm, tm=1024, tf=1024):
    rows = x2d.shape[0]
    tm = min(tm, rows)
    return pl.pallas_call(
        functools.partial(_mlp_kernel, final_norm=final_norm),
        grid=(rows // tm, D_FF // tf),
        in_specs=[
            pl.BlockSpec((tm, D_MODEL), lambda i, j: (i, 0)),
            pl.BlockSpec((1, D_MODEL), lambda i, j: (0, 0)),
            pl.BlockSpec((D_MODEL, tf), lambda i, j: (0, j)),
            pl.BlockSpec((tf, D_MODEL), lambda i, j: (j, 0)),
            pl.BlockSpec((1, D_MODEL), lambda i, j: (0, 0)),
        ],
        out_specs=pl.BlockSpec((tm, D_MODEL), lambda i, j: (i, 0)),
        out_shape=jax.ShapeDtypeStruct((rows, D_MODEL), F32),
        scratch_shapes=[pltpu.VMEM((tm, D_MODEL), BF16), pltpu.VMEM((tm, D_MODEL), F32)],
        compiler_params=_cparams(("arbitrary", "arbitrary")),
        name="mlp",
    )(x2d, norm_w.reshape(1, D_MODEL), w1_bf, w2_bf, final_w.reshape(1, D_MODEL))


def _sc_kernel(x_ref, xp_ref, nw_ref, win_ref, cw_ref, wout_ref, o_ref, cu_scr, *, tm, per_b):
    i = pl.program_id(0)
    x = x_ref[...]
    nw = nw_ref[...]
    y = jnp.dot(_rms(x, nw).astype(BF16), win_ref[...], preferred_element_type=F32)
    yp = jnp.dot(_rms(xp_ref[...], nw).astype(BF16), win_ref[:, D_MODEL:], preferred_element_type=F32)
    cu_prev = yp[:, :D_MODEL] * yp[:, D_MODEL:]
    cu_scr[0:8, :] = jnp.where(i % per_b == 0, 0.0, cu_prev)
    cu_scr[8:8 + tm, :] = y[:, D_MODEL:2 * D_MODEL] * y[:, 2 * D_MODEL:]
    conv = cw_ref[0:1, :] * cu_scr[8 - 2:8 - 2 + tm, :]
    for j in range(1, SC_WIDTH):
        conv = conv + cw_ref[j:j + 1, :] * cu_scr[8 - 2 + j:8 - 2 + j + tm, :]
    gated = (y[:, :D_MODEL] * conv).astype(BF16)
    o_ref[...] = x + jnp.dot(gated, wout_ref[...], preferred_element_type=F32)


def _sc_mixer(x2d, norm_w, win_bf, conv_w, wout_bf, S, tm=512):
    rows = x2d.shape[0]
    per_b = S // tm
    return pl.pallas_call(
        functools.partial(_sc_kernel, tm=tm, per_b=per_b),
        grid=(rows // tm,),
        in_specs=[
            pl.BlockSpec((tm, D_MODEL), lambda i: (i, 0)),
            pl.BlockSpec((8, D_MODEL), lambda i: (jnp.maximum(i * (tm // 8) - 1, 0), 0)),
            pl.BlockSpec((1, D_MODEL), lambda i: (0, 0)),
            pl.BlockSpec((D_MODEL, 3 * D_MODEL), lambda i: (0, 0)),
            pl.BlockSpec((SC_WIDTH, D_MODEL), lambda i: (0, 0)),
            pl.BlockSpec((D_MODEL, D_MODEL), lambda i: (0, 0)),
        ],
        out_specs=pl.BlockSpec((tm, D_MODEL), lambda i: (i, 0)),
        out_shape=jax.ShapeDtypeStruct((rows, D_MODEL), F32),
        scratch_shapes=[pltpu.VMEM((tm + 8, D_MODEL), F32)],
        compiler_params=_cparams(("arbitrary",)),
        name="short_conv_mixer",
    )(x2d, x2d, norm_w.reshape(1, D_MODEL), win_bf, conv_w, wout_bf)


def _hybrid_mixer(x2d, positions, norm_w, w_in, ck_pos, ck_w1, ck_w2, cv_pos, cv_w1, cv_w2,
                  gdn_conv, gdn_a_log, gdn_dt_bias, gdn_norm, B, S):
    tm = 512
    invf = (ROPE_THETA ** (-jnp.arange(0, HD, 2, dtype=F32) / HD)).reshape(HD // 2, 1)
    pos3 = positions.astype(F32).reshape(B * S // tm, 1, tm)
    qT, kr, vT, ckv, gqkv, gz, sm, gT, aT = _proj0(x2d, norm_w, _prep_w_in(w_in), pos3, invf, B, S, tm)
    w1s, p8, w2s = _prep_compress_weights(ck_pos, ck_w1, ck_w2, cv_pos, cv_w1, cv_w2)
    kc, vcT = _compress(ckv, w1s, p8, w2s, B, S)
    o_nsa = _nsa(qT, kr, vT, kc, vcT, gT, B, S)
    o_gdn = _gdn(gqkv, sm, aT, gz, gdn_conv, gdn_a_log, gdn_dt_bias, gdn_norm, B, S)
    return o_nsa, o_gdn


def kernel(x, mem, positions, norm_mix, norm_xattn, norm_mlp, hyb_w_in, hyb_cmp_k_pos, hyb_cmp_k_w1, hyb_cmp_k_w2, hyb_cmp_v_pos, hyb_cmp_v_w1, hyb_cmp_v_w2, hyb_gdn_conv, hyb_gdn_a_log, hyb_gdn_dt_bias, hyb_gdn_norm, hyb_w_out, sc_w_in, sc_conv, sc_w_out, mem_norm, xa_wq, xa_wkv, xa_wo, mlp_w1, mlp_w2, final_norm):
    B, S, _ = x.shape
    depth = norm_mix.shape[0]
    x2d = x.reshape(B * S, D_MODEL)
    memkv = _memkv(mem.reshape(-1, D_MODEL), mem_norm, xa_wkv.astype(BF16))
    for layer in range(depth):
        j = layer // 2
        xa = (norm_xattn[layer], xa_wq[layer].astype(BF16), memkv[layer], xa_wo[layer].astype(BF16), S)
        if layer % 2 == 0:
            o_nsa, o_gdn = _hybrid_mixer(x2d, positions, norm_mix[layer], hyb_w_in[j], hyb_cmp_k_pos[j],
                                         hyb_cmp_k_w1[j], hyb_cmp_k_w2[j], hyb_cmp_v_pos[j], hyb_cmp_v_w1[j],
                                         hyb_cmp_v_w2[j], hyb_gdn_conv[j], hyb_gdn_a_log[j], hyb_gdn_dt_bias[j],
                                         hyb_gdn_norm[j], B, S)
            x2d = _mix_xattn(x2d, o_nsa, o_gdn, hyb_w_out[j].astype(BF16), *xa)
        else:
            x2d = _sc_mixer(x2d, norm_mix[layer], sc_w_in[j].astype(BF16), sc_conv[j], sc_w_out[j].astype(BF16), S)
            x2d = _xattn(x2d, *xa)
        x2d = _mlp(x2d, norm_mlp[layer], mlp_w1[layer].astype(BF16), mlp_w2[layer].astype(BF16),
                   final_norm, layer == depth - 1)
    return x2d.reshape(B, S, D_MODEL)
```

```python
import functools

import numpy as np
import jax
import jax.numpy as jnp
from jax import lax
from jax.experimental import pallas as pl
from jax.experimental.pallas import tpu as pltpu

F32 = jnp.float32
BF16 = jnp.bfloat16
HI = lax.Precision.HIGHEST

D_MODEL = 1024
MEM_LEN = 256
RMS_EPS = 1e-6
ROPE_THETA = 10000.0
NEG_INF = -1e30
FORCE_BONUS = 1e3
LOG2E = 1.4426950408889634

NSA_HEADS = 8
NSA_KV_GROUPS = 2
NSA_HPG = NSA_HEADS // NSA_KV_GROUPS
HD = 64
CMP_LEN = 32
CMP_STRIDE = 16
CMP_HID = 2 * HD
SLC_BLK = 64
SLC_TOPK = 16
WINDOW = 512

GDN_HEADS = 8
GDN_CONV = 4
GDN_CHUNK = 64
SC_WIDTH = 3
XA_HEADS = 4
XA_HD = 128
D_FF = 4 * D_MODEL

NSA_WIDTH = NSA_HEADS * HD
NSA_KV_WIDTH = NSA_KV_GROUPS * HD
GDN_WIDTH = GDN_HEADS * HD
XA_WIDTH = XA_HEADS * XA_HD
IN_SIZES = (NSA_WIDTH,) + (NSA_KV_WIDTH,) * 6 + (3 * NSA_HEADS,) + (GDN_WIDTH,) * 3 + (GDN_HEADS, GDN_HEADS, GDN_WIDTH)

V7X_VMEM_BYTES = 64 * 1024 * 1024
VMEM_LIMIT = V7X_VMEM_BYTES * 3 // 4
LANES = 128
F32_SUBLANES = 8
BF16_ROWS = 16

C_Q, C_KR, C_VT, C_CKV, C_GQKV, C_GZ, C_SM, NC0 = 0, 512, 768, 1024, 1280, 2816, 3328, 3456
SM_GATE, SM_A, SM_B = 0, 24, 32
NSA_CHUNK = 256
KR_WIDTH = 3 * LANES
KR_WIN = 2 * LANES

NT_DIMS = (((1,), (1,)), ((), ()))
TN_DIMS = (((0,), (0,)), ((), ()))


def _cparams(sem):
    return pltpu.CompilerParams(dimension_semantics=sem, vmem_limit_bytes=VMEM_LIMIT)


def _rms(x, w):
    return x * lax.rsqrt(jnp.mean(x * x, axis=-1, keepdims=True) + RMS_EPS) * w


def _softplus(x):
    return jnp.maximum(x, 0.0) + jnp.log(1.0 + jnp.exp(-jnp.abs(x)))


def _memkv_kernel(m_ref, nw_ref, w_ref, o_ref):
    hn = _rms(m_ref[...], nw_ref[...]).astype(BF16)
    o_ref[...] = jnp.dot(hn, w_ref[...], preferred_element_type=F32).astype(BF16)


def _memkv(mem2d, mem_norm, wkv_bf):
    rows = mem2d.shape[0]
    depth = wkv_bf.shape[0]
    tm = 512
    return pl.pallas_call(
        _memkv_kernel,
        grid=(depth, rows // tm),
        in_specs=[
            pl.BlockSpec((tm, D_MODEL), lambda l, i: (i, 0)),
            pl.BlockSpec((1, D_MODEL), lambda l, i: (0, 0)),
            pl.BlockSpec((None, D_MODEL, 2 * XA_WIDTH), lambda l, i: (l, 0, 0)),
        ],
        out_specs=pl.BlockSpec((None, tm, 2 * XA_WIDTH), lambda l, i: (l, i, 0)),
        out_shape=jax.ShapeDtypeStruct((depth, rows, 2 * XA_WIDTH), BF16),
        compiler_params=_cparams(("arbitrary", "arbitrary")),
        name="memkv",
    )(mem2d, mem_norm.reshape(1, D_MODEL), wkv_bf)


def _proj0_kernel(x_ref, nw_ref, w_ref, pos_ref, invf_ref,
                  qT_ref, kr_ref, vT_ref, ckv_ref, gqkv_ref, gz_ref, sm_ref, gT_ref, aT_ref):
    tm = x_ref.shape[0]
    hn = _rms(x_ref[...], nw_ref[...]).astype(BF16)
    y = jnp.dot(hn, w_ref[...], preferred_element_type=F32)

    ang = invf_ref[...] * pos_ref[...]
    c = jnp.cos(ang)
    s = jnp.sin(ang)
    cos_n = jnp.concatenate([c, c, c, c], axis=0).T
    sin_n = jnp.concatenate([-s, s, -s, s], axis=0).T
    lane = lax.broadcasted_iota(jnp.int32, (tm, LANES), 1)
    first_half = (lane % HD) < (HD // 2)

    def rope(xc):
        rot = jnp.where(first_half, pltpu.roll(xc, LANES - HD // 2, 1), pltpu.roll(xc, HD // 2, 1))
        return xc * cos_n + rot * sin_n

    roped = [rope(y[:, LANES * i:LANES * (i + 1)]) for i in range(6)]
    q = jnp.concatenate(roped[:4], axis=1) * (LOG2E * HD ** -0.5)
    qT_ref[...] = q.T.astype(BF16)
    row = lax.broadcasted_iota(jnp.int32, (tm, LANES), 0)
    blk_hot = jnp.where(lane - HD == (row // SLC_BLK) % (NSA_CHUNK // SLC_BLK), 1.0, 0.0)
    k_slc = roped[4]
    lo = lane < HD
    kr_ref[...] = jnp.concatenate([jnp.where(lo, k_slc, blk_hot),
                                   jnp.where(lo, pltpu.roll(k_slc, HD, 1), blk_hot),
                                   roped[5]], axis=1).astype(BF16)
    vT_ref[...] = y[:, C_VT:C_CKV].T.astype(BF16)
    for i in range(4):
        ckv_ref[i] = y[:, C_CKV + HD * i:C_CKV + HD * (i + 1)].astype(BF16)
    gqkv_ref[...] = y[:, C_GQKV:C_GZ]
    gz_ref[...] = y[:, C_GZ:C_SM]
    sm = y[:, C_SM:NC0]
    sm_ref[...] = sm
    gT_ref[...] = jax.nn.sigmoid(sm).T[:3 * NSA_HEADS, :]
    aT_ref[...] = sm.T[SM_A:SM_A + GDN_HEADS, :]


def _prep_w_in(w_in):
    offs = np.cumsum((0,) + IN_SIZES)
    (nq, kcmp, vcmp, kslc, vslc, kwin, vwin, ngate, gq, gk, gv, ga, gb, gz) = [
        w_in[:, offs[i]:offs[i + 1]] for i in range(len(IN_SIZES))]
    pad = jnp.zeros((D_MODEL, LANES - 3 * NSA_HEADS - 2 * GDN_HEADS), w_in.dtype)
    small = jnp.concatenate([ngate, ga, gb, pad], axis=1)
    return jnp.concatenate([nq, kslc, kwin, vslc, vwin, kcmp, vcmp, gq, gk, gv, gz, small], axis=1).astype(BF16)


def _proj0(x2d, norm_w, w_bf, pos3, invf, B, S, tm):
    rows = B * S
    nt = S // tm
    row_spec = lambda n: pl.BlockSpec((tm, n), lambda i: (i, 0))
    t_spec = lambda n: pl.BlockSpec((None, n, tm), lambda i: (i // nt, 0, i % nt))
    return pl.pallas_call(
        _proj0_kernel,
        grid=(rows // tm,),
        in_specs=[
            row_spec(D_MODEL),
            pl.BlockSpec((1, D_MODEL), lambda i: (0, 0)),
            pl.BlockSpec((D_MODEL, NC0), lambda i: (0, 0)),
            pl.BlockSpec((None, 1, tm), lambda i: (i, 0, 0)),
            pl.BlockSpec((HD // 2, 1), lambda i: (0, 0)),
        ],
        out_specs=[
            t_spec(NSA_WIDTH),
            row_spec(KR_WIDTH),
            t_spec(2 * NSA_KV_WIDTH),
            pl.BlockSpec((4, tm, HD), lambda i: (0, i, 0)),
            row_spec(3 * GDN_WIDTH),
            row_spec(GDN_WIDTH),
            row_spec(LANES),
            t_spec(3 * NSA_HEADS),
            t_spec(GDN_HEADS),
        ],
        out_shape=[
            jax.ShapeDtypeStruct((B, NSA_WIDTH, S), BF16),
            jax.ShapeDtypeStruct((rows, KR_WIDTH), BF16),
            jax.ShapeDtypeStruct((B, 2 * NSA_KV_WIDTH, S), BF16),
            jax.ShapeDtypeStruct((4, rows, HD), BF16),
            jax.ShapeDtypeStruct((rows, 3 * GDN_WIDTH), F32),
            jax.ShapeDtypeStruct((rows, GDN_WIDTH), F32),
            jax.ShapeDtypeStruct((rows, LANES), F32),
            jax.ShapeDtypeStruct((B, 3 * NSA_HEADS, S), F32),
            jax.ShapeDtypeStruct((B, GDN_HEADS, S), F32),
        ],
        compiler_params=_cparams(("arbitrary",)),
        name="proj0",
    )(x2d, norm_w.reshape(1, D_MODEL), w_bf, pos3, invf)


def _compress_kernel(t_ref, w1_ref, pos_ref, w2_ref, o_ref, oT_ref):
    nseg = t_ref.shape[0]
    w1 = w1_ref[...]
    ab = jnp.dot(t_ref[...], w1, preferred_element_type=F32)
    pc = jnp.dot(pos_ref[...].astype(BF16), w1, preferred_element_type=F32)
    bias = pc[0:1, :CMP_HID] + pc[1:2, CMP_HID:]
    hid = ab[:, :CMP_HID] + pltpu.roll(ab[:, CMP_HID:], nseg - 1, 0) + bias
    act = hid * jax.nn.sigmoid(hid)
    out = jnp.dot(act.astype(BF16), w2_ref[...], preferred_element_type=F32)
    o_ref[...] = out
    oT_ref[...] = out.T[:HD, :]


def _compress(ckv, w1s, pos8, w2s, B, S):
    nseg = S // CMP_STRIDE
    t = ckv.reshape(4, B, nseg, CMP_STRIDE * HD)
    return pl.pallas_call(
        _compress_kernel,
        grid=(4, B),
        in_specs=[
            pl.BlockSpec((None, None, nseg, CMP_STRIDE * HD), lambda k, b: (k, b, 0, 0)),
            pl.BlockSpec((None, CMP_STRIDE * HD, 2 * CMP_HID), lambda k, b: (k // 2, 0, 0)),
            pl.BlockSpec((None, F32_SUBLANES, CMP_STRIDE * HD), lambda k, b: (k // 2, 0, 0)),
            pl.BlockSpec((None, CMP_HID, LANES), lambda k, b: (k // 2, 0, 0)),
        ],
        out_specs=[
            pl.BlockSpec((None, None, nseg, LANES), lambda k, b: (k, b, 0, 0)),
            pl.BlockSpec((None, None, HD, nseg), lambda k, b: (k, b, 0, 0)),
        ],
        out_shape=[
            jax.ShapeDtypeStruct((4, B, nseg, LANES), F32),
            jax.ShapeDtypeStruct((4, B, HD, nseg), F32),
        ],
        compiler_params=_cparams(("arbitrary", "arbitrary")),
        name="nsa_compress",
    )(t, w1s, pos8, w2s)


def _prep_compress_weights(k_pos, k_w1, k_w2, v_pos, v_w1, v_w2):
    half = CMP_STRIDE * HD

    def w1cat(w1):
        return jnp.concatenate([w1[:half], w1[half:]], axis=1)

    def pos8(p):
        flat = p.reshape(2, half)
        return jnp.concatenate([flat, jnp.zeros((F32_SUBLANES - 2, half), p.dtype)], axis=0)

    def w2pad(w2):
        return jnp.concatenate([w2, jnp.zeros((CMP_HID, LANES - HD), w2.dtype)], axis=1)

    w1s = jnp.stack([w1cat(k_w1), w1cat(v_w1)]).astype(BF16)
    p8 = jnp.stack([pos8(k_pos), pos8(v_pos)])
    w2s = jnp.stack([w2pad(k_w2), w2pad(v_w2)]).astype(BF16)
    return w1s, p8, w2s


def _nsa_kernel(qT_ref, kr_ref, vT_ref, kc_ref, vcT_ref, gT_ref, ovT_ref, o_ref,
                imp_scr, sel_scr, s_scr, p_scr, b_scr, *, S, TQ):
    CH = TQ
    groups = range(NSA_KV_GROUPS)
    n_cmp = S // CMP_STRIDE
    n_slc = S // SLC_BLK
    n_sel = min(SLC_TOPK, n_slc)
    qi = pl.program_id(1)
    q0 = qi * TQ
    t1 = q0 + lax.broadcasted_iota(jnp.int32, (1, TQ), 1)
    t4 = jnp.concatenate([t1] * NSA_HPG, axis=1)
    n_io = lax.broadcasted_iota(jnp.int32, (n_cmp, 1), 0)
    j_io = lax.broadcasted_iota(jnp.int32, (n_slc, 1), 0)
    k_io = lax.broadcasted_iota(jnp.int32, (CH, 1), 0)
    n_causal = (q0 + TQ) // SLC_BLK

    qs, q64s = [], []
    for g in groups:
        q64 = jnp.concatenate(
            [qT_ref[NSA_HPG * HD * g + HD * h:NSA_HPG * HD * g + HD * (h + 1), :] for h in range(NSA_HPG)], axis=1)
        zq = jnp.zeros_like(q64)
        qs.append(jnp.concatenate([q64, zq] if g == 0 else [zq, q64], axis=0))
        q64s.append(q64)

    s_cmp = [jnp.dot(kc_ref[g, :, 0:HD].astype(BF16), q64s[g], preferred_element_type=F32)
             for g in groups]
    valid = (n_io * CMP_STRIDE + (CMP_LEN - 1)) <= t4
    any_valid = t4 >= CMP_LEN - 1
    p_cmp, psum = [], []
    for g in groups:
        s = jnp.where(valid, s_cmp[g], NEG_INF)
        m = jnp.max(s, axis=0, keepdims=True)
        e = jnp.exp2(s - m)
        l = jnp.sum(e, axis=0, keepdims=True)
        p = e * jnp.where(any_valid, 1.0 / l, 0.0)
        p_cmp.append(p.astype(BF16))
        ph = p[:, 0:TQ]
        for h in range(1, NSA_HPG):
            ph = ph + p[:, h * TQ:(h + 1) * TQ]
        psum.append(ph)
    o_cmp = [jnp.dot(vcT_ref[g].astype(BF16), p_cmp[g], preferred_element_type=F32) for g in groups]
    imps = [jnp.dot(ovT_ref[...], psum[g], precision=HI, preferred_element_type=F32) for g in groups]
    cur = t1 // SLC_BLK
    forced = (j_io == 0) | (j_io == cur) | (j_io == cur - 1)
    causal = j_io <= cur
    vals = [jnp.where(causal, jnp.where(forced, imps[g] + FORCE_BONUS, imps[g]), -1.0) for g in groups]
    for g in groups:
        imp_scr[g] = vals[g]

    def rank_body(i2, cnts):
        out = list(cnts)
        for i in (2 * i2, 2 * i2 + 1):
            tie = jnp.where(j_io > i, 1.0, 0.0)
            for g in groups:
                vi = imp_scr[g, pl.ds(i, 1), :]
                out[g] = out[g] + jnp.where(vi > vals[g], 1.0, 0.0) + jnp.where(vi == vals[g], tie, 0.0)
        return tuple(out)

    cnts = lax.fori_loop(0, jnp.where(n_causal > n_sel, n_causal // 2, 0), rank_body,
                         tuple(jnp.zeros((n_slc, TQ), F32) for _ in groups))
    n_lanes = NSA_HPG * TQ
    nb = CH // SLC_BLK
    for g in groups:
        selb = jnp.where(cnts[g] < float(n_sel), 0.0, NEG_INF)
        for jc in range(S // CH):
            rows = jnp.concatenate([selb[nb * jc:nb * (jc + 1), :], jnp.zeros((BF16_ROWS - nb, TQ), F32)], axis=0)
            sel_scr[g, jc] = jnp.concatenate([rows] * NSA_HPG, axis=1).astype(BF16)

    ones_rows = jnp.ones((BF16_ROWS, CH), BF16)
    zero_rows = jnp.zeros((HD - BF16_ROWS, n_lanes), BF16)
    acc_rows = HD + BF16_ROWS

    def slc_scores(jc):
        r0 = pl.multiple_of(jc * CH, CH)
        return [jnp.dot(kr_ref[pl.ds(r0, CH), LANES * g:LANES * (g + 1)],
                        jnp.concatenate([q64s[g], sel_scr[g, jc], zero_rows], axis=0),
                        preferred_element_type=F32) for g in groups]

    def values(jc, vrow):
        r0 = pl.multiple_of(jc * CH, CH)
        return [jnp.concatenate([vT_ref[vrow + HD * g:vrow + HD * (g + 1), pl.ds(r0, CH)], ones_rows], axis=0)
                for g in groups]

    def probs(sc, bias, m_i):
        if bias is not None:
            sc = sc + jnp.concatenate([bias] * NSA_HPG, axis=1)
        m_new = jnp.maximum(m_i, jnp.max(sc, axis=0, keepdims=True))
        return m_new, jnp.exp2(m_i - m_new), jnp.exp2(sc - m_new).astype(BF16)

    def pv(vt, pr):
        return jnp.dot(vt, pr, preferred_element_type=F32)

    def m_init():
        return jnp.full((1, n_lanes), NEG_INF, F32)

    def acc_init():
        return jnp.zeros((acc_rows, n_lanes), F32)

    u_io = lax.broadcasted_iota(jnp.int32, (1, TQ), 1)
    causal_bias = jnp.where(k_io <= u_io, 0.0, NEG_INF)
    window_bias = jnp.where(k_io > u_io, 0.0, NEG_INF)

    def slc_half(jc, slot, ms):
        nxt = slc_scores(jc + 1)
        v_prev = values(jnp.maximum(jc - 1, 0), 0)
        pvs = [pv(v_prev[g], p_scr[g]) for g in groups]
        for g in groups:
            s_scr[1 - slot, g] = nxt[g]
        out = []
        for g in groups:
            m_new, alpha, pr = probs(s_scr[slot, g], None, ms[g])
            b_scr[g] = (b_scr[g] + pvs[g]) * alpha
            p_scr[g] = pr
            out.append(m_new)
        return tuple(out)

    first = slc_scores(0)
    for g in groups:
        s_scr[0, g] = first[g]
        p_scr[g] = jnp.zeros((CH, n_lanes), BF16)
        b_scr[g] = acc_init()
    ms = lax.fori_loop(0, qi // 2, lambda i, m: slc_half(2 * i + 1, 1, slc_half(2 * i, 0, m)),
                       tuple(m_init() for _ in groups))

    def slc_odd(_, m):
        m = slc_half(qi - 1, 0, m)
        s_scr[0] = s_scr[1]
        return m

    ms = lax.fori_loop(0, qi % 2, slc_odd, ms)

    n_back = WINDOW // CH
    n_win = n_back + 1
    w0 = jnp.maximum(qi - n_back, 0)
    rw = pl.multiple_of(w0 * CH, CH)
    v_prev = values(jnp.maximum(qi - 1, 0), 0)
    pvs = [pv(v_prev[g], p_scr[g]) for g in groups]
    k_win = kr_ref[pl.ds(rw, n_win * CH), KR_WIN:KR_WIN + LANES]
    s_win = [jnp.dot(k_win, qs[g], preferred_element_type=F32) for g in groups]
    hidden = jnp.full((CH, TQ), NEG_INF, F32)
    slab_bias = []
    for c in range(n_win):
        d = w0 + c - qi
        slab_bias.append(jnp.where(d == 0, causal_bias,
                                   jnp.where(d > 0, hidden, jnp.where(d == -n_back, window_bias, 0.0))))
    win_bias = jnp.concatenate(slab_bias, axis=0)
    slc_parts = [probs(s_scr[0, g], causal_bias, ms[g]) for g in groups]
    win_parts = [probs(s_win[g], win_bias, m_init()) for g in groups]
    v_slc = values(qi, 0)
    v_win = [jnp.concatenate([vT_ref[NSA_KV_WIDTH + HD * g:NSA_KV_WIDTH + HD * (g + 1), pl.ds(rw, n_win * CH)],
                              jnp.ones((BF16_ROWS, n_win * CH), BF16)], axis=0) for g in groups]
    o_slc, o_win = [], []
    for g in groups:
        acc = (b_scr[g] + pvs[g]) * slc_parts[g][1] + pv(v_slc[g], slc_parts[g][2])
        o_slc.append(acc[:HD] / acc[HD:HD + 1])
        acc = pv(v_win[g], win_parts[g][2])
        o_win.append(acc[:HD] / acc[HD:HD + 1])

    for g in groups:
        rows = []
        for h in range(NSA_HPG):
            hh = NSA_HPG * g + h
            sl = slice(h * TQ, (h + 1) * TQ)
            rows.append(gT_ref[3 * hh:3 * hh + 1, :] * o_cmp[g][:, sl]
                        + gT_ref[3 * hh + 1:3 * hh + 2, :] * o_slc[g][:, sl]
                        + gT_ref[3 * hh + 2:3 * hh + 3, :] * o_win[g][:, sl])
        og = jnp.concatenate(rows, axis=0)
        o_ref[:, NSA_HPG * HD * g:NSA_HPG * HD * (g + 1)] = og.T


def _overlap_T(S):
    n_cmp = S // CMP_STRIDE
    n_slc = S // SLC_BLK
    cs = np.arange(n_cmp)[:, None] * CMP_STRIDE
    js = np.arange(n_slc)[None, :] * SLC_BLK
    ov = np.clip(np.minimum(cs + CMP_LEN, js + SLC_BLK) - np.maximum(cs, js), 0, None) / CMP_LEN
    ov[n_cmp - 1] = 0.0
    return jnp.asarray(ov.T, dtype=F32)


def _nsa(qT, kr, vT, kc, vcT, gT, B, S, TQ=NSA_CHUNK):
    n_cmp = S // CMP_STRIDE
    n_slc = S // SLC_BLK
    nq = S // TQ
    return pl.pallas_call(
        functools.partial(_nsa_kernel, S=S, TQ=TQ),
        grid=(B, nq),
        in_specs=[
            pl.BlockSpec((None, NSA_WIDTH, TQ), lambda b, i: (b, 0, i)),
            pl.BlockSpec((None, S, KR_WIDTH), lambda b, i: (b, 0, 0)),
            pl.BlockSpec((None, 2 * NSA_KV_WIDTH, S), lambda b, i: (b, 0, 0)),
            pl.BlockSpec((2, None, n_cmp, LANES), lambda b, i: (0, b, 0, 0)),
            pl.BlockSpec((2, None, HD, n_cmp), lambda b, i: (1, b, 0, 0)),
            pl.BlockSpec((None, 3 * NSA_HEADS, TQ), lambda b, i: (b, 0, i)),
            pl.BlockSpec((n_slc, n_cmp), lambda b, i: (0, 0)),
        ],
        out_specs=pl.BlockSpec((TQ, NSA_WIDTH), lambda b, i: (b * nq + i, 0)),
        out_shape=jax.ShapeDtypeStruct((B * S, NSA_WIDTH), F32),
        scratch_shapes=[
            pltpu.VMEM((NSA_KV_GROUPS, n_slc, TQ), F32),
            pltpu.VMEM((NSA_KV_GROUPS, S // TQ, BF16_ROWS, NSA_HPG * TQ), BF16),
            pltpu.VMEM((2, NSA_KV_GROUPS, TQ, NSA_HPG * TQ), F32),
            pltpu.VMEM((NSA_KV_GROUPS, TQ, NSA_HPG * TQ), BF16),
            pltpu.VMEM((NSA_KV_GROUPS, HD + BF16_ROWS, NSA_HPG * TQ), F32),
        ],
        compiler_params=_cparams(("arbitrary", "arbitrary")),
        name="nsa_attention",
    )(qT, kr.reshape(B, S, KR_WIDTH), vT, kc, vcT, gT, _overlap_T(S))


def _gdn_kernel(x_ref, xp_ref, sm_ref, aT_ref, z_ref, cw_ref, alog_ref, dt_ref, alogc_ref, dtc_ref, nw_ref,
                ebd_ref, eg_ref, eb_ref, lblk_ref, lT_ref, o_ref,
                xe_scr, st_scr, *, tg):
    C = GDN_CHUNK
    n_ch = tg // C
    n_pr = GDN_HEADS // 2
    i = pl.program_id(1)

    @pl.when(i == 0)
    def _():
        st_scr[...] = jnp.zeros_like(st_scr)

    sm = sm_ref[...]
    g128 = -jnp.exp(alog_ref[...]) * _softplus(sm + dt_ref[...])
    gcs128 = jnp.dot(lblk_ref[...], g128, precision=HI, preferred_element_type=F32)
    g_all = jnp.dot(gcs128, eg_ref[...], precision=HI, preferred_element_type=F32)
    b_all = jnp.dot(jax.nn.sigmoid(sm).astype(BF16), eb_ref[...], preferred_element_type=F32)
    gT = -jnp.exp(alogc_ref[...]) * _softplus(aT_ref[...] + dtc_ref[...])
    gts = [jnp.dot(gT, lT_ref[c], precision=HI, preferred_element_type=F32) for c in range(n_ch)]

    halo = F32_SUBLANES
    xe_scr[0:halo, :] = jnp.where(i == 0, 0.0, xp_ref[...])
    xe_scr[halo:halo + tg, :] = x_ref[...]
    ebd = ebd_ref[...]
    n_split = 2
    rh = tg // n_split
    q_parts, k_parts, v_parts = [], [], []
    for hf in range(n_split):
        base = halo - (GDN_CONV - 1) + rh * hf
        y = cw_ref[0:1, :] * xe_scr[base:base + rh, :]
        for j in range(1, GDN_CONV):
            y = y + cw_ref[j:j + 1, :] * xe_scr[base + j:base + j + rh, :]
        qkv = y * jax.nn.sigmoid(y)
        q = qkv[:, 0:GDN_WIDTH]
        k = qkv[:, GDN_WIDTH:2 * GDN_WIDTH]
        qss = jnp.dot((q * q).astype(BF16), ebd, preferred_element_type=F32)
        kss = jnp.dot((k * k).astype(BF16), ebd, preferred_element_type=F32)
        q_parts.append(q * lax.rsqrt(qss + 1e-6) * (HD ** -0.5))
        k_parts.append(k * lax.rsqrt(kss + 1e-6))
        v_parts.append(qkv[:, 2 * GDN_WIDTH:])
    q_all = jnp.concatenate(q_parts, axis=0)
    k_all = jnp.concatenate(k_parts, axis=0)
    v_all = jnp.concatenate(v_parts, axis=0)

    ii = lax.broadcasted_iota(jnp.int32, (C, LANES), 0)
    lane = lax.broadcasted_iota(jnp.int32, (C, LANES), 1)
    jj = lane % HD
    eye2 = jnp.where(ii == jj, 1.0, 0.0)
    lo_half = lane < HD
    lane2 = lax.broadcasted_iota(jnp.int32, (LANES, LANES), 1)
    row2 = lax.broadcasted_iota(jnp.int32, (LANES, LANES), 0)
    same_head = (lane2 < HD) == (row2 < HD)

    def bd(x):
        xb = x.astype(BF16)
        zero = jnp.zeros_like(xb)
        return jnp.concatenate([jnp.where(lo_half, xb, zero), jnp.where(lo_half, zero, xb)], axis=0)

    def mm(a, b):
        return jnp.dot(a.astype(BF16), b, preferred_element_type=F32)

    chains = [(c, pr) for c in range(n_ch) for pr in range(n_pr)]

    def blk(arr, c, pr):
        return arr[C * c:C * (c + 1), LANES * pr:LANES * (pr + 1)]

    qg, kdec, dec, aq, vb_bd, kbg_bd, eglast = {}, {}, {}, {}, {}, {}, {}
    for ch in chains:
        c, pr = ch
        qn, k_, gc, be = blk(q_all, c, pr), blk(k_all, c, pr), blk(g_all, c, pr), blk(b_all, c, pr)
        kb = k_ * be
        eg = jnp.exp(gc)
        glast = gc[C - 1:C, :]
        qg[ch] = qn * eg
        kdec[ch] = k_ * jnp.exp(glast - gc)
        eglast[ch] = jnp.exp(glast)
        vb_bd[ch] = bd(blk(v_all, c, pr) * be)
        kbg_bd[ch] = bd(kb * eg)
        gt = gts[c]
        gct = jnp.concatenate([jnp.broadcast_to(gt[2 * pr:2 * pr + 1, :], (C, HD)),
                               jnp.broadcast_to(gt[2 * pr + 1:2 * pr + 2, :], (C, HD))], axis=1)
        dec[ch] = jnp.exp(jnp.where(ii >= jj, gc - gct, NEG_INF))
        aq[ch] = lax.dot_general(jnp.concatenate([kb, qn], axis=0).astype(BF16), bd(k_), NT_DIMS,
                                 preferred_element_type=F32)
    a = {ch: jnp.where(ii > jj, aq[ch][:C] * dec[ch], 0.0) for ch in chains}
    qk = {ch: aq[ch][C:] * dec[ch] for ch in chains}
    x = {ch: eye2 - a[ch] for ch in chains}
    pw = {ch: mm(a[ch], bd(a[ch])) for ch in chains}
    for _ in range(4):
        pbd = {ch: bd(pw[ch]) for ch in chains}
        x = {ch: x[ch] + mm(x[ch], pbd[ch]) for ch in chains}
        pw = {ch: mm(pw[ch], pbd[ch]) for ch in chains}
    x = {ch: x[ch] + mm(x[ch], bd(pw[ch])) for ch in chains}
    uw = {ch: mm(x[ch], jnp.concatenate([vb_bd[ch], kbg_bd[ch]], axis=1)) for ch in chains}

    st = [st_scr[:, LANES * pr:LANES * (pr + 1)] for pr in range(n_pr)]
    o_rows = []
    for c in range(n_ch):
        ws = [mm(jnp.concatenate([uw[c, pr][:, LANES:], qg[c, pr]], axis=0), st[pr].astype(BF16))
              for pr in range(n_pr)]
        v_new = [uw[c, pr][:, :LANES] - ws[pr][:C] for pr in range(n_pr)]
        o_rows.append(jnp.concatenate([ws[pr][C:] + mm(qk[c, pr], bd(v_new[pr])) for pr in range(n_pr)], axis=1))
        upd = [lax.dot_general(kdec[c, pr].astype(BF16), v_new[pr].astype(BF16), TN_DIMS,
                               preferred_element_type=F32) for pr in range(n_pr)]
        st = [st[pr] * eglast[c, pr] + jnp.where(same_head, upd[pr], 0.0) for pr in range(n_pr)]
    st_scr[...] = jnp.concatenate(st, axis=1)

    o = jnp.concatenate(o_rows, axis=0)
    oms = jnp.dot((o * o).astype(BF16), ebd, preferred_element_type=F32) * (1.0 / HD)
    z = z_ref[...]
    o_ref[...] = o * lax.rsqrt(oms + RMS_EPS) * nw_ref[...] * (z * jax.nn.sigmoid(z))


def _gdn(gqkv, sm, aT, gz, conv_w, a_log, dt_bias, norm_w, B, S, tg=256):
    rows = B * S
    nt = S // tg
    n_ch = tg // GDN_CHUNK
    head_of = np.arange(GDN_WIDTH) // HD
    ebd = jnp.asarray(head_of[:, None] == head_of[None, :], dtype=BF16)
    eg = np.zeros((LANES, GDN_WIDTH), np.float32)
    eb = np.zeros((LANES, GDN_WIDTH), np.float32)
    eg[SM_A + head_of, np.arange(GDN_WIDTH)] = 1.0
    eb[SM_B + head_of, np.arange(GDN_WIDTH)] = 1.0
    r = np.arange(tg)
    lblk = ((r[:, None] // GDN_CHUNK == r[None, :] // GDN_CHUNK) & (r[None, :] <= r[:, None])).astype(np.float32)
    lT = np.stack([lblk.T[:, GDN_CHUNK * c:GDN_CHUNK * (c + 1)] for c in range(n_ch)])
    a_log = a_log.astype(F32)
    dt_bias = dt_bias.astype(F32)
    alog128 = jnp.zeros((1, LANES), F32).at[0, SM_A:SM_A + GDN_HEADS].set(a_log)
    dt128 = jnp.zeros((1, LANES), F32).at[0, SM_A:SM_A + GDN_HEADS].set(dt_bias)
    nw512 = jnp.tile(norm_w.astype(F32), GDN_HEADS).reshape(1, GDN_WIDTH)
    full = lambda shape: pl.BlockSpec(shape, lambda b, i: (0,) * len(shape))
    return pl.pallas_call(
        functools.partial(_gdn_kernel, tg=tg),
        grid=(B, nt),
        in_specs=[
            pl.BlockSpec((tg, 3 * GDN_WIDTH), lambda b, i: (b * nt + i, 0)),
            pl.BlockSpec((F32_SUBLANES, 3 * GDN_WIDTH),
                         lambda b, i: (jnp.maximum((b * nt + i) * (tg // F32_SUBLANES) - 1, 0), 0)),
            pl.BlockSpec((tg, LANES), lambda b, i: (b * nt + i, 0)),
            pl.BlockSpec((None, GDN_HEADS, tg), lambda b, i: (b, 0, i)),
            pl.BlockSpec((tg, GDN_WIDTH), lambda b, i: (b * nt + i, 0)),
            full((GDN_CONV, 3 * GDN_WIDTH)),
            full((1, LANES)),
            full((1, LANES)),
            full((GDN_HEADS, 1)),
            full((GDN_HEADS, 1)),
            full((1, GDN_WIDTH)),
            full((GDN_WIDTH, GDN_WIDTH)),
            full((LANES, GDN_WIDTH)),
            full((LANES, GDN_WIDTH)),
            full((tg, tg)),
            full((n_ch, tg, GDN_CHUNK)),
        ],
        out_specs=pl.BlockSpec((tg, GDN_WIDTH), lambda b, i: (b * nt + i, 0)),
        out_shape=jax.ShapeDtypeStruct((rows, GDN_WIDTH), F32),
        scratch_shapes=[
            pltpu.VMEM((tg + F32_SUBLANES, 3 * GDN_WIDTH), F32),
            pltpu.VMEM((LANES, GDN_WIDTH), F32),
        ],
        compiler_params=_cparams(("arbitrary", "arbitrary")),
        name="gdn",
    )(gqkv, gqkv, sm, aT, gz, conv_w, alog128, dt128, a_log.reshape(GDN_HEADS, 1), dt_bias.reshape(GDN_HEADS, 1),
      nw512, ebd, jnp.asarray(eg), jnp.asarray(eb, dtype=BF16), jnp.asarray(lblk), jnp.asarray(lT))


def _mix_xattn_kernel(x_ref, a_ref, b_ref, wa_ref, wb_ref, nw_ref, wq_ref, kv_ref, wo_ref, o_ref):
    mix = jnp.dot(a_ref[...].astype(BF16), wa_ref[...], preferred_element_type=F32)
    mix = mix + jnp.dot(b_ref[...].astype(BF16), wb_ref[...], preferred_element_type=F32)
    _xattn_rows(x_ref[...] + mix, nw_ref, wq_ref, kv_ref, wo_ref, o_ref)


def _xattn_kernel(x_ref, nw_ref, wq_ref, kv_ref, wo_ref, o_ref):
    _xattn_rows(x_ref[...], nw_ref, wq_ref, kv_ref, wo_ref, o_ref)


def _xattn_rows(x, nw_ref, wq_ref, kv_ref, wo_ref, o_ref):
    hn = _rms(x, nw_ref[...]).astype(BF16)
    q = jnp.dot(hn, wq_ref[...], preferred_element_type=F32) * (XA_HD ** -0.5)
    heads = range(XA_HEADS)
    ss = [lax.dot_general(q[:, XA_HD * h:XA_HD * (h + 1)].astype(BF16), kv_ref[:, XA_HD * h:XA_HD * (h + 1)],
                          NT_DIMS, preferred_element_type=F32) for h in heads]
    es = [jnp.exp(ss[h] - jnp.max(ss[h], axis=-1, keepdims=True)).astype(BF16) for h in heads]
    ones = jnp.ones((kv_ref.shape[0], XA_HD), BF16)
    pvs = [jnp.dot(es[h], jnp.concatenate([kv_ref[:, XA_WIDTH + XA_HD * h:XA_WIDTH + XA_HD * (h + 1)], ones], axis=1),
                   preferred_element_type=F32) for h in heads]
    o = jnp.concatenate([pvs[h][:, :XA_HD] / pvs[h][:, XA_HD:XA_HD + 1] for h in heads], axis=1).astype(BF16)
    o_ref[...] = x + jnp.dot(o, wo_ref[...], preferred_element_type=F32)


def _xattn(x2d, norm_w, wq_bf, memkv_l, wo_bf, S, tm=512):
    rows = x2d.shape[0]
    per_b = S // tm
    mem_len = memkv_l.shape[0] // (rows // S)
    return pl.pallas_call(
        _xattn_kernel,
        grid=(rows // tm,),
        in_specs=[
            pl.BlockSpec((tm, D_MODEL), lambda i: (i, 0)),
            pl.BlockSpec((1, D_MODEL), lambda i: (0, 0)),
            pl.BlockSpec((D_MODEL, XA_WIDTH), lambda i: (0, 0)),
            pl.BlockSpec((mem_len, 2 * XA_WIDTH), lambda i: (i // per_b, 0)),
            pl.BlockSpec((XA_WIDTH, D_MODEL), lambda i: (0, 0)),
        ],
        out_specs=pl.BlockSpec((tm, D_MODEL), lambda i: (i, 0)),
        out_shape=jax.ShapeDtypeStruct((rows, D_MODEL), F32),
        compiler_params=_cparams(("arbitrary",)),
        name="cross_attention",
    )(x2d, norm_w.reshape(1, D_MODEL), wq_bf, memkv_l, wo_bf)


def _mix_xattn(x2d, o_nsa, o_gdn, w_out_bf, norm_w, wq_bf, memkv_l, wo_bf, S, tm=512):
    rows = x2d.shape[0]
    per_b = S // tm
    mem_len = memkv_l.shape[0] // (rows // S)
    return pl.pallas_call(
        _mix_xattn_kernel,
        grid=(rows // tm,),
        in_specs=[
            pl.BlockSpec((tm, D_MODEL), lambda i: (i, 0)),
            pl.BlockSpec((tm, NSA_WIDTH), lambda i: (i, 0)),
            pl.BlockSpec((tm, GDN_WIDTH), lambda i: (i, 0)),
            pl.BlockSpec((NSA_WIDTH, D_MODEL), lambda i: (0, 0)),
            pl.BlockSpec((GDN_WIDTH, D_MODEL), lambda i: (1, 0)),
            pl.BlockSpec((1, D_MODEL), lambda i: (0, 0)),
            pl.BlockSpec((D_MODEL, XA_WIDTH), lambda i: (0, 0)),
            pl.BlockSpec((mem_len, 2 * XA_WIDTH), lambda i: (i // per_b, 0)),
            pl.BlockSpec((XA_WIDTH, D_MODEL), lambda i: (0, 0)),
        ],
        out_specs=pl.BlockSpec((tm, D_MODEL), lambda i: (i, 0)),
        out_shape=jax.ShapeDtypeStruct((rows, D_MODEL), F32),
        compiler_params=_cparams(("arbitrary",)),
        name="mix_cross_attention",
    )(x2d, o_nsa, o_gdn, w_out_bf, w_out_bf, norm_w.reshape(1, D_MODEL), wq_bf, memkv_l, wo_bf)


def _mlp_kernel(x_ref, nw_ref, w1_ref, w2_ref, fw_ref, o_ref, hn_scr, acc_scr, *, final_norm):
    j = pl.program_id(1)

    @pl.when(j == 0)
    def _():
        x = x_ref[...]
        hn_scr[...] = _rms(x, nw_ref[...]).astype(BF16)
        acc_scr[...] = x

    h = jnp.dot(hn_scr[...], w1_ref[...], preferred_element_type=F32)
    h = jnp.square(jnp.maximum(h, 0.0)).astype(BF16)
    acc_scr[...] += jnp.dot(h, w2_ref[...], preferred_element_type=F32)

    @pl.when(j == pl.num_programs(1) - 1)
    def _():
        y = acc_scr[...]
        if final_norm:
            y = _rms(y, fw_ref[...])
        o_ref[...] = y


def _mlp(x2d, norm_w, w1_bf, w2_bf, final_w, final_norm, tm=1024, tf=1024):
    rows = x2d.shape[0]
    tm = min(tm, rows)
    return pl.pallas_call(
        functools.partial(_mlp_kernel, final_norm=final_norm),
        grid=(rows // tm, D_FF // tf),
        in_specs=[
            pl.BlockSpec((tm, D_MODEL), lambda i, j: (i, 0)),
            pl.BlockSpec((1, D_MODEL), lambda i, j: (0, 0)),
            pl.BlockSpec((D_MODEL, tf), lambda i, j: (0, j)),
            pl.BlockSpec((tf, D_MODEL), lambda i, j: (j, 0)),
            pl.BlockSpec((1, D_MODEL), lambda i, j: (0, 0)),
        ],
        out_specs=pl.BlockSpec((tm, D_MODEL), lambda i, j: (i, 0)),
        out_shape=jax.ShapeDtypeStruct((rows, D_MODEL), F32),
        scratch_shapes=[pltpu.VMEM((tm, D_MODEL), BF16), pltpu.VMEM((tm, D_MODEL), F32)],
        compiler_params=_cparams(("arbitrary", "arbitrary")),
        name="mlp",
    )(x2d, norm_w.reshape(1, D_MODEL), w1_bf, w2_bf, final_w.reshape(1, D_MODEL))


def _sc_kernel(x_ref, xp_ref, nw_ref, win_ref, cw_ref, wout_ref, o_ref, cu_scr, *, tm, per_b):
    i = pl.program_id(0)
    x = x_ref[...]
    nw = nw_ref[...]
    halo = F32_SUBLANES
    y = jnp.dot(_rms(x, nw).astype(BF16), win_ref[...], preferred_element_type=F32)
    yp = jnp.dot(_rms(xp_ref[...], nw).astype(BF16), win_ref[:, D_MODEL:], preferred_element_type=F32)
    cu_scr[0:halo, :] = jnp.where(i % per_b == 0, 0.0, yp[:, :D_MODEL] * yp[:, D_MODEL:])
    cu_scr[halo:halo + tm, :] = y[:, D_MODEL:2 * D_MODEL] * y[:, 2 * D_MODEL:]
    base = halo - (SC_WIDTH - 1)
    conv = cw_ref[0:1, :] * cu_scr[base:base + tm, :]
    for j in range(1, SC_WIDTH):
        conv = conv + cw_ref[j:j + 1, :] * cu_scr[base + j:base + j + tm, :]
    gated = (y[:, :D_MODEL] * conv).astype(BF16)
    o_ref[...] = x + jnp.dot(gated, wout_ref[...], preferred_element_type=F32)


def _sc_mixer(x2d, norm_w, win_bf, conv_w, wout_bf, S, tm=512):
    rows = x2d.shape[0]
    per_b = S // tm
    return pl.pallas_call(
        functools.partial(_sc_kernel, tm=tm, per_b=per_b),
        grid=(rows // tm,),
        in_specs=[
            pl.BlockSpec((tm, D_MODEL), lambda i: (i, 0)),
            pl.BlockSpec((F32_SUBLANES, D_MODEL), lambda i: (jnp.maximum(i * (tm // F32_SUBLANES) - 1, 0), 0)),
            pl.BlockSpec((1, D_MODEL), lambda i: (0, 0)),
            pl.BlockSpec((D_MODEL, 3 * D_MODEL), lambda i: (0, 0)),
            pl.BlockSpec((SC_WIDTH, D_MODEL), lambda i: (0, 0)),
            pl.BlockSpec((D_MODEL, D_MODEL), lambda i: (0, 0)),
        ],
        out_specs=pl.BlockSpec((tm, D_MODEL), lambda i: (i, 0)),
        out_shape=jax.ShapeDtypeStruct((rows, D_MODEL), F32),
        scratch_shapes=[pltpu.VMEM((tm + F32_SUBLANES, D_MODEL), F32)],
        compiler_params=_cparams(("arbitrary",)),
        name="short_conv_mixer",
    )(x2d, x2d, norm_w.reshape(1, D_MODEL), win_bf, conv_w, wout_bf)


def _hybrid_mixer(x2d, positions, norm_w, w_in, ck_pos, ck_w1, ck_w2, cv_pos, cv_w1, cv_w2,
                  gdn_conv, gdn_a_log, gdn_dt_bias, gdn_norm, B, S):
    tm = 512
    invf = (ROPE_THETA ** (-jnp.arange(0, HD, 2, dtype=F32) / HD)).reshape(HD // 2, 1)
    pos3 = positions.astype(F32).reshape(B * S // tm, 1, tm)
    qT, kr, vT, ckv, gqkv, gz, sm, gT, aT = _proj0(x2d, norm_w, _prep_w_in(w_in), pos3, invf, B, S, tm)
    w1s, p8, w2s = _prep_compress_weights(ck_pos, ck_w1, ck_w2, cv_pos, cv_w1, cv_w2)
    kc, vcT = _compress(ckv, w1s, p8, w2s, B, S)
    o_nsa = _nsa(qT, kr, vT, kc, vcT, gT, B, S)
    o_gdn = _gdn(gqkv, sm, aT, gz, gdn_conv, gdn_a_log, gdn_dt_bias, gdn_norm, B, S)
    return o_nsa, o_gdn


def kernel(x, mem, positions, norm_mix, norm_xattn, norm_mlp, hyb_w_in, hyb_cmp_k_pos, hyb_cmp_k_w1, hyb_cmp_k_w2, hyb_cmp_v_pos, hyb_cmp_v_w1, hyb_cmp_v_w2, hyb_gdn_conv, hyb_gdn_a_log, hyb_gdn_dt_bias, hyb_gdn_norm, hyb_w_out, sc_w_in, sc_conv, sc_w_out, mem_norm, xa_wq, xa_wkv, xa_wo, mlp_w1, mlp_w2, final_norm):
    B, S, _ = x.shape
    depth = norm_mix.shape[0]
    x2d = x.reshape(B * S, D_MODEL)
    memkv = _memkv(mem.reshape(-1, D_MODEL), mem_norm, xa_wkv.astype(BF16))
    for layer in range(depth):
        j = layer // 2
        xa = (norm_xattn[layer], xa_wq[layer].astype(BF16), memkv[layer], xa_wo[layer].astype(BF16), S)
        if layer % 2 == 0:
            o_nsa, o_gdn = _hybrid_mixer(x2d, positions, norm_mix[layer], hyb_w_in[j], hyb_cmp_k_pos[j],
                                         hyb_cmp_k_w1[j], hyb_cmp_k_w2[j], hyb_cmp_v_pos[j], hyb_cmp_v_w1[j],
                                         hyb_cmp_v_w2[j], hyb_gdn_conv[j], hyb_gdn_a_log[j], hyb_gdn_dt_bias[j],
                                         hyb_gdn_norm[j], B, S)
            x2d = _mix_xattn(x2d, o_nsa, o_gdn, hyb_w_out[j].astype(BF16), *xa)
        else:
            x2d = _sc_mixer(x2d, norm_mix[layer], sc_w_in[j].astype(BF16), sc_conv[j], sc_w_out[j].astype(BF16), S)
            x2d = _xattn(x2d, *xa)
        x2d = _mlp(x2d, norm_mlp[layer], mlp_w1[layer].astype(BF16), mlp_w2[layer].astype(BF16),
                   final_norm, layer == depth - 1)
    return x2d.reshape(B, S, D_MODEL)
```

```python
import functools

import numpy as np
import jax
import jax.numpy as jnp
from jax import lax
from jax.experimental import pallas as pl
from jax.experimental.pallas import tpu as pltpu

F32 = jnp.float32
BF16 = jnp.bfloat16
HI = lax.Precision.HIGHEST

D_MODEL = 1024
MEM_LEN = 256
RMS_EPS = 1e-6
ROPE_THETA = 10000.0
NEG_INF = -1e30
FORCE_BONUS = 1e3
LOG2E = 1.4426950408889634

NSA_HEADS = 8
NSA_KV_GROUPS = 2
NSA_HPG = NSA_HEADS // NSA_KV_GROUPS
HD = 64
CMP_LEN = 32
CMP_STRIDE = 16
CMP_HID = 2 * HD
SLC_BLK = 64
SLC_TOPK = 16
WINDOW = 512

GDN_HEADS = 8
GDN_CONV = 4
GDN_CHUNK = 64
SC_WIDTH = 3
XA_HEADS = 4
XA_HD = 128
D_FF = 4 * D_MODEL

NSA_WIDTH = NSA_HEADS * HD
NSA_KV_WIDTH = NSA_KV_GROUPS * HD
GDN_WIDTH = GDN_HEADS * HD
XA_WIDTH = XA_HEADS * XA_HD
IN_SIZES = (NSA_WIDTH,) + (NSA_KV_WIDTH,) * 6 + (3 * NSA_HEADS,) + (GDN_WIDTH,) * 3 + (GDN_HEADS, GDN_HEADS, GDN_WIDTH)

V7X_VMEM_BYTES = 64 * 1024 * 1024
VMEM_LIMIT = V7X_VMEM_BYTES * 3 // 4
LANES = 128
F32_SUBLANES = 8
BF16_ROWS = 16

C_Q, C_KR, C_VT, C_CKV, C_GQKV, C_GZ, C_SM, NC0 = 0, 512, 768, 1024, 1280, 2816, 3328, 3456
SM_GATE, SM_A, SM_B = 0, 24, 32
NSA_CHUNK = 256
KR_WIDTH = 3 * LANES
KR_WIN = 2 * LANES

NT_DIMS = (((1,), (1,)), ((), ()))
TN_DIMS = (((0,), (0,)), ((), ()))


def _cparams(sem):
    return pltpu.CompilerParams(dimension_semantics=sem, vmem_limit_bytes=VMEM_LIMIT)


def _rms(x, w):
    return x * lax.rsqrt(jnp.mean(x * x, axis=-1, keepdims=True) + RMS_EPS) * w


def _softplus(x):
    return jnp.maximum(x, 0.0) + jnp.log(1.0 + jnp.exp(-jnp.abs(x)))


def _memkv_kernel(m_ref, nw_ref, w_ref, o_ref):
    hn = _rms(m_ref[...], nw_ref[...]).astype(BF16)
    o_ref[...] = jnp.dot(hn, w_ref[...], preferred_element_type=F32).astype(BF16)


def _memkv(mem2d, mem_norm, wkv_bf):
    rows = mem2d.shape[0]
    depth = wkv_bf.shape[0]
    tm = 512
    return pl.pallas_call(
        _memkv_kernel,
        grid=(depth, rows // tm),
        in_specs=[
            pl.BlockSpec((tm, D_MODEL), lambda l, i: (i, 0)),
            pl.BlockSpec((1, D_MODEL), lambda l, i: (0, 0)),
            pl.BlockSpec((None, D_MODEL, 2 * XA_WIDTH), lambda l, i: (l, 0, 0)),
        ],
        out_specs=pl.BlockSpec((None, tm, 2 * XA_WIDTH), lambda l, i: (l, i, 0)),
        out_shape=jax.ShapeDtypeStruct((depth, rows, 2 * XA_WIDTH), BF16),
        compiler_params=_cparams(("arbitrary", "arbitrary")),
        name="memkv",
    )(mem2d, mem_norm.reshape(1, D_MODEL), wkv_bf)


def _proj0_kernel(x_ref, nw_ref, w_ref, pos_ref, invf_ref,
                  qT_ref, kr_ref, vT_ref, ckv_ref, gqkv_ref, gz_ref, sm_ref, gT_ref, aT_ref):
    tm = x_ref.shape[0]
    hn = _rms(x_ref[...], nw_ref[...]).astype(BF16)
    y = jnp.dot(hn, w_ref[...], preferred_element_type=F32)

    ang = invf_ref[...] * pos_ref[...]
    c = jnp.cos(ang)
    s = jnp.sin(ang)
    cos_n = jnp.concatenate([c, c, c, c], axis=0).T
    sin_n = jnp.concatenate([-s, s, -s, s], axis=0).T
    lane = lax.broadcasted_iota(jnp.int32, (tm, LANES), 1)
    first_half = (lane % HD) < (HD // 2)

    def rope(xc):
        rot = jnp.where(first_half, pltpu.roll(xc, LANES - HD // 2, 1), pltpu.roll(xc, HD // 2, 1))
        return xc * cos_n + rot * sin_n

    roped = [rope(y[:, LANES * i:LANES * (i + 1)]) for i in range(6)]
    q = jnp.concatenate(roped[:4], axis=1) * (LOG2E * HD ** -0.5)
    qT_ref[...] = q.T.astype(BF16)
    row = lax.broadcasted_iota(jnp.int32, (tm, LANES), 0)
    blk_hot = jnp.where(lane - HD == (row // SLC_BLK) % (NSA_CHUNK // SLC_BLK), 1.0, 0.0)
    k_slc = roped[4]
    lo = lane < HD
    kr_ref[...] = jnp.concatenate([jnp.where(lo, k_slc, blk_hot),
                                   jnp.where(lo, pltpu.roll(k_slc, HD, 1), blk_hot),
                                   roped[5]], axis=1).astype(BF16)
    vT_ref[...] = y[:, C_VT:C_CKV].T.astype(BF16)
    for i in range(4):
        ckv_ref[i] = y[:, C_CKV + HD * i:C_CKV + HD * (i + 1)].astype(BF16)
    gqkv_ref[...] = y[:, C_GQKV:C_GZ]
    gz_ref[...] = y[:, C_GZ:C_SM]
    sm = y[:, C_SM:NC0]
    sm_ref[...] = sm
    gT_ref[...] = jax.nn.sigmoid(sm).T[:3 * NSA_HEADS, :]
    aT_ref[...] = sm.T[SM_A:SM_A + GDN_HEADS, :]


def _prep_w_in(w_in):
    offs = np.cumsum((0,) + IN_SIZES)
    (nq, kcmp, vcmp, kslc, vslc, kwin, vwin, ngate, gq, gk, gv, ga, gb, gz) = [
        w_in[:, offs[i]:offs[i + 1]] for i in range(len(IN_SIZES))]
    pad = jnp.zeros((D_MODEL, LANES - 3 * NSA_HEADS - 2 * GDN_HEADS), w_in.dtype)
    small = jnp.concatenate([ngate, ga, gb, pad], axis=1)
    return jnp.concatenate([nq, kslc, kwin, vslc, vwin, kcmp, vcmp, gq, gk, gv, gz, small], axis=1).astype(BF16)


def _proj0(x2d, norm_w, w_bf, pos3, invf, B, S, tm):
    rows = B * S
    nt = S // tm
    row_spec = lambda n: pl.BlockSpec((tm, n), lambda i: (i, 0))
    t_spec = lambda n: pl.BlockSpec((None, n, tm), lambda i: (i // nt, 0, i % nt))
    return pl.pallas_call(
        _proj0_kernel,
        grid=(rows // tm,),
        in_specs=[
            row_spec(D_MODEL),
            pl.BlockSpec((1, D_MODEL), lambda i: (0, 0)),
            pl.BlockSpec((D_MODEL, NC0), lambda i: (0, 0)),
            pl.BlockSpec((None, 1, tm), lambda i: (i, 0, 0)),
            pl.BlockSpec((HD // 2, 1), lambda i: (0, 0)),
        ],
        out_specs=[
            t_spec(NSA_WIDTH),
            row_spec(KR_WIDTH),
            t_spec(2 * NSA_KV_WIDTH),
            pl.BlockSpec((4, tm, HD), lambda i: (0, i, 0)),
            row_spec(3 * GDN_WIDTH),
            row_spec(GDN_WIDTH),
            row_spec(LANES),
            t_spec(3 * NSA_HEADS),
            t_spec(GDN_HEADS),
        ],
        out_shape=[
            jax.ShapeDtypeStruct((B, NSA_WIDTH, S), BF16),
            jax.ShapeDtypeStruct((rows, KR_WIDTH), BF16),
            jax.ShapeDtypeStruct((B, 2 * NSA_KV_WIDTH, S), BF16),
            jax.ShapeDtypeStruct((4, rows, HD), BF16),
            jax.ShapeDtypeStruct((rows, 3 * GDN_WIDTH), F32),
            jax.ShapeDtypeStruct((rows, GDN_WIDTH), F32),
            jax.ShapeDtypeStruct((rows, LANES), F32),
            jax.ShapeDtypeStruct((B, 3 * NSA_HEADS, S), F32),
            jax.ShapeDtypeStruct((B, GDN_HEADS, S), F32),
        ],
        compiler_params=_cparams(("arbitrary",)),
        name="proj0",
    )(x2d, norm_w.reshape(1, D_MODEL), w_bf, pos3, invf)


def _compress_kernel(t_ref, w1_ref, pos_ref, w2_ref, o_ref, oT_ref):
    nseg = t_ref.shape[0]
    w1 = w1_ref[...]
    ab = jnp.dot(t_ref[...], w1, preferred_element_type=F32)
    pc = jnp.dot(pos_ref[...].astype(BF16), w1, preferred_element_type=F32)
    bias = pc[0:1, :CMP_HID] + pc[1:2, CMP_HID:]
    hid = ab[:, :CMP_HID] + pltpu.roll(ab[:, CMP_HID:], nseg - 1, 0) + bias
    act = hid * jax.nn.sigmoid(hid)
    out = jnp.dot(act.astype(BF16), w2_ref[...], preferred_element_type=F32)
    o_ref[...] = out
    oT_ref[...] = out.T[:HD, :]


def _compress(ckv, w1s, pos8, w2s, B, S):
    nseg = S // CMP_STRIDE
    t = ckv.reshape(4, B, nseg, CMP_STRIDE * HD)
    return pl.pallas_call(
        _compress_kernel,
        grid=(4, B),
        in_specs=[
            pl.BlockSpec((None, None, nseg, CMP_STRIDE * HD), lambda k, b: (k, b, 0, 0)),
            pl.BlockSpec((None, CMP_STRIDE * HD, 2 * CMP_HID), lambda k, b: (k // 2, 0, 0)),
            pl.BlockSpec((None, F32_SUBLANES, CMP_STRIDE * HD), lambda k, b: (k // 2, 0, 0)),
            pl.BlockSpec((None, CMP_HID, LANES), lambda k, b: (k // 2, 0, 0)),
        ],
        out_specs=[
            pl.BlockSpec((None, None, nseg, LANES), lambda k, b: (k, b, 0, 0)),
            pl.BlockSpec((None, None, HD, nseg), lambda k, b: (k, b, 0, 0)),
        ],
        out_shape=[
            jax.ShapeDtypeStruct((4, B, nseg, LANES), F32),
            jax.ShapeDtypeStruct((4, B, HD, nseg), F32),
        ],
        compiler_params=_cparams(("arbitrary", "arbitrary")),
        name="nsa_compress",
    )(t, w1s, pos8, w2s)


def _prep_compress_weights(k_pos, k_w1, k_w2, v_pos, v_w1, v_w2):
    half = CMP_STRIDE * HD

    def w1cat(w1):
        return jnp.concatenate([w1[:half], w1[half:]], axis=1)

    def pos8(p):
        flat = p.reshape(2, half)
        return jnp.concatenate([flat, jnp.zeros((F32_SUBLANES - 2, half), p.dtype)], axis=0)

    def w2pad(w2):
        return jnp.concatenate([w2, jnp.zeros((CMP_HID, LANES - HD), w2.dtype)], axis=1)

    w1s = jnp.stack([w1cat(k_w1), w1cat(v_w1)]).astype(BF16)
    p8 = jnp.stack([pos8(k_pos), pos8(v_pos)])
    w2s = jnp.stack([w2pad(k_w2), w2pad(v_w2)]).astype(BF16)
    return w1s, p8, w2s


def _nsa_kernel(qT_ref, kr_ref, vT_ref, kc_ref, vcT_ref, gT_ref, ovT_ref, o_ref,
                imp_scr, sel_scr, s_scr, p_scr, b_scr, *, S, TQ):
    CH = TQ
    groups = range(NSA_KV_GROUPS)
    n_cmp = S // CMP_STRIDE
    n_slc = S // SLC_BLK
    n_sel = min(SLC_TOPK, n_slc)
    qi = pl.program_id(1)
    q0 = qi * TQ
    t1 = q0 + lax.broadcasted_iota(jnp.int32, (1, TQ), 1)
    t4 = jnp.concatenate([t1] * NSA_HPG, axis=1)
    n_io = lax.broadcasted_iota(jnp.int32, (n_cmp, 1), 0)
    j_io = lax.broadcasted_iota(jnp.int32, (n_slc, 1), 0)
    k_io = lax.broadcasted_iota(jnp.int32, (CH, 1), 0)
    n_causal = (q0 + TQ) // SLC_BLK

    qs, q64s = [], []
    for g in groups:
        q64 = jnp.concatenate(
            [qT_ref[NSA_HPG * HD * g + HD * h:NSA_HPG * HD * g + HD * (h + 1), :] for h in range(NSA_HPG)], axis=1)
        zq = jnp.zeros_like(q64)
        qs.append(jnp.concatenate([q64, zq] if g == 0 else [zq, q64], axis=0))
        q64s.append(q64)

    s_cmp = [jnp.dot(kc_ref[g, :, 0:HD].astype(BF16), q64s[g], preferred_element_type=F32)
             for g in groups]
    valid = (n_io * CMP_STRIDE + (CMP_LEN - 1)) <= t4
    any_valid = t4 >= CMP_LEN - 1
    p_cmp, psum = [], []
    for g in groups:
        s = jnp.where(valid, s_cmp[g], NEG_INF)
        m = jnp.max(s, axis=0, keepdims=True)
        e = jnp.exp2(s - m)
        l = jnp.sum(e, axis=0, keepdims=True)
        p = e * jnp.where(any_valid, 1.0 / l, 0.0)
        p_cmp.append(p.astype(BF16))
        ph = p[:, 0:TQ]
        for h in range(1, NSA_HPG):
            ph = ph + p[:, h * TQ:(h + 1) * TQ]
        psum.append(ph)
    o_cmp = [jnp.dot(vcT_ref[g].astype(BF16), p_cmp[g], preferred_element_type=F32) for g in groups]
    imps = [jnp.dot(ovT_ref[...], psum[g], precision=HI, preferred_element_type=F32) for g in groups]
    cur = t1 // SLC_BLK
    forced = (j_io == 0) | (j_io == cur) | (j_io == cur - 1)
    causal = j_io <= cur
    vals = [jnp.where(causal, jnp.where(forced, imps[g] + FORCE_BONUS, imps[g]), -1.0) for g in groups]
    for g in groups:
        imp_scr[g] = vals[g]

    def rank_body(i2, cnts):
        out = list(cnts)
        for i in (2 * i2, 2 * i2 + 1):
            tie = jnp.where(j_io > i, 1.0, 0.0)
            for g in groups:
                vi = imp_scr[g, pl.ds(i, 1), :]
                out[g] = out[g] + jnp.where(vi > vals[g], 1.0, 0.0) + jnp.where(vi == vals[g], tie, 0.0)
        return tuple(out)

    cnts = lax.fori_loop(0, jnp.where(n_causal > n_sel, n_causal // 2, 0), rank_body,
                         tuple(jnp.zeros((n_slc, TQ), F32) for _ in groups))
    n_lanes = NSA_HPG * TQ
    nb = CH // SLC_BLK
    for g in groups:
        selb = jnp.where(cnts[g] < float(n_sel), 0.0, NEG_INF)
        for jc in range(S // CH):
            rows = jnp.concatenate([selb[nb * jc:nb * (jc + 1), :], jnp.zeros((BF16_ROWS - nb, TQ), F32)], axis=0)
            sel_scr[g, jc] = jnp.concatenate([rows] * NSA_HPG, axis=1).astype(BF16)

    ones_rows = jnp.ones((BF16_ROWS, CH), BF16)
    zero_rows = jnp.zeros((HD - BF16_ROWS, n_lanes), BF16)
    acc_rows = HD + BF16_ROWS

    def slc_scores(jc):
        r0 = pl.multiple_of(jc * CH, CH)
        return [jnp.dot(kr_ref[pl.ds(r0, CH), LANES * g:LANES * (g + 1)],
                        jnp.concatenate([q64s[g], sel_scr[g, jc], zero_rows], axis=0),
                        preferred_element_type=F32) for g in groups]

    def values(jc, vrow):
        r0 = pl.multiple_of(jc * CH, CH)
        return [jnp.concatenate([vT_ref[vrow + HD * g:vrow + HD * (g + 1), pl.ds(r0, CH)], ones_rows], axis=0)
                for g in groups]

    def probs(sc, bias, m_i):
        if bias is not None:
            sc = sc + jnp.concatenate([bias] * NSA_HPG, axis=1)
        m_new = jnp.maximum(m_i, jnp.max(sc, axis=0, keepdims=True))
        return m_new, jnp.exp2(m_i - m_new), jnp.exp2(sc - m_new).astype(BF16)

    def pv(vt, pr):
        return jnp.dot(vt, pr, preferred_element_type=F32)

    def m_init():
        return jnp.full((1, n_lanes), NEG_INF, F32)

    def acc_init():
        return jnp.zeros((acc_rows, n_lanes), F32)

    u_io = lax.broadcasted_iota(jnp.int32, (1, TQ), 1)
    causal_bias = jnp.where(k_io <= u_io, 0.0, NEG_INF)
    window_bias = jnp.where(k_io > u_io, 0.0, NEG_INF)

    def slc_half(jc, slot, ms):
        nxt = slc_scores(jc + 1)
        v_prev = values(jnp.maximum(jc - 1, 0), 0)
        pvs = [pv(v_prev[g], p_scr[g]) for g in groups]
        for g in groups:
            s_scr[1 - slot, g] = nxt[g]
        out = []
        for g in groups:
            m_new, alpha, pr = probs(s_scr[slot, g], None, ms[g])
            b_scr[g] = (b_scr[g] + pvs[g]) * alpha
            p_scr[g] = pr
            out.append(m_new)
        return tuple(out)

    first = slc_scores(0)
    for g in groups:
        s_scr[0, g] = first[g]
        p_scr[g] = jnp.zeros((CH, n_lanes), BF16)
        b_scr[g] = acc_init()
    ms = lax.fori_loop(0, qi // 2, lambda i, m: slc_half(2 * i + 1, 1, slc_half(2 * i, 0, m)),
                       tuple(m_init() for _ in groups))

    def slc_odd(_, m):
        m = slc_half(qi - 1, 0, m)
        s_scr[0] = s_scr[1]
        return m

    ms = lax.fori_loop(0, qi % 2, slc_odd, ms)

    n_back = WINDOW // CH
    n_win = n_back + 1
    w0 = jnp.maximum(qi - n_back, 0)
    rw = pl.multiple_of(w0 * CH, CH)
    v_prev = values(jnp.maximum(qi - 1, 0), 0)
    pvs = [pv(v_prev[g], p_scr[g]) for g in groups]
    k_win = kr_ref[pl.ds(rw, n_win * CH), KR_WIN:KR_WIN + LANES]
    s_win = [jnp.dot(k_win, qs[g], preferred_element_type=F32) for g in groups]
    hidden = jnp.full((CH, TQ), NEG_INF, F32)
    slab_bias = []
    for c in range(n_win):
        d = w0 + c - qi
        slab_bias.append(jnp.where(d == 0, causal_bias,
                                   jnp.where(d > 0, hidden, jnp.where(d == -n_back, window_bias, 0.0))))
    win_bias = jnp.concatenate(slab_bias, axis=0)
    slc_parts = [probs(s_scr[0, g], causal_bias, ms[g]) for g in groups]
    win_parts = [probs(s_win[g], win_bias, m_init()) for g in groups]
    v_slc = values(qi, 0)
    v_win = [jnp.concatenate([vT_ref[NSA_KV_WIDTH + HD * g:NSA_KV_WIDTH + HD * (g + 1), pl.ds(rw, n_win * CH)],
                              jnp.ones((BF16_ROWS, n_win * CH), BF16)], axis=0) for g in groups]
    o_slc, o_win = [], []
    for g in groups:
        acc = (b_scr[g] + pvs[g]) * slc_parts[g][1] + pv(v_slc[g], slc_parts[g][2])
        o_slc.append(acc[:HD] / acc[HD:HD + 1])
        acc = pv(v_win[g], win_parts[g][2])
        o_win.append(acc[:HD] / acc[HD:HD + 1])

    for g in groups:
        rows = []
        for h in range(NSA_HPG):
            hh = NSA_HPG * g + h
            sl = slice(h * TQ, (h + 1) * TQ)
            rows.append(gT_ref[3 * hh:3 * hh + 1, :] * o_cmp[g][:, sl]
                        + gT_ref[3 * hh + 1:3 * hh + 2, :] * o_slc[g][:, sl]
                        + gT_ref[3 * hh + 2:3 * hh + 3, :] * o_win[g][:, sl])
        og = jnp.concatenate(rows, axis=0)
        o_ref[:, NSA_HPG * HD * g:NSA_HPG * HD * (g + 1)] = og.T


def _overlap_T(S):
    n_cmp = S // CMP_STRIDE
    n_slc = S // SLC_BLK
    cs = np.arange(n_cmp)[:, None] * CMP_STRIDE
    js = np.arange(n_slc)[None, :] * SLC_BLK
    ov = np.clip(np.minimum(cs + CMP_LEN, js + SLC_BLK) - np.maximum(cs, js), 0, None) / CMP_LEN
    ov[n_cmp - 1] = 0.0
    return jnp.asarray(ov.T, dtype=F32)


def _nsa(qT, kr, vT, kc, vcT, gT, B, S, TQ=NSA_CHUNK):
    n_cmp = S // CMP_STRIDE
    n_slc = S // SLC_BLK
    nq = S // TQ
    return pl.pallas_call(
        functools.partial(_nsa_kernel, S=S, TQ=TQ),
        grid=(B, nq),
        in_specs=[
            pl.BlockSpec((None, NSA_WIDTH, TQ), lambda b, i: (b, 0, i)),
            pl.BlockSpec((None, S, KR_WIDTH), lambda b, i: (b, 0, 0)),
            pl.BlockSpec((None, 2 * NSA_KV_WIDTH, S), lambda b, i: (b, 0, 0)),
            pl.BlockSpec((2, None, n_cmp, LANES), lambda b, i: (0, b, 0, 0)),
            pl.BlockSpec((2, None, HD, n_cmp), lambda b, i: (1, b, 0, 0)),
            pl.BlockSpec((None, 3 * NSA_HEADS, TQ), lambda b, i: (b, 0, i)),
            pl.BlockSpec((n_slc, n_cmp), lambda b, i: (0, 0)),
        ],
        out_specs=pl.BlockSpec((TQ, NSA_WIDTH), lambda b, i: (b * nq + i, 0)),
        out_shape=jax.ShapeDtypeStruct((B * S, NSA_WIDTH), F32),
        scratch_shapes=[
            pltpu.VMEM((NSA_KV_GROUPS, n_slc, TQ), F32),
            pltpu.VMEM((NSA_KV_GROUPS, S // TQ, BF16_ROWS, NSA_HPG * TQ), BF16),
            pltpu.VMEM((2, NSA_KV_GROUPS, TQ, NSA_HPG * TQ), F32),
            pltpu.VMEM((NSA_KV_GROUPS, TQ, NSA_HPG * TQ), BF16),
            pltpu.VMEM((NSA_KV_GROUPS, HD + BF16_ROWS, NSA_HPG * TQ), F32),
        ],
        compiler_params=_cparams(("arbitrary", "arbitrary")),
        name="nsa_attention",
    )(qT, kr.reshape(B, S, KR_WIDTH), vT, kc, vcT, gT, _overlap_T(S))


def _gdn_kernel(x_ref, xp_ref, sm_ref, aT_ref, z_ref, cw_ref, alog_ref, dt_ref, alogc_ref, dtc_ref, nw_ref,
                ebd_ref, eg_ref, eb_ref, lblk_ref, lT_ref, o_ref,
                xe_scr, st_scr, *, tg):
    C = GDN_CHUNK
    n_ch = tg // C
    n_pr = GDN_HEADS // 2
    i = pl.program_id(1)

    @pl.when(i == 0)
    def _():
        st_scr[...] = jnp.zeros_like(st_scr)

    def decay_terms():
        sm = sm_ref[...]
        g128 = -jnp.exp(alog_ref[...]) * _softplus(sm + dt_ref[...])
        gcs128 = jnp.dot(lblk_ref[...], g128, precision=HI, preferred_element_type=F32)
        g_exp = jnp.dot(gcs128, eg_ref[...], precision=HI, preferred_element_type=F32)
        b_exp = jnp.dot(jax.nn.sigmoid(sm).astype(BF16), eb_ref[...], preferred_element_type=F32)
        gT = -jnp.exp(alogc_ref[...]) * _softplus(aT_ref[...] + dtc_ref[...])
        return g_exp, b_exp, [jnp.dot(gT, lT_ref[c], precision=HI, preferred_element_type=F32)
                              for c in range(n_ch)]

    halo = F32_SUBLANES
    xe_scr[0:halo, :] = jnp.where(i == 0, 0.0, xp_ref[...])
    xe_scr[halo:halo + tg, :] = x_ref[...]
    ebd = ebd_ref[...]
    g_all, b_all, gts = decay_terms()
    n_split = 2
    rh = tg // n_split

    def conv_half(hf):
        base = halo - (GDN_CONV - 1) + rh * hf
        y = cw_ref[0:1, :] * xe_scr[base:base + rh, :]
        for j in range(1, GDN_CONV):
            y = y + cw_ref[j:j + 1, :] * xe_scr[base + j:base + j + rh, :]
        qkv = y * jax.nn.sigmoid(y)
        q = qkv[:, 0:GDN_WIDTH]
        k = qkv[:, GDN_WIDTH:2 * GDN_WIDTH]
        qss = jnp.dot((q * q).astype(BF16), ebd, preferred_element_type=F32)
        kss = jnp.dot((k * k).astype(BF16), ebd, preferred_element_type=F32)
        return q * lax.rsqrt(qss + 1e-6) * (HD ** -0.5), k * lax.rsqrt(kss + 1e-6), qkv[:, 2 * GDN_WIDTH:]

    halves = [conv_half(hf) for hf in range(n_split)]
    q_all, k_all, v_all = (jnp.concatenate([hv[n] for hv in halves], axis=0) for n in range(3))

    ii = lax.broadcasted_iota(jnp.int32, (C, LANES), 0)
    lane = lax.broadcasted_iota(jnp.int32, (C, LANES), 1)
    jj = lane % HD
    eye2 = jnp.where(ii == jj, 1.0, 0.0)
    lo_half = lane < HD
    lane2 = lax.broadcasted_iota(jnp.int32, (LANES, LANES), 1)
    row2 = lax.broadcasted_iota(jnp.int32, (LANES, LANES), 0)
    same_head = (lane2 < HD) == (row2 < HD)

    def bd(x):
        xb = x.astype(BF16)
        zero = jnp.zeros_like(xb)
        return jnp.concatenate([jnp.where(lo_half, xb, zero), jnp.where(lo_half, zero, xb)], axis=0)

    def mm(a, b):
        return jnp.dot(a.astype(BF16), b, preferred_element_type=F32)

    chains = [(c, pr) for c in range(n_ch) for pr in range(n_pr)]

    def blk(arr, c, pr):
        return arr[C * c:C * (c + 1), LANES * pr:LANES * (pr + 1)]

    qg, kdec, dec, aq, vb_bd, kbg_bd, eglast = {}, {}, {}, {}, {}, {}, {}
    for ch in chains:
        c, pr = ch
        qn, k_, gc, be = blk(q_all, c, pr), blk(k_all, c, pr), blk(g_all, c, pr), blk(b_all, c, pr)
        kb = k_ * be
        eg = jnp.exp(gc)
        glast = gc[C - 1:C, :]
        qg[ch] = qn * eg
        kdec[ch] = k_ * jnp.exp(glast - gc)
        eglast[ch] = jnp.exp(glast)
        vb_bd[ch] = bd(blk(v_all, c, pr) * be)
        kbg_bd[ch] = bd(kb * eg)
        gt = gts[c]
        gct = jnp.concatenate([jnp.broadcast_to(gt[2 * pr:2 * pr + 1, :], (C, HD)),
                               jnp.broadcast_to(gt[2 * pr + 1:2 * pr + 2, :], (C, HD))], axis=1)
        dec[ch] = jnp.exp(jnp.where(ii >= jj, gc - gct, NEG_INF))
        aq[ch] = lax.dot_general(jnp.concatenate([kb, qn], axis=0).astype(BF16), bd(k_), NT_DIMS,
                                 preferred_element_type=F32)
    a = {ch: jnp.where(ii > jj, aq[ch][:C] * dec[ch], 0.0) for ch in chains}
    qk = {ch: aq[ch][C:] * dec[ch] for ch in chains}
    x = {ch: eye2 - a[ch] for ch in chains}
    pw = {ch: mm(a[ch], bd(a[ch])) for ch in chains}
    for _ in range(4):
        pbd = {ch: bd(pw[ch]) for ch in chains}
        x = {ch: x[ch] + mm(x[ch], pbd[ch]) for ch in chains}
        pw = {ch: mm(pw[ch], pbd[ch]) for ch in chains}
    x = {ch: x[ch] + mm(x[ch], bd(pw[ch])) for ch in chains}
    uw = {ch: mm(x[ch], jnp.concatenate([vb_bd[ch], kbg_bd[ch]], axis=1)) for ch in chains}

    ku_kw = {ch: lax.dot_general(kdec[ch].astype(BF16), uw[ch].astype(BF16), TN_DIMS,
                                 preferred_element_type=F32) for ch in chains}
    qu_qw = {ch: mm(qk[ch], jnp.concatenate([bd(uw[ch][:, :LANES]), bd(uw[ch][:, LANES:])], axis=1))
             for ch in chains}
    lhs = {ch: jnp.concatenate([jnp.where(same_head, ku_kw[ch][:, LANES:], 0.0),
                                qg[ch] - qu_qw[ch][:, LANES:]], axis=0).astype(BF16) for ch in chains}
    st = [st_scr[:, LANES * pr:LANES * (pr + 1)] for pr in range(n_pr)]
    o_rows = []
    for c in range(n_ch):
        prod = [jnp.dot(lhs[c, pr], st[pr].astype(BF16), preferred_element_type=F32)
                for pr in range(n_pr)]
        o_rows.append(jnp.concatenate([prod[pr][LANES:] + qu_qw[c, pr][:, :LANES] for pr in range(n_pr)], axis=1))
        st = [st[pr] * eglast[c, pr] - prod[pr][:LANES] + jnp.where(same_head, ku_kw[c, pr][:, :LANES], 0.0)
              for pr in range(n_pr)]
    st_scr[...] = jnp.concatenate(st, axis=1)

    o = jnp.concatenate(o_rows, axis=0)
    oms = jnp.dot((o * o).astype(BF16), ebd, preferred_element_type=F32) * (1.0 / HD)
    z = z_ref[...]
    o_ref[...] = o * lax.rsqrt(oms + RMS_EPS) * nw_ref[...] * (z * jax.nn.sigmoid(z))


def _gdn(gqkv, sm, aT, gz, conv_w, a_log, dt_bias, norm_w, B, S, tg=256):
    rows = B * S
    nt = S // tg
    n_ch = tg // GDN_CHUNK
    head_of = np.arange(GDN_WIDTH) // HD
    ebd = jnp.asarray(head_of[:, None] == head_of[None, :], dtype=BF16)
    eg = np.zeros((LANES, GDN_WIDTH), np.float32)
    eb = np.zeros((LANES, GDN_WIDTH), np.float32)
    eg[SM_A + head_of, np.arange(GDN_WIDTH)] = 1.0
    eb[SM_B + head_of, np.arange(GDN_WIDTH)] = 1.0
    r = np.arange(tg)
    lblk = ((r[:, None] // GDN_CHUNK == r[None, :] // GDN_CHUNK) & (r[None, :] <= r[:, None])).astype(np.float32)
    lT = np.stack([lblk.T[:, GDN_CHUNK * c:GDN_CHUNK * (c + 1)] for c in range(n_ch)])
    a_log = a_log.astype(F32)
    dt_bias = dt_bias.astype(F32)
    alog128 = jnp.zeros((1, LANES), F32).at[0, SM_A:SM_A + GDN_HEADS].set(a_log)
    dt128 = jnp.zeros((1, LANES), F32).at[0, SM_A:SM_A + GDN_HEADS].set(dt_bias)
    nw512 = jnp.tile(norm_w.astype(F32), GDN_HEADS).reshape(1, GDN_WIDTH)
    full = lambda shape: pl.BlockSpec(shape, lambda b, i: (0,) * len(shape))
    return pl.pallas_call(
        functools.partial(_gdn_kernel, tg=tg),
        grid=(B, nt),
        in_specs=[
            pl.BlockSpec((tg, 3 * GDN_WIDTH), lambda b, i: (b * nt + i, 0)),
            pl.BlockSpec((F32_SUBLANES, 3 * GDN_WIDTH),
                         lambda b, i: (jnp.maximum((b * nt + i) * (tg // F32_SUBLANES) - 1, 0), 0)),
            pl.BlockSpec((tg, LANES), lambda b, i: (b * nt + i, 0)),
            pl.BlockSpec((None, GDN_HEADS, tg), lambda b, i: (b, 0, i)),
            pl.BlockSpec((tg, GDN_WIDTH), lambda b, i: (b * nt + i, 0)),
            full((GDN_CONV, 3 * GDN_WIDTH)),
            full((1, LANES)),
            full((1, LANES)),
            full((GDN_HEADS, 1)),
            full((GDN_HEADS, 1)),
            full((1, GDN_WIDTH)),
            full((GDN_WIDTH, GDN_WIDTH)),
            full((LANES, GDN_WIDTH)),
            full((LANES, GDN_WIDTH)),
            full((tg, tg)),
            full((n_ch, tg, GDN_CHUNK)),
        ],
        out_specs=pl.BlockSpec((tg, GDN_WIDTH), lambda b, i: (b * nt + i, 0)),
        out_shape=jax.ShapeDtypeStruct((rows, GDN_WIDTH), F32),
        scratch_shapes=[
            pltpu.VMEM((tg + F32_SUBLANES, 3 * GDN_WIDTH), F32),
            pltpu.VMEM((LANES, GDN_WIDTH), F32),
        ],
        compiler_params=_cparams(("arbitrary", "arbitrary")),
        name="gdn",
    )(gqkv, gqkv, sm, aT, gz, conv_w, alog128, dt128, a_log.reshape(GDN_HEADS, 1), dt_bias.reshape(GDN_HEADS, 1),
      nw512, ebd, jnp.asarray(eg), jnp.asarray(eb, dtype=BF16), jnp.asarray(lblk), jnp.asarray(lT))


def _mix_xattn_kernel(x_ref, a_ref, b_ref, wa_ref, wb_ref, nw_ref, wq_ref, kv_ref, wo_ref, o_ref):
    mix = jnp.dot(a_ref[...].astype(BF16), wa_ref[...], preferred_element_type=F32)
    mix = mix + jnp.dot(b_ref[...].astype(BF16), wb_ref[...], preferred_element_type=F32)
    _xattn_rows(x_ref[...] + mix, nw_ref, wq_ref, kv_ref, wo_ref, o_ref)


def _xattn_kernel(x_ref, nw_ref, wq_ref, kv_ref, wo_ref, o_ref):
    _xattn_rows(x_ref[...], nw_ref, wq_ref, kv_ref, wo_ref, o_ref)


def _xattn_rows(x, nw_ref, wq_ref, kv_ref, wo_ref, o_ref):
    hn = _rms(x, nw_ref[...]).astype(BF16)
    q = jnp.dot(hn, wq_ref[...], preferred_element_type=F32) * (XA_HD ** -0.5)
    heads = range(XA_HEADS)
    ss = [lax.dot_general(q[:, XA_HD * h:XA_HD * (h + 1)].astype(BF16), kv_ref[:, XA_HD * h:XA_HD * (h + 1)],
                          NT_DIMS, preferred_element_type=F32) for h in heads]
    es = [jnp.exp(ss[h] - jnp.max(ss[h], axis=-1, keepdims=True)).astype(BF16) for h in heads]
    ones = jnp.ones((kv_ref.shape[0], XA_HD), BF16)
    pvs = [jnp.dot(es[h], jnp.concatenate([kv_ref[:, XA_WIDTH + XA_HD * h:XA_WIDTH + XA_HD * (h + 1)], ones], axis=1),
                   preferred_element_type=F32) for h in heads]
    o = jnp.concatenate([pvs[h][:, :XA_HD] / pvs[h][:, XA_HD:XA_HD + 1] for h in heads], axis=1).astype(BF16)
    o_ref[...] = x + jnp.dot(o, wo_ref[...], preferred_element_type=F32)


def _xattn(x2d, norm_w, wq_bf, memkv_l, wo_bf, S, tm=512):
    rows = x2d.shape[0]
    per_b = S // tm
    mem_len = memkv_l.shape[0] // (rows // S)
    return pl.pallas_call(
        _xattn_kernel,
        grid=(rows // tm,),
        in_specs=[
            pl.BlockSpec((tm, D_MODEL), lambda i: (i, 0)),
            pl.BlockSpec((1, D_MODEL), lambda i: (0, 0)),
            pl.BlockSpec((D_MODEL, XA_WIDTH), lambda i: (0, 0)),
            pl.BlockSpec((mem_len, 2 * XA_WIDTH), lambda i: (i // per_b, 0)),
            pl.BlockSpec((XA_WIDTH, D_MODEL), lambda i: (0, 0)),
        ],
        out_specs=pl.BlockSpec((tm, D_MODEL), lambda i: (i, 0)),
        out_shape=jax.ShapeDtypeStruct((rows, D_MODEL), F32),
        compiler_params=_cparams(("arbitrary",)),
        name="cross_attention",
    )(x2d, norm_w.reshape(1, D_MODEL), wq_bf, memkv_l, wo_bf)


def _mix_xattn(x2d, o_nsa, o_gdn, w_out_bf, norm_w, wq_bf, memkv_l, wo_bf, S, tm=512):
    rows = x2d.shape[0]
    per_b = S // tm
    mem_len = memkv_l.shape[0] // (rows // S)
    return pl.pallas_call(
        _mix_xattn_kernel,
        grid=(rows // tm,),
        in_specs=[
            pl.BlockSpec((tm, D_MODEL), lambda i: (i, 0)),
            pl.BlockSpec((tm, NSA_WIDTH), lambda i: (i, 0)),
            pl.BlockSpec((tm, GDN_WIDTH), lambda i: (i, 0)),
            pl.BlockSpec((NSA_WIDTH, D_MODEL), lambda i: (0, 0)),
            pl.BlockSpec((GDN_WIDTH, D_MODEL), lambda i: (1, 0)),
            pl.BlockSpec((1, D_MODEL), lambda i: (0, 0)),
            pl.BlockSpec((D_MODEL, XA_WIDTH), lambda i: (0, 0)),
            pl.BlockSpec((mem_len, 2 * XA_WIDTH), lambda i: (i // per_b, 0)),
            pl.BlockSpec((XA_WIDTH, D_MODEL), lambda i: (0, 0)),
        ],
        out_specs=pl.BlockSpec((tm, D_MODEL), lambda i: (i, 0)),
        out_shape=jax.ShapeDtypeStruct((rows, D_MODEL), F32),
        compiler_params=_cparams(("arbitrary",)),
        name="mix_cross_attention",
    )(x2d, o_nsa, o_gdn, w_out_bf, w_out_bf, norm_w.reshape(1, D_MODEL), wq_bf, memkv_l, wo_bf)


def _mlp_kernel(x_ref, nw_ref, w1_ref, w2_ref, fw_ref, o_ref, hn_scr, acc_scr, *, final_norm):
    j = pl.program_id(1)

    @pl.when(j == 0)
    def _():
        x = x_ref[...]
        hn_scr[...] = _rms(x, nw_ref[...]).astype(BF16)
        acc_scr[...] = x

    h = jnp.dot(hn_scr[...], w1_ref[...], preferred_element_type=F32)
    h = jnp.square(jnp.maximum(h, 0.0)).astype(BF16)
    acc_scr[...] += jnp.dot(h, w2_ref[...], preferred_element_type=F32)

    @pl.when(j == pl.num_programs(1) - 1)
    def _():
        y = acc_scr[...]
        if final_norm:
            y = _rms(y, fw_ref[...])
        o_ref[...] = y


def _mlp(x2d, norm_w, w1_bf, w2_bf, final_w, final_norm, tm=1024, tf=1024):
    rows = x2d.shape[0]
    tm = min(tm, rows)
    return pl.pallas_call(
        functools.partial(_mlp_kernel, final_norm=final_norm),
        grid=(rows // tm, D_FF // tf),
        in_specs=[
            pl.BlockSpec((tm, D_MODEL), lambda i, j: (i, 0)),
            pl.BlockSpec((1, D_MODEL), lambda i, j: (0, 0)),
            pl.BlockSpec((D_MODEL, tf), lambda i, j: (0, j)),
            pl.BlockSpec((tf, D_MODEL), lambda i, j: (j, 0)),
            pl.BlockSpec((1, D_MODEL), lambda i, j: (0, 0)),
        ],
        out_specs=pl.BlockSpec((tm, D_MODEL), lambda i, j: (i, 0)),
        out_shape=jax.ShapeDtypeStruct((rows, D_MODEL), F32),
        scratch_shapes=[pltpu.VMEM((tm, D_MODEL), BF16), pltpu.VMEM((tm, D_MODEL), F32)],
        compiler_params=_cparams(("arbitrary", "arbitrary")),
        name="mlp",
    )(x2d, norm_w.reshape(1, D_MODEL), w1_bf, w2_bf, final_w.reshape(1, D_MODEL))


def _sc_kernel(x_ref, xp_ref, nw_ref, win_ref, cw_ref, wout_ref, o_ref, cu_scr, *, tm, per_b):
    i = pl.program_id(0)
    x = x_ref[...]
    nw = nw_ref[...]
    halo = F32_SUBLANES
    y = jnp.dot(_rms(x, nw).astype(BF16), win_ref[...], preferred_element_type=F32)
    yp = jnp.dot(_rms(xp_ref[...], nw).astype(BF16), win_ref[:, D_MODEL:], preferred_element_type=F32)
    cu_scr[0:halo, :] = jnp.where(i % per_b == 0, 0.0, yp[:, :D_MODEL] * yp[:, D_MODEL:])
    cu_scr[halo:halo + tm, :] = y[:, D_MODEL:2 * D_MODEL] * y[:, 2 * D_MODEL:]
    base = halo - (SC_WIDTH - 1)
    conv = cw_ref[0:1, :] * cu_scr[base:base + tm, :]
    for j in range(1, SC_WIDTH):
        conv = conv + cw_ref[j:j + 1, :] * cu_scr[base + j:base + j + tm, :]
    gated = (y[:, :D_MODEL] * conv).astype(BF16)
    o_ref[...] = x + jnp.dot(gated, wout_ref[...], preferred_element_type=F32)


def _sc_mixer(x2d, norm_w, win_bf, conv_w, wout_bf, S, tm=512):
    rows = x2d.shape[0]
    per_b = S // tm
    return pl.pallas_call(
        functools.partial(_sc_kernel, tm=tm, per_b=per_b),
        grid=(rows // tm,),
        in_specs=[
            pl.BlockSpec((tm, D_MODEL), lambda i: (i, 0)),
            pl.BlockSpec((F32_SUBLANES, D_MODEL), lambda i: (jnp.maximum(i * (tm // F32_SUBLANES) - 1, 0), 0)),
            pl.BlockSpec((1, D_MODEL), lambda i: (0, 0)),
            pl.BlockSpec((D_MODEL, 3 * D_MODEL), lambda i: (0, 0)),
            pl.BlockSpec((SC_WIDTH, D_MODEL), lambda i: (0, 0)),
            pl.BlockSpec((D_MODEL, D_MODEL), lambda i: (0, 0)),
        ],
        out_specs=pl.BlockSpec((tm, D_MODEL), lambda i: (i, 0)),
        out_shape=jax.ShapeDtypeStruct((rows, D_MODEL), F32),
        scratch_shapes=[pltpu.VMEM((tm + F32_SUBLANES, D_MODEL), F32)],
        compiler_params=_cparams(("arbitrary",)),
        name="short_conv_mixer",
    )(x2d, x2d, norm_w.reshape(1, D_MODEL), win_bf, conv_w, wout_bf)


def _hybrid_mixer(x2d, positions, norm_w, w_in, ck_pos, ck_w1, ck_w2, cv_pos, cv_w1, cv_w2,
                  gdn_conv, gdn_a_log, gdn_dt_bias, gdn_norm, B, S):
    tm = 512
    invf = (ROPE_THETA ** (-jnp.arange(0, HD, 2, dtype=F32) / HD)).reshape(HD // 2, 1)
    pos3 = positions.astype(F32).reshape(B * S // tm, 1, tm)
    qT, kr, vT, ckv, gqkv, gz, sm, gT, aT = _proj0(x2d, norm_w, _prep_w_in(w_in), pos3, invf, B, S, tm)
    w1s, p8, w2s = _prep_compress_weights(ck_pos, ck_w1, ck_w2, cv_pos, cv_w1, cv_w2)
    kc, vcT = _compress(ckv, w1s, p8, w2s, B, S)
    o_nsa = _nsa(qT, kr, vT, kc, vcT, gT, B, S)
    o_gdn = _gdn(gqkv, sm, aT, gz, gdn_conv, gdn_a_log, gdn_dt_bias, gdn_norm, B, S)
    return o_nsa, o_gdn


def kernel(x, mem, positions, norm_mix, norm_xattn, norm_mlp, hyb_w_in, hyb_cmp_k_pos, hyb_cmp_k_w1, hyb_cmp_k_w2, hyb_cmp_v_pos, hyb_cmp_v_w1, hyb_cmp_v_w2, hyb_gdn_conv, hyb_gdn_a_log, hyb_gdn_dt_bias, hyb_gdn_norm, hyb_w_out, sc_w_in, sc_conv, sc_w_out, mem_norm, xa_wq, xa_wkv, xa_wo, mlp_w1, mlp_w2, final_norm):
    B, S, _ = x.shape
    depth = norm_mix.shape[0]
    x2d = x.reshape(B * S, D_MODEL)
    memkv = _memkv(mem.reshape(-1, D_MODEL), mem_norm, xa_wkv.astype(BF16))
    for layer in range(depth):
        j = layer // 2
        xa = (norm_xattn[layer], xa_wq[layer].astype(BF16), memkv[layer], xa_wo[layer].astype(BF16), S)
        if layer % 2 == 0:
            o_nsa, o_gdn = _hybrid_mixer(x2d, positions, norm_mix[layer], hyb_w_in[j], hyb_cmp_k_pos[j],
                                         hyb_cmp_k_w1[j], hyb_cmp_k_w2[j], hyb_cmp_v_pos[j], hyb_cmp_v_w1[j],
                                         hyb_cmp_v_w2[j], hyb_gdn_conv[j], hyb_gdn_a_log[j], hyb_gdn_dt_bias[j],
                                         hyb_gdn_norm[j], B, S)
            x2d = _mix_xattn(x2d, o_nsa, o_gdn, hyb_w_out[j].astype(BF16), *xa)
        else:
            x2d = _sc_mixer(x2d, norm_mix[layer], sc_w_in[j].astype(BF16), sc_conv[j], sc_w_out[j].astype(BF16), S)
            x2d = _xattn(x2d, *xa)
        x2d = _mlp(x2d, norm_mlp[layer], mlp_w1[layer].astype(BF16), mlp_w2[layer].astype(BF16),
                   final_norm, layer == depth - 1)
    return x2d.reshape(B, S, D_MODEL)
```

```python
import functools

import numpy as np
import jax
import jax.numpy as jnp
from jax import lax
from jax.experimental import pallas as pl
from jax.experimental.pallas import tpu as pltpu

F32 = jnp.float32
BF16 = jnp.bfloat16
HI = lax.Precision.HIGHEST

D_MODEL = 1024
MEM_LEN = 256
RMS_EPS = 1e-6
ROPE_THETA = 10000.0
NEG_INF = -1e30
FORCE_BONUS = 1e3
LOG2E = 1.4426950408889634

NSA_HEADS = 8
NSA_KV_GROUPS = 2
NSA_HPG = NSA_HEADS // NSA_KV_GROUPS
HD = 64
CMP_LEN = 32
CMP_STRIDE = 16
CMP_HID = 2 * HD
SLC_BLK = 64
SLC_TOPK = 16
WINDOW = 512

GDN_HEADS = 8
GDN_CONV = 4
GDN_CHUNK = 64
SC_WIDTH = 3
XA_HEADS = 4
XA_HD = 128
D_FF = 4 * D_MODEL

NSA_WIDTH = NSA_HEADS * HD
NSA_KV_WIDTH = NSA_KV_GROUPS * HD
GDN_WIDTH = GDN_HEADS * HD
XA_WIDTH = XA_HEADS * XA_HD
IN_SIZES = (NSA_WIDTH,) + (NSA_KV_WIDTH,) * 6 + (3 * NSA_HEADS,) + (GDN_WIDTH,) * 3 + (GDN_HEADS, GDN_HEADS, GDN_WIDTH)

V7X_VMEM_BYTES = 64 * 1024 * 1024
VMEM_LIMIT = V7X_VMEM_BYTES * 3 // 4
LANES = 128
F32_SUBLANES = 8
BF16_ROWS = 16

C_Q, C_KR, C_VT, C_CKV, C_GQKV, C_GZ, C_SM, NC0 = 0, 512, 768, 1024, 1280, 2816, 3328, 3456
SM_GATE, SM_A, SM_B = 0, 24, 32
NSA_CHUNK = 256
KR_WIDTH = 3 * LANES
KR_WIN = 2 * LANES

NT_DIMS = (((1,), (1,)), ((), ()))
TN_DIMS = (((0,), (0,)), ((), ()))


def _cparams(sem):
    return pltpu.CompilerParams(dimension_semantics=sem, vmem_limit_bytes=VMEM_LIMIT)


def _rms(x, w):
    return x * lax.rsqrt(jnp.mean(x * x, axis=-1, keepdims=True) + RMS_EPS) * w


def _softplus(x):
    return jnp.maximum(x, 0.0) + jnp.log(1.0 + jnp.exp(-jnp.abs(x)))


def _memkv_kernel(m_ref, nw_ref, w_ref, o_ref):
    hn = _rms(m_ref[...], nw_ref[...]).astype(BF16)
    o_ref[...] = jnp.dot(hn, w_ref[...], preferred_element_type=F32).astype(BF16)


def _memkv(mem2d, mem_norm, wkv_bf):
    rows = mem2d.shape[0]
    depth = wkv_bf.shape[0]
    tm = 512
    return pl.pallas_call(
        _memkv_kernel,
        grid=(depth, rows // tm),
        in_specs=[
            pl.BlockSpec((tm, D_MODEL), lambda l, i: (i, 0)),
            pl.BlockSpec((1, D_MODEL), lambda l, i: (0, 0)),
            pl.BlockSpec((None, D_MODEL, 2 * XA_WIDTH), lambda l, i: (l, 0, 0)),
        ],
        out_specs=pl.BlockSpec((None, tm, 2 * XA_WIDTH), lambda l, i: (l, i, 0)),
        out_shape=jax.ShapeDtypeStruct((depth, rows, 2 * XA_WIDTH), BF16),
        compiler_params=_cparams(("arbitrary", "arbitrary")),
        name="memkv",
    )(mem2d, mem_norm.reshape(1, D_MODEL), wkv_bf)


def _proj0_kernel(x_ref, nw_ref, w_ref, pos_ref, invf_ref,
                  qT_ref, kr_ref, vT_ref, ckv_ref, gqkv_ref, gz_ref, sm_ref, gT_ref, aT_ref):
    tm = x_ref.shape[0]
    hn = _rms(x_ref[...], nw_ref[...]).astype(BF16)
    y = jnp.dot(hn, w_ref[...], preferred_element_type=F32)

    ang = invf_ref[...] * pos_ref[...]
    c = jnp.cos(ang)
    s = jnp.sin(ang)
    cos_n = jnp.concatenate([c, c, c, c], axis=0).T
    sin_n = jnp.concatenate([-s, s, -s, s], axis=0).T
    lane = lax.broadcasted_iota(jnp.int32, (tm, LANES), 1)
    first_half = (lane % HD) < (HD // 2)

    def rope(xc):
        rot = jnp.where(first_half, pltpu.roll(xc, LANES - HD // 2, 1), pltpu.roll(xc, HD // 2, 1))
        return xc * cos_n + rot * sin_n

    roped = [rope(y[:, LANES * i:LANES * (i + 1)]) for i in range(6)]
    q = jnp.concatenate(roped[:4], axis=1) * (LOG2E * HD ** -0.5)
    qT_ref[...] = q.T.astype(BF16)
    row = lax.broadcasted_iota(jnp.int32, (tm, LANES), 0)
    blk_hot = jnp.where(lane - HD == (row // SLC_BLK) % (NSA_CHUNK // SLC_BLK), 1.0, 0.0)
    k_slc = roped[4]
    lo = lane < HD
    kr_ref[...] = jnp.concatenate([jnp.where(lo, k_slc, blk_hot),
                                   jnp.where(lo, pltpu.roll(k_slc, HD, 1), blk_hot),
                                   roped[5]], axis=1).astype(BF16)
    vT_ref[...] = y[:, C_VT:C_CKV].T.astype(BF16)
    for i in range(4):
        ckv_ref[i] = y[:, C_CKV + HD * i:C_CKV + HD * (i + 1)].astype(BF16)
    gqkv_ref[...] = y[:, C_GQKV:C_GZ]
    gz_ref[...] = y[:, C_GZ:C_SM]
    sm = y[:, C_SM:NC0]
    sm_ref[...] = sm
    gT_ref[...] = jax.nn.sigmoid(sm).T[:3 * NSA_HEADS, :]
    aT_ref[...] = sm.T[SM_A:SM_A + GDN_HEADS, :]


def _prep_w_in(w_in):
    offs = np.cumsum((0,) + IN_SIZES)
    (nq, kcmp, vcmp, kslc, vslc, kwin, vwin, ngate, gq, gk, gv, ga, gb, gz) = [
        w_in[:, offs[i]:offs[i + 1]] for i in range(len(IN_SIZES))]
    pad = jnp.zeros((D_MODEL, LANES - 3 * NSA_HEADS - 2 * GDN_HEADS), w_in.dtype)
    small = jnp.concatenate([ngate, ga, gb, pad], axis=1)
    return jnp.concatenate([nq, kslc, kwin, vslc, vwin, kcmp, vcmp, gq, gk, gv, gz, small], axis=1).astype(BF16)


def _proj0(x2d, norm_w, w_bf, pos3, invf, B, S, tm):
    rows = B * S
    nt = S // tm
    row_spec = lambda n: pl.BlockSpec((tm, n), lambda i: (i, 0))
    t_spec = lambda n: pl.BlockSpec((None, n, tm), lambda i: (i // nt, 0, i % nt))
    return pl.pallas_call(
        _proj0_kernel,
        grid=(rows // tm,),
        in_specs=[
            row_spec(D_MODEL),
            pl.BlockSpec((1, D_MODEL), lambda i: (0, 0)),
            pl.BlockSpec((D_MODEL, NC0), lambda i: (0, 0)),
            pl.BlockSpec((None, 1, tm), lambda i: (i, 0, 0)),
            pl.BlockSpec((HD // 2, 1), lambda i: (0, 0)),
        ],
        out_specs=[
            t_spec(NSA_WIDTH),
            row_spec(KR_WIDTH),
            t_spec(2 * NSA_KV_WIDTH),
            pl.BlockSpec((4, tm, HD), lambda i: (0, i, 0)),
            row_spec(3 * GDN_WIDTH),
            row_spec(GDN_WIDTH),
            row_spec(LANES),
            t_spec(3 * NSA_HEADS),
            t_spec(GDN_HEADS),
        ],
        out_shape=[
            jax.ShapeDtypeStruct((B, NSA_WIDTH, S), BF16),
            jax.ShapeDtypeStruct((rows, KR_WIDTH), BF16),
            jax.ShapeDtypeStruct((B, 2 * NSA_KV_WIDTH, S), BF16),
            jax.ShapeDtypeStruct((4, rows, HD), BF16),
            jax.ShapeDtypeStruct((rows, 3 * GDN_WIDTH), F32),
            jax.ShapeDtypeStruct((rows, GDN_WIDTH), F32),
            jax.ShapeDtypeStruct((rows, LANES), F32),
            jax.ShapeDtypeStruct((B, 3 * NSA_HEADS, S), F32),
            jax.ShapeDtypeStruct((B, GDN_HEADS, S), F32),
        ],
        compiler_params=_cparams(("arbitrary",)),
        name="proj0",
    )(x2d, norm_w.reshape(1, D_MODEL), w_bf, pos3, invf)


def _compress_kernel(t_ref, w1_ref, pos_ref, w2_ref, o_ref, oT_ref):
    nseg = t_ref.shape[0]
    w1 = w1_ref[...]
    ab = jnp.dot(t_ref[...], w1, preferred_element_type=F32)
    pc = jnp.dot(pos_ref[...].astype(BF16), w1, preferred_element_type=F32)
    bias = pc[0:1, :CMP_HID] + pc[1:2, CMP_HID:]
    hid = ab[:, :CMP_HID] + pltpu.roll(ab[:, CMP_HID:], nseg - 1, 0) + bias
    act = hid * jax.nn.sigmoid(hid)
    out = jnp.dot(act.astype(BF16), w2_ref[...], preferred_element_type=F32)
    o_ref[...] = out
    oT_ref[...] = out.T[:HD, :]


def _compress(ckv, w1s, pos8, w2s, B, S):
    nseg = S // CMP_STRIDE
    t = ckv.reshape(4, B, nseg, CMP_STRIDE * HD)
    return pl.pallas_call(
        _compress_kernel,
        grid=(4, B),
        in_specs=[
            pl.BlockSpec((None, None, nseg, CMP_STRIDE * HD), lambda k, b: (k, b, 0, 0)),
            pl.BlockSpec((None, CMP_STRIDE * HD, 2 * CMP_HID), lambda k, b: (k // 2, 0, 0)),
            pl.BlockSpec((None, F32_SUBLANES, CMP_STRIDE * HD), lambda k, b: (k // 2, 0, 0)),
            pl.BlockSpec((None, CMP_HID, LANES), lambda k, b: (k // 2, 0, 0)),
        ],
        out_specs=[
            pl.BlockSpec((None, None, nseg, LANES), lambda k, b: (k, b, 0, 0)),
            pl.BlockSpec((None, None, HD, nseg), lambda k, b: (k, b, 0, 0)),
        ],
        out_shape=[
            jax.ShapeDtypeStruct((4, B, nseg, LANES), F32),
            jax.ShapeDtypeStruct((4, B, HD, nseg), F32),
        ],
        compiler_params=_cparams(("arbitrary", "arbitrary")),
        name="nsa_compress",
    )(t, w1s, pos8, w2s)


def _prep_compress_weights(k_pos, k_w1, k_w2, v_pos, v_w1, v_w2):
    half = CMP_STRIDE * HD

    def w1cat(w1):
        return jnp.concatenate([w1[:half], w1[half:]], axis=1)

    def pos8(p):
        flat = p.reshape(2, half)
        return jnp.concatenate([flat, jnp.zeros((F32_SUBLANES - 2, half), p.dtype)], axis=0)

    def w2pad(w2):
        return jnp.concatenate([w2, jnp.zeros((CMP_HID, LANES - HD), w2.dtype)], axis=1)

    w1s = jnp.stack([w1cat(k_w1), w1cat(v_w1)]).astype(BF16)
    p8 = jnp.stack([pos8(k_pos), pos8(v_pos)])
    w2s = jnp.stack([w2pad(k_w2), w2pad(v_w2)]).astype(BF16)
    return w1s, p8, w2s


def _nsa_kernel(qT_ref, kr_ref, vT_ref, kc_ref, vcT_ref, gT_ref, ovT_ref, o_ref,
                imp_scr, sel_scr, s_scr, p_scr, b_scr, *, S, TQ):
    CH = TQ
    groups = range(NSA_KV_GROUPS)
    n_cmp = S // CMP_STRIDE
    n_slc = S // SLC_BLK
    n_sel = min(SLC_TOPK, n_slc)
    qi = pl.program_id(1)
    q0 = qi * TQ
    t1 = q0 + lax.broadcasted_iota(jnp.int32, (1, TQ), 1)
    t4 = jnp.concatenate([t1] * NSA_HPG, axis=1)
    n_io = lax.broadcasted_iota(jnp.int32, (n_cmp, 1), 0)
    j_io = lax.broadcasted_iota(jnp.int32, (n_slc, 1), 0)
    k_io = lax.broadcasted_iota(jnp.int32, (CH, 1), 0)
    n_causal = (q0 + TQ) // SLC_BLK

    qs, q64s = [], []
    for g in groups:
        q64 = jnp.concatenate(
            [qT_ref[NSA_HPG * HD * g + HD * h:NSA_HPG * HD * g + HD * (h + 1), :] for h in range(NSA_HPG)], axis=1)
        zq = jnp.zeros_like(q64)
        qs.append(jnp.concatenate([q64, zq] if g == 0 else [zq, q64], axis=0))
        q64s.append(q64)

    n_lanes = NSA_HPG * TQ

    def probs(sc, bias, m_i):
        if bias is not None:
            sc = sc + jnp.concatenate([bias] * NSA_HPG, axis=1)
        m_new = jnp.maximum(m_i, jnp.max(sc, axis=0, keepdims=True))
        return m_new, jnp.exp2(m_i - m_new), jnp.exp2(sc - m_new).astype(BF16)

    def pv(vt, pr):
        return jnp.dot(vt, pr, preferred_element_type=F32)

    def m_init():
        return jnp.full((1, n_lanes), NEG_INF, F32)

    u_io = lax.broadcasted_iota(jnp.int32, (1, TQ), 1)
    causal_bias = jnp.where(k_io <= u_io, 0.0, NEG_INF)
    window_bias = jnp.where(k_io > u_io, 0.0, NEG_INF)

    s_cmp = [jnp.dot(kc_ref[g, :, 0:HD].astype(BF16), q64s[g], preferred_element_type=F32)
             for g in groups]
    n_back = WINDOW // CH
    n_win = n_back + 1
    w0 = jnp.maximum(qi - n_back, 0)
    rw = pl.multiple_of(w0 * CH, CH)
    k_win = kr_ref[pl.ds(rw, n_win * CH), KR_WIN:KR_WIN + LANES]
    s_win = [jnp.dot(k_win, qs[g], preferred_element_type=F32) for g in groups]
    valid = (n_io * CMP_STRIDE + (CMP_LEN - 1)) <= t4
    any_valid = t4 >= CMP_LEN - 1
    p_cmp, psum = [], []
    for g in groups:
        s = jnp.where(valid, s_cmp[g], NEG_INF)
        m = jnp.max(s, axis=0, keepdims=True)
        e = jnp.exp2(s - m)
        l = jnp.sum(e, axis=0, keepdims=True)
        p = e * jnp.where(any_valid, 1.0 / l, 0.0)
        p_cmp.append(p.astype(BF16))
        ph = p[:, 0:TQ]
        for h in range(1, NSA_HPG):
            ph = ph + p[:, h * TQ:(h + 1) * TQ]
        psum.append(ph)
    o_cmp = [jnp.dot(vcT_ref[g].astype(BF16), p_cmp[g], preferred_element_type=F32) for g in groups]
    imps = [jnp.dot(ovT_ref[...], psum[g], precision=HI, preferred_element_type=F32) for g in groups]
    hidden = jnp.full((CH, TQ), NEG_INF, F32)
    slab_bias = []
    for c in range(n_win):
        d = w0 + c - qi
        slab_bias.append(jnp.where(d == 0, causal_bias,
                                   jnp.where(d > 0, hidden, jnp.where(d == -n_back, window_bias, 0.0))))
    win_bias = jnp.concatenate(slab_bias, axis=0)
    win_p = [probs(s_win[g], win_bias, m_init())[2] for g in groups]
    o_win = []
    for g in groups:
        v_win = jnp.concatenate([vT_ref[NSA_KV_WIDTH + HD * g:NSA_KV_WIDTH + HD * (g + 1), pl.ds(rw, n_win * CH)],
                                 jnp.ones((BF16_ROWS, n_win * CH), BF16)], axis=0)
        acc = pv(v_win, win_p[g])
        o_win.append(acc[:HD] / acc[HD:HD + 1])
    cur = t1 // SLC_BLK
    forced = (j_io == 0) | (j_io == cur) | (j_io == cur - 1)
    causal = j_io <= cur
    vals = [jnp.where(causal, jnp.where(forced, imps[g] + FORCE_BONUS, imps[g]), -1.0) for g in groups]
    for g in groups:
        imp_scr[g] = vals[g]

    def rank_body(i2, cnts):
        out = list(cnts)
        for i in (2 * i2, 2 * i2 + 1):
            tie = jnp.where(j_io > i, 1.0, 0.0)
            for g in groups:
                vi = imp_scr[g, pl.ds(i, 1), :]
                out[g] = out[g] + jnp.where(vi > vals[g], 1.0, 0.0) + jnp.where(vi == vals[g], tie, 0.0)
        return tuple(out)

    cnts = lax.fori_loop(0, jnp.where(n_causal > n_sel, n_causal // 2, 0), rank_body,
                         tuple(jnp.zeros((n_slc, TQ), F32) for _ in groups))
    nb = CH // SLC_BLK
    for g in groups:
        selb = jnp.where(cnts[g] < float(n_sel), 0.0, NEG_INF)
        for jc in range(S // CH):
            rows = jnp.concatenate([selb[nb * jc:nb * (jc + 1), :], jnp.zeros((BF16_ROWS - nb, TQ), F32)], axis=0)
            sel_scr[g, jc] = jnp.concatenate([rows] * NSA_HPG, axis=1).astype(BF16)

    ones_rows = jnp.ones((BF16_ROWS, CH), BF16)
    zero_rows = jnp.zeros((HD - BF16_ROWS, n_lanes), BF16)
    acc_rows = HD + BF16_ROWS

    def slc_scores(jc):
        r0 = pl.multiple_of(jc * CH, CH)
        return [jnp.dot(kr_ref[pl.ds(r0, CH), LANES * g:LANES * (g + 1)],
                        jnp.concatenate([q64s[g], sel_scr[g, jc], zero_rows], axis=0),
                        preferred_element_type=F32) for g in groups]

    def values(jc, vrow):
        r0 = pl.multiple_of(jc * CH, CH)
        return [jnp.concatenate([vT_ref[vrow + HD * g:vrow + HD * (g + 1), pl.ds(r0, CH)], ones_rows], axis=0)
                for g in groups]

    def acc_init():
        return jnp.zeros((acc_rows, n_lanes), F32)

    def slc_half(jc, slot, ms):
        nxt = slc_scores(jc + 1)
        v_prev = values(jnp.maximum(jc - 1, 0), 0)
        pvs = [pv(v_prev[g], p_scr[g]) for g in groups]
        for g in groups:
            s_scr[1 - slot, g] = nxt[g]
        out = []
        for g in groups:
            m_new, alpha, pr = probs(s_scr[slot, g], None, ms[g])
            b_scr[g] = (b_scr[g] + pvs[g]) * alpha
            p_scr[g] = pr
            out.append(m_new)
        return tuple(out)

    first = slc_scores(0)
    for g in groups:
        s_scr[0, g] = first[g]
        p_scr[g] = jnp.zeros((CH, n_lanes), BF16)
        b_scr[g] = acc_init()
    ms = lax.fori_loop(0, qi // 2, lambda i, m: slc_half(2 * i + 1, 1, slc_half(2 * i, 0, m)),
                       tuple(m_init() for _ in groups))

    def slc_odd(_, m):
        m = slc_half(qi - 1, 0, m)
        s_scr[0] = s_scr[1]
        return m

    ms = lax.fori_loop(0, qi % 2, slc_odd, ms)

    v_prev = values(jnp.maximum(qi - 1, 0), 0)
    pvs = [pv(v_prev[g], p_scr[g]) for g in groups]
    slc_parts = [probs(s_scr[0, g], causal_bias, ms[g]) for g in groups]
    v_slc = values(qi, 0)
    o_slc = []
    for g in groups:
        acc = (b_scr[g] + pvs[g]) * slc_parts[g][1] + pv(v_slc[g], slc_parts[g][2])
        o_slc.append(acc[:HD] / acc[HD:HD + 1])

    for g in groups:
        rows = []
        for h in range(NSA_HPG):
            hh = NSA_HPG * g + h
            sl = slice(h * TQ, (h + 1) * TQ)
            rows.append(gT_ref[3 * hh:3 * hh + 1, :] * o_cmp[g][:, sl]
                        + gT_ref[3 * hh + 1:3 * hh + 2, :] * o_slc[g][:, sl]
                        + gT_ref[3 * hh + 2:3 * hh + 3, :] * o_win[g][:, sl])
        og = jnp.concatenate(rows, axis=0)
        o_ref[:, NSA_HPG * HD * g:NSA_HPG * HD * (g + 1)] = og.T


def _overlap_T(S):
    n_cmp = S // CMP_STRIDE
    n_slc = S // SLC_BLK
    cs = np.arange(n_cmp)[:, None] * CMP_STRIDE
    js = np.arange(n_slc)[None, :] * SLC_BLK
    ov = np.clip(np.minimum(cs + CMP_LEN, js + SLC_BLK) - np.maximum(cs, js), 0, None) / CMP_LEN
    ov[n_cmp - 1] = 0.0
    return jnp.asarray(ov.T, dtype=F32)


def _nsa(qT, kr, vT, kc, vcT, gT, B, S, TQ=NSA_CHUNK):
    n_cmp = S // CMP_STRIDE
    n_slc = S // SLC_BLK
    nq = S // TQ
    return pl.pallas_call(
        functools.partial(_nsa_kernel, S=S, TQ=TQ),
        grid=(B, nq),
        in_specs=[
            pl.BlockSpec((None, NSA_WIDTH, TQ), lambda b, i: (b, 0, i)),
            pl.BlockSpec((None, S, KR_WIDTH), lambda b, i: (b, 0, 0)),
            pl.BlockSpec((None, 2 * NSA_KV_WIDTH, S), lambda b, i: (b, 0, 0)),
            pl.BlockSpec((2, None, n_cmp, LANES), lambda b, i: (0, b, 0, 0)),
            pl.BlockSpec((2, None, HD, n_cmp), lambda b, i: (1, b, 0, 0)),
            pl.BlockSpec((None, 3 * NSA_HEADS, TQ), lambda b, i: (b, 0, i)),
            pl.BlockSpec((n_slc, n_cmp), lambda b, i: (0, 0)),
        ],
        out_specs=pl.BlockSpec((TQ, NSA_WIDTH), lambda b, i: (b * nq + i, 0)),
        out_shape=jax.ShapeDtypeStruct((B * S, NSA_WIDTH), F32),
        scratch_shapes=[
            pltpu.VMEM((NSA_KV_GROUPS, n_slc, TQ), F32),
            pltpu.VMEM((NSA_KV_GROUPS, S // TQ, BF16_ROWS, NSA_HPG * TQ), BF16),
            pltpu.VMEM((2, NSA_KV_GROUPS, TQ, NSA_HPG * TQ), F32),
            pltpu.VMEM((NSA_KV_GROUPS, TQ, NSA_HPG * TQ), BF16),
            pltpu.VMEM((NSA_KV_GROUPS, HD + BF16_ROWS, NSA_HPG * TQ), F32),
        ],
        compiler_params=_cparams(("arbitrary", "arbitrary")),
        name="nsa_attention",
    )(qT, kr.reshape(B, S, KR_WIDTH), vT, kc, vcT, gT, _overlap_T(S))


def _gdn_kernel(x_ref, xp_ref, sm_ref, aT_ref, z_ref, cw_ref, alog_ref, dt_ref, alogc_ref, dtc_ref, nw_ref,
                ebd_ref, eg_ref, eb_ref, lblk_ref, lT_ref, o_ref,
                xe_scr, st_scr, *, tg):
    C = GDN_CHUNK
    n_ch = tg // C
    n_pr = GDN_HEADS // 2
    i = pl.program_id(1)

    @pl.when(i == 0)
    def _():
        st_scr[...] = jnp.zeros_like(st_scr)

    def decay_terms():
        sm = sm_ref[...]
        g128 = -jnp.exp(alog_ref[...]) * _softplus(sm + dt_ref[...])
        gcs128 = jnp.dot(lblk_ref[...], g128, precision=HI, preferred_element_type=F32)
        g_exp = jnp.dot(gcs128, eg_ref[...], precision=HI, preferred_element_type=F32)
        b_exp = jnp.dot(jax.nn.sigmoid(sm).astype(BF16), eb_ref[...], preferred_element_type=F32)
        gT = -jnp.exp(alogc_ref[...]) * _softplus(aT_ref[...] + dtc_ref[...])
        return g_exp, b_exp, [jnp.dot(gT, lT_ref[c], precision=HI, preferred_element_type=F32)
                              for c in range(n_ch)]

    halo = F32_SUBLANES
    xe_scr[0:halo, :] = jnp.where(i == 0, 0.0, xp_ref[...])
    xe_scr[halo:halo + tg, :] = x_ref[...]
    ebd = ebd_ref[...]
    g_all, b_all, gts = decay_terms()
    n_split = 2
    rh = tg // n_split

    def conv_half(hf):
        base = halo - (GDN_CONV - 1) + rh * hf
        y = cw_ref[0:1, :] * xe_scr[base:base + rh, :]
        for j in range(1, GDN_CONV):
            y = y + cw_ref[j:j + 1, :] * xe_scr[base + j:base + j + rh, :]
        qkv = y * jax.nn.sigmoid(y)
        q = qkv[:, 0:GDN_WIDTH]
        k = qkv[:, GDN_WIDTH:2 * GDN_WIDTH]
        qss = jnp.dot((q * q).astype(BF16), ebd, preferred_element_type=F32)
        kss = jnp.dot((k * k).astype(BF16), ebd, preferred_element_type=F32)
        return q * lax.rsqrt(qss + 1e-6) * (HD ** -0.5), k * lax.rsqrt(kss + 1e-6), qkv[:, 2 * GDN_WIDTH:]

    halves = [conv_half(hf) for hf in range(n_split)]
    q_all, k_all, v_all = (jnp.concatenate([hv[n] for hv in halves], axis=0) for n in range(3))

    ii = lax.broadcasted_iota(jnp.int32, (C, LANES), 0)
    lane = lax.broadcasted_iota(jnp.int32, (C, LANES), 1)
    jj = lane % HD
    eye2 = jnp.where(ii == jj, 1.0, 0.0)
    lo_half = lane < HD
    lane2 = lax.broadcasted_iota(jnp.int32, (LANES, LANES), 1)
    row2 = lax.broadcasted_iota(jnp.int32, (LANES, LANES), 0)
    same_head = (lane2 < HD) == (row2 < HD)

    def bd(x):
        xb = x.astype(BF16)
        zero = jnp.zeros_like(xb)
        return jnp.concatenate([jnp.where(lo_half, xb, zero), jnp.where(lo_half, zero, xb)], axis=0)

    def mm(a, b):
        return jnp.dot(a.astype(BF16), b, preferred_element_type=F32)

    chains = [(c, pr) for c in range(n_ch) for pr in range(n_pr)]

    def blk(arr, c, pr):
        return arr[C * c:C * (c + 1), LANES * pr:LANES * (pr + 1)]

    qg, kdec, dec, aq, vb_bd, kbg_bd, eglast = {}, {}, {}, {}, {}, {}, {}
    for ch in chains:
        c, pr = ch
        qn, k_, gc, be = blk(q_all, c, pr), blk(k_all, c, pr), blk(g_all, c, pr), blk(b_all, c, pr)
        kb = k_ * be
        eg = jnp.exp(gc)
        glast = gc[C - 1:C, :]
        qg[ch] = qn * eg
        kdec[ch] = k_ * jnp.exp(glast - gc)
        eglast[ch] = jnp.exp(glast)
        vb_bd[ch] = bd(blk(v_all, c, pr) * be)
        kbg_bd[ch] = bd(kb * eg)
        gt = gts[c]
        gct = jnp.concatenate([jnp.broadcast_to(gt[2 * pr:2 * pr + 1, :], (C, HD)),
                               jnp.broadcast_to(gt[2 * pr + 1:2 * pr + 2, :], (C, HD))], axis=1)
        dec[ch] = jnp.exp(jnp.where(ii >= jj, gc - gct, NEG_INF))
        aq[ch] = lax.dot_general(jnp.concatenate([kb, qn], axis=0).astype(BF16), bd(k_), NT_DIMS,
                                 preferred_element_type=F32)
    a = {ch: jnp.where(ii > jj, aq[ch][:C] * dec[ch], 0.0) for ch in chains}
    qk = {ch: aq[ch][C:] * dec[ch] for ch in chains}
    x = {ch: eye2 - a[ch] for ch in chains}
    pw = {ch: mm(a[ch], bd(a[ch])) for ch in chains}
    for _ in range(4):
        pbd = {ch: bd(pw[ch]) for ch in chains}
        x = {ch: x[ch] + mm(x[ch], pbd[ch]) for ch in chains}
        pw = {ch: mm(pw[ch], pbd[ch]) for ch in chains}
    x = {ch: x[ch] + mm(x[ch], bd(pw[ch])) for ch in chains}
    uw = {ch: mm(x[ch], jnp.concatenate([vb_bd[ch], kbg_bd[ch]], axis=1)) for ch in chains}

    ku_kw = {ch: lax.dot_general(kdec[ch].astype(BF16), uw[ch].astype(BF16), TN_DIMS,
                                 preferred_element_type=F32) for ch in chains}
    qu_qw = {ch: mm(qk[ch], jnp.concatenate([bd(uw[ch][:, :LANES]), bd(uw[ch][:, LANES:])], axis=1))
             for ch in chains}
    lhs = {ch: jnp.concatenate([jnp.where(same_head, ku_kw[ch][:, LANES:], 0.0),
                                qg[ch] - qu_qw[ch][:, LANES:]], axis=0).astype(BF16) for ch in chains}
    st = [st_scr[:, LANES * pr:LANES * (pr + 1)] for pr in range(n_pr)]
    o_rows = []
    for c in range(n_ch):
        prod = [jnp.dot(lhs[c, pr], st[pr].astype(BF16), preferred_element_type=F32)
                for pr in range(n_pr)]
        o_rows.append(jnp.concatenate([prod[pr][LANES:] + qu_qw[c, pr][:, :LANES] for pr in range(n_pr)], axis=1))
        st = [st[pr] * eglast[c, pr] - prod[pr][:LANES] + jnp.where(same_head, ku_kw[c, pr][:, :LANES], 0.0)
              for pr in range(n_pr)]
    st_scr[...] = jnp.concatenate(st, axis=1)

    o = jnp.concatenate(o_rows, axis=0)
    oms = jnp.dot((o * o).astype(BF16), ebd, preferred_element_type=F32) * (1.0 / HD)
    z = z_ref[...]
    o_ref[...] = o * lax.rsqrt(oms + RMS_EPS) * nw_ref[...] * (z * jax.nn.sigmoid(z))


def _gdn(gqkv, sm, aT, gz, conv_w, a_log, dt_bias, norm_w, B, S, tg=256):
    rows = B * S
    nt = S // tg
    n_ch = tg // GDN_CHUNK
    head_of = np.arange(GDN_WIDTH) // HD
    ebd = jnp.asarray(head_of[:, None] == head_of[None, :], dtype=BF16)
    eg = np.zeros((LANES, GDN_WIDTH), np.float32)
    eb = np.zeros((LANES, GDN_WIDTH), np.float32)
    eg[SM_A + head_of, np.arange(GDN_WIDTH)] = 1.0
    eb[SM_B + head_of, np.arange(GDN_WIDTH)] = 1.0
    r = np.arange(tg)
    lblk = ((r[:, None] // GDN_CHUNK == r[None, :] // GDN_CHUNK) & (r[None, :] <= r[:, None])).astype(np.float32)
    lT = np.stack([lblk.T[:, GDN_CHUNK * c:GDN_CHUNK * (c + 1)] for c in range(n_ch)])
    a_log = a_log.astype(F32)
    dt_bias = dt_bias.astype(F32)
    alog128 = jnp.zeros((1, LANES), F32).at[0, SM_A:SM_A + GDN_HEADS].set(a_log)
    dt128 = jnp.zeros((1, LANES), F32).at[0, SM_A:SM_A + GDN_HEADS].set(dt_bias)
    nw512 = jnp.tile(norm_w.astype(F32), GDN_HEADS).reshape(1, GDN_WIDTH)
    full = lambda shape: pl.BlockSpec(shape, lambda b, i: (0,) * len(shape))
    return pl.pallas_call(
        functools.partial(_gdn_kernel, tg=tg),
        grid=(B, nt),
        in_specs=[
            pl.BlockSpec((tg, 3 * GDN_WIDTH), lambda b, i: (b * nt + i, 0)),
            pl.BlockSpec((F32_SUBLANES, 3 * GDN_WIDTH),
                         lambda b, i: (jnp.maximum((b * nt + i) * (tg // F32_SUBLANES) - 1, 0), 0)),
            pl.BlockSpec((tg, LANES), lambda b, i: (b * nt + i, 0)),
            pl.BlockSpec((None, GDN_HEADS, tg), lambda b, i: (b, 0, i)),
            pl.BlockSpec((tg, GDN_WIDTH), lambda b, i: (b * nt + i, 0)),
            full((GDN_CONV, 3 * GDN_WIDTH)),
            full((1, LANES)),
            full((1, LANES)),
            full((GDN_HEADS, 1)),
            full((GDN_HEADS, 1)),
            full((1, GDN_WIDTH)),
            full((GDN_WIDTH, GDN_WIDTH)),
            full((LANES, GDN_WIDTH)),
            full((LANES, GDN_WIDTH)),
            full((tg, tg)),
            full((n_ch, tg, GDN_CHUNK)),
        ],
        out_specs=pl.BlockSpec((tg, GDN_WIDTH), lambda b, i: (b * nt + i, 0)),
        out_shape=jax.ShapeDtypeStruct((rows, GDN_WIDTH), F32),
        scratch_shapes=[
            pltpu.VMEM((tg + F32_SUBLANES, 3 * GDN_WIDTH), F32),
            pltpu.VMEM((LANES, GDN_WIDTH), F32),
        ],
        compiler_params=_cparams(("arbitrary", "arbitrary")),
        name="gdn",
    )(gqkv, gqkv, sm, aT, gz, conv_w, alog128, dt128, a_log.reshape(GDN_HEADS, 1), dt_bias.reshape(GDN_HEADS, 1),
      nw512, ebd, jnp.asarray(eg), jnp.asarray(eb, dtype=BF16), jnp.asarray(lblk), jnp.asarray(lT))


def _mix_xattn_kernel(x_ref, a_ref, b_ref, wa_ref, wb_ref, nw_ref, wq_ref, kv_ref, wo_ref, o_ref):
    mix = jnp.dot(a_ref[...].astype(BF16), wa_ref[...], preferred_element_type=F32)
    mix = mix + jnp.dot(b_ref[...].astype(BF16), wb_ref[...], preferred_element_type=F32)
    _xattn_rows(x_ref[...] + mix, nw_ref, wq_ref, kv_ref, wo_ref, o_ref)


def _xattn_kernel(x_ref, nw_ref, wq_ref, kv_ref, wo_ref, o_ref):
    _xattn_rows(x_ref[...], nw_ref, wq_ref, kv_ref, wo_ref, o_ref)


def _xattn_rows(x, nw_ref, wq_ref, kv_ref, wo_ref, o_ref):
    hn = _rms(x, nw_ref[...]).astype(BF16)
    q = jnp.dot(hn, wq_ref[...], preferred_element_type=F32) * (XA_HD ** -0.5)
    heads = range(XA_HEADS)
    ss = [lax.dot_general(q[:, XA_HD * h:XA_HD * (h + 1)].astype(BF16), kv_ref[:, XA_HD * h:XA_HD * (h + 1)],
                          NT_DIMS, preferred_element_type=F32) for h in heads]
    es = [jnp.exp(ss[h] - jnp.max(ss[h], axis=-1, keepdims=True)).astype(BF16) for h in heads]
    ones = jnp.ones((kv_ref.shape[0], XA_HD), BF16)
    pvs = [jnp.dot(es[h], jnp.concatenate([kv_ref[:, XA_WIDTH + XA_HD * h:XA_WIDTH + XA_HD * (h + 1)], ones], axis=1),
                   preferred_element_type=F32) for h in heads]
    o = jnp.concatenate([pvs[h][:, :XA_HD] / pvs[h][:, XA_HD:XA_HD + 1] for h in heads], axis=1).astype(BF16)
    o_ref[...] = x + jnp.dot(o, wo_ref[...], preferred_element_type=F32)


def _xattn(x2d, norm_w, wq_bf, memkv_l, wo_bf, S, tm=512):
    rows = x2d.shape[0]
    per_b = S // tm
    mem_len = memkv_l.shape[0] // (rows // S)
    return pl.pallas_call(
        _xattn_kernel,
        grid=(rows // tm,),
        in_specs=[
            pl.BlockSpec((tm, D_MODEL), lambda i: (i, 0)),
            pl.BlockSpec((1, D_MODEL), lambda i: (0, 0)),
            pl.BlockSpec((D_MODEL, XA_WIDTH), lambda i: (0, 0)),
            pl.BlockSpec((mem_len, 2 * XA_WIDTH), lambda i: (i // per_b, 0)),
            pl.BlockSpec((XA_WIDTH, D_MODEL), lambda i: (0, 0)),
        ],
        out_specs=pl.BlockSpec((tm, D_MODEL), lambda i: (i, 0)),
        out_shape=jax.ShapeDtypeStruct((rows, D_MODEL), F32),
        compiler_params=_cparams(("arbitrary",)),
        name="cross_attention",
    )(x2d, norm_w.reshape(1, D_MODEL), wq_bf, memkv_l, wo_bf)


def _mix_xattn(x2d, o_nsa, o_gdn, w_out_bf, norm_w, wq_bf, memkv_l, wo_bf, S, tm=512):
    rows = x2d.shape[0]
    per_b = S // tm
    mem_len = memkv_l.shape[0] // (rows // S)
    return pl.pallas_call(
        _mix_xattn_kernel,
        grid=(rows // tm,),
        in_specs=[
            pl.BlockSpec((tm, D_MODEL), lambda i: (i, 0)),
            pl.BlockSpec((tm, NSA_WIDTH), lambda i: (i, 0)),
            pl.BlockSpec((tm, GDN_WIDTH), lambda i: (i, 0)),
            pl.BlockSpec((NSA_WIDTH, D_MODEL), lambda i: (0, 0)),
            pl.BlockSpec((GDN_WIDTH, D_MODEL), lambda i: (1, 0)),
            pl.BlockSpec((1, D_MODEL), lambda i: (0, 0)),
            pl.BlockSpec((D_MODEL, XA_WIDTH), lambda i: (0, 0)),
            pl.BlockSpec((mem_len, 2 * XA_WIDTH), lambda i: (i // per_b, 0)),
            pl.BlockSpec((XA_WIDTH, D_MODEL), lambda i: (0, 0)),
        ],
        out_specs=pl.BlockSpec((tm, D_MODEL), lambda i: (i, 0)),
        out_shape=jax.ShapeDtypeStruct((rows, D_MODEL), F32),
        compiler_params=_cparams(("arbitrary",)),
        name="mix_cross_attention",
    )(x2d, o_nsa, o_gdn, w_out_bf, w_out_bf, norm_w.reshape(1, D_MODEL), wq_bf, memkv_l, wo_bf)


def _mlp_kernel(x_ref, nw_ref, w1_ref, w2_ref, fw_ref, o_ref, hn_scr, acc_scr, *, final_norm):
    j = pl.program_id(1)

    @pl.when(j == 0)
    def _():
        x = x_ref[...]
        hn_scr[...] = _rms(x, nw_ref[...]).astype(BF16)
        acc_scr[...] = x

    h = jnp.dot(hn_scr[...], w1_ref[...], preferred_element_type=F32)
    h = jnp.square(jnp.maximum(h, 0.0)).astype(BF16)
    acc_scr[...] += jnp.dot(h, w2_ref[...], preferred_element_type=F32)

    @pl.when(j == pl.num_programs(1) - 1)
    def _():
        y = acc_scr[...]
        if final_norm:
            y = _rms(y, fw_ref[...])
        o_ref[...] = y


def _mlp(x2d, norm_w, w1_bf, w2_bf, final_w, final_norm, tm=1024, tf=1024):
    rows = x2d.shape[0]
    tm = min(tm, rows)
    return pl.pallas_call(
        functools.partial(_mlp_kernel, final_norm=final_norm),
        grid=(rows // tm, D_FF // tf),
        in_specs=[
            pl.BlockSpec((tm, D_MODEL), lambda i, j: (i, 0)),
            pl.BlockSpec((1, D_MODEL), lambda i, j: (0, 0)),
            pl.BlockSpec((D_MODEL, tf), lambda i, j: (0, j)),
            pl.BlockSpec((tf, D_MODEL), lambda i, j: (j, 0)),
            pl.BlockSpec((1, D_MODEL), lambda i, j: (0, 0)),
        ],
        out_specs=pl.BlockSpec((tm, D_MODEL), lambda i, j: (i, 0)),
        out_shape=jax.ShapeDtypeStruct((rows, D_MODEL), F32),
        scratch_shapes=[pltpu.VMEM((tm, D_MODEL), BF16), pltpu.VMEM((tm, D_MODEL), F32)],
        compiler_params=_cparams(("arbitrary", "arbitrary")),
        name="mlp",
    )(x2d, norm_w.reshape(1, D_MODEL), w1_bf, w2_bf, final_w.reshape(1, D_MODEL))


def _sc_kernel(x_ref, xp_ref, nw_ref, win_ref, cw_ref, wout_ref, o_ref, cu_scr, *, tm, per_b):
    i = pl.program_id(0)
    x = x_ref[...]
    nw = nw_ref[...]
    halo = F32_SUBLANES
    y = jnp.dot(_rms(x, nw).astype(BF16), win_ref[...], preferred_element_type=F32)
    yp = jnp.dot(_rms(xp_ref[...], nw).astype(BF16), win_ref[:, D_MODEL:], preferred_element_type=F32)
    cu_scr[0:halo, :] = jnp.where(i % per_b == 0, 0.0, yp[:, :D_MODEL] * yp[:, D_MODEL:])
    cu_scr[halo:halo + tm, :] = y[:, D_MODEL:2 * D_MODEL] * y[:, 2 * D_MODEL:]
    base = halo - (SC_WIDTH - 1)
    conv = cw_ref[0:1, :] * cu_scr[base:base + tm, :]
    for j in range(1, SC_WIDTH):
        conv = conv + cw_ref[j:j + 1, :] * cu_scr[base + j:base + j + tm, :]
    gated = (y[:, :D_MODEL] * conv).astype(BF16)
    o_ref[...] = x + jnp.dot(gated, wout_ref[...], preferred_element_type=F32)


def _sc_mixer(x2d, norm_w, win_bf, conv_w, wout_bf, S, tm=512):
    rows = x2d.shape[0]
    per_b = S // tm
    return pl.pallas_call(
        functools.partial(_sc_kernel, tm=tm, per_b=per_b),
        grid=(rows // tm,),
        in_specs=[
            pl.BlockSpec((tm, D_MODEL), lambda i: (i, 0)),
            pl.BlockSpec((F32_SUBLANES, D_MODEL), lambda i: (jnp.maximum(i * (tm // F32_SUBLANES) - 1, 0), 0)),
            pl.BlockSpec((1, D_MODEL), lambda i: (0, 0)),
            pl.BlockSpec((D_MODEL, 3 * D_MODEL), lambda i: (0, 0)),
            pl.BlockSpec((SC_WIDTH, D_MODEL), lambda i: (0, 0)),
            pl.BlockSpec((D_MODEL, D_MODEL), lambda i: (0, 0)),
        ],
        out_specs=pl.BlockSpec((tm, D_MODEL), lambda i: (i, 0)),
        out_shape=jax.ShapeDtypeStruct((rows, D_MODEL), F32),
        scratch_shapes=[pltpu.VMEM((tm + F32_SUBLANES, D_MODEL), F32)],
        compiler_params=_cparams(("arbitrary",)),
        name="short_conv_mixer",
    )(x2d, x2d, norm_w.reshape(1, D_MODEL), win_bf, conv_w, wout_bf)


def _hybrid_mixer(x2d, positions, norm_w, w_in, ck_pos, ck_w1, ck_w2, cv_pos, cv_w1, cv_w2,
                  gdn_conv, gdn_a_log, gdn_dt_bias, gdn_norm, B, S):
    tm = 512
    invf = (ROPE_THETA ** (-jnp.arange(0, HD, 2, dtype=F32) / HD)).reshape(HD // 2, 1)
    pos3 = positions.astype(F32).reshape(B * S // tm, 1, tm)
    qT, kr, vT, ckv, gqkv, gz, sm, gT, aT = _proj0(x2d, norm_w, _prep_w_in(w_in), pos3, invf, B, S, tm)
    w1s, p8, w2s = _prep_compress_weights(ck_pos, ck_w1, ck_w2, cv_pos, cv_w1, cv_w2)
    kc, vcT = _compress(ckv, w1s, p8, w2s, B, S)
    o_nsa = _nsa(qT, kr, vT, kc, vcT, gT, B, S)
    o_gdn = _gdn(gqkv, sm, aT, gz, gdn_conv, gdn_a_log, gdn_dt_bias, gdn_norm, B, S)
    return o_nsa, o_gdn


def kernel(x, mem, positions, norm_mix, norm_xattn, norm_mlp, hyb_w_in, hyb_cmp_k_pos, hyb_cmp_k_w1, hyb_cmp_k_w2, hyb_cmp_v_pos, hyb_cmp_v_w1, hyb_cmp_v_w2, hyb_gdn_conv, hyb_gdn_a_log, hyb_gdn_dt_bias, hyb_gdn_norm, hyb_w_out, sc_w_in, sc_conv, sc_w_out, mem_norm, xa_wq, xa_wkv, xa_wo, mlp_w1, mlp_w2, final_norm):
    B, S, _ = x.shape
    depth = norm_mix.shape[0]
    x2d = x.reshape(B * S, D_MODEL)
    memkv = _memkv(mem.reshape(-1, D_MODEL), mem_norm, xa_wkv.astype(BF16))
    for layer in range(depth):
        j = layer // 2
        xa = (norm_xattn[layer], xa_wq[layer].astype(BF16), memkv[layer], xa_wo[layer].astype(BF16), S)
        if layer % 2 == 0:
            o_nsa, o_gdn = _hybrid_mixer(x2d, positions, norm_mix[layer], hyb_w_in[j], hyb_cmp_k_pos[j],
                                         hyb_cmp_k_w1[j], hyb_cmp_k_w2[j], hyb_cmp_v_pos[j], hyb_cmp_v_w1[j],
                                         hyb_cmp_v_w2[j], hyb_gdn_conv[j], hyb_gdn_a_log[j], hyb_gdn_dt_bias[j],
                                         hyb_gdn_norm[j], B, S)
            x2d = _mix_xattn(x2d, o_nsa, o_gdn, hyb_w_out[j].astype(BF16), *xa)
        else:
            x2d = _sc_mixer(x2d, norm_mix[layer], sc_w_in[j].astype(BF16), sc_conv[j], sc_w_out[j].astype(BF16), S)
            x2d = _xattn(x2d, *xa)
        x2d = _mlp(x2d, norm_mlp[layer], mlp_w1[layer].astype(BF16), mlp_w2[layer].astype(BF16),
                   final_norm, layer == depth - 1)
    return x2d.reshape(B, S, D_MODEL)
```

```python
import functools

import numpy as np
import jax
import jax.numpy as jnp
from jax import lax
from jax.experimental import pallas as pl
from jax.experimental.pallas import tpu as pltpu

F32 = jnp.float32
BF16 = jnp.bfloat16
HI = lax.Precision.HIGHEST

D_MODEL = 1024
MEM_LEN = 256
RMS_EPS = 1e-6
ROPE_THETA = 10000.0
NEG_INF = -1e30
FORCE_BONUS = 1e3
LOG2E = 1.4426950408889634

NSA_HEADS = 8
NSA_KV_GROUPS = 2
NSA_HPG = NSA_HEADS // NSA_KV_GROUPS
HD = 64
CMP_LEN = 32
CMP_STRIDE = 16
CMP_HID = 2 * HD
SLC_BLK = 64
SLC_TOPK = 16
WINDOW = 512

GDN_HEADS = 8
GDN_CONV = 4
GDN_CHUNK = 64
SC_WIDTH = 3
XA_HEADS = 4
XA_HD = 128
D_FF = 4 * D_MODEL

NSA_WIDTH = NSA_HEADS * HD
NSA_KV_WIDTH = NSA_KV_GROUPS * HD
GDN_WIDTH = GDN_HEADS * HD
XA_WIDTH = XA_HEADS * XA_HD
IN_SIZES = (NSA_WIDTH,) + (NSA_KV_WIDTH,) * 6 + (3 * NSA_HEADS,) + (GDN_WIDTH,) * 3 + (GDN_HEADS, GDN_HEADS, GDN_WIDTH)

V7X_VMEM_BYTES = 64 * 1024 * 1024
VMEM_LIMIT = V7X_VMEM_BYTES * 3 // 4
LANES = 128
F32_SUBLANES = 8
BF16_ROWS = 16

C_Q, C_KR, C_VT, C_CKV, C_GQKV, C_GZ, C_SM, NC0 = 0, 512, 768, 1024, 1280, 2816, 3328, 3456
SM_GATE, SM_A, SM_B = 0, 24, 32
NSA_CHUNK = 256
KR_WIDTH = 3 * LANES
KR_WIN = 2 * LANES

NT_DIMS = (((1,), (1,)), ((), ()))
TN_DIMS = (((0,), (0,)), ((), ()))


def _cparams(sem):
    return pltpu.CompilerParams(dimension_semantics=sem, vmem_limit_bytes=VMEM_LIMIT)


def _rms(x, w):
    return x * lax.rsqrt(jnp.mean(x * x, axis=-1, keepdims=True) + RMS_EPS) * w


def _softplus(x):
    return jnp.maximum(x, 0.0) + jnp.log(1.0 + jnp.exp(-jnp.abs(x)))


def _memkv_kernel(m_ref, nw_ref, w_ref, o_ref):
    hn = _rms(m_ref[...], nw_ref[...]).astype(BF16)
    o_ref[...] = jnp.dot(hn, w_ref[...], preferred_element_type=F32).astype(BF16)


def _memkv(mem2d, mem_norm, wkv_bf):
    rows = mem2d.shape[0]
    depth = wkv_bf.shape[0]
    tm = 512
    return pl.pallas_call(
        _memkv_kernel,
        grid=(depth, rows // tm),
        in_specs=[
            pl.BlockSpec((tm, D_MODEL), lambda l, i: (i, 0)),
            pl.BlockSpec((1, D_MODEL), lambda l, i: (0, 0)),
            pl.BlockSpec((None, D_MODEL, 2 * XA_WIDTH), lambda l, i: (l, 0, 0)),
        ],
        out_specs=pl.BlockSpec((None, tm, 2 * XA_WIDTH), lambda l, i: (l, i, 0)),
        out_shape=jax.ShapeDtypeStruct((depth, rows, 2 * XA_WIDTH), BF16),
        compiler_params=_cparams(("arbitrary", "arbitrary")),
        name="memkv",
    )(mem2d, mem_norm.reshape(1, D_MODEL), wkv_bf)


def _proj0_kernel(x_ref, nw_ref, w_ref, pos_ref, invf_ref,
                  qT_ref, kr_ref, vT_ref, ckv_ref, gqkv_ref, gz_ref, sm_ref, gT_ref, aT_ref):
    tm = x_ref.shape[0]
    hn = _rms(x_ref[...], nw_ref[...]).astype(BF16)
    y = jnp.dot(hn, w_ref[...], preferred_element_type=F32)

    ang = invf_ref[...] * pos_ref[...]
    c = jnp.cos(ang)
    s = jnp.sin(ang)
    cos_n = jnp.concatenate([c, c, c, c], axis=0).T
    sin_n = jnp.concatenate([-s, s, -s, s], axis=0).T
    lane = lax.broadcasted_iota(jnp.int32, (tm, LANES), 1)
    first_half = (lane % HD) < (HD // 2)

    def rope(xc):
        rot = jnp.where(first_half, pltpu.roll(xc, LANES - HD // 2, 1), pltpu.roll(xc, HD // 2, 1))
        return xc * cos_n + rot * sin_n

    roped = [rope(y[:, LANES * i:LANES * (i + 1)]) for i in range(6)]
    q = jnp.concatenate(roped[:4], axis=1) * (LOG2E * HD ** -0.5)
    qT_ref[...] = q.T.astype(BF16)
    row = lax.broadcasted_iota(jnp.int32, (tm, LANES), 0)
    blk_hot = jnp.where(lane - HD == (row // SLC_BLK) % (NSA_CHUNK // SLC_BLK), 1.0, 0.0)
    k_slc = roped[4]
    lo = lane < HD
    kr_ref[...] = jnp.concatenate([jnp.where(lo, k_slc, blk_hot),
                                   jnp.where(lo, pltpu.roll(k_slc, HD, 1), blk_hot),
                                   roped[5]], axis=1).astype(BF16)
    vT_ref[...] = y[:, C_VT:C_CKV].T.astype(BF16)
    for i in range(4):
        ckv_ref[i] = y[:, C_CKV + HD * i:C_CKV + HD * (i + 1)].astype(BF16)
    gqkv_ref[...] = y[:, C_GQKV:C_GZ]
    gz_ref[...] = y[:, C_GZ:C_SM]
    sm = y[:, C_SM:NC0]
    sm_ref[...] = sm
    gT_ref[...] = jax.nn.sigmoid(sm).T[:3 * NSA_HEADS, :]
    aT_ref[...] = sm.T[SM_A:SM_A + GDN_HEADS, :]


def _prep_w_in(w_in):
    offs = np.cumsum((0,) + IN_SIZES)
    (nq, kcmp, vcmp, kslc, vslc, kwin, vwin, ngate, gq, gk, gv, ga, gb, gz) = [
        w_in[:, offs[i]:offs[i + 1]] for i in range(len(IN_SIZES))]
    pad = jnp.zeros((D_MODEL, LANES - 3 * NSA_HEADS - 2 * GDN_HEADS), w_in.dtype)
    small = jnp.concatenate([ngate, ga, gb, pad], axis=1)
    return jnp.concatenate([nq, kslc, kwin, vslc, vwin, kcmp, vcmp, gq, gk, gv, gz, small], axis=1).astype(BF16)


def _proj0(x2d, norm_w, w_bf, pos3, invf, B, S, tm):
    rows = B * S
    nt = S // tm
    row_spec = lambda n: pl.BlockSpec((tm, n), lambda i: (i, 0))
    t_spec = lambda n: pl.BlockSpec((None, n, tm), lambda i: (i // nt, 0, i % nt))
    return pl.pallas_call(
        _proj0_kernel,
        grid=(rows // tm,),
        in_specs=[
            row_spec(D_MODEL),
            pl.BlockSpec((1, D_MODEL), lambda i: (0, 0)),
            pl.BlockSpec((D_MODEL, NC0), lambda i: (0, 0)),
            pl.BlockSpec((None, 1, tm), lambda i: (i, 0, 0)),
            pl.BlockSpec((HD // 2, 1), lambda i: (0, 0)),
        ],
        out_specs=[
            t_spec(NSA_WIDTH),
            row_spec(KR_WIDTH),
            t_spec(2 * NSA_KV_WIDTH),
            pl.BlockSpec((4, tm, HD), lambda i: (0, i, 0)),
            row_spec(3 * GDN_WIDTH),
            row_spec(GDN_WIDTH),
            row_spec(LANES),
            t_spec(3 * NSA_HEADS),
            t_spec(GDN_HEADS),
        ],
        out_shape=[
            jax.ShapeDtypeStruct((B, NSA_WIDTH, S), BF16),
            jax.ShapeDtypeStruct((rows, KR_WIDTH), BF16),
            jax.ShapeDtypeStruct((B, 2 * NSA_KV_WIDTH, S), BF16),
            jax.ShapeDtypeStruct((4, rows, HD), BF16),
            jax.ShapeDtypeStruct((rows, 3 * GDN_WIDTH), F32),
            jax.ShapeDtypeStruct((rows, GDN_WIDTH), F32),
            jax.ShapeDtypeStruct((rows, LANES), F32),
            jax.ShapeDtypeStruct((B, 3 * NSA_HEADS, S), F32),
            jax.ShapeDtypeStruct((B, GDN_HEADS, S), F32),
        ],
        compiler_params=_cparams(("arbitrary",)),
        name="proj0",
    )(x2d, norm_w.reshape(1, D_MODEL), w_bf, pos3, invf)


def _compress_kernel(t_ref, w1_ref, pos_ref, w2_ref, o_ref, oT_ref):
    nseg = t_ref.shape[0]
    w1 = w1_ref[...]
    ab = jnp.dot(t_ref[...], w1, preferred_element_type=F32)
    pc = jnp.dot(pos_ref[...].astype(BF16), w1, preferred_element_type=F32)
    bias = pc[0:1, :CMP_HID] + pc[1:2, CMP_HID:]
    hid = ab[:, :CMP_HID] + pltpu.roll(ab[:, CMP_HID:], nseg - 1, 0) + bias
    act = hid * jax.nn.sigmoid(hid)
    out = jnp.dot(act.astype(BF16), w2_ref[...], preferred_element_type=F32)
    o_ref[...] = out
    oT_ref[...] = out.T[:HD, :]


def _compress(ckv, w1s, pos8, w2s, B, S):
    nseg = S // CMP_STRIDE
    t = ckv.reshape(4, B, nseg, CMP_STRIDE * HD)
    return pl.pallas_call(
        _compress_kernel,
        grid=(4, B),
        in_specs=[
            pl.BlockSpec((None, None, nseg, CMP_STRIDE * HD), lambda k, b: (k, b, 0, 0)),
            pl.BlockSpec((None, CMP_STRIDE * HD, 2 * CMP_HID), lambda k, b: (k // 2, 0, 0)),
            pl.BlockSpec((None, F32_SUBLANES, CMP_STRIDE * HD), lambda k, b: (k // 2, 0, 0)),
            pl.BlockSpec((None, CMP_HID, LANES), lambda k, b: (k // 2, 0, 0)),
        ],
        out_specs=[
            pl.BlockSpec((None, None, nseg, LANES), lambda k, b: (k, b, 0, 0)),
            pl.BlockSpec((None, None, HD, nseg), lambda k, b: (k, b, 0, 0)),
        ],
        out_shape=[
            jax.ShapeDtypeStruct((4, B, nseg, LANES), F32),
            jax.ShapeDtypeStruct((4, B, HD, nseg), F32),
        ],
        compiler_params=_cparams(("arbitrary", "arbitrary")),
        name="nsa_compress",
    )(t, w1s, pos8, w2s)


def _prep_compress_weights(k_pos, k_w1, k_w2, v_pos, v_w1, v_w2):
    half = CMP_STRIDE * HD

    def w1cat(w1):
        return jnp.concatenate([w1[:half], w1[half:]], axis=1)

    def pos8(p):
        flat = p.reshape(2, half)
        return jnp.concatenate([flat, jnp.zeros((F32_SUBLANES - 2, half), p.dtype)], axis=0)

    def w2pad(w2):
        return jnp.concatenate([w2, jnp.zeros((CMP_HID, LANES - HD), w2.dtype)], axis=1)

    w1s = jnp.stack([w1cat(k_w1), w1cat(v_w1)]).astype(BF16)
    p8 = jnp.stack([pos8(k_pos), pos8(v_pos)])
    w2s = jnp.stack([w2pad(k_w2), w2pad(v_w2)]).astype(BF16)
    return w1s, p8, w2s


def _nsa_kernel(qT_ref, kr_ref, vT_ref, kc_ref, vcT_ref, gT_ref, ovT_ref, o_ref,
                imp_scr, sel_scr, s_scr, p_scr, b_scr, *, S, TQ):
    CH = TQ
    groups = range(NSA_KV_GROUPS)
    n_cmp = S // CMP_STRIDE
    n_slc = S // SLC_BLK
    n_sel = min(SLC_TOPK, n_slc)
    qi = pl.program_id(1)
    q0 = qi * TQ
    t1 = q0 + lax.broadcasted_iota(jnp.int32, (1, TQ), 1)
    t4 = jnp.concatenate([t1] * NSA_HPG, axis=1)
    n_io = lax.broadcasted_iota(jnp.int32, (n_cmp, 1), 0)
    j_io = lax.broadcasted_iota(jnp.int32, (n_slc, 1), 0)
    k_io = lax.broadcasted_iota(jnp.int32, (CH, 1), 0)
    n_causal = (q0 + TQ) // SLC_BLK

    qs, q64s = [], []
    for g in groups:
        q64 = jnp.concatenate(
            [qT_ref[NSA_HPG * HD * g + HD * h:NSA_HPG * HD * g + HD * (h + 1), :] for h in range(NSA_HPG)], axis=1)
        zq = jnp.zeros_like(q64)
        qs.append(jnp.concatenate([q64, zq] if g == 0 else [zq, q64], axis=0))
        q64s.append(q64)

    n_lanes = NSA_HPG * TQ

    def probs(sc, bias, m_i):
        if bias is not None:
            sc = sc + jnp.concatenate([bias] * NSA_HPG, axis=1)
        m_new = jnp.maximum(m_i, jnp.max(sc, axis=0, keepdims=True))
        return m_new, jnp.exp2(m_i - m_new), jnp.exp2(sc - m_new).astype(BF16)

    def pv(vt, pr):
        return jnp.dot(vt, pr, preferred_element_type=F32)

    def m_init():
        return jnp.full((1, n_lanes), NEG_INF, F32)

    u_io = lax.broadcasted_iota(jnp.int32, (1, TQ), 1)
    causal_bias = jnp.where(k_io <= u_io, 0.0, NEG_INF)
    window_bias = jnp.where(k_io > u_io, 0.0, NEG_INF)

    s_cmp = [jnp.dot(kc_ref[g, :, 0:HD].astype(BF16), q64s[g], preferred_element_type=F32)
             for g in groups]
    n_back = WINDOW // CH
    n_win = n_back + 1
    w0 = jnp.maximum(qi - n_back, 0)
    rw = pl.multiple_of(w0 * CH, CH)
    k_win = kr_ref[pl.ds(rw, n_win * CH), KR_WIN:KR_WIN + LANES]
    s_win = [jnp.dot(k_win, qs[g], preferred_element_type=F32) for g in groups]
    valid = (n_io * CMP_STRIDE + (CMP_LEN - 1)) <= t4
    any_valid = t4 >= CMP_LEN - 1
    p_cmp, psum = [], []
    for g in groups:
        s = jnp.where(valid, s_cmp[g], NEG_INF)
        m = jnp.max(s, axis=0, keepdims=True)
        e = jnp.exp2(s - m)
        l = jnp.sum(e, axis=0, keepdims=True)
        p = e * jnp.where(any_valid, 1.0 / l, 0.0)
        p_cmp.append(p.astype(BF16))
        ph = p[:, 0:TQ]
        for h in range(1, NSA_HPG):
            ph = ph + p[:, h * TQ:(h + 1) * TQ]
        psum.append(ph)
    o_cmp = [jnp.dot(vcT_ref[g].astype(BF16), p_cmp[g], preferred_element_type=F32) for g in groups]
    imps = [jnp.dot(ovT_ref[...], psum[g], precision=HI, preferred_element_type=F32) for g in groups]
    hidden = jnp.full((CH, TQ), NEG_INF, F32)
    slab_bias = []
    for c in range(n_win):
        d = w0 + c - qi
        slab_bias.append(jnp.where(d == 0, causal_bias,
                                   jnp.where(d > 0, hidden, jnp.where(d == -n_back, window_bias, 0.0))))
    win_bias = jnp.concatenate(slab_bias, axis=0)
    win_p = [probs(s_win[g], win_bias, m_init())[2] for g in groups]
    o_win = []
    for g in groups:
        v_win = jnp.concatenate([vT_ref[NSA_KV_WIDTH + HD * g:NSA_KV_WIDTH + HD * (g + 1), pl.ds(rw, n_win * CH)],
                                 jnp.ones((BF16_ROWS, n_win * CH), BF16)], axis=0)
        acc = pv(v_win, win_p[g])
        o_win.append(acc[:HD] / acc[HD:HD + 1])
    cur = t1 // SLC_BLK
    forced = (j_io == 0) | (j_io == cur) | (j_io == cur - 1)
    causal = j_io <= cur
    vals = [jnp.where(causal, jnp.where(forced, imps[g] + FORCE_BONUS, imps[g]), -1.0) for g in groups]
    for g in groups:
        imp_scr[g] = vals[g]

    def rank_body(i2, cnts):
        out = list(cnts)
        for i in (2 * i2, 2 * i2 + 1):
            tie = jnp.where(j_io > i, 1.0, 0.0)
            for g in groups:
                vi = imp_scr[g, pl.ds(i, 1), :]
                out[g] = out[g] + jnp.where(vi > vals[g], 1.0, 0.0) + jnp.where(vi == vals[g], tie, 0.0)
        return tuple(out)

    cnts = lax.fori_loop(0, jnp.where(n_causal > n_sel, n_causal // 2, 0), rank_body,
                         tuple(jnp.zeros((n_slc, TQ), F32) for _ in groups))
    nb = CH // SLC_BLK
    for g in groups:
        selb = jnp.where(cnts[g] < float(n_sel), 0.0, NEG_INF)
        for jc in range(S // CH):
            rows = jnp.concatenate([selb[nb * jc:nb * (jc + 1), :], jnp.zeros((BF16_ROWS - nb, TQ), F32)], axis=0)
            sel_scr[g, jc] = jnp.concatenate([rows] * NSA_HPG, axis=1).astype(BF16)

    ones_rows = jnp.ones((BF16_ROWS, CH), BF16)
    zero_rows = jnp.zeros((HD - BF16_ROWS, n_lanes), BF16)
    acc_rows = HD + BF16_ROWS

    def slc_scores(jc):
        r0 = pl.multiple_of(jc * CH, CH)
        return [jnp.dot(kr_ref[pl.ds(r0, CH), LANES * g:LANES * (g + 1)],
                        jnp.concatenate([q64s[g], sel_scr[g, jc], zero_rows], axis=0),
                        preferred_element_type=F32) for g in groups]

    def values(jc, vrow):
        r0 = pl.multiple_of(jc * CH, CH)
        return [jnp.concatenate([vT_ref[vrow + HD * g:vrow + HD * (g + 1), pl.ds(r0, CH)], ones_rows], axis=0)
                for g in groups]

    def acc_init():
        return jnp.zeros((acc_rows, n_lanes), F32)

    def slc_half(jc, slot, ms):
        nxt = slc_scores(jc + 1)
        v_prev = values(jnp.maximum(jc - 1, 0), 0)
        pvs = [pv(v_prev[g], p_scr[g]) for g in groups]
        for g in groups:
            s_scr[1 - slot, g] = nxt[g]
        out = []
        for g in groups:
            m_new, alpha, pr = probs(s_scr[slot, g], None, ms[g])
            b_scr[g] = (b_scr[g] + pvs[g]) * alpha
            p_scr[g] = pr
            out.append(m_new)
        return tuple(out)

    first = slc_scores(0)
    for g in groups:
        s_scr[0, g] = first[g]
        p_scr[g] = jnp.zeros((CH, n_lanes), BF16)
        b_scr[g] = acc_init()
    ms = lax.fori_loop(0, qi // 2, lambda i, m: slc_half(2 * i + 1, 1, slc_half(2 * i, 0, m)),
                       tuple(m_init() for _ in groups))

    def slc_odd(_, m):
        m = slc_half(qi - 1, 0, m)
        s_scr[0] = s_scr[1]
        return m

    ms = lax.fori_loop(0, qi % 2, slc_odd, ms)

    v_prev = values(jnp.maximum(qi - 1, 0), 0)
    pvs = [pv(v_prev[g], p_scr[g]) for g in groups]
    slc_parts = [probs(s_scr[0, g], causal_bias, ms[g]) for g in groups]
    v_slc = values(qi, 0)
    o_slc = []
    for g in groups:
        acc = (b_scr[g] + pvs[g]) * slc_parts[g][1] + pv(v_slc[g], slc_parts[g][2])
        o_slc.append(acc[:HD] / acc[HD:HD + 1])

    for g in groups:
        rows = []
        for h in range(NSA_HPG):
            hh = NSA_HPG * g + h
            sl = slice(h * TQ, (h + 1) * TQ)
            rows.append(gT_ref[3 * hh:3 * hh + 1, :] * o_cmp[g][:, sl]
                        + gT_ref[3 * hh + 1:3 * hh + 2, :] * o_slc[g][:, sl]
                        + gT_ref[3 * hh + 2:3 * hh + 3, :] * o_win[g][:, sl])
        og = jnp.concatenate(rows, axis=0)
        o_ref[:, NSA_HPG * HD * g:NSA_HPG * HD * (g + 1)] = og.T


def _overlap_T(S):
    n_cmp = S // CMP_STRIDE
    n_slc = S // SLC_BLK
    cs = np.arange(n_cmp)[:, None] * CMP_STRIDE
    js = np.arange(n_slc)[None, :] * SLC_BLK
    ov = np.clip(np.minimum(cs + CMP_LEN, js + SLC_BLK) - np.maximum(cs, js), 0, None) / CMP_LEN
    ov[n_cmp - 1] = 0.0
    return jnp.asarray(ov.T, dtype=F32)


def _nsa(qT, kr, vT, kc, vcT, gT, B, S, TQ=NSA_CHUNK):
    n_cmp = S // CMP_STRIDE
    n_slc = S // SLC_BLK
    nq = S // TQ
    return pl.pallas_call(
        functools.partial(_nsa_kernel, S=S, TQ=TQ),
        grid=(B, nq),
        in_specs=[
            pl.BlockSpec((None, NSA_WIDTH, TQ), lambda b, i: (b, 0, i)),
            pl.BlockSpec((None, S, KR_WIDTH), lambda b, i: (b, 0, 0)),
            pl.BlockSpec((None, 2 * NSA_KV_WIDTH, S), lambda b, i: (b, 0, 0)),
            pl.BlockSpec((2, None, n_cmp, LANES), lambda b, i: (0, b, 0, 0)),
            pl.BlockSpec((2, None, HD, n_cmp), lambda b, i: (1, b, 0, 0)),
            pl.BlockSpec((None, 3 * NSA_HEADS, TQ), lambda b, i: (b, 0, i)),
            pl.BlockSpec((n_slc, n_cmp), lambda b, i: (0, 0)),
        ],
        out_specs=pl.BlockSpec((TQ, NSA_WIDTH), lambda b, i: (b * nq + i, 0)),
        out_shape=jax.ShapeDtypeStruct((B * S, NSA_WIDTH), F32),
        scratch_shapes=[
            pltpu.VMEM((NSA_KV_GROUPS, n_slc, TQ), F32),
            pltpu.VMEM((NSA_KV_GROUPS, S // TQ, BF16_ROWS, NSA_HPG * TQ), BF16),
            pltpu.VMEM((2, NSA_KV_GROUPS, TQ, NSA_HPG * TQ), F32),
            pltpu.VMEM((NSA_KV_GROUPS, TQ, NSA_HPG * TQ), BF16),
            pltpu.VMEM((NSA_KV_GROUPS, HD + BF16_ROWS, NSA_HPG * TQ), F32),
        ],
        compiler_params=_cparams(("arbitrary", "arbitrary")),
        name="nsa_attention",
    )(qT, kr.reshape(B, S, KR_WIDTH), vT, kc, vcT, gT, _overlap_T(S))


def _gdn_kernel(x_ref, xp_ref, sm_ref, aT_ref, z_ref, cw_ref, alog_ref, dt_ref, alogc_ref, dtc_ref, nw_ref,
                ebd_ref, eg_ref, eb_ref, lblk_ref, lT_ref, o_ref,
                xe_scr, st_scr, *, tg):
    C = GDN_CHUNK
    n_ch = tg // C
    n_pr = GDN_HEADS // 2
    i = pl.program_id(1)

    @pl.when(i == 0)
    def _():
        st_scr[...] = jnp.zeros_like(st_scr)

    def decay_terms():
        sm = sm_ref[...]
        g128 = -jnp.exp(alog_ref[...]) * _softplus(sm + dt_ref[...])
        gcs128 = jnp.dot(lblk_ref[...], g128, precision=HI, preferred_element_type=F32)
        g_exp = jnp.dot(gcs128, eg_ref[...], precision=HI, preferred_element_type=F32)
        b_exp = jnp.dot(jax.nn.sigmoid(sm).astype(BF16), eb_ref[...], preferred_element_type=F32)
        gT = -jnp.exp(alogc_ref[...]) * _softplus(aT_ref[...] + dtc_ref[...])
        return g_exp, b_exp, [jnp.dot(gT, lT_ref[c], precision=HI, preferred_element_type=F32)
                              for c in range(n_ch)]

    halo = F32_SUBLANES
    xe_scr[0:halo, :] = jnp.where(i == 0, 0.0, xp_ref[...])
    xe_scr[halo:halo + tg, :] = x_ref[...]
    ebd = ebd_ref[...]
    g_all, b_all, gts = decay_terms()
    n_split = 2
    rh = tg // n_split

    def conv_half(hf):
        base = halo - (GDN_CONV - 1) + rh * hf
        y = cw_ref[0:1, :] * xe_scr[base:base + rh, :]
        for j in range(1, GDN_CONV):
            y = y + cw_ref[j:j + 1, :] * xe_scr[base + j:base + j + rh, :]
        qkv = y * jax.nn.sigmoid(y)
        q = qkv[:, 0:GDN_WIDTH]
        k = qkv[:, GDN_WIDTH:2 * GDN_WIDTH]
        qss = jnp.dot((q * q).astype(BF16), ebd, preferred_element_type=F32)
        kss = jnp.dot((k * k).astype(BF16), ebd, preferred_element_type=F32)
        return q * lax.rsqrt(qss + 1e-6) * (HD ** -0.5), k * lax.rsqrt(kss + 1e-6), qkv[:, 2 * GDN_WIDTH:]

    halves = [conv_half(hf) for hf in range(n_split)]
    q_all, k_all, v_all = (jnp.concatenate([hv[n] for hv in halves], axis=0) for n in range(3))

    ii = lax.broadcasted_iota(jnp.int32, (C, LANES), 0)
    lane = lax.broadcasted_iota(jnp.int32, (C, LANES), 1)
    jj = lane % HD
    eye2 = jnp.where(ii == jj, 1.0, 0.0)
    lo_half = lane < HD
    lane2 = lax.broadcasted_iota(jnp.int32, (LANES, LANES), 1)
    row2 = lax.broadcasted_iota(jnp.int32, (LANES, LANES), 0)
    same_head = (lane2 < HD) == (row2 < HD)

    def bd(x):
        xb = x.astype(BF16)
        zero = jnp.zeros_like(xb)
        return jnp.concatenate([jnp.where(lo_half, xb, zero), jnp.where(lo_half, zero, xb)], axis=0)

    def mm(a, b):
        return jnp.dot(a.astype(BF16), b, preferred_element_type=F32)

    chains = [(c, pr) for c in range(n_ch) for pr in range(n_pr)]

    def blk(arr, c, pr):
        return arr[C * c:C * (c + 1), LANES * pr:LANES * (pr + 1)]

    qg, kdec, dec, aq, vb_bd, kbg_bd, eglast = {}, {}, {}, {}, {}, {}, {}
    for ch in chains:
        c, pr = ch
        qn, k_, gc, be = blk(q_all, c, pr), blk(k_all, c, pr), blk(g_all, c, pr), blk(b_all, c, pr)
        kb = k_ * be
        eg = jnp.exp(gc)
        glast = gc[C - 1:C, :]
        qg[ch] = qn * eg
        kdec[ch] = k_ * jnp.exp(glast - gc)
        eglast[ch] = jnp.exp(glast)
        vb_bd[ch] = bd(blk(v_all, c, pr) * be)
        kbg_bd[ch] = bd(kb * eg)
        gt = gts[c]
        gct = jnp.concatenate([jnp.broadcast_to(gt[2 * pr:2 * pr + 1, :], (C, HD)),
                               jnp.broadcast_to(gt[2 * pr + 1:2 * pr + 2, :], (C, HD))], axis=1)
        dec[ch] = jnp.exp(jnp.where(ii >= jj, gc - gct, NEG_INF))
        aq[ch] = lax.dot_general(jnp.concatenate([kb, qn], axis=0).astype(BF16), bd(k_), NT_DIMS,
                                 preferred_element_type=F32)
    a = {ch: jnp.where(ii > jj, aq[ch][:C] * dec[ch], 0.0) for ch in chains}
    qk = {ch: aq[ch][C:] * dec[ch] for ch in chains}
    x = {ch: eye2 - a[ch] for ch in chains}
    pw = {ch: mm(a[ch], bd(a[ch])) for ch in chains}
    for _ in range(4):
        pbd = {ch: bd(pw[ch]) for ch in chains}
        x = {ch: x[ch] + mm(x[ch], pbd[ch]) for ch in chains}
        pw = {ch: mm(pw[ch], pbd[ch]) for ch in chains}
    x = {ch: x[ch] + mm(x[ch], bd(pw[ch])) for ch in chains}
    uw = {ch: mm(x[ch], jnp.concatenate([vb_bd[ch], kbg_bd[ch]], axis=1)) for ch in chains}

    ku_kw = {ch: lax.dot_general(kdec[ch].astype(BF16), uw[ch].astype(BF16), TN_DIMS,
                                 preferred_element_type=F32) for ch in chains}
    qu_qw = {ch: mm(qk[ch], jnp.concatenate([bd(uw[ch][:, :LANES]), bd(uw[ch][:, LANES:])], axis=1))
             for ch in chains}
    lhs = {ch: jnp.concatenate([jnp.where(same_head, ku_kw[ch][:, LANES:], 0.0),
                                qg[ch] - qu_qw[ch][:, LANES:]], axis=0).astype(BF16) for ch in chains}
    st = [st_scr[:, LANES * pr:LANES * (pr + 1)] for pr in range(n_pr)]
    o_rows = []
    for c in range(n_ch):
        prod = [jnp.dot(lhs[c, pr], st[pr].astype(BF16), preferred_element_type=F32)
                for pr in range(n_pr)]
        o_rows.append(jnp.concatenate([prod[pr][LANES:] + qu_qw[c, pr][:, :LANES] for pr in range(n_pr)], axis=1))
        st = [st[pr] * eglast[c, pr] - prod[pr][:LANES] + jnp.where(same_head, ku_kw[c, pr][:, :LANES], 0.0)
              for pr in range(n_pr)]
    st_scr[...] = jnp.concatenate(st, axis=1)

    o = jnp.concatenate(o_rows, axis=0)
    oms = jnp.dot((o * o).astype(BF16), ebd, preferred_element_type=F32) * (1.0 / HD)
    z = z_ref[...]
    o_ref[...] = o * lax.rsqrt(oms + RMS_EPS) * nw_ref[...] * (z * jax.nn.sigmoid(z))


def _gdn(gqkv, sm, aT, gz, conv_w, a_log, dt_bias, norm_w, B, S, tg=256):
    rows = B * S
    nt = S // tg
    n_ch = tg // GDN_CHUNK
    head_of = np.arange(GDN_WIDTH) // HD
    ebd = jnp.asarray(head_of[:, None] == head_of[None, :], dtype=BF16)
    eg = np.zeros((LANES, GDN_WIDTH), np.float32)
    eb = np.zeros((LANES, GDN_WIDTH), np.float32)
    eg[SM_A + head_of, np.arange(GDN_WIDTH)] = 1.0
    eb[SM_B + head_of, np.arange(GDN_WIDTH)] = 1.0
    r = np.arange(tg)
    lblk = ((r[:, None] // GDN_CHUNK == r[None, :] // GDN_CHUNK) & (r[None, :] <= r[:, None])).astype(np.float32)
    lT = np.stack([lblk.T[:, GDN_CHUNK * c:GDN_CHUNK * (c + 1)] for c in range(n_ch)])
    a_log = a_log.astype(F32)
    dt_bias = dt_bias.astype(F32)
    alog128 = jnp.zeros((1, LANES), F32).at[0, SM_A:SM_A + GDN_HEADS].set(a_log)
    dt128 = jnp.zeros((1, LANES), F32).at[0, SM_A:SM_A + GDN_HEADS].set(dt_bias)
    nw512 = jnp.tile(norm_w.astype(F32), GDN_HEADS).reshape(1, GDN_WIDTH)
    full = lambda shape: pl.BlockSpec(shape, lambda b, i: (0,) * len(shape))
    return pl.pallas_call(
        functools.partial(_gdn_kernel, tg=tg),
        grid=(B, nt),
        in_specs=[
            pl.BlockSpec((tg, 3 * GDN_WIDTH), lambda b, i: (b * nt + i, 0)),
            pl.BlockSpec((F32_SUBLANES, 3 * GDN_WIDTH),
                         lambda b, i: (jnp.maximum((b * nt + i) * (tg // F32_SUBLANES) - 1, 0), 0)),
            pl.BlockSpec((tg, LANES), lambda b, i: (b * nt + i, 0)),
            pl.BlockSpec((None, GDN_HEADS, tg), lambda b, i: (b, 0, i)),
            pl.BlockSpec((tg, GDN_WIDTH), lambda b, i: (b * nt + i, 0)),
            full((GDN_CONV, 3 * GDN_WIDTH)),
            full((1, LANES)),
            full((1, LANES)),
            full((GDN_HEADS, 1)),
            full((GDN_HEADS, 1)),
            full((1, GDN_WIDTH)),
            full((GDN_WIDTH, GDN_WIDTH)),
            full((LANES, GDN_WIDTH)),
            full((LANES, GDN_WIDTH)),
            full((tg, tg)),
            full((n_ch, tg, GDN_CHUNK)),
        ],
        out_specs=pl.BlockSpec((tg, GDN_WIDTH), lambda b, i: (b * nt + i, 0)),
        out_shape=jax.ShapeDtypeStruct((rows, GDN_WIDTH), F32),
        scratch_shapes=[
            pltpu.VMEM((tg + F32_SUBLANES, 3 * GDN_WIDTH), F32),
            pltpu.VMEM((LANES, GDN_WIDTH), F32),
        ],
        compiler_params=_cparams(("arbitrary", "arbitrary")),
        name="gdn",
    )(gqkv, gqkv, sm, aT, gz, conv_w, alog128, dt128, a_log.reshape(GDN_HEADS, 1), dt_bias.reshape(GDN_HEADS, 1),
      nw512, ebd, jnp.asarray(eg), jnp.asarray(eb, dtype=BF16), jnp.asarray(lblk), jnp.asarray(lT))


def _mix_xattn_kernel(x_ref, a_ref, b_ref, wa_ref, wb_ref, nw_ref, wq_ref, kv_ref, wo_ref, o_ref):
    mix = jnp.dot(a_ref[...].astype(BF16), wa_ref[...], preferred_element_type=F32)
    mix = mix + jnp.dot(b_ref[...].astype(BF16), wb_ref[...], preferred_element_type=F32)
    _xattn_rows(x_ref[...] + mix, nw_ref, wq_ref, kv_ref, wo_ref, o_ref)


def _xattn_kernel(x_ref, nw_ref, wq_ref, kv_ref, wo_ref, o_ref):
    _xattn_rows(x_ref[...], nw_ref, wq_ref, kv_ref, wo_ref, o_ref)


def _xattn_rows(x, nw_ref, wq_ref, kv_ref, wo_ref, o_ref):
    hn = _rms(x, nw_ref[...]).astype(BF16)
    q = jnp.dot(hn, wq_ref[...], preferred_element_type=F32) * (XA_HD ** -0.5)
    heads = range(XA_HEADS)
    ss = [lax.dot_general(q[:, XA_HD * h:XA_HD * (h + 1)].astype(BF16), kv_ref[:, XA_HD * h:XA_HD * (h + 1)],
                          NT_DIMS, preferred_element_type=F32) for h in heads]
    es = [jnp.exp(ss[h] - jnp.max(ss[h], axis=-1, keepdims=True)).astype(BF16) for h in heads]
    ones = jnp.ones((kv_ref.shape[0], XA_HD), BF16)
    pvs = [jnp.dot(es[h], jnp.concatenate([kv_ref[:, XA_WIDTH + XA_HD * h:XA_WIDTH + XA_HD * (h + 1)], ones], axis=1),
                   preferred_element_type=F32) for h in heads]
    o = jnp.concatenate([pvs[h][:, :XA_HD] / pvs[h][:, XA_HD:XA_HD + 1] for h in heads], axis=1).astype(BF16)
    o_ref[...] = x + jnp.dot(o, wo_ref[...], preferred_element_type=F32)


def _xattn(x2d, norm_w, wq_bf, memkv_l, wo_bf, S, tm=512):
    rows = x2d.shape[0]
    per_b = S // tm
    mem_len = memkv_l.shape[0] // (rows // S)
    return pl.pallas_call(
        _xattn_kernel,
        grid=(rows // tm,),
        in_specs=[
            pl.BlockSpec((tm, D_MODEL), lambda i: (i, 0)),
            pl.BlockSpec((1, D_MODEL), lambda i: (0, 0)),
            pl.BlockSpec((D_MODEL, XA_WIDTH), lambda i: (0, 0)),
            pl.BlockSpec((mem_len, 2 * XA_WIDTH), lambda i: (i // per_b, 0)),
            pl.BlockSpec((XA_WIDTH, D_MODEL), lambda i: (0, 0)),
        ],
        out_specs=pl.BlockSpec((tm, D_MODEL), lambda i: (i, 0)),
        out_shape=jax.ShapeDtypeStruct((rows, D_MODEL), F32),
        compiler_params=_cparams(("arbitrary",)),
        name="cross_attention",
    )(x2d, norm_w.reshape(1, D_MODEL), wq_bf, memkv_l, wo_bf)


def _mix_xattn(x2d, o_nsa, o_gdn, w_out_bf, norm_w, wq_bf, memkv_l, wo_bf, S, tm=512):
    rows = x2d.shape[0]
    per_b = S // tm
    mem_len = memkv_l.shape[0] // (rows // S)
    return pl.pallas_call(
        _mix_xattn_kernel,
        grid=(rows // tm,),
        in_specs=[
            pl.BlockSpec((tm, D_MODEL), lambda i: (i, 0)),
            pl.BlockSpec((tm, NSA_WIDTH), lambda i: (i, 0)),
            pl.BlockSpec((tm, GDN_WIDTH), lambda i: (i, 0)),
            pl.BlockSpec((NSA_WIDTH, D_MODEL), lambda i: (0, 0)),
            pl.BlockSpec((GDN_WIDTH, D_MODEL), lambda i: (1, 0)),
            pl.BlockSpec((1, D_MODEL), lambda i: (0, 0)),
            pl.BlockSpec((D_MODEL, XA_WIDTH), lambda i: (0, 0)),
            pl.BlockSpec((mem_len, 2 * XA_WIDTH), lambda i: (i // per_b, 0)),
            pl.BlockSpec((XA_WIDTH, D_MODEL), lambda i: (0, 0)),
        ],
        out_specs=pl.BlockSpec((tm, D_MODEL), lambda i: (i, 0)),
        out_shape=jax.ShapeDtypeStruct((rows, D_MODEL), F32),
        compiler_params=_cparams(("arbitrary",)),
        name="mix_cross_attention",
    )(x2d, o_nsa, o_gdn, w_out_bf, w_out_bf, norm_w.reshape(1, D_MODEL), wq_bf, memkv_l, wo_bf)


def _mlp_kernel(x_ref, nw_ref, w1_ref, w2_ref, fw_ref, o_ref, hn_scr, acc_scr, *, final_norm):
    j = pl.program_id(1)

    @pl.when(j == 0)
    def _():
        x = x_ref[...]
        hn_scr[...] = _rms(x, nw_ref[...]).astype(BF16)
        acc_scr[...] = x

    h = jnp.dot(hn_scr[...], w1_ref[...], preferred_element_type=F32)
    h = jnp.square(jnp.maximum(h, 0.0)).astype(BF16)
    acc_scr[...] += jnp.dot(h, w2_ref[...], preferred_element_type=F32)

    @pl.when(j == pl.num_programs(1) - 1)
    def _():
        y = acc_scr[...]
        if final_norm:
            y = _rms(y, fw_ref[...])
        o_ref[...] = y


def _mlp(x2d, norm_w, w1_bf, w2_bf, final_w, final_norm, tm=1024, tf=1024):
    rows = x2d.shape[0]
    tm = min(tm, rows)
    return pl.pallas_call(
        functools.partial(_mlp_kernel, final_norm=final_norm),
        grid=(rows // tm, D_FF // tf),
        in_specs=[
            pl.BlockSpec((tm, D_MODEL), lambda i, j: (i, 0)),
            pl.BlockSpec((1, D_MODEL), lambda i, j: (0, 0)),
            pl.BlockSpec((D_MODEL, tf), lambda i, j: (0, j)),
            pl.BlockSpec((tf, D_MODEL), lambda i, j: (j, 0)),
            pl.BlockSpec((1, D_MODEL), lambda i, j: (0, 0)),
        ],
        out_specs=pl.BlockSpec((tm, D_MODEL), lambda i, j: (i, 0)),
        out_shape=jax.ShapeDtypeStruct((rows, D_MODEL), F32),
        scratch_shapes=[pltpu.VMEM((tm, D_MODEL), BF16), pltpu.VMEM((tm, D_MODEL), F32)],
        compiler_params=_cparams(("arbitrary", "arbitrary")),
        name="mlp",
    )(x2d, norm_w.reshape(1, D_MODEL), w1_bf, w2_bf, final_w.reshape(1, D_MODEL))


def _sc_kernel(x_ref, nw_ref, win_ref, cw_ref, wout_ref, o_ref, cu_scr, *, tm, per_b):
    i = pl.program_id(0)
    halo = F32_SUBLANES

    @pl.when(i % per_b == 0)
    def _():
        cu_scr[tm:tm + halo, :] = jnp.zeros((halo, D_MODEL), F32)

    x = x_ref[...]
    y = jnp.dot(_rms(x, nw_ref[...]).astype(BF16), win_ref[...], preferred_element_type=F32)
    cu_scr[0:halo, :] = cu_scr[tm:tm + halo, :]
    cu_scr[halo:halo + tm, :] = y[:, D_MODEL:2 * D_MODEL] * y[:, 2 * D_MODEL:]
    base = halo - (SC_WIDTH - 1)
    conv = cw_ref[0:1, :] * cu_scr[base:base + tm, :]
    for j in range(1, SC_WIDTH):
        conv = conv + cw_ref[j:j + 1, :] * cu_scr[base + j:base + j + tm, :]
    gated = (y[:, :D_MODEL] * conv).astype(BF16)
    o_ref[...] = x + jnp.dot(gated, wout_ref[...], preferred_element_type=F32)


def _sc_mixer(x2d, norm_w, win_bf, conv_w, wout_bf, S, tm=512):
    rows = x2d.shape[0]
    per_b = S // tm
    return pl.pallas_call(
        functools.partial(_sc_kernel, tm=tm, per_b=per_b),
        grid=(rows // tm,),
        in_specs=[
            pl.BlockSpec((tm, D_MODEL), lambda i: (i, 0)),
            pl.BlockSpec((1, D_MODEL), lambda i: (0, 0)),
            pl.BlockSpec((D_MODEL, 3 * D_MODEL), lambda i: (0, 0)),
            pl.BlockSpec((SC_WIDTH, D_MODEL), lambda i: (0, 0)),
            pl.BlockSpec((D_MODEL, D_MODEL), lambda i: (0, 0)),
        ],
        out_specs=pl.BlockSpec((tm, D_MODEL), lambda i: (i, 0)),
        out_shape=jax.ShapeDtypeStruct((rows, D_MODEL), F32),
        scratch_shapes=[pltpu.VMEM((tm + F32_SUBLANES, D_MODEL), F32)],
        compiler_params=_cparams(("arbitrary",)),
        name="short_conv_mixer",
    )(x2d, norm_w.reshape(1, D_MODEL), win_bf, conv_w, wout_bf)


def _hybrid_mixer(x2d, positions, norm_w, w_in, ck_pos, ck_w1, ck_w2, cv_pos, cv_w1, cv_w2,
                  gdn_conv, gdn_a_log, gdn_dt_bias, gdn_norm, B, S):
    tm = 512
    invf = (ROPE_THETA ** (-jnp.arange(0, HD, 2, dtype=F32) / HD)).reshape(HD // 2, 1)
    pos3 = positions.astype(F32).reshape(B * S // tm, 1, tm)
    qT, kr, vT, ckv, gqkv, gz, sm, gT, aT = _proj0(x2d, norm_w, _prep_w_in(w_in), pos3, invf, B, S, tm)
    w1s, p8, w2s = _prep_compress_weights(ck_pos, ck_w1, ck_w2, cv_pos, cv_w1, cv_w2)
    kc, vcT = _compress(ckv, w1s, p8, w2s, B, S)
    o_nsa = _nsa(qT, kr, vT, kc, vcT, gT, B, S)
    o_gdn = _gdn(gqkv, sm, aT, gz, gdn_conv, gdn_a_log, gdn_dt_bias, gdn_norm, B, S)
    return o_nsa, o_gdn


def kernel(x, mem, positions, norm_mix, norm_xattn, norm_mlp, hyb_w_in, hyb_cmp_k_pos, hyb_cmp_k_w1, hyb_cmp_k_w2, hyb_cmp_v_pos, hyb_cmp_v_w1, hyb_cmp_v_w2, hyb_gdn_conv, hyb_gdn_a_log, hyb_gdn_dt_bias, hyb_gdn_norm, hyb_w_out, sc_w_in, sc_conv, sc_w_out, mem_norm, xa_wq, xa_wkv, xa_wo, mlp_w1, mlp_w2, final_norm):
    B, S, _ = x.shape
    depth = norm_mix.shape[0]
    x2d = x.reshape(B * S, D_MODEL)
    memkv = _memkv(mem.reshape(-1, D_MODEL), mem_norm, xa_wkv.astype(BF16))
    for layer in range(depth):
        j = layer // 2
        xa = (norm_xattn[layer], xa_wq[layer].astype(BF16), memkv[layer], xa_wo[layer].astype(BF16), S)
        if layer % 2 == 0:
            o_nsa, o_gdn = _hybrid_mixer(x2d, positions, norm_mix[layer], hyb_w_in[j], hyb_cmp_k_pos[j],
                                         hyb_cmp_k_w1[j], hyb_cmp_k_w2[j], hyb_cmp_v_pos[j], hyb_cmp_v_w1[j],
                                         hyb_cmp_v_w2[j], hyb_gdn_conv[j], hyb_gdn_a_log[j], hyb_gdn_dt_bias[j],
                                         hyb_gdn_norm[j], B, S)
            x2d = _mix_xattn(x2d, o_nsa, o_gdn, hyb_w_out[j].astype(BF16), *xa)
        else:
            x2d = _sc_mixer(x2d, norm_mix[layer], sc_w_in[j].astype(BF16), sc_conv[j], sc_w_out[j].astype(BF16), S)
            x2d = _xattn(x2d, *xa)
        x2d = _mlp(x2d, norm_mlp[layer], mlp_w1[layer].astype(BF16), mlp_w2[layer].astype(BF16),
                   final_norm, layer == depth - 1)
    return x2d.reshape(B, S, D_MODEL)
```

```python
import functools

import numpy as np
import jax
import jax.numpy as jnp
from jax import lax
from jax.experimental import pallas as pl
from jax.experimental.pallas import tpu as pltpu

F32 = jnp.float32
BF16 = jnp.bfloat16
HI = lax.Precision.HIGHEST

D_MODEL = 1024
MEM_LEN = 256
RMS_EPS = 1e-6
ROPE_THETA = 10000.0
NEG_INF = -1e30
FORCE_BONUS = 1e3
LOG2E = 1.4426950408889634

NSA_HEADS = 8
NSA_KV_GROUPS = 2
NSA_HPG = NSA_HEADS // NSA_KV_GROUPS
HD = 64
CMP_LEN = 32
CMP_STRIDE = 16
CMP_HID = 2 * HD
SLC_BLK = 64
SLC_TOPK = 16
WINDOW = 512

GDN_HEADS = 8
GDN_CONV = 4
GDN_CHUNK = 64
SC_WIDTH = 3
XA_HEADS = 4
XA_HD = 128
D_FF = 4 * D_MODEL

NSA_WIDTH = NSA_HEADS * HD
NSA_KV_WIDTH = NSA_KV_GROUPS * HD
GDN_WIDTH = GDN_HEADS * HD
XA_WIDTH = XA_HEADS * XA_HD
IN_SIZES = (NSA_WIDTH,) + (NSA_KV_WIDTH,) * 6 + (3 * NSA_HEADS,) + (GDN_WIDTH,) * 3 + (GDN_HEADS, GDN_HEADS, GDN_WIDTH)

V7X_VMEM_BYTES = 64 * 1024 * 1024
VMEM_LIMIT = V7X_VMEM_BYTES * 3 // 4
LANES = 128
F32_SUBLANES = 8
BF16_ROWS = 16

C_Q, C_KR, C_VT, C_CKV, C_GQKV, C_GZ, C_SM, NC0 = 0, 512, 768, 1024, 1280, 2816, 3328, 3456
SM_GATE, SM_A, SM_B = 0, 24, 32
NSA_CHUNK = 256
KR_WIDTH = 3 * LANES
KR_WIN = 2 * LANES

NT_DIMS = (((1,), (1,)), ((), ()))
TN_DIMS = (((0,), (0,)), ((), ()))


def _cparams(sem):
    return pltpu.CompilerParams(dimension_semantics=sem, vmem_limit_bytes=VMEM_LIMIT)


def _rms(x, w):
    return x * lax.rsqrt(jnp.mean(x * x, axis=-1, keepdims=True) + RMS_EPS) * w


def _softplus(x):
    return jnp.maximum(x, 0.0) + jnp.log(1.0 + jnp.exp(-jnp.abs(x)))


def _memkv_kernel(m_ref, nw_ref, w_ref, o_ref):
    hn = _rms(m_ref[...], nw_ref[...]).astype(BF16)
    o_ref[...] = jnp.dot(hn, w_ref[...], preferred_element_type=F32).astype(BF16)


def _memkv(mem2d, mem_norm, wkv_bf):
    rows = mem2d.shape[0]
    depth = wkv_bf.shape[0]
    tm = 512
    return pl.pallas_call(
        _memkv_kernel,
        grid=(depth, rows // tm),
        in_specs=[
            pl.BlockSpec((tm, D_MODEL), lambda l, i: (i, 0)),
            pl.BlockSpec((1, D_MODEL), lambda l, i: (0, 0)),
            pl.BlockSpec((None, D_MODEL, 2 * XA_WIDTH), lambda l, i: (l, 0, 0)),
        ],
        out_specs=pl.BlockSpec((None, tm, 2 * XA_WIDTH), lambda l, i: (l, i, 0)),
        out_shape=jax.ShapeDtypeStruct((depth, rows, 2 * XA_WIDTH), BF16),
        compiler_params=_cparams(("arbitrary", "arbitrary")),
        name="memkv",
    )(mem2d, mem_norm.reshape(1, D_MODEL), wkv_bf)


def _proj0_kernel(x_ref, nw_ref, w_ref, pos_ref, invf_ref,
                  qT_ref, kr_ref, vT_ref, ckv_ref, gqkv_ref, gz_ref, sm_ref, gT_ref):
    tm = x_ref.shape[0]
    hn = _rms(x_ref[...], nw_ref[...]).astype(BF16)
    y = jnp.dot(hn, w_ref[...], preferred_element_type=F32)

    ang = invf_ref[...] * pos_ref[...]
    c = jnp.cos(ang)
    s = jnp.sin(ang)
    cos_n = jnp.concatenate([c, c, c, c], axis=0).T
    sin_n = jnp.concatenate([-s, s, -s, s], axis=0).T
    lane = lax.broadcasted_iota(jnp.int32, (tm, LANES), 1)
    first_half = (lane % HD) < (HD // 2)

    def rope(xc):
        rot = jnp.where(first_half, pltpu.roll(xc, LANES - HD // 2, 1), pltpu.roll(xc, HD // 2, 1))
        return xc * cos_n + rot * sin_n

    roped = [rope(y[:, LANES * i:LANES * (i + 1)]) for i in range(6)]
    q = jnp.concatenate(roped[:4], axis=1) * (LOG2E * HD ** -0.5)
    qT_ref[...] = q.T.astype(BF16)
    row = lax.broadcasted_iota(jnp.int32, (tm, LANES), 0)
    blk_hot = jnp.where(lane - HD == (row // SLC_BLK) % (NSA_CHUNK // SLC_BLK), 1.0, 0.0)
    k_slc = roped[4]
    lo = lane < HD
    kr_ref[...] = jnp.concatenate([jnp.where(lo, k_slc, blk_hot),
                                   jnp.where(lo, pltpu.roll(k_slc, HD, 1), blk_hot),
                                   roped[5]], axis=1).astype(BF16)
    vT_ref[...] = y[:, C_VT:C_CKV].T.astype(BF16)
    for i in range(4):
        ckv_ref[i] = y[:, C_CKV + HD * i:C_CKV + HD * (i + 1)].astype(BF16)
    gqkv_ref[...] = y[:, C_GQKV:C_GZ]
    gz_ref[...] = y[:, C_GZ:C_SM]
    sm = y[:, C_SM:NC0]
    sm_ref[...] = sm
    gT_ref[...] = jax.nn.sigmoid(sm).T[:3 * NSA_HEADS, :]


def _prep_w_in(w_in):
    offs = np.cumsum((0,) + IN_SIZES)
    (nq, kcmp, vcmp, kslc, vslc, kwin, vwin, ngate, gq, gk, gv, ga, gb, gz) = [
        w_in[:, offs[i]:offs[i + 1]] for i in range(len(IN_SIZES))]
    pad = jnp.zeros((D_MODEL, LANES - 3 * NSA_HEADS - 2 * GDN_HEADS), w_in.dtype)
    small = jnp.concatenate([ngate, ga, gb, pad], axis=1)
    return jnp.concatenate([nq, kslc, kwin, vslc, vwin, kcmp, vcmp, gq, gk, gv, gz, small], axis=1).astype(BF16)


def _proj0(x2d, norm_w, w_bf, pos3, invf, B, S, tm):
    rows = B * S
    nt = S // tm
    row_spec = lambda n: pl.BlockSpec((tm, n), lambda i: (i, 0))
    t_spec = lambda n: pl.BlockSpec((None, n, tm), lambda i: (i // nt, 0, i % nt))
    return pl.pallas_call(
        _proj0_kernel,
        grid=(rows // tm,),
        in_specs=[
            row_spec(D_MODEL),
            pl.BlockSpec((1, D_MODEL), lambda i: (0, 0)),
            pl.BlockSpec((D_MODEL, NC0), lambda i: (0, 0)),
            pl.BlockSpec((None, 1, tm), lambda i: (i, 0, 0)),
            pl.BlockSpec((HD // 2, 1), lambda i: (0, 0)),
        ],
        out_specs=[
            t_spec(NSA_WIDTH),
            row_spec(KR_WIDTH),
            t_spec(2 * NSA_KV_WIDTH),
            pl.BlockSpec((4, tm, HD), lambda i: (0, i, 0)),
            row_spec(3 * GDN_WIDTH),
            row_spec(GDN_WIDTH),
            row_spec(LANES),
            t_spec(3 * NSA_HEADS),
        ],
        out_shape=[
            jax.ShapeDtypeStruct((B, NSA_WIDTH, S), BF16),
            jax.ShapeDtypeStruct((rows, KR_WIDTH), BF16),
            jax.ShapeDtypeStruct((B, 2 * NSA_KV_WIDTH, S), BF16),
            jax.ShapeDtypeStruct((4, rows, HD), BF16),
            jax.ShapeDtypeStruct((rows, 3 * GDN_WIDTH), F32),
            jax.ShapeDtypeStruct((rows, GDN_WIDTH), F32),
            jax.ShapeDtypeStruct((rows, LANES), F32),
            jax.ShapeDtypeStruct((B, 3 * NSA_HEADS, S), F32),
        ],
        compiler_params=_cparams(("arbitrary",)),
        name="proj0",
    )(x2d, norm_w.reshape(1, D_MODEL), w_bf, pos3, invf)


def _compress_kernel(t_ref, w1_ref, pos_ref, w2_ref, o_ref, oT_ref):
    nseg = t_ref.shape[0]
    w1 = w1_ref[...]
    ab = jnp.dot(t_ref[...], w1, preferred_element_type=F32)
    pc = jnp.dot(pos_ref[...].astype(BF16), w1, preferred_element_type=F32)
    bias = pc[0:1, :CMP_HID] + pc[1:2, CMP_HID:]
    hid = ab[:, :CMP_HID] + pltpu.roll(ab[:, CMP_HID:], nseg - 1, 0) + bias
    act = hid * jax.nn.sigmoid(hid)
    out = jnp.dot(act.astype(BF16), w2_ref[...], preferred_element_type=F32)
    o_ref[...] = out
    oT_ref[...] = out.T[:HD, :]


def _compress(ckv, w1s, pos8, w2s, B, S):
    nseg = S // CMP_STRIDE
    t = ckv.reshape(4, B, nseg, CMP_STRIDE * HD)
    return pl.pallas_call(
        _compress_kernel,
        grid=(4, B),
        in_specs=[
            pl.BlockSpec((None, None, nseg, CMP_STRIDE * HD), lambda k, b: (k, b, 0, 0)),
            pl.BlockSpec((None, CMP_STRIDE * HD, 2 * CMP_HID), lambda k, b: (k // 2, 0, 0)),
            pl.BlockSpec((None, F32_SUBLANES, CMP_STRIDE * HD), lambda k, b: (k // 2, 0, 0)),
            pl.BlockSpec((None, CMP_HID, LANES), lambda k, b: (k // 2, 0, 0)),
        ],
        out_specs=[
            pl.BlockSpec((None, None, nseg, LANES), lambda k, b: (k, b, 0, 0)),
            pl.BlockSpec((None, None, HD, nseg), lambda k, b: (k, b, 0, 0)),
        ],
        out_shape=[
            jax.ShapeDtypeStruct((4, B, nseg, LANES), F32),
            jax.ShapeDtypeStruct((4, B, HD, nseg), F32),
        ],
        compiler_params=_cparams(("arbitrary", "arbitrary")),
        name="nsa_compress",
    )(t, w1s, pos8, w2s)


def _prep_compress_weights(k_pos, k_w1, k_w2, v_pos, v_w1, v_w2):
    half = CMP_STRIDE * HD

    def w1cat(w1):
        return jnp.concatenate([w1[:half], w1[half:]], axis=1)

    def pos8(p):
        flat = p.reshape(2, half)
        return jnp.concatenate([flat, jnp.zeros((F32_SUBLANES - 2, half), p.dtype)], axis=0)

    def w2pad(w2):
        return jnp.concatenate([w2, jnp.zeros((CMP_HID, LANES - HD), w2.dtype)], axis=1)

    w1s = jnp.stack([w1cat(k_w1), w1cat(v_w1)]).astype(BF16)
    p8 = jnp.stack([pos8(k_pos), pos8(v_pos)])
    w2s = jnp.stack([w2pad(k_w2), w2pad(v_w2)]).astype(BF16)
    return w1s, p8, w2s


def _nsa_kernel(qT_ref, kr_ref, vT_ref, kc_ref, vcT_ref, gT_ref, ovT_ref, o_ref,
                imp_scr, sel_scr, s_scr, p_scr, b_scr, *, S, TQ):
    CH = TQ
    groups = range(NSA_KV_GROUPS)
    n_cmp = S // CMP_STRIDE
    n_slc = S // SLC_BLK
    n_sel = min(SLC_TOPK, n_slc)
    qi = pl.program_id(1)
    q0 = qi * TQ
    t1 = q0 + lax.broadcasted_iota(jnp.int32, (1, TQ), 1)
    t4 = jnp.concatenate([t1] * NSA_HPG, axis=1)
    n_io = lax.broadcasted_iota(jnp.int32, (n_cmp, 1), 0)
    j_io = lax.broadcasted_iota(jnp.int32, (n_slc, 1), 0)
    k_io = lax.broadcasted_iota(jnp.int32, (CH, 1), 0)
    n_causal = (q0 + TQ) // SLC_BLK

    qs, q64s = [], []
    for g in groups:
        q64 = jnp.concatenate(
            [qT_ref[NSA_HPG * HD * g + HD * h:NSA_HPG * HD * g + HD * (h + 1), :] for h in range(NSA_HPG)], axis=1)
        zq = jnp.zeros_like(q64)
        qs.append(jnp.concatenate([q64, zq] if g == 0 else [zq, q64], axis=0))
        q64s.append(q64)

    n_lanes = NSA_HPG * TQ

    def probs(sc, bias, m_i):
        if bias is not None:
            sc = sc + jnp.concatenate([bias] * NSA_HPG, axis=1)
        m_new = jnp.maximum(m_i, jnp.max(sc, axis=0, keepdims=True))
        return m_new, jnp.exp2(m_i - m_new), jnp.exp2(sc - m_new).astype(BF16)

    def pv(vt, pr):
        return jnp.dot(vt, pr, preferred_element_type=F32)

    def m_init():
        return jnp.full((1, n_lanes), NEG_INF, F32)

    u_io = lax.broadcasted_iota(jnp.int32, (1, TQ), 1)
    causal_bias = jnp.where(k_io <= u_io, 0.0, NEG_INF)
    window_bias = jnp.where(k_io > u_io, 0.0, NEG_INF)

    s_cmp = [jnp.dot(kc_ref[g, :, 0:HD].astype(BF16), q64s[g], preferred_element_type=F32)
             for g in groups]
    n_back = WINDOW // CH
    n_win = n_back + 1
    w0 = jnp.maximum(qi - n_back, 0)
    rw = pl.multiple_of(w0 * CH, CH)
    k_win = kr_ref[pl.ds(rw, n_win * CH), KR_WIN:KR_WIN + LANES]
    s_win = [jnp.dot(k_win, qs[g], preferred_element_type=F32) for g in groups]
    valid = (n_io * CMP_STRIDE + (CMP_LEN - 1)) <= t4
    any_valid = t4 >= CMP_LEN - 1
    p_cmp, psum = [], []
    for g in groups:
        s = jnp.where(valid, s_cmp[g], NEG_INF)
        m = jnp.max(s, axis=0, keepdims=True)
        e = jnp.exp2(s - m)
        l = jnp.sum(e, axis=0, keepdims=True)
        p = e * jnp.where(any_valid, 1.0 / l, 0.0)
        p_cmp.append(p.astype(BF16))
        ph = p[:, 0:TQ]
        for h in range(1, NSA_HPG):
            ph = ph + p[:, h * TQ:(h + 1) * TQ]
        psum.append(ph)
    o_cmp = [jnp.dot(vcT_ref[g].astype(BF16), p_cmp[g], preferred_element_type=F32) for g in groups]
    imps = [jnp.dot(ovT_ref[...], psum[g], precision=HI, preferred_element_type=F32) for g in groups]
    hidden = jnp.full((CH, TQ), NEG_INF, F32)
    slab_bias = []
    for c in range(n_win):
        d = w0 + c - qi
        slab_bias.append(jnp.where(d == 0, causal_bias,
                                   jnp.where(d > 0, hidden, jnp.where(d == -n_back, window_bias, 0.0))))
    win_bias = jnp.concatenate(slab_bias, axis=0)
    win_p = [probs(s_win[g], win_bias, m_init())[2] for g in groups]
    o_win = []
    for g in groups:
        v_win = jnp.concatenate([vT_ref[NSA_KV_WIDTH + HD * g:NSA_KV_WIDTH + HD * (g + 1), pl.ds(rw, n_win * CH)],
                                 jnp.ones((BF16_ROWS, n_win * CH), BF16)], axis=0)
        acc = pv(v_win, win_p[g])
        o_win.append(acc[:HD] / acc[HD:HD + 1])
    cur = t1 // SLC_BLK
    forced = (j_io == 0) | (j_io == cur) | (j_io == cur - 1)
    causal = j_io <= cur
    vals = [jnp.where(causal, jnp.where(forced, imps[g] + FORCE_BONUS, imps[g]), -1.0) for g in groups]
    for g in groups:
        imp_scr[g] = vals[g]

    def rank_body(i2, cnts):
        out = list(cnts)
        for i in (2 * i2, 2 * i2 + 1):
            tie = jnp.where(j_io > i, 1.0, 0.0)
            for g in groups:
                vi = imp_scr[g, pl.ds(i, 1), :]
                out[g] = out[g] + jnp.where(vi > vals[g], 1.0, 0.0) + jnp.where(vi == vals[g], tie, 0.0)
        return tuple(out)

    cnts = lax.fori_loop(0, jnp.where(n_causal > n_sel, n_causal // 2, 0), rank_body,
                         tuple(jnp.zeros((n_slc, TQ), F32) for _ in groups))
    nb = CH // SLC_BLK
    for g in groups:
        selb = jnp.where(cnts[g] < float(n_sel), 0.0, NEG_INF)
        for jc in range(S // CH):
            rows = jnp.concatenate([selb[nb * jc:nb * (jc + 1), :], jnp.zeros((BF16_ROWS - nb, TQ), F32)], axis=0)
            sel_scr[g, jc] = jnp.concatenate([rows] * NSA_HPG, axis=1).astype(BF16)

    ones_rows = jnp.ones((BF16_ROWS, CH), BF16)
    zero_rows = jnp.zeros((HD - BF16_ROWS, n_lanes), BF16)
    acc_rows = HD + BF16_ROWS

    def slc_scores(jc):
        r0 = pl.multiple_of(jc * CH, CH)
        return [jnp.dot(kr_ref[pl.ds(r0, CH), LANES * g:LANES * (g + 1)],
                        jnp.concatenate([q64s[g], sel_scr[g, jc], zero_rows], axis=0),
                        preferred_element_type=F32) for g in groups]

    def values(jc, vrow):
        r0 = pl.multiple_of(jc * CH, CH)
        return [jnp.concatenate([vT_ref[vrow + HD * g:vrow + HD * (g + 1), pl.ds(r0, CH)], ones_rows], axis=0)
                for g in groups]

    def acc_init():
        return jnp.zeros((acc_rows, n_lanes), F32)

    def slc_half(jc, slot, ms):
        nxt = slc_scores(jc + 1)
        v_prev = values(jnp.maximum(jc - 1, 0), 0)
        pvs = [pv(v_prev[g], p_scr[g]) for g in groups]
        for g in groups:
            s_scr[1 - slot, g] = nxt[g]
        out = []
        for g in groups:
            m_new, alpha, pr = probs(s_scr[slot, g], None, ms[g])
            b_scr[g] = (b_scr[g] + pvs[g]) * alpha
            p_scr[g] = pr
            out.append(m_new)
        return tuple(out)

    first = slc_scores(0)
    for g in groups:
        s_scr[0, g] = first[g]
        p_scr[g] = jnp.zeros((CH, n_lanes), BF16)
        b_scr[g] = acc_init()
    ms = lax.fori_loop(0, qi // 2, lambda i, m: slc_half(2 * i + 1, 1, slc_half(2 * i, 0, m)),
                       tuple(m_init() for _ in groups))

    def slc_odd(_, m):
        m = slc_half(qi - 1, 0, m)
        s_scr[0] = s_scr[1]
        return m

    ms = lax.fori_loop(0, qi % 2, slc_odd, ms)

    v_prev = values(jnp.maximum(qi - 1, 0), 0)
    pvs = [pv(v_prev[g], p_scr[g]) for g in groups]
    slc_parts = [probs(s_scr[0, g], causal_bias, ms[g]) for g in groups]
    v_slc = values(qi, 0)
    o_slc = []
    for g in groups:
        acc = (b_scr[g] + pvs[g]) * slc_parts[g][1] + pv(v_slc[g], slc_parts[g][2])
        o_slc.append(acc[:HD] / acc[HD:HD + 1])

    for g in groups:
        rows = []
        for h in range(NSA_HPG):
            hh = NSA_HPG * g + h
            sl = slice(h * TQ, (h + 1) * TQ)
            rows.append(gT_ref[3 * hh:3 * hh + 1, :] * o_cmp[g][:, sl]
                        + gT_ref[3 * hh + 1:3 * hh + 2, :] * o_slc[g][:, sl]
                        + gT_ref[3 * hh + 2:3 * hh + 3, :] * o_win[g][:, sl])
        og = jnp.concatenate(rows, axis=0)
        o_ref[:, NSA_HPG * HD * g:NSA_HPG * HD * (g + 1)] = og.T


def _overlap_T(S):
    n_cmp = S // CMP_STRIDE
    n_slc = S // SLC_BLK
    cs = np.arange(n_cmp)[:, None] * CMP_STRIDE
    js = np.arange(n_slc)[None, :] * SLC_BLK
    ov = np.clip(np.minimum(cs + CMP_LEN, js + SLC_BLK) - np.maximum(cs, js), 0, None) / CMP_LEN
    ov[n_cmp - 1] = 0.0
    return jnp.asarray(ov.T, dtype=F32)


def _nsa(qT, kr, vT, kc, vcT, gT, B, S, TQ=NSA_CHUNK):
    n_cmp = S // CMP_STRIDE
    n_slc = S // SLC_BLK
    nq = S // TQ
    return pl.pallas_call(
        functools.partial(_nsa_kernel, S=S, TQ=TQ),
        grid=(B, nq),
        in_specs=[
            pl.BlockSpec((None, NSA_WIDTH, TQ), lambda b, i: (b, 0, i)),
            pl.BlockSpec((None, S, KR_WIDTH), lambda b, i: (b, 0, 0)),
            pl.BlockSpec((None, 2 * NSA_KV_WIDTH, S), lambda b, i: (b, 0, 0)),
            pl.BlockSpec((2, None, n_cmp, LANES), lambda b, i: (0, b, 0, 0)),
            pl.BlockSpec((2, None, HD, n_cmp), lambda b, i: (1, b, 0, 0)),
            pl.BlockSpec((None, 3 * NSA_HEADS, TQ), lambda b, i: (b, 0, i)),
            pl.BlockSpec((n_slc, n_cmp), lambda b, i: (0, 0)),
        ],
        out_specs=pl.BlockSpec((TQ, NSA_WIDTH), lambda b, i: (b * nq + i, 0)),
        out_shape=jax.ShapeDtypeStruct((B * S, NSA_WIDTH), F32),
        scratch_shapes=[
            pltpu.VMEM((NSA_KV_GROUPS, n_slc, TQ), F32),
            pltpu.VMEM((NSA_KV_GROUPS, S // TQ, BF16_ROWS, NSA_HPG * TQ), BF16),
            pltpu.VMEM((2, NSA_KV_GROUPS, TQ, NSA_HPG * TQ), F32),
            pltpu.VMEM((NSA_KV_GROUPS, TQ, NSA_HPG * TQ), BF16),
            pltpu.VMEM((NSA_KV_GROUPS, HD + BF16_ROWS, NSA_HPG * TQ), F32),
        ],
        compiler_params=_cparams(("arbitrary", "arbitrary")),
        name="nsa_attention",
    )(qT, kr.reshape(B, S, KR_WIDTH), vT, kc, vcT, gT, _overlap_T(S))


def _gdn_kernel(x_ref, xp_ref, sm_ref, z_ref, cw_ref, alog_ref, dt_ref, nw_ref,
                ebd_ref, eg_ref, eb_ref, lblk_ref, o_ref,
                xe_scr, st_scr, *, tg):
    C = GDN_CHUNK
    n_ch = tg // C
    n_pr = GDN_HEADS // 2
    i = pl.program_id(1)

    @pl.when(i == 0)
    def _():
        st_scr[...] = jnp.zeros_like(st_scr)

    def decay_terms():
        sm = sm_ref[...]
        g128 = -jnp.exp(alog_ref[...]) * _softplus(sm + dt_ref[...])
        gcs128 = jnp.dot(lblk_ref[...], g128, precision=HI, preferred_element_type=F32)
        g_exp = jnp.dot(gcs128, eg_ref[...], precision=HI, preferred_element_type=F32)
        b_exp = jnp.dot(jax.nn.sigmoid(sm).astype(BF16), eb_ref[...], preferred_element_type=F32)
        return g_exp, b_exp, [gcs128[C * c:C * (c + 1), :].T[SM_A:SM_A + GDN_HEADS, :]
                              for c in range(n_ch)]

    halo = F32_SUBLANES
    xe_scr[0:halo, :] = jnp.where(i == 0, 0.0, xp_ref[...])
    xe_scr[halo:halo + tg, :] = x_ref[...]
    ebd = ebd_ref[...]
    g_all, b_all, gts = decay_terms()
    n_split = 2
    rh = tg // n_split

    def conv_half(hf):
        base = halo - (GDN_CONV - 1) + rh * hf
        y = cw_ref[0:1, :] * xe_scr[base:base + rh, :]
        for j in range(1, GDN_CONV):
            y = y + cw_ref[j:j + 1, :] * xe_scr[base + j:base + j + rh, :]
        qkv = y * jax.nn.sigmoid(y)
        q = qkv[:, 0:GDN_WIDTH]
        k = qkv[:, GDN_WIDTH:2 * GDN_WIDTH]
        ss = jnp.dot(jnp.concatenate([q * q, k * k], axis=0).astype(BF16), ebd, preferred_element_type=F32)
        return (q * lax.rsqrt(ss[:rh] + 1e-6) * (HD ** -0.5), k * lax.rsqrt(ss[rh:] + 1e-6),
                qkv[:, 2 * GDN_WIDTH:])

    halves = [conv_half(hf) for hf in range(n_split)]
    q_all, k_all, v_all = (jnp.concatenate([hv[n] for hv in halves], axis=0) for n in range(3))

    ii = lax.broadcasted_iota(jnp.int32, (C, LANES), 0)
    lane = lax.broadcasted_iota(jnp.int32, (C, LANES), 1)
    jj = lane % HD
    eye2 = jnp.where(ii == jj, 1.0, 0.0)
    lo_half = lane < HD
    lane2 = lax.broadcasted_iota(jnp.int32, (LANES, LANES), 1)
    row2 = lax.broadcasted_iota(jnp.int32, (LANES, LANES), 0)
    same_head = (lane2 < HD) == (row2 < HD)

    def bd(x):
        xb = x.astype(BF16)
        zero = jnp.zeros_like(xb)
        return jnp.concatenate([jnp.where(lo_half, xb, zero), jnp.where(lo_half, zero, xb)], axis=0)

    def mm(a, b):
        return jnp.dot(a.astype(BF16), b, preferred_element_type=F32)

    chains = [(c, pr) for c in range(n_ch) for pr in range(n_pr)]

    def blk(arr, c, pr):
        return arr[C * c:C * (c + 1), LANES * pr:LANES * (pr + 1)]

    qg, kdec, dec, aq, vb_bd, kbg_bd, eglast = {}, {}, {}, {}, {}, {}, {}
    for ch in chains:
        c, pr = ch
        qn, k_, gc, be = blk(q_all, c, pr), blk(k_all, c, pr), blk(g_all, c, pr), blk(b_all, c, pr)
        kb = k_ * be
        eg = jnp.exp(gc)
        glast = gc[C - 1:C, :]
        qg[ch] = qn * eg
        kdec[ch] = k_ * jnp.exp(glast - gc)
        eglast[ch] = jnp.exp(glast)
        vb_bd[ch] = bd(blk(v_all, c, pr) * be)
        kbg_bd[ch] = bd(kb * eg)
        gt = gts[c]
        gct = jnp.concatenate([jnp.broadcast_to(gt[2 * pr:2 * pr + 1, :], (C, HD)),
                               jnp.broadcast_to(gt[2 * pr + 1:2 * pr + 2, :], (C, HD))], axis=1)
        dec[ch] = jnp.exp(jnp.where(ii >= jj, gc - gct, NEG_INF))
        aq[ch] = lax.dot_general(jnp.concatenate([kb, qn], axis=0).astype(BF16), bd(k_), NT_DIMS,
                                 preferred_element_type=F32)
    a = {ch: jnp.where(ii > jj, aq[ch][:C] * dec[ch], 0.0) for ch in chains}
    qk = {ch: aq[ch][C:] * dec[ch] for ch in chains}
    x = {ch: eye2 - a[ch] for ch in chains}
    pw = {ch: mm(a[ch], bd(a[ch])) for ch in chains}
    for _ in range(4):
        both = {ch: mm(jnp.concatenate([x[ch], pw[ch]], axis=0), bd(pw[ch])) for ch in chains}
        x = {ch: x[ch] + both[ch][:C] for ch in chains}
        pw = {ch: both[ch][C:] for ch in chains}
    x = {ch: x[ch] + mm(x[ch], bd(pw[ch])) for ch in chains}
    uw = {ch: mm(x[ch], jnp.concatenate([vb_bd[ch], kbg_bd[ch]], axis=1)) for ch in chains}

    ku_kw = {ch: lax.dot_general(kdec[ch].astype(BF16), uw[ch].astype(BF16), TN_DIMS,
                                 preferred_element_type=F32) for ch in chains}
    qu_qw = {ch: mm(qk[ch], jnp.concatenate([bd(uw[ch][:, :LANES]), bd(uw[ch][:, LANES:])], axis=1))
             for ch in chains}
    lhs = {ch: jnp.concatenate([jnp.where(same_head, ku_kw[ch][:, LANES:], 0.0),
                                qg[ch] - qu_qw[ch][:, LANES:]], axis=0).astype(BF16) for ch in chains}
    st = [st_scr[:, LANES * pr:LANES * (pr + 1)] for pr in range(n_pr)]
    o_rows = []
    for c in range(n_ch):
        prod = [jnp.dot(lhs[c, pr], st[pr].astype(BF16), preferred_element_type=F32)
                for pr in range(n_pr)]
        o_rows.append(jnp.concatenate([prod[pr][LANES:] + qu_qw[c, pr][:, :LANES] for pr in range(n_pr)], axis=1))
        st = [st[pr] * eglast[c, pr] - prod[pr][:LANES] + jnp.where(same_head, ku_kw[c, pr][:, :LANES], 0.0)
              for pr in range(n_pr)]
    st_scr[...] = jnp.concatenate(st, axis=1)

    o = jnp.concatenate(o_rows, axis=0)
    oms = jnp.dot((o * o).astype(BF16), ebd, preferred_element_type=F32) * (1.0 / HD)
    z = z_ref[...]
    o_ref[...] = o * lax.rsqrt(oms + RMS_EPS) * nw_ref[...] * (z * jax.nn.sigmoid(z))


def _gdn(gqkv, sm, gz, conv_w, a_log, dt_bias, norm_w, B, S, tg=256):
    rows = B * S
    nt = S // tg
    head_of = np.arange(GDN_WIDTH) // HD
    ebd = jnp.asarray(head_of[:, None] == head_of[None, :], dtype=BF16)
    eg = np.zeros((LANES, GDN_WIDTH), np.float32)
    eb = np.zeros((LANES, GDN_WIDTH), np.float32)
    eg[SM_A + head_of, np.arange(GDN_WIDTH)] = 1.0
    eb[SM_B + head_of, np.arange(GDN_WIDTH)] = 1.0
    r = np.arange(tg)
    lblk = ((r[:, None] // GDN_CHUNK == r[None, :] // GDN_CHUNK) & (r[None, :] <= r[:, None])).astype(np.float32)
    alog128 = jnp.zeros((1, LANES), F32).at[0, SM_A:SM_A + GDN_HEADS].set(a_log.astype(F32))
    dt128 = jnp.zeros((1, LANES), F32).at[0, SM_A:SM_A + GDN_HEADS].set(dt_bias.astype(F32))
    nw512 = jnp.tile(norm_w.astype(F32), GDN_HEADS).reshape(1, GDN_WIDTH)
    full = lambda shape: pl.BlockSpec(shape, lambda b, i: (0,) * len(shape))
    return pl.pallas_call(
        functools.partial(_gdn_kernel, tg=tg),
        grid=(B, nt),
        in_specs=[
            pl.BlockSpec((tg, 3 * GDN_WIDTH), lambda b, i: (b * nt + i, 0)),
            pl.BlockSpec((F32_SUBLANES, 3 * GDN_WIDTH),
                         lambda b, i: (jnp.maximum((b * nt + i) * (tg // F32_SUBLANES) - 1, 0), 0)),
            pl.BlockSpec((tg, LANES), lambda b, i: (b * nt + i, 0)),
            pl.BlockSpec((tg, GDN_WIDTH), lambda b, i: (b * nt + i, 0)),
            full((GDN_CONV, 3 * GDN_WIDTH)),
            full((1, LANES)),
            full((1, LANES)),
            full((1, GDN_WIDTH)),
            full((GDN_WIDTH, GDN_WIDTH)),
            full((LANES, GDN_WIDTH)),
            full((LANES, GDN_WIDTH)),
            full((tg, tg)),
        ],
        out_specs=pl.BlockSpec((tg, GDN_WIDTH), lambda b, i: (b * nt + i, 0)),
        out_shape=jax.ShapeDtypeStruct((rows, GDN_WIDTH), F32),
        scratch_shapes=[
            pltpu.VMEM((tg + F32_SUBLANES, 3 * GDN_WIDTH), F32),
            pltpu.VMEM((LANES, GDN_WIDTH), F32),
        ],
        compiler_params=_cparams(("arbitrary", "arbitrary")),
        name="gdn",
    )(gqkv, gqkv, sm, gz, conv_w, alog128, dt128, nw512, ebd, jnp.asarray(eg), jnp.asarray(eb, dtype=BF16),
      jnp.asarray(lblk))


def _mix_xattn_kernel(x_ref, a_ref, b_ref, wa_ref, wb_ref, nw_ref, wq_ref, kv_ref, wo_ref, o_ref):
    mix = jnp.dot(a_ref[...].astype(BF16), wa_ref[...], preferred_element_type=F32)
    mix = mix + jnp.dot(b_ref[...].astype(BF16), wb_ref[...], preferred_element_type=F32)
    _xattn_rows(x_ref[...] + mix, nw_ref, wq_ref, kv_ref, wo_ref, o_ref)


def _xattn_kernel(x_ref, nw_ref, wq_ref, kv_ref, wo_ref, o_ref):
    _xattn_rows(x_ref[...], nw_ref, wq_ref, kv_ref, wo_ref, o_ref)


def _xattn_rows(x, nw_ref, wq_ref, kv_ref, wo_ref, o_ref):
    hn = _rms(x, nw_ref[...]).astype(BF16)
    q = jnp.dot(hn, wq_ref[...], preferred_element_type=F32) * (XA_HD ** -0.5)
    heads = range(XA_HEADS)
    ss = [lax.dot_general(q[:, XA_HD * h:XA_HD * (h + 1)].astype(BF16), kv_ref[:, XA_HD * h:XA_HD * (h + 1)],
                          NT_DIMS, preferred_element_type=F32) for h in heads]
    es = [jnp.exp(ss[h] - jnp.max(ss[h], axis=-1, keepdims=True)).astype(BF16) for h in heads]
    ones = jnp.ones((kv_ref.shape[0], XA_HD), BF16)
    pvs = [jnp.dot(es[h], jnp.concatenate([kv_ref[:, XA_WIDTH + XA_HD * h:XA_WIDTH + XA_HD * (h + 1)], ones], axis=1),
                   preferred_element_type=F32) for h in heads]
    o = jnp.concatenate([pvs[h][:, :XA_HD] / pvs[h][:, XA_HD:XA_HD + 1] for h in heads], axis=1).astype(BF16)
    o_ref[...] = x + jnp.dot(o, wo_ref[...], preferred_element_type=F32)


def _xattn(x2d, norm_w, wq_bf, memkv_l, wo_bf, S, tm=512):
    rows = x2d.shape[0]
    per_b = S // tm
    mem_len = memkv_l.shape[0] // (rows // S)
    return pl.pallas_call(
        _xattn_kernel,
        grid=(rows // tm,),
        in_specs=[
            pl.BlockSpec((tm, D_MODEL), lambda i: (i, 0)),
            pl.BlockSpec((1, D_MODEL), lambda i: (0, 0)),
            pl.BlockSpec((D_MODEL, XA_WIDTH), lambda i: (0, 0)),
            pl.BlockSpec((mem_len, 2 * XA_WIDTH), lambda i: (i // per_b, 0)),
            pl.BlockSpec((XA_WIDTH, D_MODEL), lambda i: (0, 0)),
        ],
        out_specs=pl.BlockSpec((tm, D_MODEL), lambda i: (i, 0)),
        out_shape=jax.ShapeDtypeStruct((rows, D_MODEL), F32),
        compiler_params=_cparams(("arbitrary",)),
        name="cross_attention",
    )(x2d, norm_w.reshape(1, D_MODEL), wq_bf, memkv_l, wo_bf)


def _mix_xattn(x2d, o_nsa, o_gdn, w_out_bf, norm_w, wq_bf, memkv_l, wo_bf, S, tm=512):
    rows = x2d.shape[0]
    per_b = S // tm
    mem_len = memkv_l.shape[0] // (rows // S)
    return pl.pallas_call(
        _mix_xattn_kernel,
        grid=(rows // tm,),
        in_specs=[
            pl.BlockSpec((tm, D_MODEL), lambda i: (i, 0)),
            pl.BlockSpec((tm, NSA_WIDTH), lambda i: (i, 0)),
            pl.BlockSpec((tm, GDN_WIDTH), lambda i: (i, 0)),
            pl.BlockSpec((NSA_WIDTH, D_MODEL), lambda i: (0, 0)),
            pl.BlockSpec((GDN_WIDTH, D_MODEL), lambda i: (1, 0)),
            pl.BlockSpec((1, D_MODEL), lambda i: (0, 0)),
            pl.BlockSpec((D_MODEL, XA_WIDTH), lambda i: (0, 0)),
            pl.BlockSpec((mem_len, 2 * XA_WIDTH), lambda i: (i // per_b, 0)),
            pl.BlockSpec((XA_WIDTH, D_MODEL), lambda i: (0, 0)),
        ],
        out_specs=pl.BlockSpec((tm, D_MODEL), lambda i: (i, 0)),
        out_shape=jax.ShapeDtypeStruct((rows, D_MODEL), F32),
        compiler_params=_cparams(("arbitrary",)),
        name="mix_cross_attention",
    )(x2d, o_nsa, o_gdn, w_out_bf, w_out_bf, norm_w.reshape(1, D_MODEL), wq_bf, memkv_l, wo_bf)


def _mlp_kernel(x_ref, nw_ref, w1_ref, w2_ref, fw_ref, o_ref, hn_scr, acc_scr, *, final_norm):
    j = pl.program_id(1)

    @pl.when(j == 0)
    def _():
        x = x_ref[...]
        hn_scr[...] = _rms(x, nw_ref[...]).astype(BF16)
        acc_scr[...] = x

    h = jnp.dot(hn_scr[...], w1_ref[...], preferred_element_type=F32)
    h = jnp.square(jnp.maximum(h, 0.0)).astype(BF16)
    acc_scr[...] += jnp.dot(h, w2_ref[...], preferred_element_type=F32)

    @pl.when(j == pl.num_programs(1) - 1)
    def _():
        y = acc_scr[...]
        if final_norm:
            y = _rms(y, fw_ref[...])
        o_ref[...] = y


def _mlp(x2d, norm_w, w1_bf, w2_bf, final_w, final_norm, tm=1024, tf=1024):
    rows = x2d.shape[0]
    tm = min(tm, rows)
    return pl.pallas_call(
        functools.partial(_mlp_kernel, final_norm=final_norm),
        grid=(rows // tm, D_FF // tf),
        in_specs=[
            pl.BlockSpec((tm, D_MODEL), lambda i, j: (i, 0)),
            pl.BlockSpec((1, D_MODEL), lambda i, j: (0, 0)),
            pl.BlockSpec((D_MODEL, tf), lambda i, j: (0, j)),
            pl.BlockSpec((tf, D_MODEL), lambda i, j: (j, 0)),
            pl.BlockSpec((1, D_MODEL), lambda i, j: (0, 0)),
        ],
        out_specs=pl.BlockSpec((tm, D_MODEL), lambda i, j: (i, 0)),
        out_shape=jax.ShapeDtypeStruct((rows, D_MODEL), F32),
        scratch_shapes=[pltpu.VMEM((tm, D_MODEL), BF16), pltpu.VMEM((tm, D_MODEL), F32)],
        compiler_params=_cparams(("arbitrary", "arbitrary")),
        name="mlp",
    )(x2d, norm_w.reshape(1, D_MODEL), w1_bf, w2_bf, final_w.reshape(1, D_MODEL))


def _sc_kernel(x_ref, nw_ref, win_ref, cw_ref, wout_ref, o_ref, cu_scr, *, tm, per_b):
    i = pl.program_id(0)
    halo = F32_SUBLANES

    @pl.when(i % per_b == 0)
    def _():
        cu_scr[tm:tm + halo, :] = jnp.zeros((halo, D_MODEL), F32)

    x = x_ref[...]
    y = jnp.dot(_rms(x, nw_ref[...]).astype(BF16), win_ref[...], preferred_element_type=F32)
    cu_scr[0:halo, :] = cu_scr[tm:tm + halo, :]
    cu_scr[halo:halo + tm, :] = y[:, D_MODEL:2 * D_MODEL] * y[:, 2 * D_MODEL:]
    base = halo - (SC_WIDTH - 1)
    conv = cw_ref[0:1, :] * cu_scr[base:base + tm, :]
    for j in range(1, SC_WIDTH):
        conv = conv + cw_ref[j:j + 1, :] * cu_scr[base + j:base + j + tm, :]
    gated = (y[:, :D_MODEL] * conv).astype(BF16)
    o_ref[...] = x + jnp.dot(gated, wout_ref[...], preferred_element_type=F32)


def _sc_mixer(x2d, norm_w, win_bf, conv_w, wout_bf, S, tm=512):
    rows = x2d.shape[0]
    per_b = S // tm
    return pl.pallas_call(
        functools.partial(_sc_kernel, tm=tm, per_b=per_b),
        grid=(rows // tm,),
        in_specs=[
            pl.BlockSpec((tm, D_MODEL), lambda i: (i, 0)),
            pl.BlockSpec((1, D_MODEL), lambda i: (0, 0)),
            pl.BlockSpec((D_MODEL, 3 * D_MODEL), lambda i: (0, 0)),
            pl.BlockSpec((SC_WIDTH, D_MODEL), lambda i: (0, 0)),
            pl.BlockSpec((D_MODEL, D_MODEL), lambda i: (0, 0)),
        ],
        out_specs=pl.BlockSpec((tm, D_MODEL), lambda i: (i, 0)),
        out_shape=jax.ShapeDtypeStruct((rows, D_MODEL), F32),
        scratch_shapes=[pltpu.VMEM((tm + F32_SUBLANES, D_MODEL), F32)],
        compiler_params=_cparams(("arbitrary",)),
        name="short_conv_mixer",
    )(x2d, norm_w.reshape(1, D_MODEL), win_bf, conv_w, wout_bf)


def _hybrid_mixer(x2d, positions, norm_w, w_in, ck_pos, ck_w1, ck_w2, cv_pos, cv_w1, cv_w2,
                  gdn_conv, gdn_a_log, gdn_dt_bias, gdn_norm, B, S):
    tm = 512
    invf = (ROPE_THETA ** (-jnp.arange(0, HD, 2, dtype=F32) / HD)).reshape(HD // 2, 1)
    pos3 = positions.astype(F32).reshape(B * S // tm, 1, tm)
    qT, kr, vT, ckv, gqkv, gz, sm, gT = _proj0(x2d, norm_w, _prep_w_in(w_in), pos3, invf, B, S, tm)
    w1s, p8, w2s = _prep_compress_weights(ck_pos, ck_w1, ck_w2, cv_pos, cv_w1, cv_w2)
    kc, vcT = _compress(ckv, w1s, p8, w2s, B, S)
    o_nsa = _nsa(qT, kr, vT, kc, vcT, gT, B, S)
    o_gdn = _gdn(gqkv, sm, gz, gdn_conv, gdn_a_log, gdn_dt_bias, gdn_norm, B, S)
    return o_nsa, o_gdn


def kernel(x, mem, positions, norm_mix, norm_xattn, norm_mlp, hyb_w_in, hyb_cmp_k_pos, hyb_cmp_k_w1, hyb_cmp_k_w2, hyb_cmp_v_pos, hyb_cmp_v_w1, hyb_cmp_v_w2, hyb_gdn_conv, hyb_gdn_a_log, hyb_gdn_dt_bias, hyb_gdn_norm, hyb_w_out, sc_w_in, sc_conv, sc_w_out, mem_norm, xa_wq, xa_wkv, xa_wo, mlp_w1, mlp_w2, final_norm):
    B, S, _ = x.shape
    depth = norm_mix.shape[0]
    x2d = x.reshape(B * S, D_MODEL)
    memkv = _memkv(mem.reshape(-1, D_MODEL), mem_norm, xa_wkv.astype(BF16))
    for layer in range(depth):
        j = layer // 2
        xa = (norm_xattn[layer], xa_wq[layer].astype(BF16), memkv[layer], xa_wo[layer].astype(BF16), S)
        if layer % 2 == 0:
            o_nsa, o_gdn = _hybrid_mixer(x2d, positions, norm_mix[layer], hyb_w_in[j], hyb_cmp_k_pos[j],
                                         hyb_cmp_k_w1[j], hyb_cmp_k_w2[j], hyb_cmp_v_pos[j], hyb_cmp_v_w1[j],
                                         hyb_cmp_v_w2[j], hyb_gdn_conv[j], hyb_gdn_a_log[j], hyb_gdn_dt_bias[j],
                                         hyb_gdn_norm[j], B, S)
            x2d = _mix_xattn(x2d, o_nsa, o_gdn, hyb_w_out[j].astype(BF16), *xa)
        else:
            x2d = _sc_mixer(x2d, norm_mix[layer], sc_w_in[j].astype(BF16), sc_conv[j], sc_w_out[j].astype(BF16), S)
            x2d = _xattn(x2d, *xa)
        x2d = _mlp(x2d, norm_mlp[layer], mlp_w1[layer].astype(BF16), mlp_w2[layer].astype(BF16),
                   final_norm, layer == depth - 1)
    return x2d.reshape(B, S, D_MODEL)
```

```python
import functools

import numpy as np
import jax
import jax.numpy as jnp
from jax import lax
from jax.experimental import pallas as pl
from jax.experimental.pallas import tpu as pltpu

F32 = jnp.float32
BF16 = jnp.bfloat16
HI = lax.Precision.HIGHEST

D_MODEL = 1024
MEM_LEN = 256
RMS_EPS = 1e-6
ROPE_THETA = 10000.0
NEG_INF = -1e30
FORCE_BONUS = 1e3
LOG2E = 1.4426950408889634

NSA_HEADS = 8
NSA_KV_GROUPS = 2
NSA_HPG = NSA_HEADS // NSA_KV_GROUPS
HD = 64
CMP_LEN = 32
CMP_STRIDE = 16
CMP_HID = 2 * HD
SLC_BLK = 64
SLC_TOPK = 16
WINDOW = 512

GDN_HEADS = 8
GDN_CONV = 4
GDN_CHUNK = 64
SC_WIDTH = 3
XA_HEADS = 4
XA_HD = 128
D_FF = 4 * D_MODEL

NSA_WIDTH = NSA_HEADS * HD
NSA_KV_WIDTH = NSA_KV_GROUPS * HD
GDN_WIDTH = GDN_HEADS * HD
XA_WIDTH = XA_HEADS * XA_HD
IN_SIZES = (NSA_WIDTH,) + (NSA_KV_WIDTH,) * 6 + (3 * NSA_HEADS,) + (GDN_WIDTH,) * 3 + (GDN_HEADS, GDN_HEADS, GDN_WIDTH)

V7X_VMEM_BYTES = 64 * 1024 * 1024
VMEM_LIMIT = V7X_VMEM_BYTES * 3 // 4
LANES = 128
F32_SUBLANES = 8
BF16_ROWS = 16

C_Q, C_KR, C_VT, C_CKV, C_GQKV, C_GZ, C_SM, NC0 = 0, 512, 768, 1024, 1280, 2816, 3328, 3456
SM_GATE, SM_A, SM_B = 0, 24, 32
NSA_CHUNK = 256
KR_WIDTH = 3 * LANES
KR_WIN = 2 * LANES

NT_DIMS = (((1,), (1,)), ((), ()))
TN_DIMS = (((0,), (0,)), ((), ()))


def _cparams(sem):
    return pltpu.CompilerParams(dimension_semantics=sem, vmem_limit_bytes=VMEM_LIMIT)


def _rms(x, w):
    return x * lax.rsqrt(jnp.mean(x * x, axis=-1, keepdims=True) + RMS_EPS) * w


def _softplus(x):
    return jnp.maximum(x, 0.0) + jnp.log(1.0 + jnp.exp(-jnp.abs(x)))


def _memkv_kernel(m_ref, nw_ref, w_ref, o_ref):
    hn = _rms(m_ref[...], nw_ref[...]).astype(BF16)
    o_ref[...] = jnp.dot(hn, w_ref[...], preferred_element_type=F32).astype(BF16)


def _memkv(mem2d, mem_norm, wkv_bf):
    rows = mem2d.shape[0]
    depth = wkv_bf.shape[0]
    tm = 512
    return pl.pallas_call(
        _memkv_kernel,
        grid=(depth, rows // tm),
        in_specs=[
            pl.BlockSpec((tm, D_MODEL), lambda l, i: (i, 0)),
            pl.BlockSpec((1, D_MODEL), lambda l, i: (0, 0)),
            pl.BlockSpec((None, D_MODEL, 2 * XA_WIDTH), lambda l, i: (l, 0, 0)),
        ],
        out_specs=pl.BlockSpec((None, tm, 2 * XA_WIDTH), lambda l, i: (l, i, 0)),
        out_shape=jax.ShapeDtypeStruct((depth, rows, 2 * XA_WIDTH), BF16),
        compiler_params=_cparams(("arbitrary", "arbitrary")),
        name="memkv",
    )(mem2d, mem_norm.reshape(1, D_MODEL), wkv_bf)


def _proj0_kernel(x_ref, nw_ref, w_ref, pos_ref, invf_ref,
                  qT_ref, kr_ref, vT_ref, ckv_ref, gqkv_ref, gz_ref, sm_ref, gT_ref):
    tm = x_ref.shape[0]
    hn = _rms(x_ref[...], nw_ref[...]).astype(BF16)
    y = jnp.dot(hn, w_ref[...], preferred_element_type=F32)

    ang = invf_ref[...] * pos_ref[...]
    c = jnp.cos(ang)
    s = jnp.sin(ang)
    cos_n = jnp.concatenate([c, c, c, c], axis=0).T
    sin_n = jnp.concatenate([-s, s, -s, s], axis=0).T
    lane = lax.broadcasted_iota(jnp.int32, (tm, LANES), 1)
    first_half = (lane % HD) < (HD // 2)

    def rope(xc):
        rot = jnp.where(first_half, pltpu.roll(xc, LANES - HD // 2, 1), pltpu.roll(xc, HD // 2, 1))
        return xc * cos_n + rot * sin_n

    roped = [rope(y[:, LANES * i:LANES * (i + 1)]) for i in range(6)]
    q = jnp.concatenate(roped[:4], axis=1) * (LOG2E * HD ** -0.5)
    qT_ref[...] = q.T.astype(BF16)
    row = lax.broadcasted_iota(jnp.int32, (tm, LANES), 0)
    blk_hot = jnp.where(lane - HD == (row // SLC_BLK) % (NSA_CHUNK // SLC_BLK), 1.0, 0.0)
    k_slc = roped[4]
    lo = lane < HD
    kr_ref[...] = jnp.concatenate([jnp.where(lo, k_slc, blk_hot),
                                   jnp.where(lo, pltpu.roll(k_slc, HD, 1), blk_hot),
                                   roped[5]], axis=1).astype(BF16)
    vT_ref[...] = y[:, C_VT:C_CKV].T.astype(BF16)
    for i in range(4):
        ckv_ref[i] = y[:, C_CKV + HD * i:C_CKV + HD * (i + 1)].astype(BF16)
    gqkv_ref[...] = y[:, C_GQKV:C_GZ]
    gz_ref[...] = y[:, C_GZ:C_SM]
    sm = y[:, C_SM:NC0]
    sm_ref[...] = sm
    gT_ref[...] = jax.nn.sigmoid(sm).T[:3 * NSA_HEADS, :]


def _prep_w_in(w_in):
    offs = np.cumsum((0,) + IN_SIZES)
    (nq, kcmp, vcmp, kslc, vslc, kwin, vwin, ngate, gq, gk, gv, ga, gb, gz) = [
        w_in[:, offs[i]:offs[i + 1]] for i in range(len(IN_SIZES))]
    pad = jnp.zeros((D_MODEL, LANES - 3 * NSA_HEADS - 2 * GDN_HEADS), w_in.dtype)
    small = jnp.concatenate([ngate, ga, gb, pad], axis=1)
    return jnp.concatenate([nq, kslc, kwin, vslc, vwin, kcmp, vcmp, gq, gk, gv, gz, small], axis=1).astype(BF16)


def _proj0(x2d, norm_w, w_bf, pos3, invf, B, S, tm):
    rows = B * S
    nt = S // tm
    row_spec = lambda n: pl.BlockSpec((tm, n), lambda i: (i, 0))
    t_spec = lambda n: pl.BlockSpec((None, n, tm), lambda i: (i // nt, 0, i % nt))
    return pl.pallas_call(
        _proj0_kernel,
        grid=(rows // tm,),
        in_specs=[
            row_spec(D_MODEL),
            pl.BlockSpec((1, D_MODEL), lambda i: (0, 0)),
            pl.BlockSpec((D_MODEL, NC0), lambda i: (0, 0)),
            pl.BlockSpec((None, 1, tm), lambda i: (i, 0, 0)),
            pl.BlockSpec((HD // 2, 1), lambda i: (0, 0)),
        ],
        out_specs=[
            t_spec(NSA_WIDTH),
            row_spec(KR_WIDTH),
            t_spec(2 * NSA_KV_WIDTH),
            pl.BlockSpec((4, tm, HD), lambda i: (0, i, 0)),
            row_spec(3 * GDN_WIDTH),
            row_spec(GDN_WIDTH),
            row_spec(LANES),
            t_spec(3 * NSA_HEADS),
        ],
        out_shape=[
            jax.ShapeDtypeStruct((B, NSA_WIDTH, S), BF16),
            jax.ShapeDtypeStruct((rows, KR_WIDTH), BF16),
            jax.ShapeDtypeStruct((B, 2 * NSA_KV_WIDTH, S), BF16),
            jax.ShapeDtypeStruct((4, rows, HD), BF16),
            jax.ShapeDtypeStruct((rows, 3 * GDN_WIDTH), F32),
            jax.ShapeDtypeStruct((rows, GDN_WIDTH), F32),
            jax.ShapeDtypeStruct((rows, LANES), F32),
            jax.ShapeDtypeStruct((B, 3 * NSA_HEADS, S), F32),
        ],
        compiler_params=_cparams(("arbitrary",)),
        name="proj0",
    )(x2d, norm_w.reshape(1, D_MODEL), w_bf, pos3, invf)


def _compress_kernel(t_ref, w1_ref, pos_ref, w2_ref, o_ref, oT_ref):
    nseg = t_ref.shape[0]
    w1 = w1_ref[...]
    ab = jnp.dot(t_ref[...], w1, preferred_element_type=F32)
    pc = jnp.dot(pos_ref[...].astype(BF16), w1, preferred_element_type=F32)
    bias = pc[0:1, :CMP_HID] + pc[1:2, CMP_HID:]
    hid = ab[:, :CMP_HID] + pltpu.roll(ab[:, CMP_HID:], nseg - 1, 0) + bias
    act = hid * jax.nn.sigmoid(hid)
    out = jnp.dot(act.astype(BF16), w2_ref[...], preferred_element_type=F32)
    o_ref[...] = out
    oT_ref[...] = out.T[:HD, :]


def _compress(ckv, w1s, pos8, w2s, B, S):
    nseg = S // CMP_STRIDE
    t = ckv.reshape(4, B, nseg, CMP_STRIDE * HD)
    return pl.pallas_call(
        _compress_kernel,
        grid=(4, B),
        in_specs=[
            pl.BlockSpec((None, None, nseg, CMP_STRIDE * HD), lambda k, b: (k, b, 0, 0)),
            pl.BlockSpec((None, CMP_STRIDE * HD, 2 * CMP_HID), lambda k, b: (k // 2, 0, 0)),
            pl.BlockSpec((None, F32_SUBLANES, CMP_STRIDE * HD), lambda k, b: (k // 2, 0, 0)),
            pl.BlockSpec((None, CMP_HID, LANES), lambda k, b: (k // 2, 0, 0)),
        ],
        out_specs=[
            pl.BlockSpec((None, None, nseg, LANES), lambda k, b: (k, b, 0, 0)),
            pl.BlockSpec((None, None, HD, nseg), lambda k, b: (k, b, 0, 0)),
        ],
        out_shape=[
            jax.ShapeDtypeStruct((4, B, nseg, LANES), F32),
            jax.ShapeDtypeStruct((4, B, HD, nseg), F32),
        ],
        compiler_params=_cparams(("arbitrary", "arbitrary")),
        name="nsa_compress",
    )(t, w1s, pos8, w2s)


def _prep_compress_weights(k_pos, k_w1, k_w2, v_pos, v_w1, v_w2):
    half = CMP_STRIDE * HD

    def w1cat(w1):
        return jnp.concatenate([w1[:half], w1[half:]], axis=1)

    def pos8(p):
        flat = p.reshape(2, half)
        return jnp.concatenate([flat, jnp.zeros((F32_SUBLANES - 2, half), p.dtype)], axis=0)

    def w2pad(w2):
        return jnp.concatenate([w2, jnp.zeros((CMP_HID, LANES - HD), w2.dtype)], axis=1)

    w1s = jnp.stack([w1cat(k_w1), w1cat(v_w1)]).astype(BF16)
    p8 = jnp.stack([pos8(k_pos), pos8(v_pos)])
    w2s = jnp.stack([w2pad(k_w2), w2pad(v_w2)]).astype(BF16)
    return w1s, p8, w2s


def _nsa_kernel(qT_ref, kr_ref, vT_ref, kc_ref, vcT_ref, gT_ref, ovT_ref, o_ref,
                imp_scr, sel_scr, s_scr, p_scr, b_scr, *, S, TQ):
    CH = TQ
    groups = range(NSA_KV_GROUPS)
    n_cmp = S // CMP_STRIDE
    n_slc = S // SLC_BLK
    n_sel = min(SLC_TOPK, n_slc)
    qi = pl.program_id(1)
    q0 = qi * TQ
    t1 = q0 + lax.broadcasted_iota(jnp.int32, (1, TQ), 1)
    t4 = jnp.concatenate([t1] * NSA_HPG, axis=1)
    n_io = lax.broadcasted_iota(jnp.int32, (n_cmp, 1), 0)
    j_io = lax.broadcasted_iota(jnp.int32, (n_slc, 1), 0)
    k_io = lax.broadcasted_iota(jnp.int32, (CH, 1), 0)
    n_causal = (q0 + TQ) // SLC_BLK

    qs, q64s = [], []
    for g in groups:
        q64 = jnp.concatenate(
            [qT_ref[NSA_HPG * HD * g + HD * h:NSA_HPG * HD * g + HD * (h + 1), :] for h in range(NSA_HPG)], axis=1)
        zq = jnp.zeros_like(q64)
        qs.append(jnp.concatenate([q64, zq] if g == 0 else [zq, q64], axis=0))
        q64s.append(q64)

    n_lanes = NSA_HPG * TQ

    def probs(sc, bias, m_i):
        if bias is not None:
            sc = sc + jnp.concatenate([bias] * NSA_HPG, axis=1)
        m_new = jnp.maximum(m_i, jnp.max(sc, axis=0, keepdims=True))
        return m_new, jnp.exp2(m_i - m_new), jnp.exp2(sc - m_new).astype(BF16)

    def pv(vt, pr):
        return jnp.dot(vt, pr, preferred_element_type=F32)

    def m_init():
        return jnp.full((1, n_lanes), NEG_INF, F32)

    u_io = lax.broadcasted_iota(jnp.int32, (1, TQ), 1)
    causal_bias = jnp.where(k_io <= u_io, 0.0, NEG_INF)
    window_bias = jnp.where(k_io > u_io, 0.0, NEG_INF)

    s_cmp = [jnp.dot(kc_ref[g, :, 0:HD].astype(BF16), q64s[g], preferred_element_type=F32)
             for g in groups]
    n_back = WINDOW // CH
    n_win = n_back + 1
    w0 = jnp.maximum(qi - n_back, 0)
    rw = pl.multiple_of(w0 * CH, CH)
    k_win = kr_ref[pl.ds(rw, n_win * CH), KR_WIN:KR_WIN + LANES]
    s_win = [jnp.dot(k_win, qs[g], preferred_element_type=F32) for g in groups]
    valid = (n_io * CMP_STRIDE + (CMP_LEN - 1)) <= t4
    any_valid = t4 >= CMP_LEN - 1
    p_cmp, psum = [], []
    for g in groups:
        s = jnp.where(valid, s_cmp[g], NEG_INF)
        m = jnp.max(s, axis=0, keepdims=True)
        e = jnp.exp2(s - m)
        l = jnp.sum(e, axis=0, keepdims=True)
        p = e * jnp.where(any_valid, 1.0 / l, 0.0)
        p_cmp.append(p.astype(BF16))
        ph = p[:, 0:TQ]
        for h in range(1, NSA_HPG):
            ph = ph + p[:, h * TQ:(h + 1) * TQ]
        psum.append(ph)
    o_cmp = [jnp.dot(vcT_ref[g].astype(BF16), p_cmp[g], preferred_element_type=F32) for g in groups]
    imps = [jnp.dot(ovT_ref[...], psum[g], precision=HI, preferred_element_type=F32) for g in groups]
    hidden = jnp.full((CH, TQ), NEG_INF, F32)
    slab_bias = []
    for c in range(n_win):
        d = w0 + c - qi
        slab_bias.append(jnp.where(d == 0, causal_bias,
                                   jnp.where(d > 0, hidden, jnp.where(d == -n_back, window_bias, 0.0))))
    win_bias = jnp.concatenate(slab_bias, axis=0)
    win_p = [probs(s_win[g], win_bias, m_init())[2] for g in groups]
    o_win = []
    for g in groups:
        v_win = jnp.concatenate([vT_ref[NSA_KV_WIDTH + HD * g:NSA_KV_WIDTH + HD * (g + 1), pl.ds(rw, n_win * CH)],
                                 jnp.ones((BF16_ROWS, n_win * CH), BF16)], axis=0)
        acc = pv(v_win, win_p[g])
        o_win.append(acc[:HD] / acc[HD:HD + 1])
    cur = t1 // SLC_BLK
    forced = (j_io == 0) | (j_io == cur) | (j_io == cur - 1)
    causal = j_io <= cur
    vals = [jnp.where(causal, jnp.where(forced, imps[g] + FORCE_BONUS, imps[g]), -1.0) for g in groups]
    for g in groups:
        imp_scr[g] = vals[g]

    per_trip = TQ // SLC_BLK

    def rank_body(it, cnts):
        out = list(cnts)
        for i in [per_trip * it + r for r in range(per_trip)]:
            tie = jnp.where(j_io > i, 1.0, 0.0)
            for g in groups:
                vi = imp_scr[g, pl.ds(i, 1), :]
                out[g] = out[g] + jnp.where(vi > vals[g], 1.0, jnp.where(vi == vals[g], tie, 0.0))
        return tuple(out)

    cnts = lax.fori_loop(0, jnp.where(n_causal > n_sel, n_causal // per_trip, 0), rank_body,
                         tuple(jnp.zeros((n_slc, TQ), F32) for _ in groups))
    nb = CH // SLC_BLK
    for g in groups:
        selb = jnp.where(cnts[g] < float(n_sel), 0.0, NEG_INF)
        for jc in range(S // CH):
            rows = jnp.concatenate([selb[nb * jc:nb * (jc + 1), :], jnp.zeros((BF16_ROWS - nb, TQ), F32)], axis=0)
            sel_scr[g, jc] = jnp.concatenate([rows] * NSA_HPG, axis=1).astype(BF16)

    ones_rows = jnp.ones((BF16_ROWS, CH), BF16)
    zero_rows = jnp.zeros((HD - BF16_ROWS, n_lanes), BF16)
    acc_rows = HD + BF16_ROWS

    def slc_scores(jc):
        r0 = pl.multiple_of(jc * CH, CH)
        return [jnp.dot(kr_ref[pl.ds(r0, CH), LANES * g:LANES * (g + 1)],
                        jnp.concatenate([q64s[g], sel_scr[g, jc], zero_rows], axis=0),
                        preferred_element_type=F32) for g in groups]

    def values(jc, vrow):
        r0 = pl.multiple_of(jc * CH, CH)
        return [jnp.concatenate([vT_ref[vrow + HD * g:vrow + HD * (g + 1), pl.ds(r0, CH)], ones_rows], axis=0)
                for g in groups]

    def acc_init():
        return jnp.zeros((acc_rows, n_lanes), F32)

    def slc_half(jc, slot, ms):
        nxt = slc_scores(jc + 1)
        v_prev = values(jnp.maximum(jc - 1, 0), 0)
        pvs = [pv(v_prev[g], p_scr[g]) for g in groups]
        for g in groups:
            s_scr[1 - slot, g] = nxt[g]
        out = []
        for g in groups:
            m_new, alpha, pr = probs(s_scr[slot, g], None, ms[g])
            b_scr[g] = (b_scr[g] + pvs[g]) * alpha
            p_scr[g] = pr
            out.append(m_new)
        return tuple(out)

    first = slc_scores(0)
    for g in groups:
        s_scr[0, g] = first[g]
        p_scr[g] = jnp.zeros((CH, n_lanes), BF16)
        b_scr[g] = acc_init()
    ms = lax.fori_loop(0, qi // 2, lambda i, m: slc_half(2 * i + 1, 1, slc_half(2 * i, 0, m)),
                       tuple(m_init() for _ in groups))

    def slc_odd(_, m):
        m = slc_half(qi - 1, 0, m)
        s_scr[0] = s_scr[1]
        return m

    ms = lax.fori_loop(0, qi % 2, slc_odd, ms)

    v_prev = values(jnp.maximum(qi - 1, 0), 0)
    pvs = [pv(v_prev[g], p_scr[g]) for g in groups]
    slc_parts = [probs(s_scr[0, g], causal_bias, ms[g]) for g in groups]
    v_slc = values(qi, 0)
    o_slc = []
    for g in groups:
        acc = (b_scr[g] + pvs[g]) * slc_parts[g][1] + pv(v_slc[g], slc_parts[g][2])
        o_slc.append(acc[:HD] / acc[HD:HD + 1])

    for g in groups:
        rows = []
        for h in range(NSA_HPG):
            hh = NSA_HPG * g + h
            sl = slice(h * TQ, (h + 1) * TQ)
            rows.append(gT_ref[3 * hh:3 * hh + 1, :] * o_cmp[g][:, sl]
                        + gT_ref[3 * hh + 1:3 * hh + 2, :] * o_slc[g][:, sl]
                        + gT_ref[3 * hh + 2:3 * hh + 3, :] * o_win[g][:, sl])
        og = jnp.concatenate(rows, axis=0)
        o_ref[:, NSA_HPG * HD * g:NSA_HPG * HD * (g + 1)] = og.T


def _overlap_T(S):
    n_cmp = S // CMP_STRIDE
    n_slc = S // SLC_BLK
    cs = np.arange(n_cmp)[:, None] * CMP_STRIDE
    js = np.arange(n_slc)[None, :] * SLC_BLK
    ov = np.clip(np.minimum(cs + CMP_LEN, js + SLC_BLK) - np.maximum(cs, js), 0, None) / CMP_LEN
    ov[n_cmp - 1] = 0.0
    return jnp.asarray(ov.T, dtype=F32)


def _nsa(qT, kr, vT, kc, vcT, gT, B, S, TQ=NSA_CHUNK):
    n_cmp = S // CMP_STRIDE
    n_slc = S // SLC_BLK
    nq = S // TQ
    return pl.pallas_call(
        functools.partial(_nsa_kernel, S=S, TQ=TQ),
        grid=(B, nq),
        in_specs=[
            pl.BlockSpec((None, NSA_WIDTH, TQ), lambda b, i: (b, 0, i)),
            pl.BlockSpec((None, S, KR_WIDTH), lambda b, i: (b, 0, 0)),
            pl.BlockSpec((None, 2 * NSA_KV_WIDTH, S), lambda b, i: (b, 0, 0)),
            pl.BlockSpec((2, None, n_cmp, LANES), lambda b, i: (0, b, 0, 0)),
            pl.BlockSpec((2, None, HD, n_cmp), lambda b, i: (1, b, 0, 0)),
            pl.BlockSpec((None, 3 * NSA_HEADS, TQ), lambda b, i: (b, 0, i)),
            pl.BlockSpec((n_slc, n_cmp), lambda b, i: (0, 0)),
        ],
        out_specs=pl.BlockSpec((TQ, NSA_WIDTH), lambda b, i: (b * nq + i, 0)),
        out_shape=jax.ShapeDtypeStruct((B * S, NSA_WIDTH), F32),
        scratch_shapes=[
            pltpu.VMEM((NSA_KV_GROUPS, n_slc, TQ), F32),
            pltpu.VMEM((NSA_KV_GROUPS, S // TQ, BF16_ROWS, NSA_HPG * TQ), BF16),
            pltpu.VMEM((2, NSA_KV_GROUPS, TQ, NSA_HPG * TQ), F32),
            pltpu.VMEM((NSA_KV_GROUPS, TQ, NSA_HPG * TQ), BF16),
            pltpu.VMEM((NSA_KV_GROUPS, HD + BF16_ROWS, NSA_HPG * TQ), F32),
        ],
        compiler_params=_cparams(("arbitrary", "arbitrary")),
        name="nsa_attention",
    )(qT, kr.reshape(B, S, KR_WIDTH), vT, kc, vcT, gT, _overlap_T(S))


def _gdn_kernel(x_ref, xp_ref, sm_ref, z_ref, cw_ref, alog_ref, dt_ref, nw_ref,
                ebd_ref, eg_ref, eb_ref, lblk_ref, o_ref,
                xe_scr, st_scr, *, tg):
    C = GDN_CHUNK
    n_ch = tg // C
    n_pr = GDN_HEADS // 2
    i = pl.program_id(1)

    @pl.when(i == 0)
    def _():
        st_scr[...] = jnp.zeros_like(st_scr)

    def decay_terms():
        sm = sm_ref[...]
        g128 = -jnp.exp(alog_ref[...]) * _softplus(sm + dt_ref[...])
        gcs128 = jnp.dot(lblk_ref[...], g128, precision=HI, preferred_element_type=F32)
        g_exp = jnp.dot(gcs128, eg_ref[...], precision=HI, preferred_element_type=F32)
        b_exp = jnp.dot(jax.nn.sigmoid(sm).astype(BF16), eb_ref[...], preferred_element_type=F32)
        return g_exp, b_exp, [gcs128[C * c:C * (c + 1), :].T[SM_A:SM_A + GDN_HEADS, :]
                              for c in range(n_ch)]

    halo = F32_SUBLANES
    xe_scr[0:halo, :] = jnp.where(i == 0, 0.0, xp_ref[...])
    xe_scr[halo:halo + tg, :] = x_ref[...]
    ebd = ebd_ref[...]
    g_all, b_all, gts = decay_terms()
    n_split = 2
    rh = tg // n_split

    def conv_half(hf):
        base = halo - (GDN_CONV - 1) + rh * hf
        y = cw_ref[0:1, :] * xe_scr[base:base + rh, :]
        for j in range(1, GDN_CONV):
            y = y + cw_ref[j:j + 1, :] * xe_scr[base + j:base + j + rh, :]
        qkv = y * jax.nn.sigmoid(y)
        q = qkv[:, 0:GDN_WIDTH]
        k = qkv[:, GDN_WIDTH:2 * GDN_WIDTH]
        ss = jnp.dot(jnp.concatenate([q * q, k * k], axis=0).astype(BF16), ebd, preferred_element_type=F32)
        return (q * lax.rsqrt(ss[:rh] + 1e-6) * (HD ** -0.5), k * lax.rsqrt(ss[rh:] + 1e-6),
                qkv[:, 2 * GDN_WIDTH:])

    halves = [conv_half(hf) for hf in range(n_split)]
    q_all, k_all, v_all = (jnp.concatenate([hv[n] for hv in halves], axis=0) for n in range(3))

    ii = lax.broadcasted_iota(jnp.int32, (C, LANES), 0)
    lane = lax.broadcasted_iota(jnp.int32, (C, LANES), 1)
    jj = lane % HD
    eye2 = jnp.where(ii == jj, 1.0, 0.0)
    lo_half = lane < HD
    lane2 = lax.broadcasted_iota(jnp.int32, (LANES, LANES), 1)
    row2 = lax.broadcasted_iota(jnp.int32, (LANES, LANES), 0)
    same_head = (lane2 < HD) == (row2 < HD)

    def bd(x):
        xb = x.astype(BF16)
        zero = jnp.zeros_like(xb)
        return jnp.concatenate([jnp.where(lo_half, xb, zero), jnp.where(lo_half, zero, xb)], axis=0)

    def mm(a, b):
        return jnp.dot(a.astype(BF16), b, preferred_element_type=F32)

    chains = [(c, pr) for c in range(n_ch) for pr in range(n_pr)]

    def blk(arr, c, pr):
        return arr[C * c:C * (c + 1), LANES * pr:LANES * (pr + 1)]

    qg, kdec, dec, aq, vb_bd, kbg_bd, eglast = {}, {}, {}, {}, {}, {}, {}
    for ch in chains:
        c, pr = ch
        qn, k_, gc, be = blk(q_all, c, pr), blk(k_all, c, pr), blk(g_all, c, pr), blk(b_all, c, pr)
        kb = k_ * be
        eg = jnp.exp(gc)
        glast = gc[C - 1:C, :]
        qg[ch] = qn * eg
        kdec[ch] = k_ * jnp.exp(glast - gc)
        eglast[ch] = jnp.exp(glast)
        vb_bd[ch] = bd(blk(v_all, c, pr) * be)
        kbg_bd[ch] = bd(kb * eg)
        gt = gts[c]
        gct = jnp.concatenate([jnp.broadcast_to(gt[2 * pr:2 * pr + 1, :], (C, HD)),
                               jnp.broadcast_to(gt[2 * pr + 1:2 * pr + 2, :], (C, HD))], axis=1)
        dec[ch] = jnp.exp(jnp.where(ii >= jj, gc - gct, NEG_INF))
        aq[ch] = lax.dot_general(jnp.concatenate([kb, qn], axis=0).astype(BF16), bd(k_), NT_DIMS,
                                 preferred_element_type=F32)
    a = {ch: jnp.where(ii > jj, aq[ch][:C] * dec[ch], 0.0) for ch in chains}
    qk = {ch: aq[ch][C:] * dec[ch] for ch in chains}
    x = {ch: eye2 - a[ch] for ch in chains}
    pw = {ch: mm(a[ch], bd(a[ch])) for ch in chains}
    for _ in range(4):
        both = {ch: mm(jnp.concatenate([x[ch], pw[ch]], axis=0), bd(pw[ch])) for ch in chains}
        x = {ch: x[ch] + both[ch][:C] for ch in chains}
        pw = {ch: both[ch][C:] for ch in chains}
    x = {ch: x[ch] + mm(x[ch], bd(pw[ch])) for ch in chains}
    uw = {ch: mm(x[ch], jnp.concatenate([vb_bd[ch], kbg_bd[ch]], axis=1)) for ch in chains}

    ku_kw = {ch: lax.dot_general(kdec[ch].astype(BF16), uw[ch].astype(BF16), TN_DIMS,
                                 preferred_element_type=F32) for ch in chains}
    qu_qw = {ch: mm(qk[ch], jnp.concatenate([bd(uw[ch][:, :LANES]), bd(uw[ch][:, LANES:])], axis=1))
             for ch in chains}
    lhs = {ch: jnp.concatenate([jnp.where(same_head, ku_kw[ch][:, LANES:], 0.0),
                                qg[ch] - qu_qw[ch][:, LANES:]], axis=0).astype(BF16) for ch in chains}
    st = [st_scr[:, LANES * pr:LANES * (pr + 1)] for pr in range(n_pr)]
    o_rows = []
    for c in range(n_ch):
        prod = [jnp.dot(lhs[c, pr], st[pr].astype(BF16), preferred_element_type=F32)
                for pr in range(n_pr)]
        o_rows.append(jnp.concatenate([prod[pr][LANES:] + qu_qw[c, pr][:, :LANES] for pr in range(n_pr)], axis=1))
        st = [st[pr] * eglast[c, pr] - prod[pr][:LANES] + jnp.where(same_head, ku_kw[c, pr][:, :LANES], 0.0)
              for pr in range(n_pr)]
    st_scr[...] = jnp.concatenate(st, axis=1)

    o = jnp.concatenate(o_rows, axis=0)
    oms = jnp.dot((o * o).astype(BF16), ebd, preferred_element_type=F32) * (1.0 / HD)
    z = z_ref[...]
    o_ref[...] = o * lax.rsqrt(oms + RMS_EPS) * nw_ref[...] * (z * jax.nn.sigmoid(z))


def _gdn(gqkv, sm, gz, conv_w, a_log, dt_bias, norm_w, B, S, tg=256):
    rows = B * S
    nt = S // tg
    head_of = np.arange(GDN_WIDTH) // HD
    ebd = jnp.asarray(head_of[:, None] == head_of[None, :], dtype=BF16)
    eg = np.zeros((LANES, GDN_WIDTH), np.float32)
    eb = np.zeros((LANES, GDN_WIDTH), np.float32)
    eg[SM_A + head_of, np.arange(GDN_WIDTH)] = 1.0
    eb[SM_B + head_of, np.arange(GDN_WIDTH)] = 1.0
    r = np.arange(tg)
    lblk = ((r[:, None] // GDN_CHUNK == r[None, :] // GDN_CHUNK) & (r[None, :] <= r[:, None])).astype(np.float32)
    alog128 = jnp.zeros((1, LANES), F32).at[0, SM_A:SM_A + GDN_HEADS].set(a_log.astype(F32))
    dt128 = jnp.zeros((1, LANES), F32).at[0, SM_A:SM_A + GDN_HEADS].set(dt_bias.astype(F32))
    nw512 = jnp.tile(norm_w.astype(F32), GDN_HEADS).reshape(1, GDN_WIDTH)
    full = lambda shape: pl.BlockSpec(shape, lambda b, i: (0,) * len(shape))
    return pl.pallas_call(
        functools.partial(_gdn_kernel, tg=tg),
        grid=(B, nt),
        in_specs=[
            pl.BlockSpec((tg, 3 * GDN_WIDTH), lambda b, i: (b * nt + i, 0)),
            pl.BlockSpec((F32_SUBLANES, 3 * GDN_WIDTH),
                         lambda b, i: (jnp.maximum((b * nt + i) * (tg // F32_SUBLANES) - 1, 0), 0)),
            pl.BlockSpec((tg, LANES), lambda b, i: (b * nt + i, 0)),
            pl.BlockSpec((tg, GDN_WIDTH), lambda b, i: (b * nt + i, 0)),
            full((GDN_CONV, 3 * GDN_WIDTH)),
            full((1, LANES)),
            full((1, LANES)),
            full((1, GDN_WIDTH)),
            full((GDN_WIDTH, GDN_WIDTH)),
            full((LANES, GDN_WIDTH)),
            full((LANES, GDN_WIDTH)),
            full((tg, tg)),
        ],
        out_specs=pl.BlockSpec((tg, GDN_WIDTH), lambda b, i: (b * nt + i, 0)),
        out_shape=jax.ShapeDtypeStruct((rows, GDN_WIDTH), F32),
        scratch_shapes=[
            pltpu.VMEM((tg + F32_SUBLANES, 3 * GDN_WIDTH), F32),
            pltpu.VMEM((LANES, GDN_WIDTH), F32),
        ],
        compiler_params=_cparams(("arbitrary", "arbitrary")),
        name="gdn",
    )(gqkv, gqkv, sm, gz, conv_w, alog128, dt128, nw512, ebd, jnp.asarray(eg), jnp.asarray(eb, dtype=BF16),
      jnp.asarray(lblk))


def _mix_xattn_kernel(x_ref, a_ref, b_ref, wa_ref, wb_ref, nw_ref, wq_ref, kv_ref, wo_ref, o_ref):
    mix = jnp.dot(a_ref[...].astype(BF16), wa_ref[...], preferred_element_type=F32)
    mix = mix + jnp.dot(b_ref[...].astype(BF16), wb_ref[...], preferred_element_type=F32)
    _xattn_rows(x_ref[...] + mix, nw_ref, wq_ref, kv_ref, wo_ref, o_ref)


def _xattn_kernel(x_ref, nw_ref, wq_ref, kv_ref, wo_ref, o_ref):
    _xattn_rows(x_ref[...], nw_ref, wq_ref, kv_ref, wo_ref, o_ref)


def _xattn_rows(x, nw_ref, wq_ref, kv_ref, wo_ref, o_ref):
    hn = _rms(x, nw_ref[...]).astype(BF16)
    q = jnp.dot(hn, wq_ref[...], preferred_element_type=F32) * (XA_HD ** -0.5)
    heads = range(XA_HEADS)
    ss = [lax.dot_general(q[:, XA_HD * h:XA_HD * (h + 1)].astype(BF16), kv_ref[:, XA_HD * h:XA_HD * (h + 1)],
                          NT_DIMS, preferred_element_type=F32) for h in heads]
    es = [jnp.exp(ss[h] - jnp.max(ss[h], axis=-1, keepdims=True)).astype(BF16) for h in heads]
    ones = jnp.ones((kv_ref.shape[0], XA_HD), BF16)
    pvs = [jnp.dot(es[h], jnp.concatenate([kv_ref[:, XA_WIDTH + XA_HD * h:XA_WIDTH + XA_HD * (h + 1)], ones], axis=1),
                   preferred_element_type=F32) for h in heads]
    o = jnp.concatenate([pvs[h][:, :XA_HD] / pvs[h][:, XA_HD:XA_HD + 1] for h in heads], axis=1).astype(BF16)
    o_ref[...] = x + jnp.dot(o, wo_ref[...], preferred_element_type=F32)


def _xattn(x2d, norm_w, wq_bf, memkv_l, wo_bf, S, tm=512):
    rows = x2d.shape[0]
    per_b = S // tm
    mem_len = memkv_l.shape[0] // (rows // S)
    return pl.pallas_call(
        _xattn_kernel,
        grid=(rows // tm,),
        in_specs=[
            pl.BlockSpec((tm, D_MODEL), lambda i: (i, 0)),
            pl.BlockSpec((1, D_MODEL), lambda i: (0, 0)),
            pl.BlockSpec((D_MODEL, XA_WIDTH), lambda i: (0, 0)),
            pl.BlockSpec((mem_len, 2 * XA_WIDTH), lambda i: (i // per_b, 0)),
            pl.BlockSpec((XA_WIDTH, D_MODEL), lambda i: (0, 0)),
        ],
        out_specs=pl.BlockSpec((tm, D_MODEL), lambda i: (i, 0)),
        out_shape=jax.ShapeDtypeStruct((rows, D_MODEL), F32),
        compiler_params=_cparams(("arbitrary",)),
        name="cross_attention",
    )(x2d, norm_w.reshape(1, D_MODEL), wq_bf, memkv_l, wo_bf)


def _mix_xattn(x2d, o_nsa, o_gdn, w_out_bf, norm_w, wq_bf, memkv_l, wo_bf, S, tm=512):
    rows = x2d.shape[0]
    per_b = S // tm
    mem_len = memkv_l.shape[0] // (rows // S)
    return pl.pallas_call(
        _mix_xattn_kernel,
        grid=(rows // tm,),
        in_specs=[
            pl.BlockSpec((tm, D_MODEL), lambda i: (i, 0)),
            pl.BlockSpec((tm, NSA_WIDTH), lambda i: (i, 0)),
            pl.BlockSpec((tm, GDN_WIDTH), lambda i: (i, 0)),
            pl.BlockSpec((NSA_WIDTH, D_MODEL), lambda i: (0, 0)),
            pl.BlockSpec((GDN_WIDTH, D_MODEL), lambda i: (1, 0)),
            pl.BlockSpec((1, D_MODEL), lambda i: (0, 0)),
            pl.BlockSpec((D_MODEL, XA_WIDTH), lambda i: (0, 0)),
            pl.BlockSpec((mem_len, 2 * XA_WIDTH), lambda i: (i // per_b, 0)),
            pl.BlockSpec((XA_WIDTH, D_MODEL), lambda i: (0, 0)),
        ],
        out_specs=pl.BlockSpec((tm, D_MODEL), lambda i: (i, 0)),
        out_shape=jax.ShapeDtypeStruct((rows, D_MODEL), F32),
        compiler_params=_cparams(("arbitrary",)),
        name="mix_cross_attention",
    )(x2d, o_nsa, o_gdn, w_out_bf, w_out_bf, norm_w.reshape(1, D_MODEL), wq_bf, memkv_l, wo_bf)


def _mlp_kernel(x_ref, nw_ref, w1_ref, w2_ref, fw_ref, o_ref, hn_scr, acc_scr, *, final_norm):
    j = pl.program_id(1)

    @pl.when(j == 0)
    def _():
        x = x_ref[...]
        hn_scr[...] = _rms(x, nw_ref[...]).astype(BF16)
        acc_scr[...] = x

    h = jnp.dot(hn_scr[...], w1_ref[...], preferred_element_type=F32)
    h = jnp.square(jnp.maximum(h, 0.0)).astype(BF16)
    acc_scr[...] += jnp.dot(h, w2_ref[...], preferred_element_type=F32)

    @pl.when(j == pl.num_programs(1) - 1)
    def _():
        y = acc_scr[...]
        if final_norm:
            y = _rms(y, fw_ref[...])
        o_ref[...] = y


def _mlp(x2d, norm_w, w1_bf, w2_bf, final_w, final_norm, tm=1024, tf=1024):
    rows = x2d.shape[0]
    tm = min(tm, rows)
    return pl.pallas_call(
        functools.partial(_mlp_kernel, final_norm=final_norm),
        grid=(rows // tm, D_FF // tf),
        in_specs=[
            pl.BlockSpec((tm, D_MODEL), lambda i, j: (i, 0)),
            pl.BlockSpec((1, D_MODEL), lambda i, j: (0, 0)),
            pl.BlockSpec((D_MODEL, tf), lambda i, j: (0, j)),
            pl.BlockSpec((tf, D_MODEL), lambda i, j: (j, 0)),
            pl.BlockSpec((1, D_MODEL), lambda i, j: (0, 0)),
        ],
        out_specs=pl.BlockSpec((tm, D_MODEL), lambda i, j: (i, 0)),
        out_shape=jax.ShapeDtypeStruct((rows, D_MODEL), F32),
        scratch_shapes=[pltpu.VMEM((tm, D_MODEL), BF16), pltpu.VMEM((tm, D_MODEL), F32)],
        compiler_params=_cparams(("arbitrary", "arbitrary")),
        name="mlp",
    )(x2d, norm_w.reshape(1, D_MODEL), w1_bf, w2_bf, final_w.reshape(1, D_MODEL))


def _sc_kernel(x_ref, nw_ref, win_ref, cw_ref, wout_ref, o_ref, cu_scr, *, tm, per_b):
    i = pl.program_id(0)
    halo = F32_SUBLANES

    @pl.when(i % per_b == 0)
    def _():
        cu_scr[tm:tm + halo, :] = jnp.zeros((halo, D_MODEL), F32)

    x = x_ref[...]
    y = jnp.dot(_rms(x, nw_ref[...]).astype(BF16), win_ref[...], preferred_element_type=F32)
    cu_scr[0:halo, :] = cu_scr[tm:tm + halo, :]
    cu_scr[halo:halo + tm, :] = y[:, D_MODEL:2 * D_MODEL] * y[:, 2 * D_MODEL:]
    base = halo - (SC_WIDTH - 1)
    conv = cw_ref[0:1, :] * cu_scr[base:base + tm, :]
    for j in range(1, SC_WIDTH):
        conv = conv + cw_ref[j:j + 1, :] * cu_scr[base + j:base + j + tm, :]
    gated = (y[:, :D_MODEL] * conv).astype(BF16)
    o_ref[...] = x + jnp.dot(gated, wout_ref[...], preferred_element_type=F32)


def _sc_mixer(x2d, norm_w, win_bf, conv_w, wout_bf, S, tm=512):
    rows = x2d.shape[0]
    per_b = S // tm
    return pl.pallas_call(
        functools.partial(_sc_kernel, tm=tm, per_b=per_b),
        grid=(rows // tm,),
        in_specs=[
            pl.BlockSpec((tm, D_MODEL), lambda i: (i, 0)),
            pl.BlockSpec((1, D_MODEL), lambda i: (0, 0)),
            pl.BlockSpec((D_MODEL, 3 * D_MODEL), lambda i: (0, 0)),
            pl.BlockSpec((SC_WIDTH, D_MODEL), lambda i: (0, 0)),
            pl.BlockSpec((D_MODEL, D_MODEL), lambda i: (0, 0)),
        ],
        out_specs=pl.BlockSpec((tm, D_MODEL), lambda i: (i, 0)),
        out_shape=jax.ShapeDtypeStruct((rows, D_MODEL), F32),
        scratch_shapes=[pltpu.VMEM((tm + F32_SUBLANES, D_MODEL), F32)],
        compiler_params=_cparams(("arbitrary",)),
        name="short_conv_mixer",
    )(x2d, norm_w.reshape(1, D_MODEL), win_bf, conv_w, wout_bf)


def _hybrid_mixer(x2d, positions, norm_w, w_in, ck_pos, ck_w1, ck_w2, cv_pos, cv_w1, cv_w2,
                  gdn_conv, gdn_a_log, gdn_dt_bias, gdn_norm, B, S):
    tm = 512
    invf = (ROPE_THETA ** (-jnp.arange(0, HD, 2, dtype=F32) / HD)).reshape(HD // 2, 1)
    pos3 = positions.astype(F32).reshape(B * S // tm, 1, tm)
    qT, kr, vT, ckv, gqkv, gz, sm, gT = _proj0(x2d, norm_w, _prep_w_in(w_in), pos3, invf, B, S, tm)
    w1s, p8, w2s = _prep_compress_weights(ck_pos, ck_w1, ck_w2, cv_pos, cv_w1, cv_w2)
    kc, vcT = _compress(ckv, w1s, p8, w2s, B, S)
    o_nsa = _nsa(qT, kr, vT, kc, vcT, gT, B, S)
    o_gdn = _gdn(gqkv, sm, gz, gdn_conv, gdn_a_log, gdn_dt_bias, gdn_norm, B, S)
    return o_nsa, o_gdn


def kernel(x, mem, positions, norm_mix, norm_xattn, norm_mlp, hyb_w_in, hyb_cmp_k_pos, hyb_cmp_k_w1, hyb_cmp_k_w2, hyb_cmp_v_pos, hyb_cmp_v_w1, hyb_cmp_v_w2, hyb_gdn_conv, hyb_gdn_a_log, hyb_gdn_dt_bias, hyb_gdn_norm, hyb_w_out, sc_w_in, sc_conv, sc_w_out, mem_norm, xa_wq, xa_wkv, xa_wo, mlp_w1, mlp_w2, final_norm):
    B, S, _ = x.shape
    depth = norm_mix.shape[0]
    x2d = x.reshape(B * S, D_MODEL)
    memkv = _memkv(mem.reshape(-1, D_MODEL), mem_norm, xa_wkv.astype(BF16))
    for layer in range(depth):
        j = layer // 2
        xa = (norm_xattn[layer], xa_wq[layer].astype(BF16), memkv[layer], xa_wo[layer].astype(BF16), S)
        if layer % 2 == 0:
            o_nsa, o_gdn = _hybrid_mixer(x2d, positions, norm_mix[layer], hyb_w_in[j], hyb_cmp_k_pos[j],
                                         hyb_cmp_k_w1[j], hyb_cmp_k_w2[j], hyb_cmp_v_pos[j], hyb_cmp_v_w1[j],
                                         hyb_cmp_v_w2[j], hyb_gdn_conv[j], hyb_gdn_a_log[j], hyb_gdn_dt_bias[j],
                                         hyb_gdn_norm[j], B, S)
            x2d = _mix_xattn(x2d, o_nsa, o_gdn, hyb_w_out[j].astype(BF16), *xa)
        else:
            x2d = _sc_mixer(x2d, norm_mix[layer], sc_w_in[j].astype(BF16), sc_conv[j], sc_w_out[j].astype(BF16), S)
            x2d = _xattn(x2d, *xa)
        x2d = _mlp(x2d, norm_mlp[layer], mlp_w1[layer].astype(BF16), mlp_w2[layer].astype(BF16),
                   final_norm, layer == depth - 1)
    return x2d.reshape(B, S, D_MODEL)
```

```python
import functools

import numpy as np
import jax
import jax.numpy as jnp
from jax import lax
from jax.experimental import pallas as pl
from jax.experimental.pallas import tpu as pltpu

F32 = jnp.float32
BF16 = jnp.bfloat16
HI = lax.Precision.HIGHEST

D_MODEL = 1024
MEM_LEN = 256
RMS_EPS = 1e-6
ROPE_THETA = 10000.0
NEG_INF = -1e30
FORCE_BONUS = 1e3
LOG2E = 1.4426950408889634

NSA_HEADS = 8
NSA_KV_GROUPS = 2
NSA_HPG = NSA_HEADS // NSA_KV_GROUPS
HD = 64
CMP_LEN = 32
CMP_STRIDE = 16
CMP_HID = 2 * HD
SLC_BLK = 64
SLC_TOPK = 16
WINDOW = 512

GDN_HEADS = 8
GDN_CONV = 4
GDN_CHUNK = 64
SC_WIDTH = 3
XA_HEADS = 4
XA_HD = 128
D_FF = 4 * D_MODEL

NSA_WIDTH = NSA_HEADS * HD
NSA_KV_WIDTH = NSA_KV_GROUPS * HD
GDN_WIDTH = GDN_HEADS * HD
XA_WIDTH = XA_HEADS * XA_HD
IN_SIZES = (NSA_WIDTH,) + (NSA_KV_WIDTH,) * 6 + (3 * NSA_HEADS,) + (GDN_WIDTH,) * 3 + (GDN_HEADS, GDN_HEADS, GDN_WIDTH)

V7X_VMEM_BYTES = 64 * 1024 * 1024
VMEM_LIMIT = V7X_VMEM_BYTES * 3 // 4
LANES = 128
F32_SUBLANES = 8
BF16_ROWS = 16

C_Q, C_KR, C_VT, C_CKV, C_GQKV, C_GZ, C_SM, NC0 = 0, 512, 768, 1024, 1280, 2816, 3328, 3456
SM_GATE, SM_A, SM_B = 0, 24, 32
NSA_CHUNK = 256
KR_WIDTH = 3 * LANES
KR_WIN = 2 * LANES

NT_DIMS = (((1,), (1,)), ((), ()))
TN_DIMS = (((0,), (0,)), ((), ()))


def _cparams(sem):
    return pltpu.CompilerParams(dimension_semantics=sem, vmem_limit_bytes=VMEM_LIMIT)


def _rms(x, w):
    return x * lax.rsqrt(jnp.mean(x * x, axis=-1, keepdims=True) + RMS_EPS) * w


def _softplus(x):
    return jnp.maximum(x, 0.0) + jnp.log(1.0 + jnp.exp(-jnp.abs(x)))


def _memkv_kernel(m_ref, nw_ref, w_ref, o_ref):
    hn = _rms(m_ref[...], nw_ref[...]).astype(BF16)
    o_ref[...] = jnp.dot(hn, w_ref[...], preferred_element_type=F32).astype(BF16)


def _memkv(mem2d, mem_norm, wkv_bf):
    rows = mem2d.shape[0]
    depth = wkv_bf.shape[0]
    tm = 512
    return pl.pallas_call(
        _memkv_kernel,
        grid=(depth, rows // tm),
        in_specs=[
            pl.BlockSpec((tm, D_MODEL), lambda l, i: (i, 0)),
            pl.BlockSpec((1, D_MODEL), lambda l, i: (0, 0)),
            pl.BlockSpec((None, D_MODEL, 2 * XA_WIDTH), lambda l, i: (l, 0, 0)),
        ],
        out_specs=pl.BlockSpec((None, tm, 2 * XA_WIDTH), lambda l, i: (l, i, 0)),
        out_shape=jax.ShapeDtypeStruct((depth, rows, 2 * XA_WIDTH), BF16),
        compiler_params=_cparams(("arbitrary", "arbitrary")),
        name="memkv",
    )(mem2d, mem_norm.reshape(1, D_MODEL), wkv_bf)


def _proj0_kernel(x_ref, nw_ref, w_ref, pos_ref, invf_ref,
                  qT_ref, kr_ref, vT_ref, ckv_ref, gqkv_ref, gz_ref, sm_ref, gT_ref):
    tm = x_ref.shape[0]
    hn = _rms(x_ref[...], nw_ref[...]).astype(BF16)
    y = jnp.dot(hn, w_ref[...], preferred_element_type=F32)

    ang = invf_ref[...] * pos_ref[...]
    c = jnp.cos(ang)
    s = jnp.sin(ang)
    cos_n = jnp.concatenate([c, c, c, c], axis=0).T
    sin_n = jnp.concatenate([-s, s, -s, s], axis=0).T
    lane = lax.broadcasted_iota(jnp.int32, (tm, LANES), 1)
    first_half = (lane % HD) < (HD // 2)

    def rope(xc):
        rot = jnp.where(first_half, pltpu.roll(xc, LANES - HD // 2, 1), pltpu.roll(xc, HD // 2, 1))
        return xc * cos_n + rot * sin_n

    roped = [rope(y[:, LANES * i:LANES * (i + 1)]) for i in range(6)]
    q = jnp.concatenate(roped[:4], axis=1) * (LOG2E * HD ** -0.5)
    qT_ref[...] = q.T.astype(BF16)
    row = lax.broadcasted_iota(jnp.int32, (tm, LANES), 0)
    blk_hot = jnp.where(lane - HD == (row // SLC_BLK) % (NSA_CHUNK // SLC_BLK), 1.0, 0.0)
    k_slc = roped[4]
    lo = lane < HD
    kr_ref[...] = jnp.concatenate([jnp.where(lo, k_slc, blk_hot),
                                   jnp.where(lo, pltpu.roll(k_slc, HD, 1), blk_hot),
                                   roped[5]], axis=1).astype(BF16)
    vT_ref[...] = y[:, C_VT:C_CKV].T.astype(BF16)
    for i in range(4):
        ckv_ref[i] = y[:, C_CKV + HD * i:C_CKV + HD * (i + 1)].astype(BF16)
    gqkv_ref[...] = y[:, C_GQKV:C_GZ]
    gz_ref[...] = y[:, C_GZ:C_SM]
    sm = y[:, C_SM:NC0]
    sm_ref[...] = sm
    gT_ref[...] = jax.nn.sigmoid(sm).T[:3 * NSA_HEADS, :]


def _prep_w_in(w_in):
    offs = np.cumsum((0,) + IN_SIZES)
    (nq, kcmp, vcmp, kslc, vslc, kwin, vwin, ngate, gq, gk, gv, ga, gb, gz) = [
        w_in[:, offs[i]:offs[i + 1]] for i in range(len(IN_SIZES))]
    pad = jnp.zeros((D_MODEL, LANES - 3 * NSA_HEADS - 2 * GDN_HEADS), w_in.dtype)
    small = jnp.concatenate([ngate, ga, gb, pad], axis=1)
    return jnp.concatenate([nq, kslc, kwin, vslc, vwin, kcmp, vcmp, gq, gk, gv, gz, small], axis=1).astype(BF16)


def _proj0(x2d, norm_w, w_bf, pos3, invf, B, S, tm):
    rows = B * S
    nt = S // tm
    row_spec = lambda n: pl.BlockSpec((tm, n), lambda i: (i, 0))
    t_spec = lambda n: pl.BlockSpec((None, n, tm), lambda i: (i // nt, 0, i % nt))
    return pl.pallas_call(
        _proj0_kernel,
        grid=(rows // tm,),
        in_specs=[
            row_spec(D_MODEL),
            pl.BlockSpec((1, D_MODEL), lambda i: (0, 0)),
            pl.BlockSpec((D_MODEL, NC0), lambda i: (0, 0)),
            pl.BlockSpec((None, 1, tm), lambda i: (i, 0, 0)),
            pl.BlockSpec((HD // 2, 1), lambda i: (0, 0)),
        ],
        out_specs=[
            t_spec(NSA_WIDTH),
            row_spec(KR_WIDTH),
            t_spec(2 * NSA_KV_WIDTH),
            pl.BlockSpec((4, tm, HD), lambda i: (0, i, 0)),
            row_spec(3 * GDN_WIDTH),
            row_spec(GDN_WIDTH),
            row_spec(LANES),
            t_spec(3 * NSA_HEADS),
        ],
        out_shape=[
            jax.ShapeDtypeStruct((B, NSA_WIDTH, S), BF16),
            jax.ShapeDtypeStruct((rows, KR_WIDTH), BF16),
            jax.ShapeDtypeStruct((B, 2 * NSA_KV_WIDTH, S), BF16),
            jax.ShapeDtypeStruct((4, rows, HD), BF16),
            jax.ShapeDtypeStruct((rows, 3 * GDN_WIDTH), F32),
            jax.ShapeDtypeStruct((rows, GDN_WIDTH), F32),
            jax.ShapeDtypeStruct((rows, LANES), F32),
            jax.ShapeDtypeStruct((B, 3 * NSA_HEADS, S), F32),
        ],
        compiler_params=_cparams(("arbitrary",)),
        name="proj0",
    )(x2d, norm_w.reshape(1, D_MODEL), w_bf, pos3, invf)


def _compress_kernel(t_ref, w1_ref, pos_ref, w2_ref, o_ref, oT_ref):
    nseg = t_ref.shape[0]
    w1 = w1_ref[...]
    ab = jnp.dot(t_ref[...], w1, preferred_element_type=F32)
    pc = jnp.dot(pos_ref[...].astype(BF16), w1, preferred_element_type=F32)
    bias = pc[0:1, :CMP_HID] + pc[1:2, CMP_HID:]
    hid = ab[:, :CMP_HID] + pltpu.roll(ab[:, CMP_HID:], nseg - 1, 0) + bias
    act = hid * jax.nn.sigmoid(hid)
    out = jnp.dot(act.astype(BF16), w2_ref[...], preferred_element_type=F32)
    o_ref[...] = out
    oT_ref[...] = out.T[:HD, :]


def _compress(ckv, w1s, pos8, w2s, B, S):
    nseg = S // CMP_STRIDE
    t = ckv.reshape(4, B, nseg, CMP_STRIDE * HD)
    return pl.pallas_call(
        _compress_kernel,
        grid=(4, B),
        in_specs=[
            pl.BlockSpec((None, None, nseg, CMP_STRIDE * HD), lambda k, b: (k, b, 0, 0)),
            pl.BlockSpec((None, CMP_STRIDE * HD, 2 * CMP_HID), lambda k, b: (k // 2, 0, 0)),
            pl.BlockSpec((None, F32_SUBLANES, CMP_STRIDE * HD), lambda k, b: (k // 2, 0, 0)),
            pl.BlockSpec((None, CMP_HID, LANES), lambda k, b: (k // 2, 0, 0)),
        ],
        out_specs=[
            pl.BlockSpec((None, None, nseg, LANES), lambda k, b: (k, b, 0, 0)),
            pl.BlockSpec((None, None, HD, nseg), lambda k, b: (k, b, 0, 0)),
        ],
        out_shape=[
            jax.ShapeDtypeStruct((4, B, nseg, LANES), F32),
            jax.ShapeDtypeStruct((4, B, HD, nseg), F32),
        ],
        compiler_params=_cparams(("arbitrary", "arbitrary")),
        name="nsa_compress",
    )(t, w1s, pos8, w2s)


def _prep_compress_weights(k_pos, k_w1, k_w2, v_pos, v_w1, v_w2):
    half = CMP_STRIDE * HD

    def w1cat(w1):
        return jnp.concatenate([w1[:half], w1[half:]], axis=1)

    def pos8(p):
        flat = p.reshape(2, half)
        return jnp.concatenate([flat, jnp.zeros((F32_SUBLANES - 2, half), p.dtype)], axis=0)

    def w2pad(w2):
        return jnp.concatenate([w2, jnp.zeros((CMP_HID, LANES - HD), w2.dtype)], axis=1)

    w1s = jnp.stack([w1cat(k_w1), w1cat(v_w1)]).astype(BF16)
    p8 = jnp.stack([pos8(k_pos), pos8(v_pos)])
    w2s = jnp.stack([w2pad(k_w2), w2pad(v_w2)]).astype(BF16)
    return w1s, p8, w2s


def _nsa_kernel(qT_ref, kr_ref, vT_ref, kc_ref, vcT_ref, gT_ref, o_ref,
                imp_scr, sel_scr, s_scr, p_scr, b_scr, ps_scr, *, S, TQ):
    CH = TQ
    groups = range(NSA_KV_GROUPS)
    n_cmp = S // CMP_STRIDE
    n_slc = S // SLC_BLK
    n_sel = min(SLC_TOPK, n_slc)
    qi = pl.program_id(1)
    q0 = qi * TQ
    t1 = q0 + lax.broadcasted_iota(jnp.int32, (1, TQ), 1)
    t4 = jnp.concatenate([t1] * NSA_HPG, axis=1)
    n_io = lax.broadcasted_iota(jnp.int32, (n_cmp, 1), 0)
    j_io = lax.broadcasted_iota(jnp.int32, (n_slc, 1), 0)
    k_io = lax.broadcasted_iota(jnp.int32, (CH, 1), 0)
    n_causal = (q0 + TQ) // SLC_BLK

    qs, q64s = [], []
    for g in groups:
        q64 = jnp.concatenate(
            [qT_ref[NSA_HPG * HD * g + HD * h:NSA_HPG * HD * g + HD * (h + 1), :] for h in range(NSA_HPG)], axis=1)
        zq = jnp.zeros_like(q64)
        qs.append(jnp.concatenate([q64, zq] if g == 0 else [zq, q64], axis=0))
        q64s.append(q64)

    n_lanes = NSA_HPG * TQ

    def probs(sc, bias, m_i):
        if bias is not None:
            sc = sc + jnp.concatenate([bias] * NSA_HPG, axis=1)
        m_new = jnp.maximum(m_i, jnp.max(sc, axis=0, keepdims=True))
        return m_new, jnp.exp2(m_i - m_new), jnp.exp2(sc - m_new).astype(BF16)

    def pv(vt, pr):
        return jnp.dot(vt, pr, preferred_element_type=F32)

    def m_init():
        return jnp.full((1, n_lanes), NEG_INF, F32)

    u_io = lax.broadcasted_iota(jnp.int32, (1, TQ), 1)
    causal_bias = jnp.where(k_io <= u_io, 0.0, NEG_INF)
    window_bias = jnp.where(k_io > u_io, 0.0, NEG_INF)

    s_cmp = [jnp.dot(kc_ref[g, :, 0:HD].astype(BF16), q64s[g], preferred_element_type=F32)
             for g in groups]
    n_back = WINDOW // CH
    n_win = n_back + 1
    w0 = jnp.maximum(qi - n_back, 0)
    rw = pl.multiple_of(w0 * CH, CH)
    k_win = kr_ref[pl.ds(rw, n_win * CH), KR_WIN:KR_WIN + LANES]
    s_win = [jnp.dot(k_win, qs[g], preferred_element_type=F32) for g in groups]
    valid = (n_io * CMP_STRIDE + (CMP_LEN - 1)) <= t4
    any_valid = t4 >= CMP_LEN - 1
    p_cmp, psum = [], []
    for g in groups:
        s = jnp.where(valid, s_cmp[g], NEG_INF)
        m = jnp.max(s, axis=0, keepdims=True)
        e = jnp.exp2(s - m)
        l = jnp.sum(e, axis=0, keepdims=True)
        p = e * jnp.where(any_valid, 1.0 / l, 0.0)
        p_cmp.append(p.astype(BF16))
        ph = p[:, 0:TQ]
        for h in range(1, NSA_HPG):
            ph = ph + p[:, h * TQ:(h + 1) * TQ]
        psum.append(ph)
    o_cmp = [jnp.dot(vcT_ref[g].astype(BF16), p_cmp[g], preferred_element_type=F32) for g in groups]
    bpb = SLC_BLK // CMP_STRIDE
    imps = []
    for g in groups:
        tiles = []
        for lt in range(TQ // LANES):
            ps_scr[g, lt, 0:F32_SUBLANES, :] = jnp.zeros((F32_SUBLANES, LANES), F32)
            ps_scr[g, lt, F32_SUBLANES:F32_SUBLANES + n_cmp, :] = psum[g][:, LANES * lt:LANES * (lt + 1)]
            imp = None
            for k in range(-(CMP_LEN // CMP_STRIDE - 1), bpb):
                lo_, hi_ = max(CMP_STRIDE * k, 0), min(CMP_STRIDE * k + CMP_LEN, SLC_BLK)
                term = (hi_ - lo_) / CMP_LEN * ps_scr[g, lt, pl.ds(F32_SUBLANES + k, n_slc, stride=bpb), :]
                imp = term if imp is None else imp + term
            tiles.append(imp)
        imps.append(jnp.concatenate(tiles, axis=1))
    hidden = jnp.full((CH, TQ), NEG_INF, F32)
    slab_bias = []
    for c in range(n_win):
        d = w0 + c - qi
        slab_bias.append(jnp.where(d == 0, causal_bias,
                                   jnp.where(d > 0, hidden, jnp.where(d == -n_back, window_bias, 0.0))))
    win_bias = jnp.concatenate(slab_bias, axis=0)
    win_p = [probs(s_win[g], win_bias, m_init())[2] for g in groups]
    o_win = []
    for g in groups:
        v_win = jnp.concatenate([vT_ref[NSA_KV_WIDTH + HD * g:NSA_KV_WIDTH + HD * (g + 1), pl.ds(rw, n_win * CH)],
                                 jnp.ones((BF16_ROWS, n_win * CH), BF16)], axis=0)
        acc = pv(v_win, win_p[g])
        o_win.append(acc[:HD] / acc[HD:HD + 1])
    cur = t1 // SLC_BLK
    forced = (j_io == 0) | (j_io == cur) | (j_io == cur - 1)
    causal = j_io <= cur
    vals = [jnp.where(causal, jnp.where(forced, imps[g] + FORCE_BONUS, imps[g]), -1.0) for g in groups]
    for g in groups:
        imp_scr[g] = vals[g]

    per_trip = TQ // SLC_BLK

    def rank_body(it, cnts):
        out = list(cnts)
        for i in [per_trip * it + r for r in range(per_trip)]:
            tie = jnp.where(j_io > i, 1.0, 0.0)
            for g in groups:
                vi = imp_scr[g, pl.ds(i, 1), :]
                out[g] = out[g] + jnp.where(vi > vals[g], 1.0, jnp.where(vi == vals[g], tie, 0.0))
        return tuple(out)

    cnts = lax.fori_loop(0, jnp.where(n_causal > n_sel, n_causal // per_trip, 0), rank_body,
                         tuple(jnp.zeros((n_slc, TQ), F32) for _ in groups))
    nb = CH // SLC_BLK
    for g in groups:
        selb = jnp.where(cnts[g] < float(n_sel), 0.0, NEG_INF)
        for jc in range(S // CH):
            rows = jnp.concatenate([selb[nb * jc:nb * (jc + 1), :], jnp.zeros((BF16_ROWS - nb, TQ), F32)], axis=0)
            sel_scr[g, jc] = jnp.concatenate([rows] * NSA_HPG, axis=1).astype(BF16)

    ones_rows = jnp.ones((BF16_ROWS, CH), BF16)
    zero_rows = jnp.zeros((HD - BF16_ROWS, n_lanes), BF16)
    acc_rows = HD + BF16_ROWS

    def slc_scores(jc):
        r0 = pl.multiple_of(jc * CH, CH)
        return [jnp.dot(kr_ref[pl.ds(r0, CH), LANES * g:LANES * (g + 1)],
                        jnp.concatenate([q64s[g], sel_scr[g, jc], zero_rows], axis=0),
                        preferred_element_type=F32) for g in groups]

    def values(jc, vrow):
        r0 = pl.multiple_of(jc * CH, CH)
        return [jnp.concatenate([vT_ref[vrow + HD * g:vrow + HD * (g + 1), pl.ds(r0, CH)], ones_rows], axis=0)
                for g in groups]

    def acc_init():
        return jnp.zeros((acc_rows, n_lanes), F32)

    def slc_half(jc, slot, ms):
        nxt = slc_scores(jc + 1)
        v_prev = values(jnp.maximum(jc - 1, 0), 0)
        pvs = [pv(v_prev[g], p_scr[g]) for g in groups]
        for g in groups:
            s_scr[1 - slot, g] = nxt[g]
        out = []
        for g in groups:
            m_new, alpha, pr = probs(s_scr[slot, g], None, ms[g])
            b_scr[g] = (b_scr[g] + pvs[g]) * alpha
            p_scr[g] = pr
            out.append(m_new)
        return tuple(out)

    first = slc_scores(0)
    for g in groups:
        s_scr[0, g] = first[g]
        p_scr[g] = jnp.zeros((CH, n_lanes), BF16)
        b_scr[g] = acc_init()
    ms = lax.fori_loop(0, qi // 2, lambda i, m: slc_half(2 * i + 1, 1, slc_half(2 * i, 0, m)),
                       tuple(m_init() for _ in groups))

    def slc_odd(_, m):
        m = slc_half(qi - 1, 0, m)
        s_scr[0] = s_scr[1]
        return m

    ms = lax.fori_loop(0, qi % 2, slc_odd, ms)

    v_prev = values(jnp.maximum(qi - 1, 0), 0)
    pvs = [pv(v_prev[g], p_scr[g]) for g in groups]
    slc_parts = [probs(s_scr[0, g], causal_bias, ms[g]) for g in groups]
    v_slc = values(qi, 0)
    o_slc = []
    for g in groups:
        acc = (b_scr[g] + pvs[g]) * slc_parts[g][1] + pv(v_slc[g], slc_parts[g][2])
        o_slc.append(acc[:HD] / acc[HD:HD + 1])

    for g in groups:
        rows = []
        for h in range(NSA_HPG):
            hh = NSA_HPG * g + h
            sl = slice(h * TQ, (h + 1) * TQ)
            rows.append(gT_ref[3 * hh:3 * hh + 1, :] * o_cmp[g][:, sl]
                        + gT_ref[3 * hh + 1:3 * hh + 2, :] * o_slc[g][:, sl]
                        + gT_ref[3 * hh + 2:3 * hh + 3, :] * o_win[g][:, sl])
        og = jnp.concatenate(rows, axis=0)
        o_ref[:, NSA_HPG * HD * g:NSA_HPG * HD * (g + 1)] = og.T


def _nsa(qT, kr, vT, kc, vcT, gT, B, S, TQ=NSA_CHUNK):
    n_cmp = S // CMP_STRIDE
    n_slc = S // SLC_BLK
    nq = S // TQ
    return pl.pallas_call(
        functools.partial(_nsa_kernel, S=S, TQ=TQ),
        grid=(B, nq),
        in_specs=[
            pl.BlockSpec((None, NSA_WIDTH, TQ), lambda b, i: (b, 0, i)),
            pl.BlockSpec((None, S, KR_WIDTH), lambda b, i: (b, 0, 0)),
            pl.BlockSpec((None, 2 * NSA_KV_WIDTH, S), lambda b, i: (b, 0, 0)),
            pl.BlockSpec((2, None, n_cmp, LANES), lambda b, i: (0, b, 0, 0)),
            pl.BlockSpec((2, None, HD, n_cmp), lambda b, i: (1, b, 0, 0)),
            pl.BlockSpec((None, 3 * NSA_HEADS, TQ), lambda b, i: (b, 0, i)),
        ],
        out_specs=pl.BlockSpec((TQ, NSA_WIDTH), lambda b, i: (b * nq + i, 0)),
        out_shape=jax.ShapeDtypeStruct((B * S, NSA_WIDTH), F32),
        scratch_shapes=[
            pltpu.VMEM((NSA_KV_GROUPS, n_slc, TQ), F32),
            pltpu.VMEM((NSA_KV_GROUPS, S // TQ, BF16_ROWS, NSA_HPG * TQ), BF16),
            pltpu.VMEM((2, NSA_KV_GROUPS, TQ, NSA_HPG * TQ), F32),
            pltpu.VMEM((NSA_KV_GROUPS, TQ, NSA_HPG * TQ), BF16),
            pltpu.VMEM((NSA_KV_GROUPS, HD + BF16_ROWS, NSA_HPG * TQ), F32),
            pltpu.VMEM((NSA_KV_GROUPS, TQ // LANES, F32_SUBLANES + n_cmp, LANES), F32),
        ],
        compiler_params=_cparams(("arbitrary", "arbitrary")),
        name="nsa_attention",
    )(qT, kr.reshape(B, S, KR_WIDTH), vT, kc, vcT, gT)


def _gdn_kernel(x_ref, xp_ref, sm_ref, z_ref, cw_ref, alog_ref, dt_ref, nw_ref,
                ebd_ref, eg_ref, eb_ref, lblk_ref, o_ref,
                xe_scr, st_scr, *, tg):
    C = GDN_CHUNK
    n_ch = tg // C
    n_pr = GDN_HEADS // 2
    i = pl.program_id(1)

    @pl.when(i == 0)
    def _():
        st_scr[...] = jnp.zeros_like(st_scr)

    def decay_terms():
        sm = sm_ref[...]
        g128 = -jnp.exp(alog_ref[...]) * _softplus(sm + dt_ref[...])
        gcs128 = jnp.dot(lblk_ref[...], g128, precision=HI, preferred_element_type=F32)
        g_exp = jnp.dot(gcs128, eg_ref[...], precision=HI, preferred_element_type=F32)
        b_exp = jnp.dot(jax.nn.sigmoid(sm).astype(BF16), eb_ref[...], preferred_element_type=F32)
        return g_exp, b_exp, [gcs128[C * c:C * (c + 1), :].T[SM_A:SM_A + GDN_HEADS, :]
                              for c in range(n_ch)]

    halo = F32_SUBLANES
    xe_scr[0:halo, :] = jnp.where(i == 0, 0.0, xp_ref[...])
    xe_scr[halo:halo + tg, :] = x_ref[...]
    ebd = ebd_ref[...]
    g_all, b_all, gts = decay_terms()
    n_split = 2
    rh = tg // n_split

    def conv_half(hf):
        base = halo - (GDN_CONV - 1) + rh * hf
        y = cw_ref[0:1, :] * xe_scr[base:base + rh, :]
        for j in range(1, GDN_CONV):
            y = y + cw_ref[j:j + 1, :] * xe_scr[base + j:base + j + rh, :]
        qkv = y * jax.nn.sigmoid(y)
        q = qkv[:, 0:GDN_WIDTH]
        k = qkv[:, GDN_WIDTH:2 * GDN_WIDTH]
        ss = jnp.dot(jnp.concatenate([q * q, k * k], axis=0).astype(BF16), ebd, preferred_element_type=F32)
        return (q * lax.rsqrt(ss[:rh] + 1e-6) * (HD ** -0.5), k * lax.rsqrt(ss[rh:] + 1e-6),
                qkv[:, 2 * GDN_WIDTH:])

    halves = [conv_half(hf) for hf in range(n_split)]
    q_all, k_all, v_all = (jnp.concatenate([hv[n] for hv in halves], axis=0) for n in range(3))

    ii = lax.broadcasted_iota(jnp.int32, (C, LANES), 0)
    lane = lax.broadcasted_iota(jnp.int32, (C, LANES), 1)
    jj = lane % HD
    eye2 = jnp.where(ii == jj, 1.0, 0.0)
    lo_half = lane < HD
    lane2 = lax.broadcasted_iota(jnp.int32, (LANES, LANES), 1)
    row2 = lax.broadcasted_iota(jnp.int32, (LANES, LANES), 0)
    same_head = (lane2 < HD) == (row2 < HD)

    def bd(x):
        xb = x.astype(BF16)
        zero = jnp.zeros_like(xb)
        return jnp.concatenate([jnp.where(lo_half, xb, zero), jnp.where(lo_half, zero, xb)], axis=0)

    def mm(a, b):
        return jnp.dot(a.astype(BF16), b, preferred_element_type=F32)

    chains = [(c, pr) for c in range(n_ch) for pr in range(n_pr)]

    def blk(arr, c, pr):
        return arr[C * c:C * (c + 1), LANES * pr:LANES * (pr + 1)]

    qg, kdec, dec, aq, vb_bd, kbg_bd, eglast = {}, {}, {}, {}, {}, {}, {}
    for ch in chains:
        c, pr = ch
        qn, k_, gc, be = blk(q_all, c, pr), blk(k_all, c, pr), blk(g_all, c, pr), blk(b_all, c, pr)
        kb = k_ * be
        eg = jnp.exp(gc)
        glast = gc[C - 1:C, :]
        qg[ch] = qn * eg
        kdec[ch] = k_ * jnp.exp(glast - gc)
        eglast[ch] = jnp.exp(glast)
        vb_bd[ch] = bd(blk(v_all, c, pr) * be)
        kbg_bd[ch] = bd(kb * eg)
        gt = gts[c]
        gct = jnp.concatenate([jnp.broadcast_to(gt[2 * pr:2 * pr + 1, :], (C, HD)),
                               jnp.broadcast_to(gt[2 * pr + 1:2 * pr + 2, :], (C, HD))], axis=1)
        dec[ch] = jnp.exp(jnp.where(ii >= jj, gc - gct, NEG_INF))
        aq[ch] = lax.dot_general(jnp.concatenate([kb, qn], axis=0).astype(BF16), bd(k_), NT_DIMS,
                                 preferred_element_type=F32)
    a = {ch: jnp.where(ii > jj, aq[ch][:C] * dec[ch], 0.0) for ch in chains}
    qk = {ch: aq[ch][C:] * dec[ch] for ch in chains}
    x = {ch: eye2 - a[ch] for ch in chains}
    pw = {ch: mm(a[ch], bd(a[ch])) for ch in chains}
    for _ in range(4):
        both = {ch: mm(jnp.concatenate([x[ch], pw[ch]], axis=0), bd(pw[ch])) for ch in chains}
        x = {ch: x[ch] + both[ch][:C] for ch in chains}
        pw = {ch: both[ch][C:] for ch in chains}
    x = {ch: x[ch] + mm(x[ch], bd(pw[ch])) for ch in chains}
    uw = {ch: mm(x[ch], jnp.concatenate([vb_bd[ch], kbg_bd[ch]], axis=1)) for ch in chains}

    ku_kw = {ch: lax.dot_general(kdec[ch].astype(BF16), uw[ch].astype(BF16), TN_DIMS,
                                 preferred_element_type=F32) for ch in chains}
    qu_qw = {ch: mm(qk[ch], jnp.concatenate([bd(uw[ch][:, :LANES]), bd(uw[ch][:, LANES:])], axis=1))
             for ch in chains}
    lhs = {ch: jnp.concatenate([jnp.where(same_head, ku_kw[ch][:, LANES:], 0.0),
                                qg[ch] - qu_qw[ch][:, LANES:]], axis=0).astype(BF16) for ch in chains}
    st = [st_scr[:, LANES * pr:LANES * (pr + 1)] for pr in range(n_pr)]
    o_rows = []
    for c in range(n_ch):
        prod = [jnp.dot(lhs[c, pr], st[pr].astype(BF16), preferred_element_type=F32)
                for pr in range(n_pr)]
        o_rows.append(jnp.concatenate([prod[pr][LANES:] + qu_qw[c, pr][:, :LANES] for pr in range(n_pr)], axis=1))
        st = [st[pr] * eglast[c, pr] - prod[pr][:LANES] + jnp.where(same_head, ku_kw[c, pr][:, :LANES], 0.0)
              for pr in range(n_pr)]
    st_scr[...] = jnp.concatenate(st, axis=1)

    o = jnp.concatenate(o_rows, axis=0)
    oms = jnp.dot((o * o).astype(BF16), ebd, preferred_element_type=F32) * (1.0 / HD)
    z = z_ref[...]
    o_ref[...] = o * lax.rsqrt(oms + RMS_EPS) * nw_ref[...] * (z * jax.nn.sigmoid(z))


def _gdn(gqkv, sm, gz, conv_w, a_log, dt_bias, norm_w, B, S, tg=256):
    rows = B * S
    nt = S // tg
    head_of = np.arange(GDN_WIDTH) // HD
    ebd = jnp.asarray(head_of[:, None] == head_of[None, :], dtype=BF16)
    eg = np.zeros((LANES, GDN_WIDTH), np.float32)
    eb = np.zeros((LANES, GDN_WIDTH), np.float32)
    eg[SM_A + head_of, np.arange(GDN_WIDTH)] = 1.0
    eb[SM_B + head_of, np.arange(GDN_WIDTH)] = 1.0
    r = np.arange(tg)
    lblk = ((r[:, None] // GDN_CHUNK == r[None, :] // GDN_CHUNK) & (r[None, :] <= r[:, None])).astype(np.float32)
    alog128 = jnp.zeros((1, LANES), F32).at[0, SM_A:SM_A + GDN_HEADS].set(a_log.astype(F32))
    dt128 = jnp.zeros((1, LANES), F32).at[0, SM_A:SM_A + GDN_HEADS].set(dt_bias.astype(F32))
    nw512 = jnp.tile(norm_w.astype(F32), GDN_HEADS).reshape(1, GDN_WIDTH)
    full = lambda shape: pl.BlockSpec(shape, lambda b, i: (0,) * len(shape))
    return pl.pallas_call(
        functools.partial(_gdn_kernel, tg=tg),
        grid=(B, nt),
        in_specs=[
            pl.BlockSpec((tg, 3 * GDN_WIDTH), lambda b, i: (b * nt + i, 0)),
            pl.BlockSpec((F32_SUBLANES, 3 * GDN_WIDTH),
                         lambda b, i: (jnp.maximum((b * nt + i) * (tg // F32_SUBLANES) - 1, 0), 0)),
            pl.BlockSpec((tg, LANES), lambda b, i: (b * nt + i, 0)),
            pl.BlockSpec((tg, GDN_WIDTH), lambda b, i: (b * nt + i, 0)),
            full((GDN_CONV, 3 * GDN_WIDTH)),
            full((1, LANES)),
            full((1, LANES)),
            full((1, GDN_WIDTH)),
            full((GDN_WIDTH, GDN_WIDTH)),
            full((LANES, GDN_WIDTH)),
            full((LANES, GDN_WIDTH)),
            full((tg, tg)),
        ],
        out_specs=pl.BlockSpec((tg, GDN_WIDTH), lambda b, i: (b * nt + i, 0)),
        out_shape=jax.ShapeDtypeStruct((rows, GDN_WIDTH), F32),
        scratch_shapes=[
            pltpu.VMEM((tg + F32_SUBLANES, 3 * GDN_WIDTH), F32),
            pltpu.VMEM((LANES, GDN_WIDTH), F32),
        ],
        compiler_params=_cparams(("arbitrary", "arbitrary")),
        name="gdn",
    )(gqkv, gqkv, sm, gz, conv_w, alog128, dt128, nw512, ebd, jnp.asarray(eg), jnp.asarray(eb, dtype=BF16),
      jnp.asarray(lblk))


def _mix_xattn_kernel(x_ref, a_ref, b_ref, wa_ref, wb_ref, nw_ref, wq_ref, kv_ref, wo_ref, o_ref):
    mix = jnp.dot(a_ref[...].astype(BF16), wa_ref[...], preferred_element_type=F32)
    mix = mix + jnp.dot(b_ref[...].astype(BF16), wb_ref[...], preferred_element_type=F32)
    _xattn_rows(x_ref[...] + mix, nw_ref, wq_ref, kv_ref, wo_ref, o_ref)


def _xattn_kernel(x_ref, nw_ref, wq_ref, kv_ref, wo_ref, o_ref):
    _xattn_rows(x_ref[...], nw_ref, wq_ref, kv_ref, wo_ref, o_ref)


def _xattn_rows(x, nw_ref, wq_ref, kv_ref, wo_ref, o_ref):
    hn = _rms(x, nw_ref[...]).astype(BF16)
    q = jnp.dot(hn, wq_ref[...], preferred_element_type=F32) * (XA_HD ** -0.5)
    heads = range(XA_HEADS)
    ss = [lax.dot_general(q[:, XA_HD * h:XA_HD * (h + 1)].astype(BF16), kv_ref[:, XA_HD * h:XA_HD * (h + 1)],
                          NT_DIMS, preferred_element_type=F32) for h in heads]
    es = [jnp.exp(ss[h] - jnp.max(ss[h], axis=-1, keepdims=True)).astype(BF16) for h in heads]
    ones = jnp.ones((kv_ref.shape[0], XA_HD), BF16)
    pvs = [jnp.dot(es[h], jnp.concatenate([kv_ref[:, XA_WIDTH + XA_HD * h:XA_WIDTH + XA_HD * (h + 1)], ones], axis=1),
                   preferred_element_type=F32) for h in heads]
    o = jnp.concatenate([pvs[h][:, :XA_HD] / pvs[h][:, XA_HD:XA_HD + 1] for h in heads], axis=1).astype(BF16)
    o_ref[...] = x + jnp.dot(o, wo_ref[...], preferred_element_type=F32)


def _xattn(x2d, norm_w, wq_bf, memkv_l, wo_bf, S, tm=512):
    rows = x2d.shape[0]
    per_b = S // tm
    mem_len = memkv_l.shape[0] // (rows // S)
    return pl.pallas_call(
        _xattn_kernel,
        grid=(rows // tm,),
        in_specs=[
            pl.BlockSpec((tm, D_MODEL), lambda i: (i, 0)),
            pl.BlockSpec((1, D_MODEL), lambda i: (0, 0)),
            pl.BlockSpec((D_MODEL, XA_WIDTH), lambda i: (0, 0)),
            pl.BlockSpec((mem_len, 2 * XA_WIDTH), lambda i: (i // per_b, 0)),
            pl.BlockSpec((XA_WIDTH, D_MODEL), lambda i: (0, 0)),
        ],
        out_specs=pl.BlockSpec((tm, D_MODEL), lambda i: (i, 0)),
        out_shape=jax.ShapeDtypeStruct((rows, D_MODEL), F32),
        compiler_params=_cparams(("arbitrary",)),
        name="cross_attention",
    )(x2d, norm_w.reshape(1, D_MODEL), wq_bf, memkv_l, wo_bf)


def _mix_xattn(x2d, o_nsa, o_gdn, w_out_bf, norm_w, wq_bf, memkv_l, wo_bf, S, tm=512):
    rows = x2d.shape[0]
    per_b = S // tm
    mem_len = memkv_l.shape[0] // (rows // S)
    return pl.pallas_call(
        _mix_xattn_kernel,
        grid=(rows // tm,),
        in_specs=[
            pl.BlockSpec((tm, D_MODEL), lambda i: (i, 0)),
            pl.BlockSpec((tm, NSA_WIDTH), lambda i: (i, 0)),
            pl.BlockSpec((tm, GDN_WIDTH), lambda i: (i, 0)),
            pl.BlockSpec((NSA_WIDTH, D_MODEL), lambda i: (0, 0)),
            pl.BlockSpec((GDN_WIDTH, D_MODEL), lambda i: (1, 0)),
            pl.BlockSpec((1, D_MODEL), lambda i: (0, 0)),
            pl.BlockSpec((D_MODEL, XA_WIDTH), lambda i: (0, 0)),
            pl.BlockSpec((mem_len, 2 * XA_WIDTH), lambda i: (i // per_b, 0)),
            pl.BlockSpec((XA_WIDTH, D_MODEL), lambda i: (0, 0)),
        ],
        out_specs=pl.BlockSpec((tm, D_MODEL), lambda i: (i, 0)),
        out_shape=jax.ShapeDtypeStruct((rows, D_MODEL), F32),
        compiler_params=_cparams(("arbitrary",)),
        name="mix_cross_attention",
    )(x2d, o_nsa, o_gdn, w_out_bf, w_out_bf, norm_w.reshape(1, D_MODEL), wq_bf, memkv_l, wo_bf)


def _mlp_kernel(x_ref, nw_ref, w1_ref, w2_ref, fw_ref, o_ref, hn_scr, acc_scr, *, final_norm):
    j = pl.program_id(1)

    @pl.when(j == 0)
    def _():
        x = x_ref[...]
        hn_scr[...] = _rms(x, nw_ref[...]).astype(BF16)
        acc_scr[...] = x

    h = jnp.dot(hn_scr[...], w1_ref[...], preferred_element_type=F32)
    h = jnp.square(jnp.maximum(h, 0.0)).astype(BF16)
    acc_scr[...] += jnp.dot(h, w2_ref[...], preferred_element_type=F32)

    @pl.when(j == pl.num_programs(1) - 1)
    def _():
        y = acc_scr[...]
        if final_norm:
            y = _rms(y, fw_ref[...])
        o_ref[...] = y


def _mlp(x2d, norm_w, w1_bf, w2_bf, final_w, final_norm, tm=1024, tf=1024):
    rows = x2d.shape[0]
    tm = min(tm, rows)
    return pl.pallas_call(
        functools.partial(_mlp_kernel, final_norm=final_norm),
        grid=(rows // tm, D_FF // tf),
        in_specs=[
            pl.BlockSpec((tm, D_MODEL), lambda i, j: (i, 0)),
            pl.BlockSpec((1, D_MODEL), lambda i, j: (0, 0)),
            pl.BlockSpec((D_MODEL, tf), lambda i, j: (0, j)),
            pl.BlockSpec((tf, D_MODEL), lambda i, j: (j, 0)),
            pl.BlockSpec((1, D_MODEL), lambda i, j: (0, 0)),
        ],
        out_specs=pl.BlockSpec((tm, D_MODEL), lambda i, j: (i, 0)),
        out_shape=jax.ShapeDtypeStruct((rows, D_MODEL), F32),
        scratch_shapes=[pltpu.VMEM((tm, D_MODEL), BF16), pltpu.VMEM((tm, D_MODEL), F32)],
        compiler_params=_cparams(("arbitrary", "arbitrary")),
        name="mlp",
    )(x2d, norm_w.reshape(1, D_MODEL), w1_bf, w2_bf, final_w.reshape(1, D_MODEL))


def _sc_kernel(x_ref, nw_ref, win_ref, cw_ref, wout_ref, o_ref, cu_scr, *, tm, per_b):
    i = pl.program_id(0)
    halo = F32_SUBLANES

    @pl.when(i % per_b == 0)
    def _():
        cu_scr[tm:tm + halo, :] = jnp.zeros((halo, D_MODEL), F32)

    x = x_ref[...]
    y = jnp.dot(_rms(x, nw_ref[...]).astype(BF16), win_ref[...], preferred_element_type=F32)
    cu_scr[0:halo, :] = cu_scr[tm:tm + halo, :]
    cu_scr[halo:halo + tm, :] = y[:, D_MODEL:2 * D_MODEL] * y[:, 2 * D_MODEL:]
    base = halo - (SC_WIDTH - 1)
    conv = cw_ref[0:1, :] * cu_scr[base:base + tm, :]
    for j in range(1, SC_WIDTH):
        conv = conv + cw_ref[j:j + 1, :] * cu_scr[base + j:base + j + tm, :]
    gated = (y[:, :D_MODEL] * conv).astype(BF16)
    o_ref[...] = x + jnp.dot(gated, wout_ref[...], preferred_element_type=F32)


def _sc_mixer(x2d, norm_w, win_bf, conv_w, wout_bf, S, tm=512):
    rows = x2d.shape[0]
    per_b = S // tm
    return pl.pallas_call(
        functools.partial(_sc_kernel, tm=tm, per_b=per_b),
        grid=(rows // tm,),
        in_specs=[
            pl.BlockSpec((tm, D_MODEL), lambda i: (i, 0)),
            pl.BlockSpec((1, D_MODEL), lambda i: (0, 0)),
            pl.BlockSpec((D_MODEL, 3 * D_MODEL), lambda i: (0, 0)),
            pl.BlockSpec((SC_WIDTH, D_MODEL), lambda i: (0, 0)),
            pl.BlockSpec((D_MODEL, D_MODEL), lambda i: (0, 0)),
        ],
        out_specs=pl.BlockSpec((tm, D_MODEL), lambda i: (i, 0)),
        out_shape=jax.ShapeDtypeStruct((rows, D_MODEL), F32),
        scratch_shapes=[pltpu.VMEM((tm + F32_SUBLANES, D_MODEL), F32)],
        compiler_params=_cparams(("arbitrary",)),
        name="short_conv_mixer",
    )(x2d, norm_w.reshape(1, D_MODEL), win_bf, conv_w, wout_bf)


def _hybrid_mixer(x2d, positions, norm_w, w_in, ck_pos, ck_w1, ck_w2, cv_pos, cv_w1, cv_w2,
                  gdn_conv, gdn_a_log, gdn_dt_bias, gdn_norm, B, S):
    tm = 512
    invf = (ROPE_THETA ** (-jnp.arange(0, HD, 2, dtype=F32) / HD)).reshape(HD // 2, 1)
    pos3 = positions.astype(F32).reshape(B * S // tm, 1, tm)
    qT, kr, vT, ckv, gqkv, gz, sm, gT = _proj0(x2d, norm_w, _prep_w_in(w_in), pos3, invf, B, S, tm)
    w1s, p8, w2s = _prep_compress_weights(ck_pos, ck_w1, ck_w2, cv_pos, cv_w1, cv_w2)
    kc, vcT = _compress(ckv, w1s, p8, w2s, B, S)
    o_nsa = _nsa(qT, kr, vT, kc, vcT, gT, B, S)
    o_gdn = _gdn(gqkv, sm, gz, gdn_conv, gdn_a_log, gdn_dt_bias, gdn_norm, B, S)
    return o_nsa, o_gdn


def kernel(x, mem, positions, norm_mix, norm_xattn, norm_mlp, hyb_w_in, hyb_cmp_k_pos, hyb_cmp_k_w1, hyb_cmp_k_w2, hyb_cmp_v_pos, hyb_cmp_v_w1, hyb_cmp_v_w2, hyb_gdn_conv, hyb_gdn_a_log, hyb_gdn_dt_bias, hyb_gdn_norm, hyb_w_out, sc_w_in, sc_conv, sc_w_out, mem_norm, xa_wq, xa_wkv, xa_wo, mlp_w1, mlp_w2, final_norm):
    B, S, _ = x.shape
    depth = norm_mix.shape[0]
    x2d = x.reshape(B * S, D_MODEL)
    memkv = _memkv(mem.reshape(-1, D_MODEL), mem_norm, xa_wkv.astype(BF16))
    for layer in range(depth):
        j = layer // 2
        xa = (norm_xattn[layer], xa_wq[layer].astype(BF16), memkv[layer], xa_wo[layer].astype(BF16), S)
        if layer % 2 == 0:
            o_nsa, o_gdn = _hybrid_mixer(x2d, positions, norm_mix[layer], hyb_w_in[j], hyb_cmp_k_pos[j],
                                         hyb_cmp_k_w1[j], hyb_cmp_k_w2[j], hyb_cmp_v_pos[j], hyb_cmp_v_w1[j],
                                         hyb_cmp_v_w2[j], hyb_gdn_conv[j], hyb_gdn_a_log[j], hyb_gdn_dt_bias[j],
                                         hyb_gdn_norm[j], B, S)
            x2d = _mix_xattn(x2d, o_nsa, o_gdn, hyb_w_out[j].astype(BF16), *xa)
        else:
            x2d = _sc_mixer(x2d, norm_mix[layer], sc_w_in[j].astype(BF16), sc_conv[j], sc_w_out[j].astype(BF16), S)
            x2d = _xattn(x2d, *xa)
        x2d = _mlp(x2d, norm_mlp[layer], mlp_w1[layer].astype(BF16), mlp_w2[layer].astype(BF16),
                   final_norm, layer == depth - 1)
    return x2d.reshape(B, S, D_MODEL)
```

```python
import functools

import numpy as np
import jax
import jax.numpy as jnp
from jax import lax
from jax.experimental import pallas as pl
from jax.experimental.pallas import tpu as pltpu

F32 = jnp.float32
BF16 = jnp.bfloat16
HI = lax.Precision.HIGHEST

D_MODEL = 1024
MEM_LEN = 256
RMS_EPS = 1e-6
ROPE_THETA = 10000.0
NEG_INF = -1e30
FORCE_BONUS = 1e3
LOG2E = 1.4426950408889634

NSA_HEADS = 8
NSA_KV_GROUPS = 2
NSA_HPG = NSA_HEADS // NSA_KV_GROUPS
HD = 64
CMP_LEN = 32
CMP_STRIDE = 16
CMP_HID = 2 * HD
SLC_BLK = 64
SLC_TOPK = 16
WINDOW = 512

GDN_HEADS = 8
GDN_CONV = 4
GDN_CHUNK = 64
SC_WIDTH = 3
XA_HEADS = 4
XA_HD = 128
D_FF = 4 * D_MODEL

NSA_WIDTH = NSA_HEADS * HD
NSA_KV_WIDTH = NSA_KV_GROUPS * HD
GDN_WIDTH = GDN_HEADS * HD
XA_WIDTH = XA_HEADS * XA_HD
IN_SIZES = (NSA_WIDTH,) + (NSA_KV_WIDTH,) * 6 + (3 * NSA_HEADS,) + (GDN_WIDTH,) * 3 + (GDN_HEADS, GDN_HEADS, GDN_WIDTH)

V7X_VMEM_BYTES = 64 * 1024 * 1024
VMEM_LIMIT = V7X_VMEM_BYTES * 3 // 4
LANES = 128
F32_SUBLANES = 8
BF16_ROWS = 16

C_Q, C_KR, C_VT, C_CKV, C_GQKV, C_GZ, C_SM, NC0 = 0, 512, 768, 1024, 1280, 2816, 3328, 3456
SM_GATE, SM_A, SM_B = 0, 24, 32
NSA_CHUNK = 256
KR_WIDTH = 3 * LANES
KR_WIN = 2 * LANES

NT_DIMS = (((1,), (1,)), ((), ()))
TN_DIMS = (((0,), (0,)), ((), ()))


def _cparams(sem):
    return pltpu.CompilerParams(dimension_semantics=sem, vmem_limit_bytes=VMEM_LIMIT)


def _rms(x, w):
    return x * lax.rsqrt(jnp.mean(x * x, axis=-1, keepdims=True) + RMS_EPS) * w


def _softplus(x):
    return jnp.maximum(x, 0.0) + jnp.log(1.0 + jnp.exp(-jnp.abs(x)))


def _memkv_kernel(m_ref, nw_ref, w_ref, o_ref):
    hn = _rms(m_ref[...], nw_ref[...]).astype(BF16)
    o_ref[...] = jnp.dot(hn, w_ref[...], preferred_element_type=F32).astype(BF16)


def _memkv(mem2d, mem_norm, wkv_bf):
    rows = mem2d.shape[0]
    depth = wkv_bf.shape[0]
    tm = 512
    return pl.pallas_call(
        _memkv_kernel,
        grid=(depth, rows // tm),
        in_specs=[
            pl.BlockSpec((tm, D_MODEL), lambda l, i: (i, 0)),
            pl.BlockSpec((1, D_MODEL), lambda l, i: (0, 0)),
            pl.BlockSpec((None, D_MODEL, 2 * XA_WIDTH), lambda l, i: (l, 0, 0)),
        ],
        out_specs=pl.BlockSpec((None, tm, 2 * XA_WIDTH), lambda l, i: (l, i, 0)),
        out_shape=jax.ShapeDtypeStruct((depth, rows, 2 * XA_WIDTH), BF16),
        compiler_params=_cparams(("arbitrary", "arbitrary")),
        name="memkv",
    )(mem2d, mem_norm.reshape(1, D_MODEL), wkv_bf)


def _proj0_kernel(x_ref, nw_ref, w_ref, pos_ref, invf_ref,
                  qT_ref, kr_ref, vT_ref, ckv_ref, gqkv_ref, gz_ref, sm_ref, gT_ref):
    tm = x_ref.shape[0]
    hn = _rms(x_ref[...], nw_ref[...]).astype(BF16)
    y = jnp.dot(hn, w_ref[...], preferred_element_type=F32)

    ang = invf_ref[...] * pos_ref[...]
    c = jnp.cos(ang)
    s = jnp.sin(ang)
    cos_n = jnp.concatenate([c, c, c, c], axis=0).T
    sin_n = jnp.concatenate([-s, s, -s, s], axis=0).T
    lane = lax.broadcasted_iota(jnp.int32, (tm, LANES), 1)
    first_half = (lane % HD) < (HD // 2)

    def rope(xc):
        rot = jnp.where(first_half, pltpu.roll(xc, LANES - HD // 2, 1), pltpu.roll(xc, HD // 2, 1))
        return xc * cos_n + rot * sin_n

    roped = [rope(y[:, LANES * i:LANES * (i + 1)]) for i in range(6)]
    q = jnp.concatenate(roped[:4], axis=1) * (LOG2E * HD ** -0.5)
    qT_ref[...] = q.T.astype(BF16)
    row = lax.broadcasted_iota(jnp.int32, (tm, LANES), 0)
    blk_hot = jnp.where(lane - HD == (row // SLC_BLK) % (NSA_CHUNK // SLC_BLK), 1.0, 0.0)
    k_slc = roped[4]
    lo = lane < HD
    kr_ref[...] = jnp.concatenate([jnp.where(lo, k_slc, blk_hot),
                                   jnp.where(lo, pltpu.roll(k_slc, HD, 1), blk_hot),
                                   roped[5]], axis=1).astype(BF16)
    vT_ref[...] = y[:, C_VT:C_CKV].T.astype(BF16)
    for i in range(4):
        ckv_ref[i] = y[:, C_CKV + HD * i:C_CKV + HD * (i + 1)].astype(BF16)
    gqkv_ref[...] = y[:, C_GQKV:C_GZ]
    gz_ref[...] = y[:, C_GZ:C_SM]
    sm = y[:, C_SM:NC0]
    sm_ref[...] = sm
    gT_ref[...] = jax.nn.sigmoid(sm).T[:3 * NSA_HEADS, :]


def _prep_w_in(w_in):
    offs = np.cumsum((0,) + IN_SIZES)
    (nq, kcmp, vcmp, kslc, vslc, kwin, vwin, ngate, gq, gk, gv, ga, gb, gz) = [
        w_in[:, offs[i]:offs[i + 1]] for i in range(len(IN_SIZES))]
    pad = jnp.zeros((D_MODEL, LANES - 3 * NSA_HEADS - 2 * GDN_HEADS), w_in.dtype)
    small = jnp.concatenate([ngate, ga, gb, pad], axis=1)
    return jnp.concatenate([nq, kslc, kwin, vslc, vwin, kcmp, vcmp, gq, gk, gv, gz, small], axis=1).astype(BF16)


def _proj0(x2d, norm_w, w_bf, pos3, invf, B, S, tm):
    rows = B * S
    nt = S // tm
    row_spec = lambda n: pl.BlockSpec((tm, n), lambda i: (i, 0))
    t_spec = lambda n: pl.BlockSpec((None, n, tm), lambda i: (i // nt, 0, i % nt))
    return pl.pallas_call(
        _proj0_kernel,
        grid=(rows // tm,),
        in_specs=[
            row_spec(D_MODEL),
            pl.BlockSpec((1, D_MODEL), lambda i: (0, 0)),
            pl.BlockSpec((D_MODEL, NC0), lambda i: (0, 0)),
            pl.BlockSpec((None, 1, tm), lambda i: (i, 0, 0)),
            pl.BlockSpec((HD // 2, 1), lambda i: (0, 0)),
        ],
        out_specs=[
            t_spec(NSA_WIDTH),
            row_spec(KR_WIDTH),
            t_spec(2 * NSA_KV_WIDTH),
            pl.BlockSpec((4, tm, HD), lambda i: (0, i, 0)),
            row_spec(3 * GDN_WIDTH),
            row_spec(GDN_WIDTH),
            row_spec(LANES),
            t_spec(3 * NSA_HEADS),
        ],
        out_shape=[
            jax.ShapeDtypeStruct((B, NSA_WIDTH, S), BF16),
            jax.ShapeDtypeStruct((rows, KR_WIDTH), BF16),
            jax.ShapeDtypeStruct((B, 2 * NSA_KV_WIDTH, S), BF16),
            jax.ShapeDtypeStruct((4, rows, HD), BF16),
            jax.ShapeDtypeStruct((rows, 3 * GDN_WIDTH), F32),
            jax.ShapeDtypeStruct((rows, GDN_WIDTH), F32),
            jax.ShapeDtypeStruct((rows, LANES), F32),
            jax.ShapeDtypeStruct((B, 3 * NSA_HEADS, S), F32),
        ],
        compiler_params=_cparams(("arbitrary",)),
        name="proj0",
    )(x2d, norm_w.reshape(1, D_MODEL), w_bf, pos3, invf)


def _compress_kernel(t_ref, w1_ref, pos_ref, w2_ref, o_ref, oT_ref):
    nseg = t_ref.shape[0]
    w1 = w1_ref[...]
    ab = jnp.dot(t_ref[...], w1, preferred_element_type=F32)
    pc = jnp.dot(pos_ref[...].astype(BF16), w1, preferred_element_type=F32)
    bias = pc[0:1, :CMP_HID] + pc[1:2, CMP_HID:]
    hid = ab[:, :CMP_HID] + pltpu.roll(ab[:, CMP_HID:], nseg - 1, 0) + bias
    act = hid * jax.nn.sigmoid(hid)
    out = jnp.dot(act.astype(BF16), w2_ref[...], preferred_element_type=F32)
    o_ref[...] = out
    oT_ref[...] = out.T[:HD, :]


def _compress(ckv, w1s, pos8, w2s, B, S):
    nseg = S // CMP_STRIDE
    t = ckv.reshape(4, B, nseg, CMP_STRIDE * HD)
    return pl.pallas_call(
        _compress_kernel,
        grid=(4, B),
        in_specs=[
            pl.BlockSpec((None, None, nseg, CMP_STRIDE * HD), lambda k, b: (k, b, 0, 0)),
            pl.BlockSpec((None, CMP_STRIDE * HD, 2 * CMP_HID), lambda k, b: (k // 2, 0, 0)),
            pl.BlockSpec((None, F32_SUBLANES, CMP_STRIDE * HD), lambda k, b: (k // 2, 0, 0)),
            pl.BlockSpec((None, CMP_HID, LANES), lambda k, b: (k // 2, 0, 0)),
        ],
        out_specs=[
            pl.BlockSpec((None, None, nseg, LANES), lambda k, b: (k, b, 0, 0)),
            pl.BlockSpec((None, None, HD, nseg), lambda k, b: (k, b, 0, 0)),
        ],
        out_shape=[
            jax.ShapeDtypeStruct((4, B, nseg, LANES), F32),
            jax.ShapeDtypeStruct((4, B, HD, nseg), F32),
        ],
        compiler_params=_cparams(("arbitrary", "arbitrary")),
        name="nsa_compress",
    )(t, w1s, pos8, w2s)


def _prep_compress_weights(k_pos, k_w1, k_w2, v_pos, v_w1, v_w2):
    half = CMP_STRIDE * HD

    def w1cat(w1):
        return jnp.concatenate([w1[:half], w1[half:]], axis=1)

    def pos8(p):
        flat = p.reshape(2, half)
        return jnp.concatenate([flat, jnp.zeros((F32_SUBLANES - 2, half), p.dtype)], axis=0)

    def w2pad(w2):
        return jnp.concatenate([w2, jnp.zeros((CMP_HID, LANES - HD), w2.dtype)], axis=1)

    w1s = jnp.stack([w1cat(k_w1), w1cat(v_w1)]).astype(BF16)
    p8 = jnp.stack([pos8(k_pos), pos8(v_pos)])
    w2s = jnp.stack([w2pad(k_w2), w2pad(v_w2)]).astype(BF16)
    return w1s, p8, w2s


def _nsa_kernel(qT_ref, kr_ref, vT_ref, kc_ref, vcT_ref, gT_ref, o_ref,
                imp_scr, sel_scr, s_scr, p_scr, b_scr, ps_scr, *, S, TQ):
    CH = TQ
    groups = range(NSA_KV_GROUPS)
    n_cmp = S // CMP_STRIDE
    n_slc = S // SLC_BLK
    n_sel = min(SLC_TOPK, n_slc)
    qi = pl.program_id(1)
    q0 = qi * TQ
    t1 = q0 + lax.broadcasted_iota(jnp.int32, (1, TQ), 1)
    t4 = jnp.concatenate([t1] * NSA_HPG, axis=1)
    n_io = lax.broadcasted_iota(jnp.int32, (n_cmp, 1), 0)
    j_io = lax.broadcasted_iota(jnp.int32, (n_slc, 1), 0)
    k_io = lax.broadcasted_iota(jnp.int32, (CH, 1), 0)
    n_causal = (q0 + TQ) // SLC_BLK

    qs, q64s = [], []
    for g in groups:
        q64 = jnp.concatenate(
            [qT_ref[NSA_HPG * HD * g + HD * h:NSA_HPG * HD * g + HD * (h + 1), :] for h in range(NSA_HPG)], axis=1)
        zq = jnp.zeros_like(q64)
        qs.append(jnp.concatenate([q64, zq] if g == 0 else [zq, q64], axis=0))
        q64s.append(q64)

    n_lanes = NSA_HPG * TQ

    def probs(sc, bias, m_i):
        if bias is not None:
            sc = sc + jnp.concatenate([bias] * NSA_HPG, axis=1)
        m_new = jnp.maximum(m_i, jnp.max(sc, axis=0, keepdims=True))
        return m_new, jnp.exp2(m_i - m_new), jnp.exp2(sc - m_new).astype(BF16)

    def pv(vt, pr):
        return jnp.dot(vt, pr, preferred_element_type=F32)

    def m_init():
        return jnp.full((1, n_lanes), NEG_INF, F32)

    u_io = lax.broadcasted_iota(jnp.int32, (1, TQ), 1)
    causal_bias = jnp.where(k_io <= u_io, 0.0, NEG_INF)
    window_bias = jnp.where(k_io > u_io, 0.0, NEG_INF)

    s_cmp = [jnp.dot(kc_ref[g, :, 0:HD].astype(BF16), q64s[g], preferred_element_type=F32)
             for g in groups]
    n_back = WINDOW // CH
    n_win = n_back + 1
    w0 = jnp.maximum(qi - n_back, 0)
    rw = pl.multiple_of(w0 * CH, CH)
    k_win = kr_ref[pl.ds(rw, n_win * CH), KR_WIN:KR_WIN + LANES]
    s_win = [jnp.dot(k_win, qs[g], preferred_element_type=F32) for g in groups]
    valid = (n_io * CMP_STRIDE + (CMP_LEN - 1)) <= t4
    any_valid = t4 >= CMP_LEN - 1
    p_cmp, psum = [], []
    for g in groups:
        s = jnp.where(valid, s_cmp[g], NEG_INF)
        m = jnp.max(s, axis=0, keepdims=True)
        e = jnp.exp2(s - m)
        l = jnp.sum(e, axis=0, keepdims=True)
        p = e * jnp.where(any_valid, 1.0 / l, 0.0)
        p_cmp.append(p.astype(BF16))
        ph = p[:, 0:TQ]
        for h in range(1, NSA_HPG):
            ph = ph + p[:, h * TQ:(h + 1) * TQ]
        psum.append(ph)
    o_cmp = [jnp.dot(vcT_ref[g].astype(BF16), p_cmp[g], preferred_element_type=F32) for g in groups]
    bpb = SLC_BLK // CMP_STRIDE
    imps = []
    for g in groups:
        tiles = []
        for lt in range(TQ // LANES):
            ps_scr[g, lt, 0:F32_SUBLANES, :] = jnp.zeros((F32_SUBLANES, LANES), F32)
            ps_scr[g, lt, F32_SUBLANES:F32_SUBLANES + n_cmp, :] = psum[g][:, LANES * lt:LANES * (lt + 1)]
            imp = None
            for k in range(-(CMP_LEN // CMP_STRIDE - 1), bpb):
                lo_, hi_ = max(CMP_STRIDE * k, 0), min(CMP_STRIDE * k + CMP_LEN, SLC_BLK)
                term = (hi_ - lo_) / CMP_LEN * ps_scr[g, lt, pl.ds(F32_SUBLANES + k, n_slc, stride=bpb), :]
                imp = term if imp is None else imp + term
            tiles.append(imp)
        imps.append(jnp.concatenate(tiles, axis=1))
    hidden = jnp.full((CH, TQ), NEG_INF, F32)
    slab_bias = []
    for c in range(n_win):
        d = w0 + c - qi
        slab_bias.append(jnp.where(d == 0, causal_bias,
                                   jnp.where(d > 0, hidden, jnp.where(d == -n_back, window_bias, 0.0))))
    win_bias = jnp.concatenate(slab_bias, axis=0)
    win_p = [probs(s_win[g], win_bias, m_init())[2] for g in groups]
    o_win = []
    for g in groups:
        v_win = jnp.concatenate([vT_ref[NSA_KV_WIDTH + HD * g:NSA_KV_WIDTH + HD * (g + 1), pl.ds(rw, n_win * CH)],
                                 jnp.ones((BF16_ROWS, n_win * CH), BF16)], axis=0)
        acc = pv(v_win, win_p[g])
        o_win.append(acc[:HD] / acc[HD:HD + 1])
    cur = t1 // SLC_BLK
    forced = (j_io == 0) | (j_io == cur) | (j_io == cur - 1)
    causal = j_io <= cur
    vals = [jnp.where(causal, jnp.where(forced, imps[g] + FORCE_BONUS, imps[g]), -1.0) for g in groups]
    for g in groups:
        imp_scr[g] = vals[g]

    per_trip = TQ // SLC_BLK

    def rank_body(it, cnts):
        out = list(cnts)
        for i in [per_trip * it + r for r in range(per_trip)]:
            tie = jnp.where(j_io > i, 1.0, 0.0)
            for g in groups:
                vi = imp_scr[g, pl.ds(i, 1), :]
                out[g] = out[g] + jnp.where(vi > vals[g], 1.0, jnp.where(vi == vals[g], tie, 0.0))
        return tuple(out)

    cnts = lax.fori_loop(0, jnp.where(n_causal > n_sel, n_causal // per_trip, 0), rank_body,
                         tuple(jnp.zeros((n_slc, TQ), F32) for _ in groups))
    nb = CH // SLC_BLK
    for g in groups:
        selb = jnp.where(cnts[g] < float(n_sel), 0.0, NEG_INF)
        for jc in range(S // CH):
            rows = jnp.concatenate([selb[nb * jc:nb * (jc + 1), :], jnp.zeros((BF16_ROWS - nb, TQ), F32)], axis=0)
            sel_scr[g, jc] = jnp.concatenate([rows] * NSA_HPG, axis=1).astype(BF16)

    ones_rows = jnp.ones((BF16_ROWS, CH), BF16)
    zero_rows = jnp.zeros((HD - BF16_ROWS, n_lanes), BF16)
    acc_rows = HD + BF16_ROWS

    def slc_scores(jc):
        r0 = pl.multiple_of(jc * CH, CH)
        return [jnp.dot(kr_ref[pl.ds(r0, CH), LANES * g:LANES * (g + 1)],
                        jnp.concatenate([q64s[g], sel_scr[g, jc], zero_rows], axis=0),
                        preferred_element_type=F32) for g in groups]

    def values(jc, vrow):
        r0 = pl.multiple_of(jc * CH, CH)
        return [jnp.concatenate([vT_ref[vrow + HD * g:vrow + HD * (g + 1), pl.ds(r0, CH)], ones_rows], axis=0)
                for g in groups]

    def acc_init():
        return jnp.zeros((acc_rows, n_lanes), F32)

    def slc_half(jc, slot, ms):
        nxt = slc_scores(jc + 1)
        v_prev = values(jnp.maximum(jc - 1, 0), 0)
        pvs = [pv(v_prev[g], p_scr[g]) for g in groups]
        for g in groups:
            s_scr[1 - slot, g] = nxt[g]
        out = []
        for g in groups:
            m_new, alpha, pr = probs(s_scr[slot, g], None, ms[g])
            b_scr[g] = (b_scr[g] + pvs[g]) * alpha
            p_scr[g] = pr
            out.append(m_new)
        return tuple(out)

    first = slc_scores(0)
    for g in groups:
        s_scr[0, g] = first[g]
        p_scr[g] = jnp.zeros((CH, n_lanes), BF16)
        b_scr[g] = acc_init()
    ms = lax.fori_loop(0, qi // 2, lambda i, m: slc_half(2 * i + 1, 1, slc_half(2 * i, 0, m)),
                       tuple(m_init() for _ in groups))

    ms = lax.fori_loop(0, qi % 2, lambda _, m: slc_half(qi - 1, 0, m), ms)
    diag_slot = qi % 2

    v_prev = values(jnp.maximum(qi - 1, 0), 0)
    pvs = [pv(v_prev[g], p_scr[g]) for g in groups]
    slc_parts = [probs(s_scr[diag_slot, g], causal_bias, ms[g]) for g in groups]
    v_slc = values(qi, 0)
    o_slc = []
    for g in groups:
        acc = (b_scr[g] + pvs[g]) * slc_parts[g][1] + pv(v_slc[g], slc_parts[g][2])
        o_slc.append(acc[:HD] / acc[HD:HD + 1])

    for g in groups:
        rows = []
        for h in range(NSA_HPG):
            hh = NSA_HPG * g + h
            sl = slice(h * TQ, (h + 1) * TQ)
            rows.append(gT_ref[3 * hh:3 * hh + 1, :] * o_cmp[g][:, sl]
                        + gT_ref[3 * hh + 1:3 * hh + 2, :] * o_slc[g][:, sl]
                        + gT_ref[3 * hh + 2:3 * hh + 3, :] * o_win[g][:, sl])
        og = jnp.concatenate(rows, axis=0)
        o_ref[:, NSA_HPG * HD * g:NSA_HPG * HD * (g + 1)] = og.T


def _nsa(qT, kr, vT, kc, vcT, gT, B, S, TQ=NSA_CHUNK):
    n_cmp = S // CMP_STRIDE
    n_slc = S // SLC_BLK
    nq = S // TQ
    return pl.pallas_call(
        functools.partial(_nsa_kernel, S=S, TQ=TQ),
        grid=(B, nq),
        in_specs=[
            pl.BlockSpec((None, NSA_WIDTH, TQ), lambda b, i: (b, 0, i)),
            pl.BlockSpec((None, S, KR_WIDTH), lambda b, i: (b, 0, 0)),
            pl.BlockSpec((None, 2 * NSA_KV_WIDTH, S), lambda b, i: (b, 0, 0)),
            pl.BlockSpec((2, None, n_cmp, LANES), lambda b, i: (0, b, 0, 0)),
            pl.BlockSpec((2, None, HD, n_cmp), lambda b, i: (1, b, 0, 0)),
            pl.BlockSpec((None, 3 * NSA_HEADS, TQ), lambda b, i: (b, 0, i)),
        ],
        out_specs=pl.BlockSpec((TQ, NSA_WIDTH), lambda b, i: (b * nq + i, 0)),
        out_shape=jax.ShapeDtypeStruct((B * S, NSA_WIDTH), F32),
        scratch_shapes=[
            pltpu.VMEM((NSA_KV_GROUPS, n_slc, TQ), F32),
            pltpu.VMEM((NSA_KV_GROUPS, S // TQ, BF16_ROWS, NSA_HPG * TQ), BF16),
            pltpu.VMEM((2, NSA_KV_GROUPS, TQ, NSA_HPG * TQ), F32),
            pltpu.VMEM((NSA_KV_GROUPS, TQ, NSA_HPG * TQ), BF16),
            pltpu.VMEM((NSA_KV_GROUPS, HD + BF16_ROWS, NSA_HPG * TQ), F32),
            pltpu.VMEM((NSA_KV_GROUPS, TQ // LANES, F32_SUBLANES + n_cmp, LANES), F32),
        ],
        compiler_params=_cparams(("arbitrary", "arbitrary")),
        name="nsa_attention",
    )(qT, kr.reshape(B, S, KR_WIDTH), vT, kc, vcT, gT)


def _gdn_kernel(x_ref, xp_ref, sm_ref, z_ref, cw_ref, alog_ref, dt_ref, nw_ref,
                ebd_ref, eg_ref, eb_ref, lblk_ref, o_ref,
                xe_scr, st_scr, *, tg):
    C = GDN_CHUNK
    n_ch = tg // C
    n_pr = GDN_HEADS // 2
    i = pl.program_id(1)

    @pl.when(i == 0)
    def _():
        st_scr[...] = jnp.zeros_like(st_scr)

    def decay_terms():
        sm = sm_ref[...]
        g128 = -jnp.exp(alog_ref[...]) * _softplus(sm + dt_ref[...])
        gcs128 = jnp.dot(lblk_ref[...], g128, precision=HI, preferred_element_type=F32)
        g_exp = jnp.dot(gcs128, eg_ref[...], precision=HI, preferred_element_type=F32)
        b_exp = jnp.dot(jax.nn.sigmoid(sm).astype(BF16), eb_ref[...], preferred_element_type=F32)
        return g_exp, b_exp, [gcs128[C * c:C * (c + 1), :].T[SM_A:SM_A + GDN_HEADS, :]
                              for c in range(n_ch)]

    halo = F32_SUBLANES
    xe_scr[0:halo, :] = jnp.where(i == 0, 0.0, xp_ref[...])
    xe_scr[halo:halo + tg, :] = x_ref[...]
    ebd = ebd_ref[...]
    g_all, b_all, gts = decay_terms()
    n_split = 2
    rh = tg // n_split

    def conv_half(hf):
        base = halo - (GDN_CONV - 1) + rh * hf
        y = cw_ref[0:1, :] * xe_scr[base:base + rh, :]
        for j in range(1, GDN_CONV):
            y = y + cw_ref[j:j + 1, :] * xe_scr[base + j:base + j + rh, :]
        qkv = y * jax.nn.sigmoid(y)
        q = qkv[:, 0:GDN_WIDTH]
        k = qkv[:, GDN_WIDTH:2 * GDN_WIDTH]
        ss = jnp.dot(jnp.concatenate([q * q, k * k], axis=0).astype(BF16), ebd, preferred_element_type=F32)
        return (q * lax.rsqrt(ss[:rh] + 1e-6) * (HD ** -0.5), k * lax.rsqrt(ss[rh:] + 1e-6),
                qkv[:, 2 * GDN_WIDTH:])

    halves = [conv_half(hf) for hf in range(n_split)]
    q_all, k_all, v_all = (jnp.concatenate([hv[n] for hv in halves], axis=0) for n in range(3))

    ii = lax.broadcasted_iota(jnp.int32, (C, LANES), 0)
    lane = lax.broadcasted_iota(jnp.int32, (C, LANES), 1)
    jj = lane % HD
    eye2 = jnp.where(ii == jj, 1.0, 0.0)
    lo_half = lane < HD
    lane2 = lax.broadcasted_iota(jnp.int32, (LANES, LANES), 1)
    row2 = lax.broadcasted_iota(jnp.int32, (LANES, LANES), 0)
    same_head = (lane2 < HD) == (row2 < HD)

    def bd(x):
        xb = x.astype(BF16)
        zero = jnp.zeros_like(xb)
        return jnp.concatenate([jnp.where(lo_half, xb, zero), jnp.where(lo_half, zero, xb)], axis=0)

    def mm(a, b):
        return jnp.dot(a.astype(BF16), b, preferred_element_type=F32)

    chains = [(c, pr) for c in range(n_ch) for pr in range(n_pr)]

    def blk(arr, c, pr):
        return arr[C * c:C * (c + 1), LANES * pr:LANES * (pr + 1)]

    qg, kdec, dec, aq, vb_bd, kbg_bd, eglast = {}, {}, {}, {}, {}, {}, {}
    for ch in chains:
        c, pr = ch
        qn, k_, gc, be = blk(q_all, c, pr), blk(k_all, c, pr), blk(g_all, c, pr), blk(b_all, c, pr)
        kb = k_ * be
        eg = jnp.exp(gc)
        glast = gc[C - 1:C, :]
        qg[ch] = qn * eg
        kdec[ch] = k_ * jnp.exp(glast - gc)
        eglast[ch] = jnp.exp(glast)
        vb_bd[ch] = bd(blk(v_all, c, pr) * be)
        kbg_bd[ch] = bd(kb * eg)
        gt = gts[c]
        gct = jnp.concatenate([jnp.broadcast_to(gt[2 * pr:2 * pr + 1, :], (C, HD)),
                               jnp.broadcast_to(gt[2 * pr + 1:2 * pr + 2, :], (C, HD))], axis=1)
        dec[ch] = jnp.exp(jnp.where(ii >= jj, gc - gct, NEG_INF))
        aq[ch] = lax.dot_general(jnp.concatenate([kb, qn], axis=0).astype(BF16), bd(k_), NT_DIMS,
                                 preferred_element_type=F32)
    a = {ch: jnp.where(ii > jj, aq[ch][:C] * dec[ch], 0.0) for ch in chains}
    qk = {ch: aq[ch][C:] * dec[ch] for ch in chains}
    x = {ch: eye2 - a[ch] for ch in chains}
    pw = {ch: mm(a[ch], bd(a[ch])) for ch in chains}
    for _ in range(4):
        both = {ch: mm(jnp.concatenate([x[ch], pw[ch]], axis=0), bd(pw[ch])) for ch in chains}
        x = {ch: x[ch] + both[ch][:C] for ch in chains}
        pw = {ch: both[ch][C:] for ch in chains}
    x = {ch: x[ch] + mm(x[ch], bd(pw[ch])) for ch in chains}
    uw = {ch: mm(x[ch], jnp.concatenate([vb_bd[ch], kbg_bd[ch]], axis=1)) for ch in chains}

    ku_kw = {ch: lax.dot_general(kdec[ch].astype(BF16), uw[ch].astype(BF16), TN_DIMS,
                                 preferred_element_type=F32) for ch in chains}
    qu_qw = {ch: mm(qk[ch], jnp.concatenate([bd(uw[ch][:, :LANES]), bd(uw[ch][:, LANES:])], axis=1))
             for ch in chains}
    lhs = {ch: jnp.concatenate([jnp.where(same_head, ku_kw[ch][:, LANES:], 0.0),
                                qg[ch] - qu_qw[ch][:, LANES:]], axis=0).astype(BF16) for ch in chains}
    st = [st_scr[:, LANES * pr:LANES * (pr + 1)] for pr in range(n_pr)]
    o_rows = []
    for c in range(n_ch):
        prod = [jnp.dot(lhs[c, pr], st[pr].astype(BF16), preferred_element_type=F32)
                for pr in range(n_pr)]
        o_rows.append(jnp.concatenate([prod[pr][LANES:] + qu_qw[c, pr][:, :LANES] for pr in range(n_pr)], axis=1))
        st = [st[pr] * eglast[c, pr] - prod[pr][:LANES] + jnp.where(same_head, ku_kw[c, pr][:, :LANES], 0.0)
              for pr in range(n_pr)]
    st_scr[...] = jnp.concatenate(st, axis=1)

    o = jnp.concatenate(o_rows, axis=0)
    oms = jnp.dot((o * o).astype(BF16), ebd, preferred_element_type=F32) * (1.0 / HD)
    z = z_ref[...]
    o_ref[...] = o * lax.rsqrt(oms + RMS_EPS) * nw_ref[...] * (z * jax.nn.sigmoid(z))


def _gdn(gqkv, sm, gz, conv_w, a_log, dt_bias, norm_w, B, S, tg=256):
    rows = B * S
    nt = S // tg
    head_of = np.arange(GDN_WIDTH) // HD
    ebd = jnp.asarray(head_of[:, None] == head_of[None, :], dtype=BF16)
    eg = np.zeros((LANES, GDN_WIDTH), np.float32)
    eb = np.zeros((LANES, GDN_WIDTH), np.float32)
    eg[SM_A + head_of, np.arange(GDN_WIDTH)] = 1.0
    eb[SM_B + head_of, np.arange(GDN_WIDTH)] = 1.0
    r = np.arange(tg)
    lblk = ((r[:, None] // GDN_CHUNK == r[None, :] // GDN_CHUNK) & (r[None, :] <= r[:, None])).astype(np.float32)
    alog128 = jnp.zeros((1, LANES), F32).at[0, SM_A:SM_A + GDN_HEADS].set(a_log.astype(F32))
    dt128 = jnp.zeros((1, LANES), F32).at[0, SM_A:SM_A + GDN_HEADS].set(dt_bias.astype(F32))
    nw512 = jnp.tile(norm_w.astype(F32), GDN_HEADS).reshape(1, GDN_WIDTH)
    full = lambda shape: pl.BlockSpec(shape, lambda b, i: (0,) * len(shape))
    return pl.pallas_call(
        functools.partial(_gdn_kernel, tg=tg),
        grid=(B, nt),
        in_specs=[
            pl.BlockSpec((tg, 3 * GDN_WIDTH), lambda b, i: (b * nt + i, 0)),
            pl.BlockSpec((F32_SUBLANES, 3 * GDN_WIDTH),
                         lambda b, i: (jnp.maximum((b * nt + i) * (tg // F32_SUBLANES) - 1, 0), 0)),
            pl.BlockSpec((tg, LANES), lambda b, i: (b * nt + i, 0)),
            pl.BlockSpec((tg, GDN_WIDTH), lambda b, i: (b * nt + i, 0)),
            full((GDN_CONV, 3 * GDN_WIDTH)),
            full((1, LANES)),
            full((1, LANES)),
            full((1, GDN_WIDTH)),
            full((GDN_WIDTH, GDN_WIDTH)),
            full((LANES, GDN_WIDTH)),
            full((LANES, GDN_WIDTH)),
            full((tg, tg)),
        ],
        out_specs=pl.BlockSpec((tg, GDN_WIDTH), lambda b, i: (b * nt + i, 0)),
        out_shape=jax.ShapeDtypeStruct((rows, GDN_WIDTH), F32),
        scratch_shapes=[
            pltpu.VMEM((tg + F32_SUBLANES, 3 * GDN_WIDTH), F32),
            pltpu.VMEM((LANES, GDN_WIDTH), F32),
        ],
        compiler_params=_cparams(("arbitrary", "arbitrary")),
        name="gdn",
    )(gqkv, gqkv, sm, gz, conv_w, alog128, dt128, nw512, ebd, jnp.asarray(eg), jnp.asarray(eb, dtype=BF16),
      jnp.asarray(lblk))


def _mix_xattn_kernel(x_ref, a_ref, b_ref, wa_ref, wb_ref, nw_ref, wq_ref, kv_ref, wo_ref, o_ref):
    mix = jnp.dot(a_ref[...].astype(BF16), wa_ref[...], preferred_element_type=F32)
    mix = mix + jnp.dot(b_ref[...].astype(BF16), wb_ref[...], preferred_element_type=F32)
    _xattn_rows(x_ref[...] + mix, nw_ref, wq_ref, kv_ref, wo_ref, o_ref)


def _xattn_kernel(x_ref, nw_ref, wq_ref, kv_ref, wo_ref, o_ref):
    _xattn_rows(x_ref[...], nw_ref, wq_ref, kv_ref, wo_ref, o_ref)


def _xattn_rows(x, nw_ref, wq_ref, kv_ref, wo_ref, o_ref):
    hn = _rms(x, nw_ref[...]).astype(BF16)
    q = jnp.dot(hn, wq_ref[...], preferred_element_type=F32) * (XA_HD ** -0.5)
    heads = range(XA_HEADS)
    ss = [lax.dot_general(q[:, XA_HD * h:XA_HD * (h + 1)].astype(BF16), kv_ref[:, XA_HD * h:XA_HD * (h + 1)],
                          NT_DIMS, preferred_element_type=F32) for h in heads]
    es = [jnp.exp(ss[h] - jnp.max(ss[h], axis=-1, keepdims=True)).astype(BF16) for h in heads]
    ones = jnp.ones((kv_ref.shape[0], XA_HD), BF16)
    pvs = [jnp.dot(es[h], jnp.concatenate([kv_ref[:, XA_WIDTH + XA_HD * h:XA_WIDTH + XA_HD * (h + 1)], ones], axis=1),
                   preferred_element_type=F32) for h in heads]
    o = jnp.concatenate([pvs[h][:, :XA_HD] / pvs[h][:, XA_HD:XA_HD + 1] for h in heads], axis=1).astype(BF16)
    o_ref[...] = x + jnp.dot(o, wo_ref[...], preferred_element_type=F32)


def _xattn(x2d, norm_w, wq_bf, memkv_l, wo_bf, S, tm=512):
    rows = x2d.shape[0]
    per_b = S // tm
    mem_len = memkv_l.shape[0] // (rows // S)
    return pl.pallas_call(
        _xattn_kernel,
        grid=(rows // tm,),
        in_specs=[
            pl.BlockSpec((tm, D_MODEL), lambda i: (i, 0)),
            pl.BlockSpec((1, D_MODEL), lambda i: (0, 0)),
            pl.BlockSpec((D_MODEL, XA_WIDTH), lambda i: (0, 0)),
            pl.BlockSpec((mem_len, 2 * XA_WIDTH), lambda i: (i // per_b, 0)),
            pl.BlockSpec((XA_WIDTH, D_MODEL), lambda i: (0, 0)),
        ],
        out_specs=pl.BlockSpec((tm, D_MODEL), lambda i: (i, 0)),
        out_shape=jax.ShapeDtypeStruct((rows, D_MODEL), F32),
        compiler_params=_cparams(("arbitrary",)),
        name="cross_attention",
    )(x2d, norm_w.reshape(1, D_MODEL), wq_bf, memkv_l, wo_bf)


def _mix_xattn(x2d, o_nsa, o_gdn, w_out_bf, norm_w, wq_bf, memkv_l, wo_bf, S, tm=512):
    rows = x2d.shape[0]
    per_b = S // tm
    mem_len = memkv_l.shape[0] // (rows // S)
    return pl.pallas_call(
        _mix_xattn_kernel,
        grid=(rows // tm,),
        in_specs=[
            pl.BlockSpec((tm, D_MODEL), lambda i: (i, 0)),
            pl.BlockSpec((tm, NSA_WIDTH), lambda i: (i, 0)),
            pl.BlockSpec((tm, GDN_WIDTH), lambda i: (i, 0)),
            pl.BlockSpec((NSA_WIDTH, D_MODEL), lambda i: (0, 0)),
            pl.BlockSpec((GDN_WIDTH, D_MODEL), lambda i: (1, 0)),
            pl.BlockSpec((1, D_MODEL), lambda i: (0, 0)),
            pl.BlockSpec((D_MODEL, XA_WIDTH), lambda i: (0, 0)),
            pl.BlockSpec((mem_len, 2 * XA_WIDTH), lambda i: (i // per_b, 0)),
            pl.BlockSpec((XA_WIDTH, D_MODEL), lambda i: (0, 0)),
        ],
        out_specs=pl.BlockSpec((tm, D_MODEL), lambda i: (i, 0)),
        out_shape=jax.ShapeDtypeStruct((rows, D_MODEL), F32),
        compiler_params=_cparams(("arbitrary",)),
        name="mix_cross_attention",
    )(x2d, o_nsa, o_gdn, w_out_bf, w_out_bf, norm_w.reshape(1, D_MODEL), wq_bf, memkv_l, wo_bf)


def _mlp_kernel(x_ref, nw_ref, w1_ref, w2_ref, fw_ref, o_ref, hn_scr, acc_scr, *, final_norm):
    j = pl.program_id(1)

    @pl.when(j == 0)
    def _():
        x = x_ref[...]
        hn_scr[...] = _rms(x, nw_ref[...]).astype(BF16)
        acc_scr[...] = x

    h = jnp.dot(hn_scr[...], w1_ref[...], preferred_element_type=F32)
    h = jnp.square(jnp.maximum(h, 0.0)).astype(BF16)
    acc_scr[...] += jnp.dot(h, w2_ref[...], preferred_element_type=F32)

    @pl.when(j == pl.num_programs(1) - 1)
    def _():
        y = acc_scr[...]
        if final_norm:
            y = _rms(y, fw_ref[...])
        o_ref[...] = y


def _mlp(x2d, norm_w, w1_bf, w2_bf, final_w, final_norm, tm=1024, tf=1024):
    rows = x2d.shape[0]
    tm = min(tm, rows)
    return pl.pallas_call(
        functools.partial(_mlp_kernel, final_norm=final_norm),
        grid=(rows // tm, D_FF // tf),
        in_specs=[
            pl.BlockSpec((tm, D_MODEL), lambda i, j: (i, 0)),
            pl.BlockSpec((1, D_MODEL), lambda i, j: (0, 0)),
            pl.BlockSpec((D_MODEL, tf), lambda i, j: (0, j)),
            pl.BlockSpec((tf, D_MODEL), lambda i, j: (j, 0)),
            pl.BlockSpec((1, D_MODEL), lambda i, j: (0, 0)),
        ],
        out_specs=pl.BlockSpec((tm, D_MODEL), lambda i, j: (i, 0)),
        out_shape=jax.ShapeDtypeStruct((rows, D_MODEL), F32),
        scratch_shapes=[pltpu.VMEM((tm, D_MODEL), BF16), pltpu.VMEM((tm, D_MODEL), F32)],
        compiler_params=_cparams(("arbitrary", "arbitrary")),
        name="mlp",
    )(x2d, norm_w.reshape(1, D_MODEL), w1_bf, w2_bf, final_w.reshape(1, D_MODEL))


def _sc_kernel(x_ref, nw_ref, win_ref, cw_ref, wout_ref, o_ref, cu_scr, *, tm, per_b):
    i = pl.program_id(0)
    halo = F32_SUBLANES

    @pl.when(i % per_b == 0)
    def _():
        cu_scr[tm:tm + halo, :] = jnp.zeros((halo, D_MODEL), F32)

    x = x_ref[...]
    y = jnp.dot(_rms(x, nw_ref[...]).astype(BF16), win_ref[...], preferred_element_type=F32)
    cu_scr[0:halo, :] = cu_scr[tm:tm + halo, :]
    cu_scr[halo:halo + tm, :] = y[:, D_MODEL:2 * D_MODEL] * y[:, 2 * D_MODEL:]
    base = halo - (SC_WIDTH - 1)
    conv = cw_ref[0:1, :] * cu_scr[base:base + tm, :]
    for j in range(1, SC_WIDTH):
        conv = conv + cw_ref[j:j + 1, :] * cu_scr[base + j:base + j + tm, :]
    gated = (y[:, :D_MODEL] * conv).astype(BF16)
    o_ref[...] = x + jnp.dot(gated, wout_ref[...], preferred_element_type=F32)


def _sc_mixer(x2d, norm_w, win_bf, conv_w, wout_bf, S, tm=512):
    rows = x2d.shape[0]
    per_b = S // tm
    return pl.pallas_call(
        functools.partial(_sc_kernel, tm=tm, per_b=per_b),
        grid=(rows // tm,),
        in_specs=[
            pl.BlockSpec((tm, D_MODEL), lambda i: (i, 0)),
            pl.BlockSpec((1, D_MODEL), lambda i: (0, 0)),
            pl.BlockSpec((D_MODEL, 3 * D_MODEL), lambda i: (0, 0)),
            pl.BlockSpec((SC_WIDTH, D_MODEL), lambda i: (0, 0)),
            pl.BlockSpec((D_MODEL, D_MODEL), lambda i: (0, 0)),
        ],
        out_specs=pl.BlockSpec((tm, D_MODEL), lambda i: (i, 0)),
        out_shape=jax.ShapeDtypeStruct((rows, D_MODEL), F32),
        scratch_shapes=[pltpu.VMEM((tm + F32_SUBLANES, D_MODEL), F32)],
        compiler_params=_cparams(("arbitrary",)),
        name="short_conv_mixer",
    )(x2d, norm_w.reshape(1, D_MODEL), win_bf, conv_w, wout_bf)


def _hybrid_mixer(x2d, positions, norm_w, w_in, ck_pos, ck_w1, ck_w2, cv_pos, cv_w1, cv_w2,
                  gdn_conv, gdn_a_log, gdn_dt_bias, gdn_norm, B, S):
    tm = 512
    invf = (ROPE_THETA ** (-jnp.arange(0, HD, 2, dtype=F32) / HD)).reshape(HD // 2, 1)
    pos3 = positions.astype(F32).reshape(B * S // tm, 1, tm)
    qT, kr, vT, ckv, gqkv, gz, sm, gT = _proj0(x2d, norm_w, _prep_w_in(w_in), pos3, invf, B, S, tm)
    w1s, p8, w2s = _prep_compress_weights(ck_pos, ck_w1, ck_w2, cv_pos, cv_w1, cv_w2)
    kc, vcT = _compress(ckv, w1s, p8, w2s, B, S)
    o_nsa = _nsa(qT, kr, vT, kc, vcT, gT, B, S)
    o_gdn = _gdn(gqkv, sm, gz, gdn_conv, gdn_a_log, gdn_dt_bias, gdn_norm, B, S)
    return o_nsa, o_gdn


def kernel(x, mem, positions, norm_mix, norm_xattn, norm_mlp, hyb_w_in, hyb_cmp_k_pos, hyb_cmp_k_w1, hyb_cmp_k_w2, hyb_cmp_v_pos, hyb_cmp_v_w1, hyb_cmp_v_w2, hyb_gdn_conv, hyb_gdn_a_log, hyb_gdn_dt_bias, hyb_gdn_norm, hyb_w_out, sc_w_in, sc_conv, sc_w_out, mem_norm, xa_wq, xa_wkv, xa_wo, mlp_w1, mlp_w2, final_norm):
    B, S, _ = x.shape
    depth = norm_mix.shape[0]
    x2d = x.reshape(B * S, D_MODEL)
    memkv = _memkv(mem.reshape(-1, D_MODEL), mem_norm, xa_wkv.astype(BF16))
    for layer in range(depth):
        j = layer // 2
        xa = (norm_xattn[layer], xa_wq[layer].astype(BF16), memkv[layer], xa_wo[layer].astype(BF16), S)
        if layer % 2 == 0:
            o_nsa, o_gdn = _hybrid_mixer(x2d, positions, norm_mix[layer], hyb_w_in[j], hyb_cmp_k_pos[j],
                                         hyb_cmp_k_w1[j], hyb_cmp_k_w2[j], hyb_cmp_v_pos[j], hyb_cmp_v_w1[j],
                                         hyb_cmp_v_w2[j], hyb_gdn_conv[j], hyb_gdn_a_log[j], hyb_gdn_dt_bias[j],
                                         hyb_gdn_norm[j], B, S)
            x2d = _mix_xattn(x2d, o_nsa, o_gdn, hyb_w_out[j].astype(BF16), *xa)
        else:
            x2d = _sc_mixer(x2d, norm_mix[layer], sc_w_in[j].astype(BF16), sc_conv[j], sc_w_out[j].astype(BF16), S)
            x2d = _xattn(x2d, *xa)
        x2d = _mlp(x2d, norm_mlp[layer], mlp_w1[layer].astype(BF16), mlp_w2[layer].astype(BF16),
                   final_norm, layer == depth - 1)
    return x2d.reshape(B, S, D_MODEL)
```

```python
import functools

import numpy as np
import jax
import jax.numpy as jnp
from jax import lax
from jax.experimental import pallas as pl
from jax.experimental.pallas import tpu as pltpu

F32 = jnp.float32
BF16 = jnp.bfloat16
HI = lax.Precision.HIGHEST

D_MODEL = 1024
MEM_LEN = 256
RMS_EPS = 1e-6
ROPE_THETA = 10000.0
NEG_INF = -1e30
FORCE_BONUS = 1e3
LOG2E = 1.4426950408889634

NSA_HEADS = 8
NSA_KV_GROUPS = 2
NSA_HPG = NSA_HEADS // NSA_KV_GROUPS
HD = 64
CMP_LEN = 32
CMP_STRIDE = 16
CMP_HID = 2 * HD
SLC_BLK = 64
SLC_TOPK = 16
WINDOW = 512

GDN_HEADS = 8
GDN_CONV = 4
GDN_CHUNK = 64
SC_WIDTH = 3
XA_HEADS = 4
XA_HD = 128
D_FF = 4 * D_MODEL

NSA_WIDTH = NSA_HEADS * HD
NSA_KV_WIDTH = NSA_KV_GROUPS * HD
GDN_WIDTH = GDN_HEADS * HD
XA_WIDTH = XA_HEADS * XA_HD
IN_SIZES = (NSA_WIDTH,) + (NSA_KV_WIDTH,) * 6 + (3 * NSA_HEADS,) + (GDN_WIDTH,) * 3 + (GDN_HEADS, GDN_HEADS, GDN_WIDTH)

V7X_VMEM_BYTES = 64 * 1024 * 1024
VMEM_LIMIT = V7X_VMEM_BYTES * 3 // 4
LANES = 128
F32_SUBLANES = 8
BF16_ROWS = 16

C_Q, C_KR, C_VT, C_CKV, C_GQKV, C_GZ, C_SM, NC0 = 0, 512, 768, 1024, 1280, 2816, 3328, 3456
SM_GATE, SM_A, SM_B = 0, 24, 32
NSA_CHUNK = 256
KR_WIDTH = 3 * LANES
KR_WIN = 2 * LANES

NT_DIMS = (((1,), (1,)), ((), ()))
TN_DIMS = (((0,), (0,)), ((), ()))


def _cparams(sem):
    return pltpu.CompilerParams(dimension_semantics=sem, vmem_limit_bytes=VMEM_LIMIT)


def _rms(x, w):
    return x * lax.rsqrt(jnp.mean(x * x, axis=-1, keepdims=True) + RMS_EPS) * w


def _softplus(x):
    return jnp.maximum(x, 0.0) + jnp.log(1.0 + jnp.exp(-jnp.abs(x)))


def _memkv_kernel(m_ref, nw_ref, w_ref, o_ref):
    hn = _rms(m_ref[...], nw_ref[...]).astype(BF16)
    o_ref[...] = jnp.dot(hn, w_ref[...], preferred_element_type=F32).astype(BF16)


def _memkv(mem2d, mem_norm, wkv_bf):
    rows = mem2d.shape[0]
    depth = wkv_bf.shape[0]
    tm = 512
    return pl.pallas_call(
        _memkv_kernel,
        grid=(depth, rows // tm),
        in_specs=[
            pl.BlockSpec((tm, D_MODEL), lambda l, i: (i, 0)),
            pl.BlockSpec((1, D_MODEL), lambda l, i: (0, 0)),
            pl.BlockSpec((None, D_MODEL, 2 * XA_WIDTH), lambda l, i: (l, 0, 0)),
        ],
        out_specs=pl.BlockSpec((None, tm, 2 * XA_WIDTH), lambda l, i: (l, i, 0)),
        out_shape=jax.ShapeDtypeStruct((depth, rows, 2 * XA_WIDTH), BF16),
        compiler_params=_cparams(("arbitrary", "arbitrary")),
        name="memkv",
    )(mem2d, mem_norm.reshape(1, D_MODEL), wkv_bf)


def _proj0_kernel(x_ref, nw_ref, w_ref, pos_ref, invf_ref,
                  qT_ref, kr_ref, vT_ref, ckv_ref, gqkv_ref, gz_ref, sm_ref, gT_ref):
    tm = x_ref.shape[0]
    hn = _rms(x_ref[...], nw_ref[...]).astype(BF16)
    y = jnp.dot(hn, w_ref[...], preferred_element_type=F32)

    ang = invf_ref[...] * pos_ref[...]
    c = jnp.cos(ang)
    s = jnp.sin(ang)
    cos_n = jnp.concatenate([c, c, c, c], axis=0).T
    sin_n = jnp.concatenate([-s, s, -s, s], axis=0).T
    lane = lax.broadcasted_iota(jnp.int32, (tm, LANES), 1)
    first_half = (lane % HD) < (HD // 2)

    def rope(xc):
        rot = jnp.where(first_half, pltpu.roll(xc, LANES - HD // 2, 1), pltpu.roll(xc, HD // 2, 1))
        return xc * cos_n + rot * sin_n

    roped = [rope(y[:, LANES * i:LANES * (i + 1)]) for i in range(6)]
    q = jnp.concatenate(roped[:4], axis=1) * (LOG2E * HD ** -0.5)
    qT_ref[...] = q.T.astype(BF16)
    row = lax.broadcasted_iota(jnp.int32, (tm, LANES), 0)
    blk_hot = jnp.where(lane - HD == (row // SLC_BLK) % (NSA_CHUNK // SLC_BLK), 1.0, 0.0)
    k_slc = roped[4]
    lo = lane < HD
    kr_ref[...] = jnp.concatenate([jnp.where(lo, k_slc, blk_hot),
                                   jnp.where(lo, pltpu.roll(k_slc, HD, 1), blk_hot),
                                   roped[5]], axis=1).astype(BF16)
    vT_ref[...] = y[:, C_VT:C_CKV].T.astype(BF16)
    for i in range(4):
        ckv_ref[i] = y[:, C_CKV + HD * i:C_CKV + HD * (i + 1)].astype(BF16)
    gqkv_ref[...] = y[:, C_GQKV:C_GZ]
    gz_ref[...] = y[:, C_GZ:C_SM]
    sm = y[:, C_SM:NC0]
    sm_ref[...] = sm
    gT_ref[...] = jax.nn.sigmoid(sm).T[:3 * NSA_HEADS, :]


def _prep_w_in(w_in):
    offs = np.cumsum((0,) + IN_SIZES)
    (nq, kcmp, vcmp, kslc, vslc, kwin, vwin, ngate, gq, gk, gv, ga, gb, gz) = [
        w_in[:, offs[i]:offs[i + 1]] for i in range(len(IN_SIZES))]
    pad = jnp.zeros((D_MODEL, LANES - 3 * NSA_HEADS - 2 * GDN_HEADS), w_in.dtype)
    small = jnp.concatenate([ngate, ga, gb, pad], axis=1)
    return jnp.concatenate([nq, kslc, kwin, vslc, vwin, kcmp, vcmp, gq, gk, gv, gz, small], axis=1).astype(BF16)


def _proj0(x2d, norm_w, w_bf, pos3, invf, B, S, tm):
    rows = B * S
    nt = S // tm
    row_spec = lambda n: pl.BlockSpec((tm, n), lambda i: (i, 0))
    t_spec = lambda n: pl.BlockSpec((None, n, tm), lambda i: (i // nt, 0, i % nt))
    return pl.pallas_call(
        _proj0_kernel,
        grid=(rows // tm,),
        in_specs=[
            row_spec(D_MODEL),
            pl.BlockSpec((1, D_MODEL), lambda i: (0, 0)),
            pl.BlockSpec((D_MODEL, NC0), lambda i: (0, 0)),
            pl.BlockSpec((None, 1, tm), lambda i: (i, 0, 0)),
            pl.BlockSpec((HD // 2, 1), lambda i: (0, 0)),
        ],
        out_specs=[
            t_spec(NSA_WIDTH),
            row_spec(KR_WIDTH),
            t_spec(2 * NSA_KV_WIDTH),
            pl.BlockSpec((4, tm, HD), lambda i: (0, i, 0)),
            row_spec(3 * GDN_WIDTH),
            row_spec(GDN_WIDTH),
            row_spec(LANES),
            t_spec(3 * NSA_HEADS),
        ],
        out_shape=[
            jax.ShapeDtypeStruct((B, NSA_WIDTH, S), BF16),
            jax.ShapeDtypeStruct((rows, KR_WIDTH), BF16),
            jax.ShapeDtypeStruct((B, 2 * NSA_KV_WIDTH, S), BF16),
            jax.ShapeDtypeStruct((4, rows, HD), BF16),
            jax.ShapeDtypeStruct((rows, 3 * GDN_WIDTH), F32),
            jax.ShapeDtypeStruct((rows, GDN_WIDTH), F32),
            jax.ShapeDtypeStruct((rows, LANES), F32),
            jax.ShapeDtypeStruct((B, 3 * NSA_HEADS, S), F32),
        ],
        compiler_params=_cparams(("arbitrary",)),
        name="proj0",
    )(x2d, norm_w.reshape(1, D_MODEL), w_bf, pos3, invf)


def _compress_kernel(t_ref, w1_ref, pos_ref, w2_ref, o_ref, oT_ref):
    nseg = t_ref.shape[0]
    w1 = w1_ref[...]
    ab = jnp.dot(t_ref[...], w1, preferred_element_type=F32)
    pc = jnp.dot(pos_ref[...].astype(BF16), w1, preferred_element_type=F32)
    bias = pc[0:1, :CMP_HID] + pc[1:2, CMP_HID:]
    hid = ab[:, :CMP_HID] + pltpu.roll(ab[:, CMP_HID:], nseg - 1, 0) + bias
    act = hid * jax.nn.sigmoid(hid)
    out = jnp.dot(act.astype(BF16), w2_ref[...], preferred_element_type=F32)
    o_ref[...] = out.astype(BF16)
    oT_ref[...] = out.T[:HD, :].astype(BF16)


def _compress(ckv, w1s, pos8, w2s, B, S):
    nseg = S // CMP_STRIDE
    t = ckv.reshape(4, B, nseg, CMP_STRIDE * HD)
    return pl.pallas_call(
        _compress_kernel,
        grid=(4, B),
        in_specs=[
            pl.BlockSpec((None, None, nseg, CMP_STRIDE * HD), lambda k, b: (k, b, 0, 0)),
            pl.BlockSpec((None, CMP_STRIDE * HD, 2 * CMP_HID), lambda k, b: (k // 2, 0, 0)),
            pl.BlockSpec((None, F32_SUBLANES, CMP_STRIDE * HD), lambda k, b: (k // 2, 0, 0)),
            pl.BlockSpec((None, CMP_HID, LANES), lambda k, b: (k // 2, 0, 0)),
        ],
        out_specs=[
            pl.BlockSpec((None, None, nseg, LANES), lambda k, b: (k, b, 0, 0)),
            pl.BlockSpec((None, None, HD, nseg), lambda k, b: (k, b, 0, 0)),
        ],
        out_shape=[
            jax.ShapeDtypeStruct((4, B, nseg, LANES), BF16),
            jax.ShapeDtypeStruct((4, B, HD, nseg), BF16),
        ],
        compiler_params=_cparams(("arbitrary", "arbitrary")),
        name="nsa_compress",
    )(t, w1s, pos8, w2s)


def _prep_compress_weights(k_pos, k_w1, k_w2, v_pos, v_w1, v_w2):
    half = CMP_STRIDE * HD

    def w1cat(w1):
        return jnp.concatenate([w1[:half], w1[half:]], axis=1)

    def pos8(p):
        flat = p.reshape(2, half)
        return jnp.concatenate([flat, jnp.zeros((F32_SUBLANES - 2, half), p.dtype)], axis=0)

    def w2pad(w2):
        return jnp.concatenate([w2, jnp.zeros((CMP_HID, LANES - HD), w2.dtype)], axis=1)

    w1s = jnp.stack([w1cat(k_w1), w1cat(v_w1)]).astype(BF16)
    p8 = jnp.stack([pos8(k_pos), pos8(v_pos)])
    w2s = jnp.stack([w2pad(k_w2), w2pad(v_w2)]).astype(BF16)
    return w1s, p8, w2s


def _nsa_kernel(qT_ref, kr_ref, vT_ref, kc_ref, vcT_ref, gT_ref, o_ref,
                imp_scr, sel_scr, s_scr, p_scr, b_scr, ps_scr, *, S, TQ):
    CH = TQ
    groups = range(NSA_KV_GROUPS)
    n_cmp = S // CMP_STRIDE
    n_slc = S // SLC_BLK
    n_sel = min(SLC_TOPK, n_slc)
    qi = pl.program_id(1)
    q0 = qi * TQ
    t1 = q0 + lax.broadcasted_iota(jnp.int32, (1, TQ), 1)
    t4 = jnp.concatenate([t1] * NSA_HPG, axis=1)
    n_io = lax.broadcasted_iota(jnp.int32, (n_cmp, 1), 0)
    j_io = lax.broadcasted_iota(jnp.int32, (n_slc, 1), 0)
    k_io = lax.broadcasted_iota(jnp.int32, (CH, 1), 0)
    n_causal = (q0 + TQ) // SLC_BLK

    qs, q64s = [], []
    for g in groups:
        q64 = jnp.concatenate(
            [qT_ref[NSA_HPG * HD * g + HD * h:NSA_HPG * HD * g + HD * (h + 1), :] for h in range(NSA_HPG)], axis=1)
        zq = jnp.zeros_like(q64)
        qs.append(jnp.concatenate([q64, zq] if g == 0 else [zq, q64], axis=0))
        q64s.append(q64)

    n_lanes = NSA_HPG * TQ

    def probs(sc, bias, m_i):
        if bias is not None:
            sc = sc + jnp.concatenate([bias] * NSA_HPG, axis=1)
        m_new = jnp.maximum(m_i, jnp.max(sc, axis=0, keepdims=True))
        return m_new, jnp.exp2(m_i - m_new), jnp.exp2(sc - m_new).astype(BF16)

    def pv(vt, pr):
        return jnp.dot(vt, pr, preferred_element_type=F32)

    def m_init():
        return jnp.full((1, n_lanes), NEG_INF, F32)

    u_io = lax.broadcasted_iota(jnp.int32, (1, TQ), 1)
    causal_bias = jnp.where(k_io <= u_io, 0.0, NEG_INF)
    window_bias = jnp.where(k_io > u_io, 0.0, NEG_INF)

    s_cmp = [jnp.dot(kc_ref[g, :, 0:HD], q64s[g], preferred_element_type=F32)
             for g in groups]
    n_back = WINDOW // CH
    n_win = n_back + 1
    w0 = jnp.maximum(qi - n_back, 0)
    rw = pl.multiple_of(w0 * CH, CH)
    k_win = kr_ref[pl.ds(rw, n_win * CH), KR_WIN:KR_WIN + LANES]
    s_win = [jnp.dot(k_win, qs[g], preferred_element_type=F32) for g in groups]
    valid = (n_io * CMP_STRIDE + (CMP_LEN - 1)) <= t4
    any_valid = t4 >= CMP_LEN - 1
    p_cmp, psum = [], []
    for g in groups:
        s = jnp.where(valid, s_cmp[g], NEG_INF)
        m = jnp.max(s, axis=0, keepdims=True)
        e = jnp.exp2(s - m)
        l = jnp.sum(e, axis=0, keepdims=True)
        p = e * jnp.where(any_valid, 1.0 / l, 0.0)
        p_cmp.append(p.astype(BF16))
        ph = p[:, 0:TQ]
        for h in range(1, NSA_HPG):
            ph = ph + p[:, h * TQ:(h + 1) * TQ]
        psum.append(ph)
    o_cmp = [jnp.dot(vcT_ref[g], p_cmp[g], preferred_element_type=F32) for g in groups]
    bpb = SLC_BLK // CMP_STRIDE
    imps = []
    for g in groups:
        tiles = []
        for lt in range(TQ // LANES):
            ps_scr[g, lt, 0:F32_SUBLANES, :] = jnp.zeros((F32_SUBLANES, LANES), F32)
            ps_scr[g, lt, F32_SUBLANES:F32_SUBLANES + n_cmp, :] = psum[g][:, LANES * lt:LANES * (lt + 1)]
            imp = None
            for k in range(-(CMP_LEN // CMP_STRIDE - 1), bpb):
                lo_, hi_ = max(CMP_STRIDE * k, 0), min(CMP_STRIDE * k + CMP_LEN, SLC_BLK)
                term = (hi_ - lo_) / CMP_LEN * ps_scr[g, lt, pl.ds(F32_SUBLANES + k, n_slc, stride=bpb), :]
                imp = term if imp is None else imp + term
            tiles.append(imp)
        imps.append(jnp.concatenate(tiles, axis=1))
    hidden = jnp.full((CH, TQ), NEG_INF, F32)
    slab_bias = []
    for c in range(n_win):
        d = w0 + c - qi
        slab_bias.append(jnp.where(d == 0, causal_bias,
                                   jnp.where(d > 0, hidden, jnp.where(d == -n_back, window_bias, 0.0))))
    win_bias = jnp.concatenate(slab_bias, axis=0)
    win_p = [probs(s_win[g], win_bias, m_init())[2] for g in groups]
    o_win = []
    for g in groups:
        v_win = jnp.concatenate([vT_ref[NSA_KV_WIDTH + HD * g:NSA_KV_WIDTH + HD * (g + 1), pl.ds(rw, n_win * CH)],
                                 jnp.ones((BF16_ROWS, n_win * CH), BF16)], axis=0)
        acc = pv(v_win, win_p[g])
        o_win.append(acc[:HD] / acc[HD:HD + 1])
    cur = t1 // SLC_BLK
    forced = (j_io == 0) | (j_io == cur) | (j_io == cur - 1)
    causal = j_io <= cur
    vals = [jnp.where(causal, jnp.where(forced, imps[g] + FORCE_BONUS, imps[g]), -1.0) for g in groups]
    for g in groups:
        imp_scr[g] = vals[g]

    per_trip = TQ // SLC_BLK

    def rank_body(it, cnts):
        out = list(cnts)
        for i in [per_trip * it + r for r in range(per_trip)]:
            tie = jnp.where(j_io > i, 1.0, 0.0)
            for g in groups:
                vi = imp_scr[g, pl.ds(i, 1), :]
                out[g] = out[g] + jnp.where(vi > vals[g], 1.0, jnp.where(vi == vals[g], tie, 0.0))
        return tuple(out)

    cnts = lax.fori_loop(0, jnp.where(n_causal > n_sel, n_causal // per_trip, 0), rank_body,
                         tuple(jnp.zeros((n_slc, TQ), F32) for _ in groups))
    nb = CH // SLC_BLK
    for g in groups:
        selb = jnp.where(cnts[g] < float(n_sel), 0.0, NEG_INF)
        for jc in range(S // CH):
            rows = jnp.concatenate([selb[nb * jc:nb * (jc + 1), :], jnp.zeros((BF16_ROWS - nb, TQ), F32)], axis=0)
            sel_scr[g, jc] = jnp.concatenate([rows] * NSA_HPG, axis=1).astype(BF16)

    ones_rows = jnp.ones((BF16_ROWS, CH), BF16)
    zero_rows = jnp.zeros((HD - BF16_ROWS, n_lanes), BF16)
    acc_rows = HD + BF16_ROWS

    def slc_scores(jc):
        r0 = pl.multiple_of(jc * CH, CH)
        return [jnp.dot(kr_ref[pl.ds(r0, CH), LANES * g:LANES * (g + 1)],
                        jnp.concatenate([q64s[g], sel_scr[g, jc], zero_rows], axis=0),
                        preferred_element_type=F32) for g in groups]

    def values(jc, vrow):
        r0 = pl.multiple_of(jc * CH, CH)
        return [jnp.concatenate([vT_ref[vrow + HD * g:vrow + HD * (g + 1), pl.ds(r0, CH)], ones_rows], axis=0)
                for g in groups]

    def acc_init():
        return jnp.zeros((acc_rows, n_lanes), F32)

    def slc_half(jc, slot, ms):
        nxt = slc_scores(jc + 1)
        v_prev = values(jnp.maximum(jc - 1, 0), 0)
        pvs = [pv(v_prev[g], p_scr[g]) for g in groups]
        for g in groups:
            s_scr[1 - slot, g] = nxt[g]
        out = []
        for g in groups:
            m_new, alpha, pr = probs(s_scr[slot, g], None, ms[g])
            b_scr[g] = (b_scr[g] + pvs[g]) * alpha
            p_scr[g] = pr
            out.append(m_new)
        return tuple(out)

    first = slc_scores(0)
    for g in groups:
        s_scr[0, g] = first[g]
        p_scr[g] = jnp.zeros((CH, n_lanes), BF16)
        b_scr[g] = acc_init()
    ms = lax.fori_loop(0, qi // 2, lambda i, m: slc_half(2 * i + 1, 1, slc_half(2 * i, 0, m)),
                       tuple(m_init() for _ in groups))

    ms = lax.fori_loop(0, qi % 2, lambda _, m: slc_half(qi - 1, 0, m), ms)
    diag_slot = qi % 2

    v_prev = values(jnp.maximum(qi - 1, 0), 0)
    pvs = [pv(v_prev[g], p_scr[g]) for g in groups]
    slc_parts = [probs(s_scr[diag_slot, g], causal_bias, ms[g]) for g in groups]
    v_slc = values(qi, 0)
    o_slc = []
    for g in groups:
        acc = (b_scr[g] + pvs[g]) * slc_parts[g][1] + pv(v_slc[g], slc_parts[g][2])
        o_slc.append(acc[:HD] / acc[HD:HD + 1])

    for g in groups:
        rows = []
        for h in range(NSA_HPG):
            hh = NSA_HPG * g + h
            sl = slice(h * TQ, (h + 1) * TQ)
            rows.append(gT_ref[3 * hh:3 * hh + 1, :] * o_cmp[g][:, sl]
                        + gT_ref[3 * hh + 1:3 * hh + 2, :] * o_slc[g][:, sl]
                        + gT_ref[3 * hh + 2:3 * hh + 3, :] * o_win[g][:, sl])
        og = jnp.concatenate(rows, axis=0)
        o_ref[:, NSA_HPG * HD * g:NSA_HPG * HD * (g + 1)] = og.T


def _nsa(qT, kr, vT, kc, vcT, gT, B, S, TQ=NSA_CHUNK):
    n_cmp = S // CMP_STRIDE
    n_slc = S // SLC_BLK
    nq = S // TQ
    return pl.pallas_call(
        functools.partial(_nsa_kernel, S=S, TQ=TQ),
        grid=(B, nq),
        in_specs=[
            pl.BlockSpec((None, NSA_WIDTH, TQ), lambda b, i: (b, 0, i)),
            pl.BlockSpec((None, S, KR_WIDTH), lambda b, i: (b, 0, 0)),
            pl.BlockSpec((None, 2 * NSA_KV_WIDTH, S), lambda b, i: (b, 0, 0)),
            pl.BlockSpec((2, None, n_cmp, LANES), lambda b, i: (0, b, 0, 0)),
            pl.BlockSpec((2, None, HD, n_cmp), lambda b, i: (1, b, 0, 0)),
            pl.BlockSpec((None, 3 * NSA_HEADS, TQ), lambda b, i: (b, 0, i)),
        ],
        out_specs=pl.BlockSpec((TQ, NSA_WIDTH), lambda b, i: (b * nq + i, 0)),
        out_shape=jax.ShapeDtypeStruct((B * S, NSA_WIDTH), F32),
        scratch_shapes=[
            pltpu.VMEM((NSA_KV_GROUPS, n_slc, TQ), F32),
            pltpu.VMEM((NSA_KV_GROUPS, S // TQ, BF16_ROWS, NSA_HPG * TQ), BF16),
            pltpu.VMEM((2, NSA_KV_GROUPS, TQ, NSA_HPG * TQ), F32),
            pltpu.VMEM((NSA_KV_GROUPS, TQ, NSA_HPG * TQ), BF16),
            pltpu.VMEM((NSA_KV_GROUPS, HD + BF16_ROWS, NSA_HPG * TQ), F32),
            pltpu.VMEM((NSA_KV_GROUPS, TQ // LANES, F32_SUBLANES + n_cmp, LANES), F32),
        ],
        compiler_params=_cparams(("arbitrary", "arbitrary")),
        name="nsa_attention",
    )(qT, kr.reshape(B, S, KR_WIDTH), vT, kc, vcT, gT)


def _gdn_kernel(x_ref, xp_ref, sm_ref, z_ref, cw_ref, alog_ref, dt_ref, nw_ref,
                ebd_ref, eg_ref, eb_ref, lblk_ref, o_ref,
                xe_scr, st_scr, *, tg):
    C = GDN_CHUNK
    n_ch = tg // C
    n_pr = GDN_HEADS // 2
    i = pl.program_id(1)

    @pl.when(i == 0)
    def _():
        st_scr[...] = jnp.zeros_like(st_scr)

    def decay_terms():
        sm = sm_ref[...]
        g128 = -jnp.exp(alog_ref[...]) * _softplus(sm + dt_ref[...])
        gcs128 = jnp.dot(lblk_ref[...], g128, precision=HI, preferred_element_type=F32)
        g_exp = jnp.dot(gcs128, eg_ref[...], precision=HI, preferred_element_type=F32)
        b_exp = jnp.dot(jax.nn.sigmoid(sm).astype(BF16), eb_ref[...], preferred_element_type=F32)
        return g_exp, b_exp, [gcs128[C * c:C * (c + 1), :].T[SM_A:SM_A + GDN_HEADS, :]
                              for c in range(n_ch)]

    halo = F32_SUBLANES
    xe_scr[0:halo, :] = jnp.where(i == 0, 0.0, xp_ref[...])
    xe_scr[halo:halo + tg, :] = x_ref[...]
    ebd = ebd_ref[...]
    g_all, b_all, gts = decay_terms()
    n_split = 2
    rh = tg // n_split

    def conv_half(hf):
        base = halo - (GDN_CONV - 1) + rh * hf
        y = cw_ref[0:1, :] * xe_scr[base:base + rh, :]
        for j in range(1, GDN_CONV):
            y = y + cw_ref[j:j + 1, :] * xe_scr[base + j:base + j + rh, :]
        qkv = y * jax.nn.sigmoid(y)
        q = qkv[:, 0:GDN_WIDTH]
        k = qkv[:, GDN_WIDTH:2 * GDN_WIDTH]
        ss = jnp.dot(jnp.concatenate([q * q, k * k], axis=0).astype(BF16), ebd, preferred_element_type=F32)
        return (q * lax.rsqrt(ss[:rh] + 1e-6) * (HD ** -0.5), k * lax.rsqrt(ss[rh:] + 1e-6),
                qkv[:, 2 * GDN_WIDTH:])

    halves = [conv_half(hf) for hf in range(n_split)]
    q_all, k_all, v_all = (jnp.concatenate([hv[n] for hv in halves], axis=0) for n in range(3))

    ii = lax.broadcasted_iota(jnp.int32, (C, LANES), 0)
    lane = lax.broadcasted_iota(jnp.int32, (C, LANES), 1)
    jj = lane % HD
    eye2 = jnp.where(ii == jj, 1.0, 0.0)
    lo_half = lane < HD
    lane2 = lax.broadcasted_iota(jnp.int32, (LANES, LANES), 1)
    row2 = lax.broadcasted_iota(jnp.int32, (LANES, LANES), 0)
    same_head = (lane2 < HD) == (row2 < HD)

    def bd(x):
        xb = x.astype(BF16)
        zero = jnp.zeros_like(xb)
        return jnp.concatenate([jnp.where(lo_half, xb, zero), jnp.where(lo_half, zero, xb)], axis=0)

    def mm(a, b):
        return jnp.dot(a.astype(BF16), b, preferred_element_type=F32)

    chains = [(c, pr) for c in range(n_ch) for pr in range(n_pr)]

    def blk(arr, c, pr):
        return arr[C * c:C * (c + 1), LANES * pr:LANES * (pr + 1)]

    qg, kdec, dec, aq, vb_bd, kbg_bd, eglast = {}, {}, {}, {}, {}, {}, {}
    for ch in chains:
        c, pr = ch
        qn, k_, gc, be = blk(q_all, c, pr), blk(k_all, c, pr), blk(g_all, c, pr), blk(b_all, c, pr)
        kb = k_ * be
        eg = jnp.exp(gc)
        glast = gc[C - 1:C, :]
        qg[ch] = qn * eg
        kdec[ch] = k_ * jnp.exp(glast - gc)
        eglast[ch] = jnp.exp(glast)
        vb_bd[ch] = bd(blk(v_all, c, pr) * be)
        kbg_bd[ch] = bd(kb * eg)
        gt = gts[c]
        gct = jnp.concatenate([jnp.broadcast_to(gt[2 * pr:2 * pr + 1, :], (C, HD)),
                               jnp.broadcast_to(gt[2 * pr + 1:2 * pr + 2, :], (C, HD))], axis=1)
        dec[ch] = jnp.exp(jnp.where(ii >= jj, gc - gct, NEG_INF))
        aq[ch] = lax.dot_general(jnp.concatenate([kb, qn], axis=0).astype(BF16), bd(k_), NT_DIMS,
                                 preferred_element_type=F32)
    a = {ch: jnp.where(ii > jj, aq[ch][:C] * dec[ch], 0.0) for ch in chains}
    qk = {ch: aq[ch][C:] * dec[ch] for ch in chains}
    x = {ch: eye2 - a[ch] for ch in chains}
    pw = {ch: mm(a[ch], bd(a[ch])) for ch in chains}
    for _ in range(4):
        both = {ch: mm(jnp.concatenate([x[ch], pw[ch]], axis=0), bd(pw[ch])) for ch in chains}
        x = {ch: x[ch] + both[ch][:C] for ch in chains}
        pw = {ch: both[ch][C:] for ch in chains}
    x = {ch: x[ch] + mm(x[ch], bd(pw[ch])) for ch in chains}
    uw = {ch: mm(x[ch], jnp.concatenate([vb_bd[ch], kbg_bd[ch]], axis=1)) for ch in chains}

    ku_kw = {ch: lax.dot_general(kdec[ch].astype(BF16), uw[ch].astype(BF16), TN_DIMS,
                                 preferred_element_type=F32) for ch in chains}
    qu_qw = {ch: mm(qk[ch], jnp.concatenate([bd(uw[ch][:, :LANES]), bd(uw[ch][:, LANES:])], axis=1))
             for ch in chains}
    lhs = {ch: jnp.concatenate([jnp.where(same_head, ku_kw[ch][:, LANES:], 0.0),
                                qg[ch] - qu_qw[ch][:, LANES:]], axis=0).astype(BF16) for ch in chains}
    st = [st_scr[:, LANES * pr:LANES * (pr + 1)] for pr in range(n_pr)]
    o_rows = []
    for c in range(n_ch):
        prod = [jnp.dot(lhs[c, pr], st[pr].astype(BF16), preferred_element_type=F32)
                for pr in range(n_pr)]
        o_rows.append(jnp.concatenate([prod[pr][LANES:] + qu_qw[c, pr][:, :LANES] for pr in range(n_pr)], axis=1))
        st = [st[pr] * eglast[c, pr] - prod[pr][:LANES] + jnp.where(same_head, ku_kw[c, pr][:, :LANES], 0.0)
              for pr in range(n_pr)]
    st_scr[...] = jnp.concatenate(st, axis=1)

    o = jnp.concatenate(o_rows, axis=0)
    oms = jnp.dot((o * o).astype(BF16), ebd, preferred_element_type=F32) * (1.0 / HD)
    z = z_ref[...]
    o_ref[...] = o * lax.rsqrt(oms + RMS_EPS) * nw_ref[...] * (z * jax.nn.sigmoid(z))


def _gdn(gqkv, sm, gz, conv_w, a_log, dt_bias, norm_w, B, S, tg=256):
    rows = B * S
    nt = S // tg
    head_of = np.arange(GDN_WIDTH) // HD
    ebd = jnp.asarray(head_of[:, None] == head_of[None, :], dtype=BF16)
    eg = np.zeros((LANES, GDN_WIDTH), np.float32)
    eb = np.zeros((LANES, GDN_WIDTH), np.float32)
    eg[SM_A + head_of, np.arange(GDN_WIDTH)] = 1.0
    eb[SM_B + head_of, np.arange(GDN_WIDTH)] = 1.0
    r = np.arange(tg)
    lblk = ((r[:, None] // GDN_CHUNK == r[None, :] // GDN_CHUNK) & (r[None, :] <= r[:, None])).astype(np.float32)
    alog128 = jnp.zeros((1, LANES), F32).at[0, SM_A:SM_A + GDN_HEADS].set(a_log.astype(F32))
    dt128 = jnp.zeros((1, LANES), F32).at[0, SM_A:SM_A + GDN_HEADS].set(dt_bias.astype(F32))
    nw512 = jnp.tile(norm_w.astype(F32), GDN_HEADS).reshape(1, GDN_WIDTH)
    full = lambda shape: pl.BlockSpec(shape, lambda b, i: (0,) * len(shape))
    return pl.pallas_call(
        functools.partial(_gdn_kernel, tg=tg),
        grid=(B, nt),
        in_specs=[
            pl.BlockSpec((tg, 3 * GDN_WIDTH), lambda b, i: (b * nt + i, 0)),
            pl.BlockSpec((F32_SUBLANES, 3 * GDN_WIDTH),
                         lambda b, i: (jnp.maximum((b * nt + i) * (tg // F32_SUBLANES) - 1, 0), 0)),
            pl.BlockSpec((tg, LANES), lambda b, i: (b * nt + i, 0)),
            pl.BlockSpec((tg, GDN_WIDTH), lambda b, i: (b * nt + i, 0)),
            full((GDN_CONV, 3 * GDN_WIDTH)),
            full((1, LANES)),
            full((1, LANES)),
            full((1, GDN_WIDTH)),
            full((GDN_WIDTH, GDN_WIDTH)),
            full((LANES, GDN_WIDTH)),
            full((LANES, GDN_WIDTH)),
            full((tg, tg)),
        ],
        out_specs=pl.BlockSpec((tg, GDN_WIDTH), lambda b, i: (b * nt + i, 0)),
        out_shape=jax.ShapeDtypeStruct((rows, GDN_WIDTH), F32),
        scratch_shapes=[
            pltpu.VMEM((tg + F32_SUBLANES, 3 * GDN_WIDTH), F32),
            pltpu.VMEM((LANES, GDN_WIDTH), F32),
        ],
        compiler_params=_cparams(("arbitrary", "arbitrary")),
        name="gdn",
    )(gqkv, gqkv, sm, gz, conv_w, alog128, dt128, nw512, ebd, jnp.asarray(eg), jnp.asarray(eb, dtype=BF16),
      jnp.asarray(lblk))


def _mix_xattn_kernel(x_ref, a_ref, b_ref, wa_ref, wb_ref, nw_ref, wq_ref, kv_ref, wo_ref, o_ref):
    mix = jnp.dot(a_ref[...].astype(BF16), wa_ref[...], preferred_element_type=F32)
    mix = mix + jnp.dot(b_ref[...].astype(BF16), wb_ref[...], preferred_element_type=F32)
    _xattn_rows(x_ref[...] + mix, nw_ref, wq_ref, kv_ref, wo_ref, o_ref)


def _xattn_kernel(x_ref, nw_ref, wq_ref, kv_ref, wo_ref, o_ref):
    _xattn_rows(x_ref[...], nw_ref, wq_ref, kv_ref, wo_ref, o_ref)


def _xattn_rows(x, nw_ref, wq_ref, kv_ref, wo_ref, o_ref):
    hn = _rms(x, nw_ref[...]).astype(BF16)
    q = jnp.dot(hn, wq_ref[...], preferred_element_type=F32) * (XA_HD ** -0.5)
    heads = range(XA_HEADS)
    ss = [lax.dot_general(q[:, XA_HD * h:XA_HD * (h + 1)].astype(BF16), kv_ref[:, XA_HD * h:XA_HD * (h + 1)],
                          NT_DIMS, preferred_element_type=F32) for h in heads]
    es = [jnp.exp(ss[h] - jnp.max(ss[h], axis=-1, keepdims=True)).astype(BF16) for h in heads]
    ones = jnp.ones((kv_ref.shape[0], XA_HD), BF16)
    pvs = [jnp.dot(es[h], jnp.concatenate([kv_ref[:, XA_WIDTH + XA_HD * h:XA_WIDTH + XA_HD * (h + 1)], ones], axis=1),
                   preferred_element_type=F32) for h in heads]
    o = jnp.concatenate([pvs[h][:, :XA_HD] / pvs[h][:, XA_HD:XA_HD + 1] for h in heads], axis=1).astype(BF16)
    o_ref[...] = x + jnp.dot(o, wo_ref[...], preferred_element_type=F32)


def _xattn(x2d, norm_w, wq_bf, memkv_l, wo_bf, S, tm=512):
    rows = x2d.shape[0]
    per_b = S // tm
    mem_len = memkv_l.shape[0] // (rows // S)
    return pl.pallas_call(
        _xattn_kernel,
        grid=(rows // tm,),
        in_specs=[
            pl.BlockSpec((tm, D_MODEL), lambda i: (i, 0)),
            pl.BlockSpec((1, D_MODEL), lambda i: (0, 0)),
            pl.BlockSpec((D_MODEL, XA_WIDTH), lambda i: (0, 0)),
            pl.BlockSpec((mem_len, 2 * XA_WIDTH), lambda i: (i // per_b, 0)),
            pl.BlockSpec((XA_WIDTH, D_MODEL), lambda i: (0, 0)),
        ],
        out_specs=pl.BlockSpec((tm, D_MODEL), lambda i: (i, 0)),
        out_shape=jax.ShapeDtypeStruct((rows, D_MODEL), F32),
        compiler_params=_cparams(("arbitrary",)),
        name="cross_attention",
    )(x2d, norm_w.reshape(1, D_MODEL), wq_bf, memkv_l, wo_bf)


def _mix_xattn(x2d, o_nsa, o_gdn, w_out_bf, norm_w, wq_bf, memkv_l, wo_bf, S, tm=512):
    rows = x2d.shape[0]
    per_b = S // tm
    mem_len = memkv_l.shape[0] // (rows // S)
    return pl.pallas_call(
        _mix_xattn_kernel,
        grid=(rows // tm,),
        in_specs=[
            pl.BlockSpec((tm, D_MODEL), lambda i: (i, 0)),
            pl.BlockSpec((tm, NSA_WIDTH), lambda i: (i, 0)),
            pl.BlockSpec((tm, GDN_WIDTH), lambda i: (i, 0)),
            pl.BlockSpec((NSA_WIDTH, D_MODEL), lambda i: (0, 0)),
            pl.BlockSpec((GDN_WIDTH, D_MODEL), lambda i: (1, 0)),
            pl.BlockSpec((1, D_MODEL), lambda i: (0, 0)),
            pl.BlockSpec((D_MODEL, XA_WIDTH), lambda i: (0, 0)),
            pl.BlockSpec((mem_len, 2 * XA_WIDTH), lambda i: (i // per_b, 0)),
            pl.BlockSpec((XA_WIDTH, D_MODEL), lambda i: (0, 0)),
        ],
        out_specs=pl.BlockSpec((tm, D_MODEL), lambda i: (i, 0)),
        out_shape=jax.ShapeDtypeStruct((rows, D_MODEL), F32),
        compiler_params=_cparams(("arbitrary",)),
        name="mix_cross_attention",
    )(x2d, o_nsa, o_gdn, w_out_bf, w_out_bf, norm_w.reshape(1, D_MODEL), wq_bf, memkv_l, wo_bf)


def _mlp_kernel(x_ref, nw_ref, w1_ref, w2_ref, fw_ref, o_ref, hn_scr, acc_scr, *, final_norm):
    j = pl.program_id(1)

    @pl.when(j == 0)
    def _():
        x = x_ref[...]
        hn_scr[...] = _rms(x, nw_ref[...]).astype(BF16)
        acc_scr[...] = x

    h = jnp.dot(hn_scr[...], w1_ref[...], preferred_element_type=F32)
    h = jnp.square(jnp.maximum(h, 0.0)).astype(BF16)
    acc_scr[...] += jnp.dot(h, w2_ref[...], preferred_element_type=F32)

    @pl.when(j == pl.num_programs(1) - 1)
    def _():
        y = acc_scr[...]
        if final_norm:
            y = _rms(y, fw_ref[...])
        o_ref[...] = y


def _mlp(x2d, norm_w, w1_bf, w2_bf, final_w, final_norm, tm=1024, tf=1024):
    rows = x2d.shape[0]
    tm = min(tm, rows)
    return pl.pallas_call(
        functools.partial(_mlp_kernel, final_norm=final_norm),
        grid=(rows // tm, D_FF // tf),
        in_specs=[
            pl.BlockSpec((tm, D_MODEL), lambda i, j: (i, 0)),
            pl.BlockSpec((1, D_MODEL), lambda i, j: (0, 0)),
            pl.BlockSpec((D_MODEL, tf), lambda i, j: (0, j)),
            pl.BlockSpec((tf, D_MODEL), lambda i, j: (j, 0)),
            pl.BlockSpec((1, D_MODEL), lambda i, j: (0, 0)),
        ],
        out_specs=pl.BlockSpec((tm, D_MODEL), lambda i, j: (i, 0)),
        out_shape=jax.ShapeDtypeStruct((rows, D_MODEL), F32),
        scratch_shapes=[pltpu.VMEM((tm, D_MODEL), BF16), pltpu.VMEM((tm, D_MODEL), F32)],
        compiler_params=_cparams(("arbitrary", "arbitrary")),
        name="mlp",
    )(x2d, norm_w.reshape(1, D_MODEL), w1_bf, w2_bf, final_w.reshape(1, D_MODEL))


def _sc_kernel(x_ref, nw_ref, win_ref, cw_ref, wout_ref, o_ref, cu_scr, *, tm, per_b):
    i = pl.program_id(0)
    halo = F32_SUBLANES

    @pl.when(i % per_b == 0)
    def _():
        cu_scr[tm:tm + halo, :] = jnp.zeros((halo, D_MODEL), F32)

    x = x_ref[...]
    y = jnp.dot(_rms(x, nw_ref[...]).astype(BF16), win_ref[...], preferred_element_type=F32)
    cu_scr[0:halo, :] = cu_scr[tm:tm + halo, :]
    cu_scr[halo:halo + tm, :] = y[:, D_MODEL:2 * D_MODEL] * y[:, 2 * D_MODEL:]
    base = halo - (SC_WIDTH - 1)
    conv = cw_ref[0:1, :] * cu_scr[base:base + tm, :]
    for j in range(1, SC_WIDTH):
        conv = conv + cw_ref[j:j + 1, :] * cu_scr[base + j:base + j + tm, :]
    gated = (y[:, :D_MODEL] * conv).astype(BF16)
    o_ref[...] = x + jnp.dot(gated, wout_ref[...], preferred_element_type=F32)


def _sc_mixer(x2d, norm_w, win_bf, conv_w, wout_bf, S, tm=512):
    rows = x2d.shape[0]
    per_b = S // tm
    return pl.pallas_call(
        functools.partial(_sc_kernel, tm=tm, per_b=per_b),
        grid=(rows // tm,),
        in_specs=[
            pl.BlockSpec((tm, D_MODEL), lambda i: (i, 0)),
            pl.BlockSpec((1, D_MODEL), lambda i: (0, 0)),
            pl.BlockSpec((D_MODEL, 3 * D_MODEL), lambda i: (0, 0)),
            pl.BlockSpec((SC_WIDTH, D_MODEL), lambda i: (0, 0)),
            pl.BlockSpec((D_MODEL, D_MODEL), lambda i: (0, 0)),
        ],
        out_specs=pl.BlockSpec((tm, D_MODEL), lambda i: (i, 0)),
        out_shape=jax.ShapeDtypeStruct((rows, D_MODEL), F32),
        scratch_shapes=[pltpu.VMEM((tm + F32_SUBLANES, D_MODEL), F32)],
        compiler_params=_cparams(("arbitrary",)),
        name="short_conv_mixer",
    )(x2d, norm_w.reshape(1, D_MODEL), win_bf, conv_w, wout_bf)


def _hybrid_mixer(x2d, positions, norm_w, w_in, ck_pos, ck_w1, ck_w2, cv_pos, cv_w1, cv_w2,
                  gdn_conv, gdn_a_log, gdn_dt_bias, gdn_norm, B, S):
    tm = 512
    invf = (ROPE_THETA ** (-jnp.arange(0, HD, 2, dtype=F32) / HD)).reshape(HD // 2, 1)
    pos3 = positions.astype(F32).reshape(B * S // tm, 1, tm)
    qT, kr, vT, ckv, gqkv, gz, sm, gT = _proj0(x2d, norm_w, _prep_w_in(w_in), pos3, invf, B, S, tm)
    w1s, p8, w2s = _prep_compress_weights(ck_pos, ck_w1, ck_w2, cv_pos, cv_w1, cv_w2)
    kc, vcT = _compress(ckv, w1s, p8, w2s, B, S)
    o_nsa = _nsa(qT, kr, vT, kc, vcT, gT, B, S)
    o_gdn = _gdn(gqkv, sm, gz, gdn_conv, gdn_a_log, gdn_dt_bias, gdn_norm, B, S)
    return o_nsa, o_gdn


def kernel(x, mem, positions, norm_mix, norm_xattn, norm_mlp, hyb_w_in, hyb_cmp_k_pos, hyb_cmp_k_w1, hyb_cmp_k_w2, hyb_cmp_v_pos, hyb_cmp_v_w1, hyb_cmp_v_w2, hyb_gdn_conv, hyb_gdn_a_log, hyb_gdn_dt_bias, hyb_gdn_norm, hyb_w_out, sc_w_in, sc_conv, sc_w_out, mem_norm, xa_wq, xa_wkv, xa_wo, mlp_w1, mlp_w2, final_norm):
    B, S, _ = x.shape
    depth = norm_mix.shape[0]
    x2d = x.reshape(B * S, D_MODEL)
    memkv = _memkv(mem.reshape(-1, D_MODEL), mem_norm, xa_wkv.astype(BF16))
    for layer in range(depth):
        j = layer // 2
        xa = (norm_xattn[layer], xa_wq[layer].astype(BF16), memkv[layer], xa_wo[layer].astype(BF16), S)
        if layer % 2 == 0:
            o_nsa, o_gdn = _hybrid_mixer(x2d, positions, norm_mix[layer], hyb_w_in[j], hyb_cmp_k_pos[j],
                                         hyb_cmp_k_w1[j], hyb_cmp_k_w2[j], hyb_cmp_v_pos[j], hyb_cmp_v_w1[j],
                                         hyb_cmp_v_w2[j], hyb_gdn_conv[j], hyb_gdn_a_log[j], hyb_gdn_dt_bias[j],
                                         hyb_gdn_norm[j], B, S)
            x2d = _mix_xattn(x2d, o_nsa, o_gdn, hyb_w_out[j].astype(BF16), *xa)
        else:
            x2d = _sc_mixer(x2d, norm_mix[layer], sc_w_in[j].astype(BF16), sc_conv[j], sc_w_out[j].astype(BF16), S)
            x2d = _xattn(x2d, *xa)
        x2d = _mlp(x2d, norm_mlp[layer], mlp_w1[layer].astype(BF16), mlp_w2[layer].astype(BF16),
                   final_norm, layer == depth - 1)
    return x2d.reshape(B, S, D_MODEL)
```
